```python
import jax, jax.numpy as jnp
from jax import lax
import numpy as np

D_MODEL = 1024
BATCH = 8
SEQ = 16384
DEPTH = 1

PLE_DIM = 256
CONV_WIDTH = 1024
CONV_K = 3
N_HEADS = 4
QK_DIM = 1024
V_DIM = 2048
DK = QK_DIM // N_HEADS
DV = V_DIM // N_HEADS
CHUNK = 64
EPS = 1e-6
SPLIT_SIZES = (CONV_WIDTH, CONV_WIDTH, CONV_WIDTH, CONV_WIDTH,
               QK_DIM, QK_DIM, V_DIM, V_DIM, V_DIM, N_HEADS, N_HEADS,
               D_MODEL, D_MODEL)
N_IN = sum(SPLIT_SIZES)
SPLIT_POINTS = tuple(int(s) for s in np.cumsum(SPLIT_SIZES)[:-1])

kernel_name = "hybrid_shortconv_mlstm_gated_merge"


def rmsnorm(x, g):
    xf = x.astype(jnp.float32)
    y = xf * lax.rsqrt(jnp.mean(xf * xf, axis=-1, keepdims=True) + EPS)
    return (y * g.astype(jnp.float32)).astype(x.dtype)


def causal_short_conv(u, w, b):
    s = u.shape[1]
    up = jnp.pad(u, ((0, 0), (CONV_K - 1, 0), (0, 0)))
    y = up[:, 0:s] * w[0]
    for j in range(1, CONV_K):
        y = y + up[:, j:j + s] * w[j]
    return y + b


def mlstm_chunkwise(q, k, v, i_raw, f_raw):
    bsz, s = q.shape[0], q.shape[1]
    nc = s // CHUNK
    f32 = jnp.float32

    def to_chunks(t):
        return t.astype(f32).reshape(bsz, nc, CHUNK, N_HEADS, -1).transpose(1, 0, 3, 2, 4)

    def gate_chunks(t):
        return t.astype(f32).reshape(bsz, nc, CHUNK, N_HEADS).transpose(1, 0, 3, 2)

    qc = to_chunks(q) * (DK ** -0.5)
    kc = to_chunks(k)
    vc = to_chunks(v)
    lic = gate_chunks(i_raw)
    lfc = jax.nn.log_sigmoid(gate_chunks(f_raw))
    mask = jnp.tril(jnp.ones((CHUNK, CHUNK), dtype=bool))

    def step(carry, xs):
        c_st, n_st, m_st = carry
        qq, kk, vv, li, lf = xs
        bcum = jnp.cumsum(lf, axis=-1)
        dmat = bcum[..., :, None] - bcum[..., None, :] + li[..., None, :]
        dmat = jnp.where(mask, dmat, -jnp.inf)
        a = bcum + m_st[..., None]
        m_row = jnp.maximum(a, jnp.max(dmat, axis=-1))
        sc = jnp.einsum('bhld,bhsd->bhls', qq, kk) * jnp.exp(dmat - m_row[..., None])
        inter = jnp.exp(a - m_row)
        num = (jnp.einsum('bhls,bhsv->bhlv', sc, vv)
               + inter[..., None] * jnp.einsum('bhld,bhdv->bhlv', qq, c_st))
        den = jnp.sum(sc, axis=-1) + inter * jnp.einsum('bhld,bhd->bhl', qq, n_st)
        h = num / jnp.maximum(jnp.abs(den), jnp.exp(-m_row))[..., None]
        b_last = bcum[..., -1]
        g = b_last[..., None] - bcum + li
        m_new = jnp.maximum(b_last + m_st, jnp.max(g, axis=-1))
        w = jnp.exp(g - m_new[..., None])
        decay = jnp.exp(b_last + m_st - m_new)
        c_new = decay[..., None, None] * c_st + jnp.einsum('bhsd,bhsv->bhdv', kk * w[..., None], vv)
        n_new = decay[..., None] * n_st + jnp.einsum('bhs,bhsd->bhd', w, kk)
        return (c_new, n_new, m_new), h

    init = (jnp.zeros((bsz, N_HEADS, DK, DV), f32),
            jnp.zeros((bsz, N_HEADS, DK), f32),
            jnp.full((bsz, N_HEADS), -jnp.inf, f32))
    _, hs = lax.scan(step, init, (qc, kc, vc, lic, lfc))
    return hs.transpose(1, 0, 3, 2, 4).reshape(bsz, s, N_HEADS, DV)


def _fwd_setup_inputs(seed: int = 0) -> dict:
    key = jax.random.key(seed)
    ks = jax.random.split(key, 16)
    nrm = jax.random.normal
    f32 = jnp.float32
    x = nrm(ks[0], (BATCH, SEQ, D_MODEL), f32)
    p = nrm(ks[1], (DEPTH, BATCH, SEQ, PLE_DIM), f32)
    g_mix = 1.0 + 0.02 * nrm(ks[2], (DEPTH, D_MODEL), f32)
    w_in = nrm(ks[3], (DEPTH, D_MODEL, N_IN), f32) * D_MODEL ** -0.5
    conv_w = nrm(ks[4], (DEPTH, CONV_K, CONV_WIDTH), f32) * CONV_K ** -0.5
    conv_b = 0.02 * nrm(ks[5], (DEPTH, CONV_WIDTH), f32)
    w_a_out = nrm(ks[6], (DEPTH, CONV_WIDTH, D_MODEL), f32) * CONV_WIDTH ** -0.5
    gb_noise = 0.1 * nrm(ks[7], (DEPTH, 2 * N_HEADS), f32)
    f_offset = jnp.concatenate([jnp.zeros((N_HEADS,), f32), jnp.linspace(3.0, 6.0, N_HEADS, dtype=f32)])
    b_gates = gb_noise + f_offset
    g_head = 1.0 + 0.02 * nrm(ks[8], (DEPTH, V_DIM), f32)
    w_b_out = nrm(ks[9], (DEPTH, V_DIM, D_MODEL), f32) * V_DIM ** -0.5
    w_o = nrm(ks[10], (DEPTH, D_MODEL, D_MODEL), f32) * D_MODEL ** -0.5
    g_ple = 1.0 + 0.02 * nrm(ks[11], (DEPTH, D_MODEL), f32)
    w_ple_gate = nrm(ks[12], (DEPTH, D_MODEL, D_MODEL), f32) * D_MODEL ** -0.5
    w_ple = nrm(ks[13], (DEPTH, PLE_DIM, D_MODEL), f32) * PLE_DIM ** -0.5
    g_final = 1.0 + 0.02 * nrm(ks[14], (D_MODEL,), f32)
    return {"x": x, "p": p, "g_mix": g_mix, "w_in": w_in, "conv_w": conv_w,
            "conv_b": conv_b, "w_a_out": w_a_out, "b_gates": b_gates,
            "g_head": g_head, "w_b_out": w_b_out, "w_o": w_o, "g_ple": g_ple,
            "w_ple_gate": w_ple_gate, "w_ple": w_ple, "g_final": g_final}


def _fwd_reference(x, p, g_mix, w_in, conv_w, conv_b, w_a_out, b_gates, g_head,
              w_b_out, w_o, g_ple, w_ple_gate, w_ple, g_final):
    bsz, s = x.shape[0], x.shape[1]
    for l in range(DEPTH):
        hn = rmsnorm(x, g_mix[l])
        proj = jnp.einsum('bsd,de->bse', hn, w_in[l])
        (xa, ba, ca, za, q, k, v, o, zb, ig, fg, ga, gb) = jnp.split(proj, SPLIT_POINTS, axis=-1)

        ya = ba * causal_short_conv(ca * xa, conv_w[l], conv_b[l])
        ya = jnp.einsum('bsc,cd->bsd', ya * jax.nn.silu(za), w_a_out[l])

        ig = ig + b_gates[l, :N_HEADS]
        fg = fg + b_gates[l, N_HEADS:]
        hb = mlstm_chunkwise(q.reshape(bsz, s, N_HEADS, DK), k.reshape(bsz, s, N_HEADS, DK),
                             v.reshape(bsz, s, N_HEADS, DV), ig, fg)
        hb = rmsnorm(hb, g_head[l].reshape(N_HEADS, DV)).reshape(bsz, s, V_DIM).astype(x.dtype)
        yb = jax.nn.sigmoid(o) * hb * jax.nn.silu(zb)
        yb = jnp.einsum('bsv,vd->bsd', yb, w_b_out[l])

        merged = jax.nn.sigmoid(ga) * ya + jax.nn.sigmoid(gb) * yb
        x = x + jnp.einsum('bsd,de->bse', merged, w_o[l])

        gate = jax.nn.sigmoid(jnp.einsum('bsd,de->bse', rmsnorm(x, g_ple[l]), w_ple_gate[l]))
        x = x + gate * jnp.einsum('bsp,pd->bsd', p[l], w_ple[l])
    return rmsnorm(x, g_final)


import jax as _jax
import jax.numpy as _jnp

TWIN_FORMAT = 'train_step'
FWD_PARAMS = ['x', 'p', 'g_mix', 'w_in', 'conv_w', 'conv_b', 'w_a_out', 'b_gates', 'g_head', 'w_b_out', 'w_o', 'g_ple', 'w_ple_gate', 'w_ple', 'g_final']
TWIN_WEIGHTS = ['g_mix', 'w_in', 'conv_w', 'conv_b', 'w_a_out', 'b_gates', 'g_head', 'w_b_out', 'w_o', 'g_ple', 'w_ple_gate', 'w_ple', 'g_final']
TWIN_DIFF_INPUT = 'x'
TWIN_INPUTS = ['x', 'p', 'g_mix', 'w_in', 'conv_w', 'conv_b', 'w_a_out', 'b_gates', 'g_head', 'w_b_out', 'w_o', 'g_ple', 'w_ple_gate', 'w_ple', 'g_final', 'loss_target', 'm_g_mix', 'm_w_in', 'm_conv_w', 'm_conv_b', 'm_w_a_out', 'm_b_gates', 'm_g_head', 'm_w_b_out', 'm_w_o', 'm_g_ple', 'm_w_ple_gate', 'm_w_ple', 'm_g_final', 'v_g_mix', 'v_w_in', 'v_conv_w', 'v_conv_b', 'v_w_a_out', 'v_b_gates', 'v_g_head', 'v_w_b_out', 'v_w_o', 'v_g_ple', 'v_w_ple_gate', 'v_w_ple', 'v_g_final']
TWIN_OUTPUTS = ['loss', 'grad_x', 'grad_g_mix', 'grad_w_in', 'grad_conv_w', 'grad_conv_b', 'grad_w_a_out', 'grad_b_gates', 'grad_g_head', 'grad_w_b_out', 'grad_w_o', 'grad_g_ple', 'grad_w_ple_gate', 'grad_w_ple', 'grad_g_final', 'delta_g_mix', 'delta_w_in', 'delta_conv_w', 'delta_conv_b', 'delta_w_a_out', 'delta_b_gates', 'delta_g_head', 'delta_w_b_out', 'delta_w_o', 'delta_g_ple', 'delta_w_ple_gate', 'delta_w_ple', 'delta_g_final', 'new_m_g_mix', 'new_m_w_in', 'new_m_conv_w', 'new_m_conv_b', 'new_m_w_a_out', 'new_m_b_gates', 'new_m_g_head', 'new_m_w_b_out', 'new_m_w_o', 'new_m_g_ple', 'new_m_w_ple_gate', 'new_m_w_ple', 'new_m_g_final', 'new_v_g_mix', 'new_v_w_in', 'new_v_conv_w', 'new_v_conv_b', 'new_v_w_a_out', 'new_v_b_gates', 'new_v_g_head', 'new_v_w_b_out', 'new_v_w_o', 'new_v_g_ple', 'new_v_w_ple_gate', 'new_v_w_ple', 'new_v_g_final']
TWIN_LEAF_KINDS = {'loss': 'loss', 'grad_x': 'grad_x', 'grad_g_mix': 'grad_w', 'grad_w_in': 'grad_w', 'grad_conv_w': 'grad_w', 'grad_conv_b': 'grad_w', 'grad_w_a_out': 'grad_w', 'grad_b_gates': 'grad_w', 'grad_g_head': 'grad_w', 'grad_w_b_out': 'grad_w', 'grad_w_o': 'grad_w', 'grad_g_ple': 'grad_w', 'grad_w_ple_gate': 'grad_w', 'grad_w_ple': 'grad_w', 'grad_g_final': 'grad_w', 'delta_g_mix': 'delta_w', 'delta_w_in': 'delta_w', 'delta_conv_w': 'delta_w', 'delta_conv_b': 'delta_w', 'delta_w_a_out': 'delta_w', 'delta_b_gates': 'delta_w', 'delta_g_head': 'delta_w', 'delta_w_b_out': 'delta_w', 'delta_w_o': 'delta_w', 'delta_g_ple': 'delta_w', 'delta_w_ple_gate': 'delta_w', 'delta_w_ple': 'delta_w', 'delta_g_final': 'delta_w', 'new_m_g_mix': 'new_m', 'new_m_w_in': 'new_m', 'new_m_conv_w': 'new_m', 'new_m_conv_b': 'new_m', 'new_m_w_a_out': 'new_m', 'new_m_b_gates': 'new_m', 'new_m_g_head': 'new_m', 'new_m_w_b_out': 'new_m', 'new_m_w_o': 'new_m', 'new_m_g_ple': 'new_m', 'new_m_w_ple_gate': 'new_m', 'new_m_w_ple': 'new_m', 'new_m_g_final': 'new_m', 'new_v_g_mix': 'new_v', 'new_v_w_in': 'new_v', 'new_v_conv_w': 'new_v', 'new_v_conv_b': 'new_v', 'new_v_w_a_out': 'new_v', 'new_v_b_gates': 'new_v', 'new_v_g_head': 'new_v', 'new_v_w_b_out': 'new_v', 'new_v_w_o': 'new_v', 'new_v_g_ple': 'new_v', 'new_v_w_ple_gate': 'new_v', 'new_v_w_ple': 'new_v', 'new_v_g_final': 'new_v'}


def _forward(args):
    return _fwd_reference(*[args[k] for k in FWD_PARAMS])


def _output_shape():
    def fwd():
        inp = _fwd_setup_inputs(0)
        return _fwd_reference(*[inp[k] for k in FWD_PARAMS])
    out = _jax.eval_shape(fwd)
    return out.shape, out.dtype

N_MICROBATCH = 1
ADAM_LR = 0.001
ADAM_B1 = 0.9
ADAM_B2 = 0.999
ADAM_EPS = 1e-08
ADAM_WD = 0.01
ADAM_STEP = 10
PER_EXAMPLE_BATCH_AXIS = {'x': 0, 'p': 1, 'loss_target': 0}
SHARED_INPUTS = []
_WEIGHT_DTYPES = {'g_mix': _jnp.float32, 'w_in': _jnp.float32, 'conv_w': _jnp.float32, 'conv_b': _jnp.float32, 'w_a_out': _jnp.float32, 'b_gates': _jnp.float32, 'g_head': _jnp.float32, 'w_b_out': _jnp.float32, 'w_o': _jnp.float32, 'g_ple': _jnp.float32, 'w_ple_gate': _jnp.float32, 'w_ple': _jnp.float32, 'g_final': _jnp.float32}
MOMENT_SCALE = {'g_mix': 2.604455e-01, 'w_in': 6.337722e-02, 'conv_w': 9.942454e-02, 'conv_b': 1.063604e-01, 'w_a_out': 9.863862e-02, 'b_gates': 1.539395e+00, 'g_head': 3.821457e-02, 'w_b_out': 5.311933e-02, 'w_o': 1.118841e-01, 'g_ple': 6.582733e-02, 'w_ple_gate': 6.318045e-02, 'w_ple': 1.625195e-01, 'g_final': 1.280759e+02}


def _to_microbatches(a, axis):
    t = _jnp.moveaxis(a, axis, 0)
    t = t.reshape((N_MICROBATCH, t.shape[0] // N_MICROBATCH) + t.shape[1:])
    return _jnp.moveaxis(t, 1, axis + 1)


def setup_inputs(seed: int = 0) -> dict:
    inp = _fwd_setup_inputs(seed)
    key = _jax.random.fold_in(_jax.random.key(seed), 7919)
    shape, _ = _output_shape()
    out = dict(inp)
    out["loss_target"] = _jax.random.normal(_jax.random.fold_in(key, 0), shape, _jnp.float32)
    for i, name in enumerate(TWIN_WEIGHTS):
        w = inp[name].astype(_jnp.float32)
        if MOMENT_SCALE is None:
            s = _jnp.sqrt(_jnp.mean(_jnp.square(w)) + 1e-30)
        else:
            s = MOMENT_SCALE[name]
        km, kv = _jax.random.split(_jax.random.fold_in(key, i + 1))
        out[name] = w
        out["m_" + name] = s * _jax.random.normal(km, w.shape, _jnp.float32)
        out["v_" + name] = (s * s) * _jax.random.uniform(kv, w.shape, _jnp.float32, 0.5, 1.5)
    if N_MICROBATCH > 1:
        for name, axis in PER_EXAMPLE_BATCH_AXIS.items():
            out[name] = _to_microbatches(out[name], axis)
    return {'x': out['x'], 'p': out['p'], 'g_mix': out['g_mix'], 'w_in': out['w_in'], 'conv_w': out['conv_w'], 'conv_b': out['conv_b'], 'w_a_out': out['w_a_out'], 'b_gates': out['b_gates'], 'g_head': out['g_head'], 'w_b_out': out['w_b_out'], 'w_o': out['w_o'], 'g_ple': out['g_ple'], 'w_ple_gate': out['w_ple_gate'], 'w_ple': out['w_ple'], 'g_final': out['g_final'], 'loss_target': out['loss_target'], 'm_g_mix': out['m_g_mix'], 'm_w_in': out['m_w_in'], 'm_conv_w': out['m_conv_w'], 'm_conv_b': out['m_conv_b'], 'm_w_a_out': out['m_w_a_out'], 'm_b_gates': out['m_b_gates'], 'm_g_head': out['m_g_head'], 'm_w_b_out': out['m_w_b_out'], 'm_w_o': out['m_w_o'], 'm_g_ple': out['m_g_ple'], 'm_w_ple_gate': out['m_w_ple_gate'], 'm_w_ple': out['m_w_ple'], 'm_g_final': out['m_g_final'], 'v_g_mix': out['v_g_mix'], 'v_w_in': out['v_w_in'], 'v_conv_w': out['v_conv_w'], 'v_conv_b': out['v_conv_b'], 'v_w_a_out': out['v_w_a_out'], 'v_b_gates': out['v_b_gates'], 'v_g_head': out['v_g_head'], 'v_w_b_out': out['v_w_b_out'], 'v_w_o': out['v_w_o'], 'v_g_ple': out['v_g_ple'], 'v_w_ple_gate': out['v_w_ple_gate'], 'v_w_ple': out['v_w_ple'], 'v_g_final': out['v_g_final']}


def _loss(weights, diff, rest, loss_target):
    with _jax.named_scope("forward"):
        args = {**rest, TWIN_DIFF_INPUT: diff, **{k: w.astype(_WEIGHT_DTYPES[k]) for k, w in weights.items()}}
        y = _forward(args)
    with _jax.named_scope("loss_head"):
        err = _jnp.square(y.astype(_jnp.float32) - loss_target)
        return 0.5 * _jnp.sum(_jnp.mean(err, axis=-1)) if err.ndim else 0.5 * err


def _adamw(w, g, m, v):
    m = ADAM_B1 * m + (1.0 - ADAM_B1) * g
    v = ADAM_B2 * v + (1.0 - ADAM_B2) * _jnp.square(g)
    m_hat = m / (1.0 - ADAM_B1 ** ADAM_STEP)
    v_hat = v / (1.0 - ADAM_B2 ** ADAM_STEP)
    delta = -ADAM_LR * (m_hat / (_jnp.sqrt(v_hat) + ADAM_EPS) + ADAM_WD * w)
    return delta, m, v


def reference(x, p, g_mix, w_in, conv_w, conv_b, w_a_out, b_gates, g_head, w_b_out, w_o, g_ple, w_ple_gate, w_ple, g_final, loss_target, m_g_mix, m_w_in, m_conv_w, m_conv_b, m_w_a_out, m_b_gates, m_g_head, m_w_b_out, m_w_o, m_g_ple, m_w_ple_gate, m_w_ple, m_g_final, v_g_mix, v_w_in, v_conv_w, v_conv_b, v_w_a_out, v_b_gates, v_g_head, v_w_b_out, v_w_o, v_g_ple, v_w_ple_gate, v_w_ple, v_g_final):
    given = dict(x=x, p=p, g_mix=g_mix, w_in=w_in, conv_w=conv_w, conv_b=conv_b, w_a_out=w_a_out, b_gates=b_gates, g_head=g_head, w_b_out=w_b_out, w_o=w_o, g_ple=g_ple, w_ple_gate=w_ple_gate, w_ple=w_ple, g_final=g_final, loss_target=loss_target, m_g_mix=m_g_mix, m_w_in=m_w_in, m_conv_w=m_conv_w, m_conv_b=m_conv_b, m_w_a_out=m_w_a_out, m_b_gates=m_b_gates, m_g_head=m_g_head, m_w_b_out=m_w_b_out, m_w_o=m_w_o, m_g_ple=m_g_ple, m_w_ple_gate=m_w_ple_gate, m_w_ple=m_w_ple, m_g_final=m_g_final, v_g_mix=v_g_mix, v_w_in=v_w_in, v_conv_w=v_conv_w, v_conv_b=v_conv_b, v_w_a_out=v_w_a_out, v_b_gates=v_b_gates, v_g_head=v_g_head, v_w_b_out=v_w_b_out, v_w_o=v_w_o, v_g_ple=v_g_ple, v_w_ple_gate=v_w_ple_gate, v_w_ple=v_w_ple, v_g_final=v_g_final)
    weights = {n: given[n] for n in TWIN_WEIGHTS}
    shared = {n: given[n] for n in SHARED_INPUTS}
    per_example = {n: given[n] for n in ['x', 'p']}
    grad_fn = _jax.value_and_grad(_loss, argnums=(0, 1))

    def one_microbatch(ex, loss_target):
        ex = dict(ex)
        diff = ex.pop(TWIN_DIFF_INPUT)
        return grad_fn(weights, diff, {**shared, **ex}, loss_target)

    if N_MICROBATCH == 1:
        loss, (grad_w, grad_x) = one_microbatch(per_example, given["loss_target"])
    else:
        def body(carry, xs):
            loss_sum, grad_sum = carry
            l_k, (gw_k, gx_k) = one_microbatch(xs[0], xs[1])
            with _jax.named_scope("update"):
                return (loss_sum + l_k, _jax.tree.map(_jnp.add, grad_sum, gw_k)), gx_k

        init = (_jnp.zeros((), _jnp.float32), _jax.tree.map(_jnp.zeros_like, weights))
        (loss, grad_w), grad_x = _jax.lax.scan(body, init, (per_example, given["loss_target"]))
    with _jax.named_scope("update"):
        delta_w, new_m, new_v = {}, {}, {}
        for n in TWIN_WEIGHTS:
            delta_w[n], new_m[n], new_v[n] = _adamw(weights[n], grad_w[n], given["m_" + n], given["v_" + n])
    return (loss, grad_x, *[grad_w[n] for n in TWIN_WEIGHTS], *[delta_w[n] for n in TWIN_WEIGHTS],
            *[new_m[n] for n in TWIN_WEIGHTS], *[new_v[n] for n in TWIN_WEIGHTS])
```

```python
import functools

import jax
import jax.numpy as jnp
from jax import lax
from jax.experimental import pallas as pl
from jax.experimental.pallas import tpu as pltpu

F32 = jnp.float32
BF = jnp.bfloat16

D_MODEL = 1024
N_HEADS = 4
DK = 256
DV = 512
V_DIM = 2048
PLE_DIM = 256
N_IN = 14344
N_MAIN = 14336
EPS = 1e-6
QK_SCALE = DK ** -0.5
NEG = -1e30
N_DEV = 8
SHARD_W = 1793
WIN_STRIDE = 1792
WIN_W = 1920
ROWS_PACK = 672
_CHUNK = 256
IG_LANE = 120
FG_LANE = 124

ADAM_LR = 0.001
ADAM_B1 = 0.9
ADAM_B2 = 0.999
ADAM_EPS = 1e-08
ADAM_WD = 0.01
ADAM_STEP = 10

VMEM_LIMIT = 56 * 1024 * 1024
MESH = pl.DeviceIdType.MESH


def _cparams(sem):
    return pltpu.CompilerParams(dimension_semantics=sem, vmem_limit_bytes=VMEM_LIMIT)


def _sigmoid(x):
    return 1.0 / (1.0 + jnp.exp(-x))


def _log_sigmoid(x):
    return jnp.minimum(x, 0.0) - jnp.log(1.0 + jnp.exp(-jnp.abs(x)))


def _dot(a, b):
    return jnp.dot(a, b, preferred_element_type=F32)


def _dot_nt(a, b):
    return lax.dot_general(a, b, (((1,), (1,)), ((), ())), preferred_element_type=F32)


def _dot_tn(a, b):
    return lax.dot_general(a, b, (((0,), (0,)), ((), ())), preferred_element_type=F32)


def _mm(a, b, *, form, out_dtype, name, tm=1024, tn=1024, tk=1024, add=None):
    if form == "nn":
        m, kc = a.shape
        n = b.shape[1]
    elif form == "nt":
        m, kc = a.shape
        n = b.shape[0]
    else:
        kc, m = a.shape
        n = b.shape[1]
    tm, tn, tk = min(tm, m), min(tn, n), min(tk, kc)
    assert m % tm == 0 and n % tn == 0 and kc % tk == 0, (name, a.shape, b.shape)
    nk = kc // tk
    if form == "tn":
        a_spec = pl.BlockSpec((tk, tm), lambda i, j, k: (k, i))
    else:
        a_spec = pl.BlockSpec((tm, tk), lambda i, j, k: (i, k))
    if form == "nt":
        b_spec = pl.BlockSpec((tn, tk), lambda i, j, k: (j, k))
    else:
        b_spec = pl.BlockSpec((tk, tn), lambda i, j, k: (k, j))
    o_spec = pl.BlockSpec((tm, tn), lambda i, j, k: (i, j))
    dot = {"nn": _dot, "nt": _dot_nt, "tn": _dot_tn}[form]
    has_add = add is not None

    def body(*refs):
        a_ref, b_ref = refs[0], refs[1]
        add_ref = refs[2] if has_add else None
        o_ref = refs[3] if has_add else refs[2]
        part = dot(a_ref[...].astype(BF), b_ref[...].astype(BF))

        def finish(total):
            if has_add:
                total = total + add_ref[...].astype(F32)
            o_ref[...] = total.astype(out_dtype)

        if nk == 1:
            finish(part)
        else:
            acc = refs[-1]
            k = pl.program_id(2)

            @pl.when(k == 0)
            def _():
                acc[...] = part

            @pl.when(k > 0)
            def _():
                acc[...] += part

            @pl.when(k == nk - 1)
            def _():
                finish(acc[...])

    in_specs = [a_spec, b_spec] + ([o_spec] if has_add else [])
    args = (a, b) + ((add,) if has_add else ())
    return pl.pallas_call(
        body, name=name, grid=(m // tm, n // tn, nk),
        in_specs=in_specs, out_specs=o_spec,
        out_shape=jax.ShapeDtypeStruct((m, n), out_dtype),
        scratch_shapes=[pltpu.VMEM((tm, tn), F32)] if nk > 1 else [],
        compiler_params=_cparams(("parallel", "parallel", "arbitrary")),
    )(*args)


def _rms_fwd(x, g, *, name, tm=512):
    t, d = x.shape
    tm = min(tm, t)

    def body(x_ref, g_ref, hn_ref, hnt_ref):
        xv = x_ref[...]
        r = lax.rsqrt(jnp.mean(xv * xv, axis=1, keepdims=True) + EPS)
        hn = xv * r * g_ref[...]
        hn_ref[...] = hn.astype(BF)
        hnt_ref[...] = hn.T.astype(BF)

    return pl.pallas_call(
        body, name=name, grid=(t // tm,),
        in_specs=[pl.BlockSpec((tm, d), lambda i: (i, 0)), pl.BlockSpec((1, d), lambda i: (0, 0))],
        out_specs=[pl.BlockSpec((tm, d), lambda i: (i, 0)), pl.BlockSpec((d, tm), lambda i: (0, i))],
        out_shape=[jax.ShapeDtypeStruct((t, d), BF), jax.ShapeDtypeStruct((d, t), BF)],
        compiler_params=_cparams(("parallel",)),
    )(x, g)


def _rms_bwd(x, g, dhn, dres, *, name, tm=512):
    t, d = x.shape
    tm = min(tm, t)

    def body(x_ref, g_ref, dhn_ref, dres_ref, dx_ref, dxb_ref, dg_ref):
        i = pl.program_id(0)
        xv = x_ref[...]
        r = lax.rsqrt(jnp.mean(xv * xv, axis=1, keepdims=True) + EPS)
        xh = xv * r
        dh = dhn_ref[...]
        dxh = dh * g_ref[...]
        dx = dres_ref[...] + r * (dxh - xh * jnp.mean(dxh * xh, axis=1, keepdims=True))
        dx_ref[...] = dx
        dxb_ref[...] = dx.astype(BF)
        part = jnp.sum(dh * xh, axis=0, keepdims=True)

        @pl.when(i == 0)
        def _():
            dg_ref[...] = jnp.zeros_like(dg_ref)

        dg_ref[0:1, :] += part

    row = pl.BlockSpec((tm, d), lambda i: (i, 0))
    return pl.pallas_call(
        body, name=name, grid=(t // tm,),
        in_specs=[row, pl.BlockSpec((1, d), lambda i: (0, 0)), row, row],
        out_specs=[row, row, pl.BlockSpec((8, d), lambda i: (0, 0))],
        out_shape=[jax.ShapeDtypeStruct((t, d), F32), jax.ShapeDtypeStruct((t, d), BF),
                   jax.ShapeDtypeStruct((8, d), F32)],
        compiler_params=_cparams(("arbitrary",)),
    )(x, g, dhn, dres)


HALO = 16
XA_BLK, BA_BLK, CA_BLK, ZA_BLK = 8, 9, 10, 11


def _shift_down(u, prev, n):
    tm = u.shape[0]
    rolled = pltpu.roll(u, n, 0)
    row = lax.broadcasted_iota(jnp.int32, u.shape, 0)
    out = rolled
    for j in range(n):
        out = jnp.where(row == j, prev[HALO - n + j:HALO - n + j + 1, :], out)
    return out


def _shift_up(u, nxt, n):
    tm = u.shape[0]
    rolled = pltpu.roll(u, tm - n, 0)
    row = lax.broadcasted_iota(jnp.int32, u.shape, 0)
    out = rolled
    for j in range(n):
        out = jnp.where(row == tm - n + j, nxt[j:j + 1, :], out)
    return out


def _branch_a_fwd(proj, conv_w8, conv_b, *, tm=512):
    t = proj.shape[0]
    d = D_MODEL
    tm = min(tm, t)
    hb = tm // HALO

    def body(xa_ref, ba_ref, ca_ref, za_ref, xap_ref, cap_ref, w_ref, b_ref, o_ref):
        i = pl.program_id(0)
        u = ca_ref[...].astype(F32) * xa_ref[...].astype(F32)
        up = cap_ref[...].astype(F32) * xap_ref[...].astype(F32)
        up = jnp.where(i == 0, 0.0, up)
        u1 = _shift_down(u, up, 1)
        u2 = _shift_down(u, up, 2)
        cv = w_ref[0:1, :] * u2 + w_ref[1:2, :] * u1 + w_ref[2:3, :] * u + b_ref[...]
        za = za_ref[...].astype(F32)
        o_ref[...] = (ba_ref[...].astype(F32) * cv * (za * _sigmoid(za))).astype(BF)

    def col(blk):
        return pl.BlockSpec((tm, d), lambda i: (i, blk))

    def prev(blk):
        return pl.BlockSpec((HALO, d), lambda i: (jnp.maximum(i * hb - 1, 0), blk))

    return pl.pallas_call(
        body, name="branch_a_fwd", grid=(t // tm,),
        in_specs=[col(XA_BLK), col(BA_BLK), col(CA_BLK), col(ZA_BLK), prev(XA_BLK), prev(CA_BLK),
                  pl.BlockSpec((8, d), lambda i: (0, 0)), pl.BlockSpec((1, d), lambda i: (0, 0))],
        out_specs=pl.BlockSpec((tm, d), lambda i: (i, 0)),
        out_shape=jax.ShapeDtypeStruct((t, d), BF),
        compiler_params=_cparams(("parallel",)),
    )(proj, proj, proj, proj, proj, proj, conv_w8, conv_b)


def _branch_a_bwd(dproj, proj, dya_pre, conv_w8, conv_b, *, tm=512):
    t = proj.shape[0]
    d = D_MODEL
    tm = min(tm, t)
    hb = tm // HALO
    nt = t // tm

    def body(dp_ref, xa_ref, ba_ref, ca_ref, za_ref, xap_ref, cap_ref, dy_ref, ban_ref, zan_ref, dyn_ref,
             w_ref, b_ref, o_ref, dc_ref):
        del dp_ref
        i = pl.program_id(0)
        xa = xa_ref[...].astype(F32)
        ca = ca_ref[...].astype(F32)
        ba = ba_ref[...].astype(F32)
        za = za_ref[...].astype(F32)
        u = ca * xa
        up = cap_ref[...].astype(F32) * xap_ref[...].astype(F32)
        up = jnp.where(i == 0, 0.0, up)
        u1 = _shift_down(u, up, 1)
        u2 = _shift_down(u, up, 2)
        w0, w1, w2 = w_ref[0:1, :], w_ref[1:2, :], w_ref[2:3, :]
        cv = w0 * u2 + w1 * u1 + w2 * u + b_ref[...]
        sg = _sigmoid(za)
        sz = za * sg
        dy = dy_ref[...].astype(F32)
        dcv = dy * ba * sz
        zan = zan_ref[...].astype(F32)
        dcvn = dyn_ref[...].astype(F32) * ban_ref[...].astype(F32) * (zan * _sigmoid(zan))
        dcvn = jnp.where(i == nt - 1, 0.0, dcvn)
        du = w2 * dcv + w1 * _shift_up(dcv, dcvn, 1) + w0 * _shift_up(dcv, dcvn, 2)
        o_ref[:, 0:d] = (du * ca).astype(BF)
        o_ref[:, d:2 * d] = (dy * cv * sz).astype(BF)
        o_ref[:, 2 * d:3 * d] = (du * xa).astype(BF)
        o_ref[:, 3 * d:4 * d] = (dy * ba * cv * sg * (1.0 + za * (1.0 - sg))).astype(BF)

        @pl.when(i == 0)
        def _():
            dc_ref[...] = jnp.zeros_like(dc_ref)

        dc_ref[0:1, :] += jnp.sum(dcv * u2, axis=0, keepdims=True)
        dc_ref[1:2, :] += jnp.sum(dcv * u1, axis=0, keepdims=True)
        dc_ref[2:3, :] += jnp.sum(dcv * u, axis=0, keepdims=True)
        dc_ref[3:4, :] += jnp.sum(dcv, axis=0, keepdims=True)

    def col(blk):
        return pl.BlockSpec((tm, d), lambda i: (i, blk))

    def prev(blk):
        return pl.BlockSpec((HALO, d), lambda i: (jnp.maximum(i * hb - 1, 0), blk))

    def nxt(blk):
        return pl.BlockSpec((HALO, d), lambda i: (jnp.minimum((i + 1) * hb, t // HALO - 1), blk))

    return pl.pallas_call(
        body, name="branch_a_bwd", grid=(nt,),
        in_specs=[pl.BlockSpec(memory_space=pl.ANY),
                  col(XA_BLK), col(BA_BLK), col(CA_BLK), col(ZA_BLK), prev(XA_BLK), prev(CA_BLK),
                  pl.BlockSpec((tm, d), lambda i: (i, 0)), nxt(BA_BLK), nxt(ZA_BLK),
                  pl.BlockSpec((HALO, d), lambda i: (jnp.minimum((i + 1) * hb, t // HALO - 1), 0)),
                  pl.BlockSpec((8, d), lambda i: (0, 0)), pl.BlockSpec((1, d), lambda i: (0, 0))],
        out_specs=[pl.BlockSpec((tm, 4 * d), lambda i: (i, 2)), pl.BlockSpec((8, d), lambda i: (0, 0))],
        out_shape=[jax.ShapeDtypeStruct(dproj.shape, BF), jax.ShapeDtypeStruct((8, d), F32)],
        input_output_aliases={0: 0},
        compiler_params=_cparams(("arbitrary",)),
    )(dproj, proj, proj, proj, proj, proj, proj, dya_pre, proj, proj, dya_pre, conv_w8, conv_b)


def _gate_vectors(gc, gt, h, lane_i, sub_i):
    ig_c = jnp.sum(jnp.where(lane_i == IG_LANE + h, gc, 0.0), axis=1, keepdims=True)
    fg_c = jnp.sum(jnp.where(lane_i == FG_LANE + h, gc, 0.0), axis=1, keepdims=True)
    ig_r = jnp.sum(jnp.where(sub_i == IG_LANE + h, gt, 0.0), axis=0, keepdims=True)
    fg_r = jnp.sum(jnp.where(sub_i == FG_LANE + h, gt, 0.0), axis=0, keepdims=True)
    return ig_c, fg_c, ig_r, fg_r


def _chunk_common(q, k, ig_c, fg_c, ig_r, fg_r, m_prev, n_prev, row, col):
    lf_c = _log_sigmoid(fg_c)
    lf_r = _log_sigmoid(fg_r)
    causal = col <= row
    b_c = jnp.sum(jnp.where(causal, lf_r, 0.0), axis=1, keepdims=True)
    b_r = jnp.sum(jnp.where(row <= col, lf_c, 0.0), axis=0, keepdims=True)
    dmat = jnp.where(causal, b_c - b_r + ig_r, NEG)
    a = b_c + m_prev
    m_row = jnp.maximum(a, jnp.max(dmat, axis=1, keepdims=True))
    est = jnp.exp(dmat - m_row)
    s = _dot_nt(q, k) * QK_SCALE * est
    inter = jnp.exp(a - m_row)
    den = jnp.sum(s, axis=1, keepdims=True) + inter * QK_SCALE * jnp.sum(
        q.astype(F32) * n_prev, axis=1, keepdims=True)
    expm = jnp.exp(-m_row)
    mx = jnp.maximum(jnp.abs(den), expm)
    b_last = jnp.sum(lf_r, axis=1, keepdims=True)
    g_r = b_last - b_r + ig_r
    g_c = b_last - b_c + ig_c
    m_new = jnp.maximum(b_last + m_prev, jnp.max(g_r, axis=1, keepdims=True))
    w_c = jnp.exp(g_c - m_new)
    decay = jnp.exp(b_last + m_prev - m_new)
    return est, s, inter, den, expm, mx, m_new, w_c, decay


def _mlstm_fwd(proj, gates, gate_bias, g_head):
    t = proj.shape[0]
    lc = min(_CHUNK, t)
    nc = t // lc

    def body(q_ref, k_ref, v_ref, o_ref, z_ref, g_ref, gb_ref, gh_ref,
             yb_ref, hr_ref, cs_ref, aux_ref, c_scr, nm_scr):
        c = pl.program_id(0)

        @pl.when(c == 0)
        def _():
            c_scr[...] = jnp.zeros_like(c_scr)
            nm_scr[...] = jnp.zeros_like(nm_scr)
            nm_scr[:, 1:2, :] = jnp.full((N_HEADS, 1, DK), NEG, F32)

        row = lax.broadcasted_iota(jnp.int32, (lc, lc), 0)
        col = lax.broadcasted_iota(jnp.int32, (lc, lc), 1)
        gc = g_ref[...] + gb_ref[...]
        gt = gc.T
        lane_i = lax.broadcasted_iota(jnp.int32, gc.shape, 1)
        sub_i = lax.broadcasted_iota(jnp.int32, gt.shape, 0)
        for h in range(N_HEADS):
            ks = slice(h * DK, (h + 1) * DK)
            vs = slice(h * DV, (h + 1) * DV)
            q = q_ref[:, ks]
            k = k_ref[:, ks]
            v = v_ref[:, vs]
            ig_c, fg_c, ig_r, fg_r = _gate_vectors(gc, gt, h, lane_i, sub_i)
            n_prev = nm_scr[h, 0:1, :]
            m_prev = nm_scr[h, 1:2, 0:1]
            est, s, inter, den, expm, mx, m_new, w_c, decay = _chunk_common(
                q, k, ig_c, fg_c, ig_r, fg_r, m_prev, n_prev, row, col)
            c_prev = c_scr[h]
            c_prev_b = c_prev.astype(BF)
            num = _dot(s.astype(BF), v) + (inter * QK_SCALE) * _dot(q, c_prev_b)
            hh = num / mx
            r = lax.rsqrt(jnp.mean(hh * hh, axis=1, keepdims=True) + EPS)
            hbn = hh * r * gh_ref[:, vs]
            o = o_ref[:, vs].astype(F32)
            z = z_ref[:, vs].astype(F32)
            yb_ref[:, vs] = (_sigmoid(o) * hbn * (z * _sigmoid(z))).astype(BF)
            hr_ref[:, vs] = hh.astype(BF)
            cs_ref[0, h] = c_prev_b
            aux_ref[0, h] = nm_scr[h]
            kw = k.astype(F32) * w_c
            c_scr[h] = decay * c_prev + _dot_tn(kw.astype(BF), v)
            nm_scr[h, 0:1, :] = decay * n_prev + jnp.sum(kw, axis=0, keepdims=True)
            nm_scr[h, 1:2, :] = jnp.broadcast_to(m_new, (1, DK))

    return pl.pallas_call(
        body, name="mlstm_fwd", grid=(nc,),
        in_specs=[pl.BlockSpec((lc, 1024), lambda c: (c, 0)),
                  pl.BlockSpec((lc, 1024), lambda c: (c, 1)),
                  pl.BlockSpec((lc, 2048), lambda c: (c, 1)),
                  pl.BlockSpec((lc, 2048), lambda c: (c, 2)),
                  pl.BlockSpec((lc, 2048), lambda c: (c, 3)),
                  pl.BlockSpec((lc, 128), lambda c: (c, 0)),
                  pl.BlockSpec((1, 128), lambda c: (0, 0)),
                  pl.BlockSpec((1, V_DIM), lambda c: (0, 0))],
        out_specs=[pl.BlockSpec((lc, V_DIM), lambda c: (c, 0)),
                   pl.BlockSpec((lc, V_DIM), lambda c: (c, 0)),
                   pl.BlockSpec((1, N_HEADS, DK, DV), lambda c: (c, 0, 0, 0)),
                   pl.BlockSpec((1, N_HEADS, 8, DK), lambda c: (c, 0, 0, 0))],
        out_shape=[jax.ShapeDtypeStruct((t, V_DIM), BF), jax.ShapeDtypeStruct((t, V_DIM), BF),
                   jax.ShapeDtypeStruct((nc, N_HEADS, DK, DV), BF),
                   jax.ShapeDtypeStruct((nc, N_HEADS, 8, DK), F32)],
        scratch_shapes=[pltpu.VMEM((N_HEADS, DK, DV), F32), pltpu.VMEM((N_HEADS, 8, DK), F32)],
        compiler_params=_cparams(("arbitrary",)),
    )(proj, proj, proj, proj, proj, gates, gate_bias, g_head)


def _mlstm_bwd(dproj, proj, gates, gate_bias, g_head, h_raw, c_states, aux, dyb_pre):
    t = proj.shape[0]
    lc = min(_CHUNK, t)
    nc = t // lc

    def body(dpin_ref, q_ref, k_ref, v_ref, o_ref, z_ref, g_ref, gb_ref, gh_ref, hr_ref, cs_ref, aux_ref,
             dy_ref, dp_ref, dg_ref, dbias_ref, dgh_ref, dc_scr, dn_scr):
        del dpin_ref
        step = pl.program_id(0)

        @pl.when(step == 0)
        def _():
            dc_scr[...] = jnp.zeros_like(dc_scr)
            dn_scr[...] = jnp.zeros_like(dn_scr)
            dbias_ref[...] = jnp.zeros_like(dbias_ref)
            dgh_ref[...] = jnp.zeros_like(dgh_ref)

        row = lax.broadcasted_iota(jnp.int32, (lc, lc), 0)
        col = lax.broadcasted_iota(jnp.int32, (lc, lc), 1)
        eye = row == col
        gc = g_ref[...] + gb_ref[...]
        gt = gc.T
        lane_i = lax.broadcasted_iota(jnp.int32, gc.shape, 1)
        sub_i = lax.broadcasted_iota(jnp.int32, gt.shape, 0)
        dgates = jnp.zeros(gc.shape, F32)
        for h in range(N_HEADS):
            ks = slice(h * DK, (h + 1) * DK)
            vs = slice(h * DV, (h + 1) * DV)
            q = q_ref[:, ks]
            k = k_ref[:, ks]
            v = v_ref[:, vs]
            ig_c, fg_c, ig_r, fg_r = _gate_vectors(gc, gt, h, lane_i, sub_i)
            n_prev = aux_ref[0, h, 0:1, :]
            m_prev = aux_ref[0, h, 1:2, 0:1]
            c_prev_b = cs_ref[0, h]
            est, s, inter, den, expm, mx, m_new, w_c, decay = _chunk_common(
                q, k, ig_c, fg_c, ig_r, fg_r, m_prev, n_prev, row, col)
            hb = hr_ref[:, vs].astype(F32)
            dyp = dy_ref[:, vs].astype(F32)
            o = o_ref[:, vs].astype(F32)
            z = z_ref[:, vs].astype(F32)
            so = _sigmoid(o)
            sgz = _sigmoid(z)
            sz = z * sgz
            r = lax.rsqrt(jnp.mean(hb * hb, axis=1, keepdims=True) + EPS)
            xh = hb * r
            gh = gh_ref[:, vs]
            hbn = xh * gh
            d_o = dyp * hbn * sz * so * (1.0 - so)
            d_z = dyp * so * hbn * sgz * (1.0 + z * (1.0 - sgz))
            dhbn = dyp * so * sz
            dgh_ref[0:1, vs] += jnp.sum(dhbn * xh, axis=0, keepdims=True)
            dxh = dhbn * gh
            dh = r * (dxh - xh * jnp.mean(dxh * xh, axis=1, keepdims=True))
            dnm = dh / mx
            hd = jnp.sum(dh * hb, axis=1, keepdims=True)
            cond = jnp.abs(den) > expm
            dden = jnp.where(cond, -hd / mx * jnp.sign(den), 0.0)
            dnm_b = dnm.astype(BF)
            p = _dot_nt(dnm_b, v) + dden
            dqk = (p * est * QK_SCALE).astype(BF)
            dq_inter = (inter * QK_SCALE) * (_dot_nt(dnm_b, c_prev_b) + dden * n_prev)
            dq = _dot(dqk, k) + dq_inter
            dc_new = dc_scr[h]
            dc_new_b = dc_new.astype(BF)
            dn_new = dn_scr[h, 0:1, :]
            kf = k.astype(F32)
            dk_state = w_c * (_dot_nt(v, dc_new_b) + dn_new)
            dk = _dot_tn(dqk, q) + dk_state
            dv = _dot_tn(s.astype(BF), dnm_b) + w_c * _dot(k, dc_new_b)
            dv1_r = jnp.sum(s * dden, axis=0, keepdims=True)
            dv1_c = (jnp.sum(jnp.where(eye, dv1_r, 0.0), axis=1, keepdims=True)
                     + w_c * jnp.sum(kf * dn_new, axis=1, keepdims=True))
            dli_c = jnp.sum(v.astype(F32) * dv, axis=1, keepdims=True) + dv1_c
            hmat = _dot((p * s).astype(BF), (row < col).astype(BF))
            from_prev_c = jnp.sum(q.astype(F32) * dq_inter, axis=1, keepdims=True)
            to_next_c = jnp.sum(kf * dk_state, axis=1, keepdims=True)
            through = decay * (
                jnp.sum(jnp.sum(dc_new * c_prev_b.astype(F32), axis=1, keepdims=True), axis=0, keepdims=True)
                + jnp.sum(dn_new * n_prev, axis=1, keepdims=True))
            dlf_r = through + jnp.sum(jnp.where(row >= col, hmat + from_prev_c, to_next_c), axis=0, keepdims=True)
            dlf_c = jnp.sum(jnp.where(eye, dlf_r, 0.0), axis=1, keepdims=True)
            dfg_c = dlf_c * _sigmoid(-fg_c)
            dgates = dgates + jnp.where(lane_i == IG_LANE + h, dli_c, 0.0) + jnp.where(
                lane_i == FG_LANE + h, dfg_c, 0.0)
            qi = q.astype(F32) * (inter * QK_SCALE)
            dc_scr[h] = decay * dc_new + _dot_tn(qi.astype(BF), dnm_b)
            dn_scr[h, 0:1, :] = decay * dn_new + jnp.sum(qi * dden, axis=0, keepdims=True)
            dp_ref[:, h * DK:(h + 1) * DK] = dq.astype(BF)
            dp_ref[:, 1024 + h * DK:1024 + (h + 1) * DK] = dk.astype(BF)
            dp_ref[:, 2048 + h * DV:2048 + (h + 1) * DV] = dv.astype(BF)
            dp_ref[:, 4096 + h * DV:4096 + (h + 1) * DV] = d_o.astype(BF)
            dp_ref[:, 6144 + h * DV:6144 + (h + 1) * DV] = d_z.astype(BF)
        dg_ref[...] = dgates
        dbias_ref[0:1, :] += jnp.sum(dgates, axis=0, keepdims=True)

    def rev(c):
        return nc - 1 - c

    return pl.pallas_call(
        body, name="mlstm_bwd", grid=(nc,),
        input_output_aliases={0: 0},
        in_specs=[pl.BlockSpec(memory_space=pl.ANY),
                  pl.BlockSpec((lc, 1024), lambda c: (rev(c), 0)),
                  pl.BlockSpec((lc, 1024), lambda c: (rev(c), 1)),
                  pl.BlockSpec((lc, 2048), lambda c: (rev(c), 1)),
                  pl.BlockSpec((lc, 2048), lambda c: (rev(c), 2)),
                  pl.BlockSpec((lc, 2048), lambda c: (rev(c), 3)),
                  pl.BlockSpec((lc, 128), lambda c: (rev(c), 0)),
                  pl.BlockSpec((1, 128), lambda c: (0, 0)),
                  pl.BlockSpec((1, V_DIM), lambda c: (0, 0)),
                  pl.BlockSpec((lc, V_DIM), lambda c: (rev(c), 0)),
                  pl.BlockSpec((1, N_HEADS, DK, DV), lambda c: (rev(c), 0, 0, 0)),
                  pl.BlockSpec((1, N_HEADS, 8, DK), lambda c: (rev(c), 0, 0, 0)),
                  pl.BlockSpec((lc, V_DIM), lambda c: (rev(c), 0))],
        out_specs=[pl.BlockSpec((lc, 8192), lambda c: (rev(c), 0)),
                   pl.BlockSpec((lc, 128), lambda c: (rev(c), 0)),
                   pl.BlockSpec((8, 128), lambda c: (0, 0)),
                   pl.BlockSpec((8, V_DIM), lambda c: (0, 0))],
        out_shape=[jax.ShapeDtypeStruct((t, N_MAIN), BF), jax.ShapeDtypeStruct((t, 128), F32),
                   jax.ShapeDtypeStruct((8, 128), F32), jax.ShapeDtypeStruct((8, V_DIM), F32)],
        scratch_shapes=[pltpu.VMEM((N_HEADS, DK, DV), F32), pltpu.VMEM((N_HEADS, 8, DK), F32)],
        compiler_params=_cparams(("arbitrary",)),
    )(dproj, proj, proj, proj, proj, proj, gates, gate_bias, g_head, h_raw, c_states, aux, dyb_pre)


GA_BLK, GB_BLK = 12, 13


def _merge_fwd(proj, ya, yb, *, tm=512):
    t = ya.shape[0]
    d = D_MODEL
    tm = min(tm, t)

    def body(ga_ref, gb_ref, ya_ref, yb_ref, o_ref):
        o_ref[...] = (_sigmoid(ga_ref[...].astype(F32)) * ya_ref[...].astype(F32)
                      + _sigmoid(gb_ref[...].astype(F32)) * yb_ref[...].astype(F32)).astype(BF)

    row = pl.BlockSpec((tm, d), lambda i: (i, 0))
    return pl.pallas_call(
        body, name="merge_fwd", grid=(t // tm,),
        in_specs=[pl.BlockSpec((tm, d), lambda i: (i, GA_BLK)), pl.BlockSpec((tm, d), lambda i: (i, GB_BLK)),
                  row, row],
        out_specs=row, out_shape=jax.ShapeDtypeStruct((t, d), BF),
        compiler_params=_cparams(("parallel",)),
    )(proj, proj, ya, yb)


def _merge_bwd(proj, ya, yb, dmerged, *, tm=512):
    t = ya.shape[0]
    d = D_MODEL
    tm = min(tm, t)

    def body(ga_ref, gb_ref, ya_ref, yb_ref, dm_ref, o_ref, dya_ref, dyb_ref):
        dm = dm_ref[...].astype(F32)
        sa = _sigmoid(ga_ref[...].astype(F32))
        sb = _sigmoid(gb_ref[...].astype(F32))
        dya_ref[...] = (dm * sa).astype(BF)
        dyb_ref[...] = (dm * sb).astype(BF)
        o_ref[:, 0:d] = (dm * ya_ref[...].astype(F32) * sa * (1.0 - sa)).astype(BF)
        o_ref[:, d:2 * d] = (dm * yb_ref[...].astype(F32) * sb * (1.0 - sb)).astype(BF)

    row = pl.BlockSpec((tm, d), lambda i: (i, 0))
    return pl.pallas_call(
        body, name="merge_bwd", grid=(t // tm,),
        in_specs=[pl.BlockSpec((tm, d), lambda i: (i, GA_BLK)), pl.BlockSpec((tm, d), lambda i: (i, GB_BLK)),
                  row, row, row],
        out_specs=[pl.BlockSpec((tm, 2 * d), lambda i: (i, 6)), row, row],
        out_shape=[jax.ShapeDtypeStruct((t, N_MAIN), BF), jax.ShapeDtypeStruct((t, d), BF),
                   jax.ShapeDtypeStruct((t, d), BF)],
        compiler_params=_cparams(("parallel",)),
    )(proj, proj, ya, yb, dmerged)


def _final(x1, gpre, pe, target, g_final, *, tm=512):
    t, d = x1.shape
    tm = min(tm, t)

    def body(x_ref, gp_ref, pe_ref, tg_ref, g_ref, dx_ref, dgp_ref, dpe_ref, sm_ref):
        i = pl.program_id(0)
        gate = _sigmoid(gp_ref[...].astype(F32))
        pe_v = pe_ref[...].astype(F32)
        x2 = x_ref[...] + gate * pe_v
        r = lax.rsqrt(jnp.mean(x2 * x2, axis=1, keepdims=True) + EPS)
        xh = x2 * r
        g = g_ref[...]
        err = xh * g - tg_ref[...]
        dy = err * (1.0 / d)
        dxh = dy * g
        dx2 = r * (dxh - xh * jnp.mean(dxh * xh, axis=1, keepdims=True))
        dx_ref[...] = dx2
        dgp_ref[...] = (dx2 * pe_v * gate * (1.0 - gate)).astype(BF)
        dpe_ref[...] = (dx2 * gate).astype(BF)

        @pl.when(i == 0)
        def _():
            sm_ref[...] = jnp.zeros_like(sm_ref)

        sm_ref[0:1, :] += (0.5 / d) * jnp.sum(err * err, axis=0, keepdims=True)
        sm_ref[1:2, :] += jnp.sum(dy * xh, axis=0, keepdims=True)

    row = pl.BlockSpec((tm, d), lambda i: (i, 0))
    return pl.pallas_call(
        body, name="final_loss", grid=(t // tm,),
        in_specs=[row, row, row, row, pl.BlockSpec((1, d), lambda i: (0, 0))],
        out_specs=[row, row, row, pl.BlockSpec((8, d), lambda i: (0, 0))],
        out_shape=[jax.ShapeDtypeStruct((t, d), F32), jax.ShapeDtypeStruct((t, d), BF),
                   jax.ShapeDtypeStruct((t, d), BF), jax.ShapeDtypeStruct((8, d), F32)],
        compiler_params=_cparams(("arbitrary",)),
    )(x1, gpre, pe, target, g_final)


def _position():
    x, y, c = lax.axis_index("x"), lax.axis_index("y"), lax.axis_index("c")
    return x, y, c


def _all_gather(srcs):
    nb = len(srcs)
    any_spec = pl.BlockSpec(memory_space=pl.ANY)

    def body(*refs):
        src = refs[:nb]
        dst = refs[nb:2 * nb]
        send_sems, recv_sems, local_sems = refs[2 * nb:]
        x, y, c = _position()
        me, sibling = (x, y, c), (x, y, 1 - c)
        chips = [(1 - x, y), (x, 1 - y), (1 - x, 1 - y)]

        def slot(b, px, py, pc):
            return dst[b].at[4 * px + 2 * py + pc]

        def copy(k, b, block, to, from_src=False):
            return pltpu.make_async_remote_copy(
                src_ref=src[b] if from_src else slot(b, *block), dst_ref=slot(b, *block),
                send_sem=send_sems.at[b, k], recv_sem=recv_sems.at[b, k],
                device_id=to, device_id_type=MESH)

        mine = [pltpu.make_async_copy(src[b], slot(b, *me), local_sems.at[b]) for b in range(nb)]
        for cp in mine:
            cp.start()
        first = [copy(0, b, me, sibling, True) for b in range(nb)]
        first += [copy(1 + j, b, me, (*chip, c), True) for j, chip in enumerate(chips) for b in range(nb)]
        for cp in first:
            cp.start()
        passed = []
        for j, chip in enumerate(chips):
            for b in range(nb):
                copy(1 + j, b, (*chip, c), me).wait_recv()
                fwd = copy(4 + j, b, (*chip, c), sibling)
                fwd.start()
                passed.append(fwd)
        for b in range(nb):
            copy(0, b, sibling, me).wait_recv()
        for j, chip in enumerate(chips):
            for b in range(nb):
                copy(4 + j, b, (*chip, 1 - c), me).wait_recv()
        for cp in first + passed:
            cp.wait_send()
        for cp in mine:
            cp.wait()

    return pl.pallas_call(
        body, name="weights_all_gather",
        in_specs=[any_spec] * nb, out_specs=[any_spec] * nb,
        out_shape=[jax.ShapeDtypeStruct((N_DEV,) + s.shape, s.dtype) for s in srcs],
        scratch_shapes=[pltpu.SemaphoreType.DMA((nb, 7)), pltpu.SemaphoreType.DMA((nb, 7)),
                        pltpu.SemaphoreType.DMA((nb,))],
    )(*srcs)


def _exchange(srcs):
    nb = len(srcs)
    any_spec = pl.BlockSpec(memory_space=pl.ANY)

    def body(*refs):
        src = refs[:nb]
        dst = refs[nb:2 * nb]
        send_sems, recv_sems, local_sems = refs[2 * nb:]
        x, y, c = _position()
        me_lin = 4 * x + 2 * y + c
        mine = [pltpu.make_async_copy(src[b].at[me_lin], dst[b].at[me_lin], local_sems.at[b]) for b in range(nb)]
        for cp in mine:
            cp.start()
        copies = []
        for f in range(1, N_DEV):
            fx, fy, fc = (f >> 2) & 1, (f >> 1) & 1, f & 1
            px = (1 - x) if fx else x
            py = (1 - y) if fy else y
            pc = (1 - c) if fc else c
            peer_lin = 4 * px + 2 * py + pc
            for b in range(nb):
                copies.append((pltpu.make_async_remote_copy(
                    src_ref=src[b].at[peer_lin], dst_ref=dst[b].at[me_lin],
                    send_sem=send_sems.at[b, f - 1], recv_sem=recv_sems.at[b, f - 1],
                    device_id=(px, py, pc), device_id_type=MESH),
                    pltpu.make_async_remote_copy(
                    src_ref=src[b].at[peer_lin], dst_ref=dst[b].at[peer_lin],
                    send_sem=send_sems.at[b, f - 1], recv_sem=recv_sems.at[b, f - 1],
                    device_id=(px, py, pc), device_id_type=MESH)))
        for send, _ in copies:
            send.start()
        for _, recv in copies:
            recv.wait_recv()
        for send, _ in copies:
            send.wait_send()
        for cp in mine:
            cp.wait()

    return pl.pallas_call(
        body, name="grads_exchange",
        in_specs=[any_spec] * nb, out_specs=[any_spec] * nb,
        out_shape=[jax.ShapeDtypeStruct(s.shape, s.dtype) for s in srcs],
        scratch_shapes=[pltpu.SemaphoreType.DMA((nb, 7)), pltpu.SemaphoreType.DMA((nb, 7)),
                        pltpu.SemaphoreType.DMA((nb,))],
    )(*srcs)


def _sum_slots(recv, *, name, tr):
    _, r, cdim = recv.shape
    tr = min(tr, r)

    def body(r_ref, o_ref):
        total = r_ref[0].astype(F32)
        for s in range(1, N_DEV):
            total = total + r_ref[s].astype(F32)
        o_ref[...] = total

    return pl.pallas_call(
        body, name=name, grid=(r // tr,),
        in_specs=[pl.BlockSpec((N_DEV, tr, cdim), lambda i: (0, i, 0))],
        out_specs=pl.BlockSpec((tr, cdim), lambda i: (i, 0)),
        out_shape=jax.ShapeDtypeStruct((r, cdim), F32),
        compiler_params=_cparams(("parallel",)),
    )(recv)


def _adamw(w, g, m, v, *, name):
    r, cdim = w.shape
    tr = 128 if r % 128 == 0 else r
    c1 = 1.0 - ADAM_B1 ** ADAM_STEP
    c2 = 1.0 - ADAM_B2 ** ADAM_STEP

    def body(w_ref, g_ref, m_ref, v_ref, d_ref, mo_ref, vo_ref):
        gv = g_ref[...]
        mn = ADAM_B1 * m_ref[...] + (1.0 - ADAM_B1) * gv
        vn = ADAM_B2 * v_ref[...] + (1.0 - ADAM_B2) * (gv * gv)
        d_ref[...] = -ADAM_LR * ((mn / c1) / (jnp.sqrt(vn / c2) + ADAM_EPS) + ADAM_WD * w_ref[...])
        mo_ref[...] = mn
        vo_ref[...] = vn

    blk = pl.BlockSpec((tr, cdim), lambda i: (i, 0))
    shp = jax.ShapeDtypeStruct((r, cdim), F32)
    return pl.pallas_call(
        body, name=name, grid=(r // tr,),
        in_specs=[blk] * 4, out_specs=[blk] * 3, out_shape=[shp] * 3,
        compiler_params=_cparams(("parallel",)),
    )(w, g, m, v)


def kernel(x, p, g_mix, w_in, conv_w, conv_b, w_a_out, b_gates, g_head, w_b_out, w_o, g_ple, w_ple_gate, w_ple, g_final, loss_target, m_g_mix, m_w_in, m_conv_w, m_conv_b, m_w_a_out, m_b_gates, m_g_head, m_w_b_out, m_w_o, m_g_ple, m_w_ple_gate, m_w_ple, m_g_final, v_g_mix, v_w_in, v_conv_w, v_conv_b, v_w_a_out, v_b_gates, v_g_head, v_w_b_out, v_w_o, v_g_ple, v_w_ple_gate, v_w_ple, v_g_final):
    d = D_MODEL
    xi, yi, ci = _position()
    me = 4 * xi + 2 * yi + ci
    t = x.shape[1]
    x2d = x.reshape(t, d)
    p2d = p.reshape(t, PLE_DIM)
    tgt = loss_target.reshape(t, d)

    win = lax.dynamic_update_slice(jnp.zeros((d, WIN_W), BF), w_in[0].astype(BF), (0, me))
    rows = jnp.concatenate([w_a_out[0].astype(BF), w_b_out[0].astype(BF), w_o[0].astype(BF),
                            w_ple_gate[0].astype(BF), w_ple[0].astype(BF).reshape(32, d)], axis=0)
    cfl = jnp.pad(conv_w[0], ((0, 5), (0, 0)))
    g_win, g_rows, g_cf = _all_gather([win, rows, cfl])

    parts = []
    for k in range(N_DEV):
        main = g_win[k, :, :WIN_STRIDE]
        if k > 0:
            main = jnp.concatenate([main[:, :128] + g_win[k - 1, :, WIN_STRIDE:], main[:, 128:]], axis=1)
        parts.append(main)
    parts.append(g_win[N_DEV - 1, :, WIN_STRIDE:])
    w_glob = jnp.concatenate(parts, axis=1)
    tail = jnp.roll(w_glob[:, 12288:], -8, axis=1)
    w_main = jnp.concatenate([w_glob[:, 4096:12288], w_glob[:, 0:4096], tail[:, :2048]], axis=1)
    w_gate = tail[:, 2048:]
    w_a = g_rows[:, 0:128].reshape(d, d)
    w_b = g_rows[:, 128:384].reshape(V_DIM, d)
    w_of = g_rows[:, 384:512].reshape(d, d)
    w_pg = g_rows[:, 512:640].reshape(d, d)
    w_pl = g_rows[:, 640:672].reshape(N_DEV, PLE_DIM, 128).transpose(1, 0, 2).reshape(PLE_DIM, d)
    conv_w8 = jnp.pad(g_cf[:, :3, :].transpose(1, 0, 2).reshape(3, d), ((0, 5), (0, 0)))
    gate_bias = jnp.pad(b_gates, ((0, 0), (IG_LANE, 0)))

    hn, hnt = _rms_fwd(x2d, g_mix, name="rms_mix")
    proj = _mm(hn, w_main, form="nn", out_dtype=BF, name="proj", tm=2048)
    gates = _mm(hn, w_gate, form="nn", out_dtype=F32, name="proj_gates", tm=2048)
    ya_pre = _branch_a_fwd(proj, conv_w8, conv_b)
    ya = _mm(ya_pre, w_a, form="nn", out_dtype=BF, name="ya", tm=2048)
    yb_pre, h_raw, c_states, aux = _mlstm_fwd(proj, gates, gate_bias, g_head)
    yb = _mm(yb_pre, w_b, form="nn", out_dtype=BF, name="yb", tk=2048)
    merged = _merge_fwd(proj, ya, yb)
    x1 = _mm(merged, w_of, form="nn", out_dtype=F32, name="x1", add=x2d)
    hn2, hn2t = _rms_fwd(x1, g_ple, name="rms_ple")
    gpre = _mm(hn2, w_pg, form="nn", out_dtype=BF, name="gpre", tm=2048)
    pe = _mm(p2d, w_pl, form="nn", out_dtype=BF, name="pe", tm=2048)
    dx2, dgpre, dpe, small_fin = _final(x1, gpre, pe, tgt, g_final.reshape(1, d))

    dw_pg = _mm(hn2t, dgpre, form="nn", out_dtype=BF, name="dw_pg")
    dw_pl = _mm(p2d, dpe, form="tn", out_dtype=BF, name="dw_ple")
    dhn2 = _mm(dgpre, w_pg, form="nt", out_dtype=F32, name="dhn2")
    dx1, dx1b, dg_ple = _rms_bwd(x1, g_ple, dhn2, dx2, name="rms_ple_bwd")
    dmerged = _mm(dx1b, w_of, form="nt", out_dtype=BF, name="dmerged")
    dw_o = _mm(merged, dx1b, form="tn", out_dtype=BF, name="dw_o")
    dproj, dya, dyb = _merge_bwd(proj, ya, yb, dmerged)
    dya_pre = _mm(dya, w_a, form="nt", out_dtype=BF, name="dya_pre")
    dw_a = _mm(ya_pre, dya, form="tn", out_dtype=BF, name="dw_a")
    dyb_pre = _mm(dyb, w_b, form="nt", out_dtype=BF, name="dyb_pre", tn=2048)
    dw_b = _mm(yb_pre, dyb, form="tn", out_dtype=BF, name="dw_b", tm=2048)
    dproj, dconv = _branch_a_bwd(dproj, proj, dya_pre, conv_w8, conv_b)
    dproj, dgates, dbias, dg_head = _mlstm_bwd(dproj, proj, gates, gate_bias, g_head, h_raw, c_states, aux,
                                               dyb_pre)
    dgates_b = dgates.astype(BF)
    dhn = _mm(dproj, w_main, form="nt", out_dtype=F32, name="dhn")
    dhn = _mm(dgates_b, w_gate, form="nt", out_dtype=F32, name="dhn_gates", add=dhn)
    dw_main = _mm(hnt, dproj, form="nn", out_dtype=BF, name="dw_main")
    dw_gate = _mm(hnt, dgates_b, form="nn", out_dtype=BF, name="dw_gate")
    grad_x, _, dg_mix = _rms_bwd(x2d, g_mix, dhn, dx1, name="rms_mix_bwd")

    tail_g = jnp.roll(jnp.concatenate([dw_main[:, 12288:], dw_gate], axis=1), 8, axis=1)
    dw_glob = jnp.concatenate([dw_main[:, 8192:12288], dw_main[:, 0:8192], tail_g], axis=1)
    s_win = jnp.stack([dw_glob[:, WIN_STRIDE * j:WIN_STRIDE * j + WIN_W] for j in range(N_DEV)])
    s_rows = jnp.concatenate([
        dw_a.reshape(N_DEV, 128, d), dw_b.reshape(N_DEV, 256, d), dw_o.reshape(N_DEV, 128, d),
        dw_pg.reshape(N_DEV, 128, d),
        dw_pl.reshape(PLE_DIM, N_DEV, 128).transpose(1, 0, 2).reshape(N_DEV, 32, d)], axis=1)
    vec = jnp.concatenate([dg_mix[0], dconv[3], dg_head[0], dg_ple[0], small_fin[1],
                           dbias[0, IG_LANE:], jnp.zeros((7 * d - 6152,), F32)]).reshape(7, d)
    conv_part = jnp.pad(dconv[:3].reshape(3, N_DEV, 128).transpose(1, 0, 2).reshape(N_DEV, 1, 384),
                        ((0, 0), (0, 0), (0, d - 384)))
    s_f32 = jnp.concatenate([jnp.broadcast_to(vec[None], (N_DEV, 7, d)), conv_part], axis=1)
    r_win, r_rows, r_f32 = _exchange([s_win, s_rows, s_f32])
    sum_win = _sum_slots(r_win, name="sum_win", tr=128)
    sum_rows = _sum_slots(r_rows, name="sum_rows", tr=96)
    sum_f32 = _sum_slots(r_f32, name="sum_f32", tr=8)

    g_w_in = lax.dynamic_slice(sum_win, (0, me), (d, SHARD_W))
    g_w_a = sum_rows[0:128]
    g_w_b = sum_rows[128:384]
    g_w_o = sum_rows[384:512]
    g_w_pg = sum_rows[512:640]
    g_w_pl = sum_rows[640:672].reshape(PLE_DIM, 128)
    vsum = sum_f32[:7].reshape(7 * d)
    g_g_mix = vsum[0:1024].reshape(1, d)
    g_conv_b = vsum[1024:2048].reshape(1, d)
    g_g_head = vsum[2048:4096].reshape(1, V_DIM)
    g_g_ple = vsum[4096:5120].reshape(1, d)
    g_g_final = vsum[5120:6144].reshape(1, d)
    g_b_gates = vsum[6144:6152].reshape(1, 8)
    g_conv_w = sum_f32[7, :384].reshape(3, 128)

    loss = lax.psum(jnp.sum(small_fin[0]), ("x", "y", "c"))

    names = ["g_mix", "w_in", "conv_w", "conv_b", "w_a_out", "b_gates", "g_head", "w_b_out", "w_o", "g_ple",
             "w_ple_gate", "w_ple", "g_final"]
    weights = [g_mix, w_in, conv_w, conv_b, w_a_out, b_gates, g_head, w_b_out, w_o, g_ple, w_ple_gate, w_ple,
               g_final]
    moms = [m_g_mix, m_w_in, m_conv_w, m_conv_b, m_w_a_out, m_b_gates, m_g_head, m_w_b_out, m_w_o, m_g_ple,
            m_w_ple_gate, m_w_ple, m_g_final]
    vels = [v_g_mix, v_w_in, v_conv_w, v_conv_b, v_w_a_out, v_b_gates, v_g_head, v_w_b_out, v_w_o, v_g_ple,
            v_w_ple_gate, v_w_ple, v_g_final]
    grads2d = [g_g_mix, g_w_in, g_conv_w, g_conv_b, g_w_a, g_b_gates, g_g_head, g_w_b, g_w_o, g_g_ple, g_w_pg,
               g_w_pl, g_g_final]
    grads, deltas, new_m, new_v = [], [], [], []
    for nm, w, m_, v_, g2 in zip(names, weights, moms, vels, grads2d):
        shp = w.shape
        w2 = w.reshape(g2.shape)
        dl, mn, vn = _adamw(w2, g2, m_.reshape(g2.shape), v_.reshape(g2.shape), name="adamw_" + nm)
        grads.append(g2.reshape(shp))
        deltas.append(dl.reshape(shp))
        new_m.append(mn.reshape(shp))
        new_v.append(vn.reshape(shp))
    return (loss, grad_x.reshape(x.shape), *grads, *deltas, *new_m, *new_v)
```

```python
import functools

import jax
import jax.numpy as jnp
from jax import lax
from jax.experimental import pallas as pl
from jax.experimental.pallas import tpu as pltpu

F32 = jnp.float32
BF = jnp.bfloat16

D_MODEL = 1024
N_HEADS = 4
DK = 256
DV = 512
V_DIM = 2048
PLE_DIM = 256
N_IN = 14344
N_MAIN = 14336
EPS = 1e-6
QK_SCALE = DK ** -0.5
NEG = -1e30
N_DEV = 8
SHARD_W = 1793
WIN_STRIDE = 1792
WIN_W = 1920
ROWS_PACK = 672
_CHUNK = 256
IG_LANE = 120
FG_LANE = 124

ADAM_LR = 0.001
ADAM_B1 = 0.9
ADAM_B2 = 0.999
ADAM_EPS = 1e-08
ADAM_WD = 0.01
ADAM_STEP = 10

VMEM_LIMIT = 56 * 1024 * 1024
MESH = pl.DeviceIdType.MESH


def _cparams(sem):
    return pltpu.CompilerParams(dimension_semantics=sem, vmem_limit_bytes=VMEM_LIMIT)


def _sigmoid(x):
    return 1.0 / (1.0 + jnp.exp(-x))


def _log_sigmoid(x):
    return jnp.minimum(x, 0.0) - jnp.log(1.0 + jnp.exp(-jnp.abs(x)))


def _dot(a, b):
    return jnp.dot(a, b, preferred_element_type=F32)


def _dot_nt(a, b):
    return lax.dot_general(a, b, (((1,), (1,)), ((), ())), preferred_element_type=F32)


def _dot_tn(a, b):
    return lax.dot_general(a, b, (((0,), (0,)), ((), ())), preferred_element_type=F32)


class _DirectComm:
    def __init__(self, srcs, kind):
        self.srcs = list(srcs)
        self.kind = kind
        self.n = len(self.srcs)
        if kind == "exchange":
            self.out_shapes = [jax.ShapeDtypeStruct(s.shape, s.dtype) for s in self.srcs]
        else:
            self.out_shapes = [jax.ShapeDtypeStruct((N_DEV,) + s.shape, s.dtype) for s in self.srcs]
        self.scratch = [pltpu.SemaphoreType.DMA((self.n, 7)), pltpu.SemaphoreType.DMA((self.n, 7)),
                        pltpu.SemaphoreType.DMA((self.n,))]

    def ops(self, src, dst, send_sems, recv_sems, local_sems):
        exchange = self.kind == "exchange"

        def descriptors():
            x, y, c = _position()
            me_lin = 4 * x + 2 * y + c
            local = [pltpu.make_async_copy(src[b].at[me_lin] if exchange else src[b], dst[b].at[me_lin],
                                           local_sems.at[b]) for b in range(self.n)]
            sends, recvs = [], []
            for f in range(1, N_DEV):
                px = (1 - x) if (f >> 2) & 1 else x
                py = (1 - y) if (f >> 1) & 1 else y
                pc = (1 - c) if f & 1 else c
                peer_lin = 4 * px + 2 * py + pc
                for b in range(self.n):
                    out = src[b].at[peer_lin] if exchange else src[b]
                    common = dict(send_sem=send_sems.at[b, f - 1], recv_sem=recv_sems.at[b, f - 1],
                                  device_id=(px, py, pc), device_id_type=MESH)
                    sends.append(pltpu.make_async_remote_copy(src_ref=out, dst_ref=dst[b].at[me_lin], **common))
                    recvs.append(pltpu.make_async_remote_copy(src_ref=out, dst_ref=dst[b].at[peer_lin], **common))
            return local, sends, recvs

        def start():
            local, sends, _ = descriptors()
            for cp in local + sends:
                cp.start()

        def wait():
            local, sends, recvs = descriptors()
            for cp in recvs:
                cp.wait_recv()
            for cp in sends:
                cp.wait_send()
            for cp in local:
                cp.wait()

        return start, wait


def _mm(a, b, *, form, out_dtype, name, tm=1024, tn=1024, tk=1024, add=None, comm=None):
    if form == "nn":
        m, kc = a.shape
        n = b.shape[1]
    elif form == "nt":
        m, kc = a.shape
        n = b.shape[0]
    else:
        kc, m = a.shape
        n = b.shape[1]
    tm, tn, tk = min(tm, m), min(tn, n), min(tk, kc)
    assert m % tm == 0 and n % tn == 0 and kc % tk == 0, (name, a.shape, b.shape)
    nk = kc // tk
    if form == "tn":
        a_spec = pl.BlockSpec((tk, tm), lambda i, j, k: (k, i))
    else:
        a_spec = pl.BlockSpec((tm, tk), lambda i, j, k: (i, k))
    if form == "nt":
        b_spec = pl.BlockSpec((tn, tk), lambda i, j, k: (j, k))
    else:
        b_spec = pl.BlockSpec((tk, tn), lambda i, j, k: (k, j))
    o_spec = pl.BlockSpec((tm, tn), lambda i, j, k: (i, j))
    dot = {"nn": _dot, "nt": _dot_nt, "tn": _dot_tn}[form]
    has_add = add is not None

    use_acc = nk > 1 and (has_add or out_dtype != F32)
    n_in = 2 + int(has_add) + (comm.n if comm else 0)
    grid = (m // tm, n // tn, nk)

    def body(*refs):
        a_ref, b_ref = refs[0], refs[1]
        add_ref = refs[2] if has_add else None
        o_ref = refs[n_in]
        acc = refs[-1] if use_acc else o_ref
        i, j, k = pl.program_id(0), pl.program_id(1), pl.program_id(2)
        if comm:
            start, wait = comm.ops(refs[n_in - comm.n:n_in], refs[n_in + 1:n_in + 1 + comm.n],
                                   *refs[n_in + 1 + comm.n:n_in + 4 + comm.n])
            pl.when((i == 0) & (j == 0) & (k == 0))(start)

        def part():
            return dot(a_ref[...].astype(BF), b_ref[...].astype(BF))

        def finish(total):
            if has_add:
                total = total + add_ref[...].astype(F32)
            o_ref[...] = total.astype(out_dtype)

        if nk == 1:
            finish(part())
        else:
            @pl.when(k == 0)
            def _():
                acc[...] = part()

            @pl.when(k > 0)
            def _():
                acc[...] += part()

            if use_acc:
                @pl.when(k == nk - 1)
                def _():
                    finish(acc[...])
        if comm:
            pl.when((i == grid[0] - 1) & (j == grid[1] - 1) & (k == nk - 1))(wait)

    any_spec = pl.BlockSpec(memory_space=pl.ANY)
    in_specs = [a_spec, b_spec] + ([o_spec] if has_add else []) + ([any_spec] * comm.n if comm else [])
    args = (a, b) + ((add,) if has_add else ()) + (tuple(comm.srcs) if comm else ())
    out_specs = [o_spec] + ([any_spec] * comm.n if comm else [])
    out_shape = [jax.ShapeDtypeStruct((m, n), out_dtype)] + (comm.out_shapes if comm else [])
    scratch = (comm.scratch if comm else []) + ([pltpu.VMEM((tm, tn), F32)] if use_acc else [])
    sem = ("arbitrary",) * 3 if comm else ("parallel", "parallel", "arbitrary")
    res = pl.pallas_call(
        body, name=name, grid=grid, in_specs=in_specs, out_specs=out_specs, out_shape=out_shape,
        scratch_shapes=scratch, compiler_params=_cparams(sem),
    )(*args)
    return (res[0], list(res[1:])) if comm else res[0]


def _rms_fwd(x, g, *, name, tm=512):
    t, d = x.shape
    tm = min(tm, t)

    def body(x_ref, g_ref, hn_ref, hnt_ref):
        xv = x_ref[...]
        r = lax.rsqrt(jnp.mean(xv * xv, axis=1, keepdims=True) + EPS)
        hn = xv * r * g_ref[...]
        hn_ref[...] = hn.astype(BF)
        hnt_ref[...] = hn.T.astype(BF)

    return pl.pallas_call(
        body, name=name, grid=(t // tm,),
        in_specs=[pl.BlockSpec((tm, d), lambda i: (i, 0)), pl.BlockSpec((1, d), lambda i: (0, 0))],
        out_specs=[pl.BlockSpec((tm, d), lambda i: (i, 0)), pl.BlockSpec((d, tm), lambda i: (0, i))],
        out_shape=[jax.ShapeDtypeStruct((t, d), BF), jax.ShapeDtypeStruct((d, t), BF)],
        compiler_params=_cparams(("parallel",)),
    )(x, g)


def _rms_bwd(x, g, dhn, dres, *, name, tm=512):
    t, d = x.shape
    tm = min(tm, t)

    def body(x_ref, g_ref, dhn_ref, dres_ref, dx_ref, dxb_ref, dg_ref):
        i = pl.program_id(0)
        xv = x_ref[...]
        r = lax.rsqrt(jnp.mean(xv * xv, axis=1, keepdims=True) + EPS)
        xh = xv * r
        dh = dhn_ref[...]
        dxh = dh * g_ref[...]
        dx = dres_ref[...] + r * (dxh - xh * jnp.mean(dxh * xh, axis=1, keepdims=True))
        dx_ref[...] = dx
        dxb_ref[...] = dx.astype(BF)
        part = jnp.sum(dh * xh, axis=0, keepdims=True)

        @pl.when(i == 0)
        def _():
            dg_ref[...] = jnp.zeros_like(dg_ref)

        dg_ref[0:1, :] += part

    row = pl.BlockSpec((tm, d), lambda i: (i, 0))
    return pl.pallas_call(
        body, name=name, grid=(t // tm,),
        in_specs=[row, pl.BlockSpec((1, d), lambda i: (0, 0)), row, row],
        out_specs=[row, row, pl.BlockSpec((8, d), lambda i: (0, 0))],
        out_shape=[jax.ShapeDtypeStruct((t, d), F32), jax.ShapeDtypeStruct((t, d), BF),
                   jax.ShapeDtypeStruct((8, d), F32)],
        compiler_params=_cparams(("arbitrary",)),
    )(x, g, dhn, dres)


HALO = 16
XA_BLK, BA_BLK, CA_BLK, ZA_BLK = 8, 9, 10, 11


def _shift_down(u, prev, n):
    tm = u.shape[0]
    rolled = pltpu.roll(u, n, 0)
    row = lax.broadcasted_iota(jnp.int32, u.shape, 0)
    out = rolled
    for j in range(n):
        out = jnp.where(row == j, prev[HALO - n + j:HALO - n + j + 1, :], out)
    return out


def _shift_up(u, nxt, n):
    tm = u.shape[0]
    rolled = pltpu.roll(u, tm - n, 0)
    row = lax.broadcasted_iota(jnp.int32, u.shape, 0)
    out = rolled
    for j in range(n):
        out = jnp.where(row == tm - n + j, nxt[j:j + 1, :], out)
    return out


def _branch_a_fwd(proj, conv_w8, conv_b, *, tm=512):
    t = proj.shape[0]
    d = D_MODEL
    tm = min(tm, t)
    hb = tm // HALO

    def body(xa_ref, ba_ref, ca_ref, za_ref, xap_ref, cap_ref, w_ref, b_ref, o_ref):
        i = pl.program_id(0)
        u = ca_ref[...].astype(F32) * xa_ref[...].astype(F32)
        up = cap_ref[...].astype(F32) * xap_ref[...].astype(F32)
        up = jnp.where(i == 0, 0.0, up)
        u1 = _shift_down(u, up, 1)
        u2 = _shift_down(u, up, 2)
        cv = w_ref[0:1, :] * u2 + w_ref[1:2, :] * u1 + w_ref[2:3, :] * u + b_ref[...]
        za = za_ref[...].astype(F32)
        o_ref[...] = (ba_ref[...].astype(F32) * cv * (za * _sigmoid(za))).astype(BF)

    def col(blk):
        return pl.BlockSpec((tm, d), lambda i: (i, blk))

    def prev(blk):
        return pl.BlockSpec((HALO, d), lambda i: (jnp.maximum(i * hb - 1, 0), blk))

    return pl.pallas_call(
        body, name="branch_a_fwd", grid=(t // tm,),
        in_specs=[col(XA_BLK), col(BA_BLK), col(CA_BLK), col(ZA_BLK), prev(XA_BLK), prev(CA_BLK),
                  pl.BlockSpec((8, d), lambda i: (0, 0)), pl.BlockSpec((1, d), lambda i: (0, 0))],
        out_specs=pl.BlockSpec((tm, d), lambda i: (i, 0)),
        out_shape=jax.ShapeDtypeStruct((t, d), BF),
        compiler_params=_cparams(("parallel",)),
    )(proj, proj, proj, proj, proj, proj, conv_w8, conv_b)


def _branch_a_bwd(dproj, proj, dya_pre, conv_w8, conv_b, *, tm=512):
    t = proj.shape[0]
    d = D_MODEL
    tm = min(tm, t)
    hb = tm // HALO
    nt = t // tm

    def body(dp_ref, xa_ref, ba_ref, ca_ref, za_ref, xap_ref, cap_ref, dy_ref, ban_ref, zan_ref, dyn_ref,
             w_ref, b_ref, o_ref, dc_ref):
        del dp_ref
        i = pl.program_id(0)
        xa = xa_ref[...].astype(F32)
        ca = ca_ref[...].astype(F32)
        ba = ba_ref[...].astype(F32)
        za = za_ref[...].astype(F32)
        u = ca * xa
        up = cap_ref[...].astype(F32) * xap_ref[...].astype(F32)
        up = jnp.where(i == 0, 0.0, up)
        u1 = _shift_down(u, up, 1)
        u2 = _shift_down(u, up, 2)
        w0, w1, w2 = w_ref[0:1, :], w_ref[1:2, :], w_ref[2:3, :]
        cv = w0 * u2 + w1 * u1 + w2 * u + b_ref[...]
        sg = _sigmoid(za)
        sz = za * sg
        dy = dy_ref[...].astype(F32)
        dcv = dy * ba * sz
        zan = zan_ref[...].astype(F32)
        dcvn = dyn_ref[...].astype(F32) * ban_ref[...].astype(F32) * (zan * _sigmoid(zan))
        dcvn = jnp.where(i == nt - 1, 0.0, dcvn)
        du = w2 * dcv + w1 * _shift_up(dcv, dcvn, 1) + w0 * _shift_up(dcv, dcvn, 2)
        o_ref[:, 0:d] = (du * ca).astype(BF)
        o_ref[:, d:2 * d] = (dy * cv * sz).astype(BF)
        o_ref[:, 2 * d:3 * d] = (du * xa).astype(BF)
        o_ref[:, 3 * d:4 * d] = (dy * ba * cv * sg * (1.0 + za * (1.0 - sg))).astype(BF)

        @pl.when(i == 0)
        def _():
            dc_ref[...] = jnp.zeros_like(dc_ref)

        dc_ref[0:1, :] += jnp.sum(dcv * u2, axis=0, keepdims=True)
        dc_ref[1:2, :] += jnp.sum(dcv * u1, axis=0, keepdims=True)
        dc_ref[2:3, :] += jnp.sum(dcv * u, axis=0, keepdims=True)
        dc_ref[3:4, :] += jnp.sum(dcv, axis=0, keepdims=True)

    def col(blk):
        return pl.BlockSpec((tm, d), lambda i: (i, blk))

    def prev(blk):
        return pl.BlockSpec((HALO, d), lambda i: (jnp.maximum(i * hb - 1, 0), blk))

    def nxt(blk):
        return pl.BlockSpec((HALO, d), lambda i: (jnp.minimum((i + 1) * hb, t // HALO - 1), blk))

    return pl.pallas_call(
        body, name="branch_a_bwd", grid=(nt,),
        in_specs=[pl.BlockSpec(memory_space=pl.ANY),
                  col(XA_BLK), col(BA_BLK), col(CA_BLK), col(ZA_BLK), prev(XA_BLK), prev(CA_BLK),
                  pl.BlockSpec((tm, d), lambda i: (i, 0)), nxt(BA_BLK), nxt(ZA_BLK),
                  pl.BlockSpec((HALO, d), lambda i: (jnp.minimum((i + 1) * hb, t // HALO - 1), 0)),
                  pl.BlockSpec((8, d), lambda i: (0, 0)), pl.BlockSpec((1, d), lambda i: (0, 0))],
        out_specs=[pl.BlockSpec((tm, 4 * d), lambda i: (i, 2)), pl.BlockSpec((8, d), lambda i: (0, 0))],
        out_shape=[jax.ShapeDtypeStruct(dproj.shape, BF), jax.ShapeDtypeStruct((8, d), F32)],
        input_output_aliases={0: 0},
        compiler_params=_cparams(("arbitrary",)),
    )(dproj, proj, proj, proj, proj, proj, proj, dya_pre, proj, proj, dya_pre, conv_w8, conv_b)


def _gate_vectors(gc, gt, h, lane_i, sub_i):
    ig_c = jnp.sum(jnp.where(lane_i == IG_LANE + h, gc, 0.0), axis=1, keepdims=True)
    fg_c = jnp.sum(jnp.where(lane_i == FG_LANE + h, gc, 0.0), axis=1, keepdims=True)
    ig_r = jnp.sum(jnp.where(sub_i == IG_LANE + h, gt, 0.0), axis=0, keepdims=True)
    fg_r = jnp.sum(jnp.where(sub_i == FG_LANE + h, gt, 0.0), axis=0, keepdims=True)
    return ig_c, fg_c, ig_r, fg_r


def _chunk_common(q, k, ig_c, fg_c, ig_r, fg_r, m_prev, n_prev, row, col):
    lf_c = _log_sigmoid(fg_c)
    lf_r = _log_sigmoid(fg_r)
    causal = col <= row
    b_c = jnp.sum(jnp.where(causal, lf_r, 0.0), axis=1, keepdims=True)
    b_r = jnp.sum(jnp.where(row <= col, lf_c, 0.0), axis=0, keepdims=True)
    dmat = jnp.where(causal, b_c - b_r + ig_r, NEG)
    a = b_c + m_prev
    m_row = jnp.maximum(a, jnp.max(dmat, axis=1, keepdims=True))
    est = jnp.exp(dmat - m_row)
    s = _dot_nt(q, k) * QK_SCALE * est
    inter = jnp.exp(a - m_row)
    den = jnp.sum(s, axis=1, keepdims=True) + inter * QK_SCALE * jnp.sum(
        q.astype(F32) * n_prev, axis=1, keepdims=True)
    expm = jnp.exp(-m_row)
    mx = jnp.maximum(jnp.abs(den), expm)
    b_last = jnp.sum(lf_r, axis=1, keepdims=True)
    g_r = b_last - b_r + ig_r
    g_c = b_last - b_c + ig_c
    m_new = jnp.maximum(b_last + m_prev, jnp.max(g_r, axis=1, keepdims=True))
    w_c = jnp.exp(g_c - m_new)
    decay = jnp.exp(b_last + m_prev - m_new)
    return est, s, inter, den, expm, mx, m_new, w_c, decay


def _mlstm_fwd(proj, gates, gate_bias, g_head):
    t = proj.shape[0]
    lc = min(_CHUNK, t)
    nc = t // lc

    def body(q_ref, k_ref, v_ref, o_ref, z_ref, g_ref, gb_ref, gh_ref,
             yb_ref, hr_ref, cs_ref, aux_ref, c_scr, nm_scr):
        c = pl.program_id(0)

        @pl.when(c == 0)
        def _():
            c_scr[...] = jnp.zeros_like(c_scr)
            nm_scr[...] = jnp.zeros_like(nm_scr)
            nm_scr[:, 1:2, :] = jnp.full((N_HEADS, 1, DK), NEG, F32)

        row = lax.broadcasted_iota(jnp.int32, (lc, lc), 0)
        col = lax.broadcasted_iota(jnp.int32, (lc, lc), 1)
        gc = g_ref[...] + gb_ref[...]
        gt = gc.T
        lane_i = lax.broadcasted_iota(jnp.int32, gc.shape, 1)
        sub_i = lax.broadcasted_iota(jnp.int32, gt.shape, 0)
        for h in range(N_HEADS):
            ks = slice(h * DK, (h + 1) * DK)
            vs = slice(h * DV, (h + 1) * DV)
            q = q_ref[:, ks]
            k = k_ref[:, ks]
            v = v_ref[:, vs]
            ig_c, fg_c, ig_r, fg_r = _gate_vectors(gc, gt, h, lane_i, sub_i)
            n_prev = nm_scr[h, 0:1, :]
            m_prev = nm_scr[h, 1:2, 0:1]
            est, s, inter, den, expm, mx, m_new, w_c, decay = _chunk_common(
                q, k, ig_c, fg_c, ig_r, fg_r, m_prev, n_prev, row, col)
            c_prev = c_scr[h]
            c_prev_b = c_prev.astype(BF)
            num = _dot(s.astype(BF), v) + (inter * QK_SCALE) * _dot(q, c_prev_b)
            hh = num / mx
            r = lax.rsqrt(jnp.mean(hh * hh, axis=1, keepdims=True) + EPS)
            hbn = hh * r * gh_ref[:, vs]
            o = o_ref[:, vs].astype(F32)
            z = z_ref[:, vs].astype(F32)
            yb_ref[:, vs] = (_sigmoid(o) * hbn * (z * _sigmoid(z))).astype(BF)
            hr_ref[:, vs] = hh.astype(BF)
            cs_ref[0, h] = c_prev_b
            aux_ref[0, h] = nm_scr[h]
            kw = k.astype(F32) * w_c
            c_scr[h] = decay * c_prev + _dot_tn(kw.astype(BF), v)
            nm_scr[h, 0:1, :] = decay * n_prev + jnp.sum(kw, axis=0, keepdims=True)
            nm_scr[h, 1:2, :] = jnp.broadcast_to(m_new, (1, DK))

    return pl.pallas_call(
        body, name="mlstm_fwd", grid=(nc,),
        in_specs=[pl.BlockSpec((lc, 1024), lambda c: (c, 0)),
                  pl.BlockSpec((lc, 1024), lambda c: (c, 1)),
                  pl.BlockSpec((lc, 2048), lambda c: (c, 1)),
                  pl.BlockSpec((lc, 2048), lambda c: (c, 2)),
                  pl.BlockSpec((lc, 2048), lambda c: (c, 3)),
                  pl.BlockSpec((lc, 128), lambda c: (c, 0)),
                  pl.BlockSpec((1, 128), lambda c: (0, 0)),
                  pl.BlockSpec((1, V_DIM), lambda c: (0, 0))],
        out_specs=[pl.BlockSpec((lc, V_DIM), lambda c: (c, 0)),
                   pl.BlockSpec((lc, V_DIM), lambda c: (c, 0)),
                   pl.BlockSpec((1, N_HEADS, DK, DV), lambda c: (c, 0, 0, 0)),
                   pl.BlockSpec((1, N_HEADS, 8, DK), lambda c: (c, 0, 0, 0))],
        out_shape=[jax.ShapeDtypeStruct((t, V_DIM), BF), jax.ShapeDtypeStruct((t, V_DIM), BF),
                   jax.ShapeDtypeStruct((nc, N_HEADS, DK, DV), BF),
                   jax.ShapeDtypeStruct((nc, N_HEADS, 8, DK), F32)],
        scratch_shapes=[pltpu.VMEM((N_HEADS, DK, DV), F32), pltpu.VMEM((N_HEADS, 8, DK), F32)],
        compiler_params=_cparams(("arbitrary",)),
    )(proj, proj, proj, proj, proj, gates, gate_bias, g_head)


def _mlstm_bwd(dproj, proj, gates, gate_bias, g_head, h_raw, c_states, aux, dyb_pre, comm):
    t = proj.shape[0]
    lc = min(_CHUNK, t)
    nc = t // lc
    n_in, n_out = 13, 4

    def body(*refs):
        (_, q_ref, k_ref, v_ref, o_ref, z_ref, g_ref, gb_ref, gh_ref, hr_ref, cs_ref, aux_ref,
         dy_ref) = refs[:n_in]
        dp_ref, dg_ref, dbias_ref, dgh_ref = refs[n_in + comm.n:n_in + comm.n + n_out]
        dc_scr, dn_scr = refs[n_in + 2 * comm.n + n_out:n_in + 2 * comm.n + n_out + 2]
        comm_start, comm_wait = comm.ops(refs[n_in:n_in + comm.n],
                                         refs[n_in + comm.n + n_out:n_in + 2 * comm.n + n_out], *refs[-3:])
        step = pl.program_id(0)
        pl.when(step == 0)(comm_start)

        @pl.when(step == 0)
        def _():
            dc_scr[...] = jnp.zeros_like(dc_scr)
            dn_scr[...] = jnp.zeros_like(dn_scr)
            dbias_ref[...] = jnp.zeros_like(dbias_ref)
            dgh_ref[...] = jnp.zeros_like(dgh_ref)

        row = lax.broadcasted_iota(jnp.int32, (lc, lc), 0)
        col = lax.broadcasted_iota(jnp.int32, (lc, lc), 1)
        eye = row == col
        gc = g_ref[...] + gb_ref[...]
        gt = gc.T
        lane_i = lax.broadcasted_iota(jnp.int32, gc.shape, 1)
        sub_i = lax.broadcasted_iota(jnp.int32, gt.shape, 0)
        dgates = jnp.zeros(gc.shape, F32)
        for h in range(N_HEADS):
            ks = slice(h * DK, (h + 1) * DK)
            vs = slice(h * DV, (h + 1) * DV)
            q = q_ref[:, ks]
            k = k_ref[:, ks]
            v = v_ref[:, vs]
            ig_c, fg_c, ig_r, fg_r = _gate_vectors(gc, gt, h, lane_i, sub_i)
            n_prev = aux_ref[0, h, 0:1, :]
            m_prev = aux_ref[0, h, 1:2, 0:1]
            c_prev_b = cs_ref[0, h]
            est, s, inter, den, expm, mx, m_new, w_c, decay = _chunk_common(
                q, k, ig_c, fg_c, ig_r, fg_r, m_prev, n_prev, row, col)
            hb = hr_ref[:, vs].astype(F32)
            dyp = dy_ref[:, vs].astype(F32)
            o = o_ref[:, vs].astype(F32)
            z = z_ref[:, vs].astype(F32)
            so = _sigmoid(o)
            sgz = _sigmoid(z)
            sz = z * sgz
            r = lax.rsqrt(jnp.mean(hb * hb, axis=1, keepdims=True) + EPS)
            xh = hb * r
            gh = gh_ref[:, vs]
            hbn = xh * gh
            d_o = dyp * hbn * sz * so * (1.0 - so)
            d_z = dyp * so * hbn * sgz * (1.0 + z * (1.0 - sgz))
            dhbn = dyp * so * sz
            dgh_ref[0:1, vs] += jnp.sum(dhbn * xh, axis=0, keepdims=True)
            dxh = dhbn * gh
            dh = r * (dxh - xh * jnp.mean(dxh * xh, axis=1, keepdims=True))
            dnm = dh / mx
            hd = jnp.sum(dh * hb, axis=1, keepdims=True)
            cond = jnp.abs(den) > expm
            dden = jnp.where(cond, -hd / mx * jnp.sign(den), 0.0)
            dnm_b = dnm.astype(BF)
            p = _dot_nt(dnm_b, v) + dden
            dqk = (p * est * QK_SCALE).astype(BF)
            dq_inter = (inter * QK_SCALE) * (_dot_nt(dnm_b, c_prev_b) + dden * n_prev)
            dq = _dot(dqk, k) + dq_inter
            dc_new = dc_scr[h]
            dc_new_b = dc_new.astype(BF)
            dn_new = dn_scr[h, 0:1, :]
            kf = k.astype(F32)
            dk_state = w_c * (_dot_nt(v, dc_new_b) + dn_new)
            dk = _dot_tn(dqk, q) + dk_state
            dv = _dot_tn(s.astype(BF), dnm_b) + w_c * _dot(k, dc_new_b)
            dv1_r = jnp.sum(s * dden, axis=0, keepdims=True)
            dv1_c = (jnp.sum(jnp.where(eye, dv1_r, 0.0), axis=1, keepdims=True)
                     + w_c * jnp.sum(kf * dn_new, axis=1, keepdims=True))
            dli_c = jnp.sum(v.astype(F32) * dv, axis=1, keepdims=True) + dv1_c
            hmat = _dot((p * s).astype(BF), (row < col).astype(BF))
            from_prev_c = jnp.sum(q.astype(F32) * dq_inter, axis=1, keepdims=True)
            to_next_c = jnp.sum(kf * dk_state, axis=1, keepdims=True)
            through = decay * (
                jnp.sum(jnp.sum(dc_new * c_prev_b.astype(F32), axis=1, keepdims=True), axis=0, keepdims=True)
                + jnp.sum(dn_new * n_prev, axis=1, keepdims=True))
            dlf_r = through + jnp.sum(jnp.where(row >= col, hmat + from_prev_c, to_next_c), axis=0, keepdims=True)
            dlf_c = jnp.sum(jnp.where(eye, dlf_r, 0.0), axis=1, keepdims=True)
            dfg_c = dlf_c * _sigmoid(-fg_c)
            dgates = dgates + jnp.where(lane_i == IG_LANE + h, dli_c, 0.0) + jnp.where(
                lane_i == FG_LANE + h, dfg_c, 0.0)
            qi = q.astype(F32) * (inter * QK_SCALE)
            dc_scr[h] = decay * dc_new + _dot_tn(qi.astype(BF), dnm_b)
            dn_scr[h, 0:1, :] = decay * dn_new + jnp.sum(qi * dden, axis=0, keepdims=True)
            dp_ref[:, h * DK:(h + 1) * DK] = dq.astype(BF)
            dp_ref[:, 1024 + h * DK:1024 + (h + 1) * DK] = dk.astype(BF)
            dp_ref[:, 2048 + h * DV:2048 + (h + 1) * DV] = dv.astype(BF)
            dp_ref[:, 4096 + h * DV:4096 + (h + 1) * DV] = d_o.astype(BF)
            dp_ref[:, 6144 + h * DV:6144 + (h + 1) * DV] = d_z.astype(BF)
        dg_ref[...] = dgates
        dbias_ref[0:1, :] += jnp.sum(dgates, axis=0, keepdims=True)
        pl.when(step == nc - 1)(comm_wait)

    def rev(c):
        return nc - 1 - c

    any_spec = pl.BlockSpec(memory_space=pl.ANY)
    res = pl.pallas_call(
        body, name="mlstm_bwd", grid=(nc,),
        input_output_aliases={0: 0},
        in_specs=[any_spec,
                  pl.BlockSpec((lc, 1024), lambda c: (rev(c), 0)),
                  pl.BlockSpec((lc, 1024), lambda c: (rev(c), 1)),
                  pl.BlockSpec((lc, 2048), lambda c: (rev(c), 1)),
                  pl.BlockSpec((lc, 2048), lambda c: (rev(c), 2)),
                  pl.BlockSpec((lc, 2048), lambda c: (rev(c), 3)),
                  pl.BlockSpec((lc, 128), lambda c: (rev(c), 0)),
                  pl.BlockSpec((1, 128), lambda c: (0, 0)),
                  pl.BlockSpec((1, V_DIM), lambda c: (0, 0)),
                  pl.BlockSpec((lc, V_DIM), lambda c: (rev(c), 0)),
                  pl.BlockSpec((1, N_HEADS, DK, DV), lambda c: (rev(c), 0, 0, 0)),
                  pl.BlockSpec((1, N_HEADS, 8, DK), lambda c: (rev(c), 0, 0, 0)),
                  pl.BlockSpec((lc, V_DIM), lambda c: (rev(c), 0))] + [any_spec] * comm.n,
        out_specs=[pl.BlockSpec((lc, 8192), lambda c: (rev(c), 0)),
                   pl.BlockSpec((lc, 128), lambda c: (rev(c), 0)),
                   pl.BlockSpec((8, 128), lambda c: (0, 0)),
                   pl.BlockSpec((8, V_DIM), lambda c: (0, 0))] + [any_spec] * comm.n,
        out_shape=[jax.ShapeDtypeStruct((t, N_MAIN), BF), jax.ShapeDtypeStruct((t, 128), F32),
                   jax.ShapeDtypeStruct((8, 128), F32), jax.ShapeDtypeStruct((8, V_DIM), F32)] + comm.out_shapes,
        scratch_shapes=[pltpu.VMEM((N_HEADS, DK, DV), F32), pltpu.VMEM((N_HEADS, 8, DK), F32)] + comm.scratch,
        compiler_params=_cparams(("arbitrary",)),
    )(dproj, proj, proj, proj, proj, proj, gates, gate_bias, g_head, h_raw, c_states, aux, dyb_pre, *comm.srcs)
    return res[0], res[1], res[2], res[3], list(res[4:])


GA_BLK, GB_BLK = 12, 13


def _merge_fwd(proj, ya, yb, *, tm=512):
    t = ya.shape[0]
    d = D_MODEL
    tm = min(tm, t)

    def body(ga_ref, gb_ref, ya_ref, yb_ref, o_ref):
        o_ref[...] = (_sigmoid(ga_ref[...].astype(F32)) * ya_ref[...].astype(F32)
                      + _sigmoid(gb_ref[...].astype(F32)) * yb_ref[...].astype(F32)).astype(BF)

    row = pl.BlockSpec((tm, d), lambda i: (i, 0))
    return pl.pallas_call(
        body, name="merge_fwd", grid=(t // tm,),
        in_specs=[pl.BlockSpec((tm, d), lambda i: (i, GA_BLK)), pl.BlockSpec((tm, d), lambda i: (i, GB_BLK)),
                  row, row],
        out_specs=row, out_shape=jax.ShapeDtypeStruct((t, d), BF),
        compiler_params=_cparams(("parallel",)),
    )(proj, proj, ya, yb)


def _merge_bwd(proj, ya, yb, dmerged, *, tm=512):
    t = ya.shape[0]
    d = D_MODEL
    tm = min(tm, t)

    def body(ga_ref, gb_ref, ya_ref, yb_ref, dm_ref, o_ref, dya_ref, dyb_ref):
        dm = dm_ref[...].astype(F32)
        sa = _sigmoid(ga_ref[...].astype(F32))
        sb = _sigmoid(gb_ref[...].astype(F32))
        dya_ref[...] = (dm * sa).astype(BF)
        dyb_ref[...] = (dm * sb).astype(BF)
        o_ref[:, 0:d] = (dm * ya_ref[...].astype(F32) * sa * (1.0 - sa)).astype(BF)
        o_ref[:, d:2 * d] = (dm * yb_ref[...].astype(F32) * sb * (1.0 - sb)).astype(BF)

    row = pl.BlockSpec((tm, d), lambda i: (i, 0))
    return pl.pallas_call(
        body, name="merge_bwd", grid=(t // tm,),
        in_specs=[pl.BlockSpec((tm, d), lambda i: (i, GA_BLK)), pl.BlockSpec((tm, d), lambda i: (i, GB_BLK)),
                  row, row, row],
        out_specs=[pl.BlockSpec((tm, 2 * d), lambda i: (i, 6)), row, row],
        out_shape=[jax.ShapeDtypeStruct((t, N_MAIN), BF), jax.ShapeDtypeStruct((t, d), BF),
                   jax.ShapeDtypeStruct((t, d), BF)],
        compiler_params=_cparams(("parallel",)),
    )(proj, proj, ya, yb, dmerged)


def _final(x1, gpre, pe, target, g_final, *, tm=512):
    t, d = x1.shape
    tm = min(tm, t)

    def body(x_ref, gp_ref, pe_ref, tg_ref, g_ref, dx_ref, dgp_ref, dpe_ref, sm_ref):
        i = pl.program_id(0)
        gate = _sigmoid(gp_ref[...].astype(F32))
        pe_v = pe_ref[...].astype(F32)
        x2 = x_ref[...] + gate * pe_v
        r = lax.rsqrt(jnp.mean(x2 * x2, axis=1, keepdims=True) + EPS)
        xh = x2 * r
        g = g_ref[...]
        err = xh * g - tg_ref[...]
        dy = err * (1.0 / d)
        dxh = dy * g
        dx2 = r * (dxh - xh * jnp.mean(dxh * xh, axis=1, keepdims=True))
        dx_ref[...] = dx2
        dgp_ref[...] = (dx2 * pe_v * gate * (1.0 - gate)).astype(BF)
        dpe_ref[...] = (dx2 * gate).astype(BF)

        @pl.when(i == 0)
        def _():
            sm_ref[...] = jnp.zeros_like(sm_ref)

        sm_ref[0:1, :] += (0.5 / d) * jnp.sum(err * err, axis=0, keepdims=True)
        sm_ref[1:2, :] += jnp.sum(dy * xh, axis=0, keepdims=True)

    row = pl.BlockSpec((tm, d), lambda i: (i, 0))
    return pl.pallas_call(
        body, name="final_loss", grid=(t // tm,),
        in_specs=[row, row, row, row, pl.BlockSpec((1, d), lambda i: (0, 0))],
        out_specs=[row, row, row, pl.BlockSpec((8, d), lambda i: (0, 0))],
        out_shape=[jax.ShapeDtypeStruct((t, d), F32), jax.ShapeDtypeStruct((t, d), BF),
                   jax.ShapeDtypeStruct((t, d), BF), jax.ShapeDtypeStruct((8, d), F32)],
        compiler_params=_cparams(("arbitrary",)),
    )(x1, gpre, pe, target, g_final)


def _position():
    x, y, c = lax.axis_index("x"), lax.axis_index("y"), lax.axis_index("c")
    return x, y, c


def _all_gather(srcs):
    nb = len(srcs)
    any_spec = pl.BlockSpec(memory_space=pl.ANY)

    def body(*refs):
        src = refs[:nb]
        dst = refs[nb:2 * nb]
        send_sems, recv_sems, local_sems = refs[2 * nb:]
        x, y, c = _position()
        me, sibling = (x, y, c), (x, y, 1 - c)
        chips = [(1 - x, y), (x, 1 - y), (1 - x, 1 - y)]

        def slot(b, px, py, pc):
            return dst[b].at[4 * px + 2 * py + pc]

        def copy(k, b, block, to, from_src=False):
            return pltpu.make_async_remote_copy(
                src_ref=src[b] if from_src else slot(b, *block), dst_ref=slot(b, *block),
                send_sem=send_sems.at[b, k], recv_sem=recv_sems.at[b, k],
                device_id=to, device_id_type=MESH)

        mine = [pltpu.make_async_copy(src[b], slot(b, *me), local_sems.at[b]) for b in range(nb)]
        for cp in mine:
            cp.start()
        first = [copy(0, b, me, sibling, True) for b in range(nb)]
        first += [copy(1 + j, b, me, (*chip, c), True) for j, chip in enumerate(chips) for b in range(nb)]
        for cp in first:
            cp.start()
        passed = []
        for j, chip in enumerate(chips):
            for b in range(nb):
                copy(1 + j, b, (*chip, c), me).wait_recv()
                fwd = copy(4 + j, b, (*chip, c), sibling)
                fwd.start()
                passed.append(fwd)
        for b in range(nb):
            copy(0, b, sibling, me).wait_recv()
        for j, chip in enumerate(chips):
            for b in range(nb):
                copy(4 + j, b, (*chip, 1 - c), me).wait_recv()
        for cp in first + passed:
            cp.wait_send()
        for cp in mine:
            cp.wait()

    return pl.pallas_call(
        body, name="weights_all_gather",
        in_specs=[any_spec] * nb, out_specs=[any_spec] * nb,
        out_shape=[jax.ShapeDtypeStruct((N_DEV,) + s.shape, s.dtype) for s in srcs],
        scratch_shapes=[pltpu.SemaphoreType.DMA((nb, 7)), pltpu.SemaphoreType.DMA((nb, 7)),
                        pltpu.SemaphoreType.DMA((nb,))],
    )(*srcs)


def _comm_call(comm, *, name):
    any_spec = pl.BlockSpec(memory_space=pl.ANY)

    def body(*refs):
        start, wait = comm.ops(refs[:comm.n], refs[comm.n:2 * comm.n], *refs[2 * comm.n:])
        start()
        wait()

    return pl.pallas_call(
        body, name=name, in_specs=[any_spec] * comm.n, out_specs=[any_spec] * comm.n,
        out_shape=comm.out_shapes, scratch_shapes=comm.scratch,
    )(*comm.srcs)


def _sum_slots(recv, *, name, tr):
    _, r, cdim = recv.shape
    tr = min(tr, r)

    def body(r_ref, o_ref):
        total = r_ref[0].astype(F32)
        for s in range(1, N_DEV):
            total = total + r_ref[s].astype(F32)
        o_ref[...] = total

    return pl.pallas_call(
        body, name=name, grid=(r // tr,),
        in_specs=[pl.BlockSpec((N_DEV, tr, cdim), lambda i: (0, i, 0))],
        out_specs=pl.BlockSpec((tr, cdim), lambda i: (i, 0)),
        out_shape=jax.ShapeDtypeStruct((r, cdim), F32),
        compiler_params=_cparams(("parallel",)),
    )(recv)


def _adamw(w, g, m, v, *, name):
    r, cdim = w.shape
    tr = 128 if r % 128 == 0 else r
    c1 = 1.0 - ADAM_B1 ** ADAM_STEP
    c2 = 1.0 - ADAM_B2 ** ADAM_STEP

    def body(w_ref, g_ref, m_ref, v_ref, d_ref, mo_ref, vo_ref):
        gv = g_ref[...]
        mn = ADAM_B1 * m_ref[...] + (1.0 - ADAM_B1) * gv
        vn = ADAM_B2 * v_ref[...] + (1.0 - ADAM_B2) * (gv * gv)
        d_ref[...] = -ADAM_LR * ((mn / c1) / (jnp.sqrt(vn / c2) + ADAM_EPS) + ADAM_WD * w_ref[...])
        mo_ref[...] = mn
        vo_ref[...] = vn

    blk = pl.BlockSpec((tr, cdim), lambda i: (i, 0))
    shp = jax.ShapeDtypeStruct((r, cdim), F32)
    return pl.pallas_call(
        body, name=name, grid=(r // tr,),
        in_specs=[blk] * 4, out_specs=[blk] * 3, out_shape=[shp] * 3,
        compiler_params=_cparams(("parallel",)),
    )(w, g, m, v)


def kernel(x, p, g_mix, w_in, conv_w, conv_b, w_a_out, b_gates, g_head, w_b_out, w_o, g_ple, w_ple_gate, w_ple, g_final, loss_target, m_g_mix, m_w_in, m_conv_w, m_conv_b, m_w_a_out, m_b_gates, m_g_head, m_w_b_out, m_w_o, m_g_ple, m_w_ple_gate, m_w_ple, m_g_final, v_g_mix, v_w_in, v_conv_w, v_conv_b, v_w_a_out, v_b_gates, v_g_head, v_w_b_out, v_w_o, v_g_ple, v_w_ple_gate, v_w_ple, v_g_final):
    d = D_MODEL
    xi, yi, ci = _position()
    me = 4 * xi + 2 * yi + ci
    t = x.shape[1]
    x2d = x.reshape(t, d)
    p2d = p.reshape(t, PLE_DIM)
    tgt = loss_target.reshape(t, d)

    win = lax.dynamic_update_slice(jnp.zeros((d, WIN_W), BF), w_in[0].astype(BF), (0, me))
    rows = jnp.concatenate([w_a_out[0].astype(BF), w_b_out[0].astype(BF), w_o[0].astype(BF),
                            w_ple_gate[0].astype(BF), w_ple[0].astype(BF).reshape(32, d)], axis=0)
    cfl = jnp.pad(conv_w[0], ((0, 5), (0, 0)))
    g_win, g_cf = _all_gather([win, cfl])

    parts = []
    for k in range(N_DEV):
        main = g_win[k, :, :WIN_STRIDE]
        if k > 0:
            main = jnp.concatenate([main[:, :128] + g_win[k - 1, :, WIN_STRIDE:], main[:, 128:]], axis=1)
        parts.append(main)
    parts.append(g_win[N_DEV - 1, :, WIN_STRIDE:])
    w_glob = jnp.concatenate(parts, axis=1)
    tail = jnp.roll(w_glob[:, 12288:], -8, axis=1)
    w_main = jnp.concatenate([w_glob[:, 4096:12288], w_glob[:, 0:4096], tail[:, :2048]], axis=1)
    w_gate = tail[:, 2048:]
    conv_w8 = jnp.pad(g_cf[:, :3, :].transpose(1, 0, 2).reshape(3, d), ((0, 5), (0, 0)))
    gate_bias = jnp.pad(b_gates, ((0, 0), (IG_LANE, 0)))

    hn, hnt = _rms_fwd(x2d, g_mix, name="rms_mix")
    proj, (g_rows,) = _mm(hn, w_main, form="nn", out_dtype=BF, name="proj", tm=2048,
                          comm=_DirectComm([rows], "gather"))
    w_a = g_rows[:, 0:128].reshape(d, d)
    w_b = g_rows[:, 128:384].reshape(V_DIM, d)
    w_of = g_rows[:, 384:512].reshape(d, d)
    w_pg = g_rows[:, 512:640].reshape(d, d)
    w_pl = g_rows[:, 640:672].reshape(N_DEV, PLE_DIM, 128).transpose(1, 0, 2).reshape(PLE_DIM, d)
    gates = _mm(hn, w_gate, form="nn", out_dtype=F32, name="proj_gates", tm=2048)
    ya_pre = _branch_a_fwd(proj, conv_w8, conv_b)
    ya = _mm(ya_pre, w_a, form="nn", out_dtype=BF, name="ya", tm=2048)
    yb_pre, h_raw, c_states, aux = _mlstm_fwd(proj, gates, gate_bias, g_head)
    yb = _mm(yb_pre, w_b, form="nn", out_dtype=BF, name="yb", tk=2048)
    merged = _merge_fwd(proj, ya, yb)
    x1 = _mm(merged, w_of, form="nn", out_dtype=F32, name="x1", add=x2d)
    hn2, hn2t = _rms_fwd(x1, g_ple, name="rms_ple")
    gpre = _mm(hn2, w_pg, form="nn", out_dtype=BF, name="gpre", tm=2048)
    pe = _mm(p2d, w_pl, form="nn", out_dtype=BF, name="pe", tm=2048)
    dx2, dgpre, dpe, small_fin = _final(x1, gpre, pe, tgt, g_final.reshape(1, d))

    dw_pg = _mm(hn2t, dgpre, form="nn", out_dtype=BF, name="dw_pg")
    dw_pl = _mm(p2d, dpe, form="tn", out_dtype=BF, name="dw_ple")
    dhn2 = _mm(dgpre, w_pg, form="nt", out_dtype=F32, name="dhn2")
    dx1, dx1b, dg_ple = _rms_bwd(x1, g_ple, dhn2, dx2, name="rms_ple_bwd")
    dmerged = _mm(dx1b, w_of, form="nt", out_dtype=BF, name="dmerged")
    dw_o = _mm(merged, dx1b, form="tn", out_dtype=BF, name="dw_o")
    dproj, dya, dyb = _merge_bwd(proj, ya, yb, dmerged)
    dya_pre = _mm(dya, w_a, form="nt", out_dtype=BF, name="dya_pre")
    dw_a = _mm(ya_pre, dya, form="tn", out_dtype=BF, name="dw_a")
    dyb_pre = _mm(dyb, w_b, form="nt", out_dtype=BF, name="dyb_pre", tn=2048)
    dw_b = _mm(yb_pre, dyb, form="tn", out_dtype=BF, name="dw_b", tm=2048)
    dproj, dconv = _branch_a_bwd(dproj, proj, dya_pre, conv_w8, conv_b)
    s_rows = jnp.concatenate([
        dw_a.reshape(N_DEV, 128, d), dw_b.reshape(N_DEV, 256, d), dw_o.reshape(N_DEV, 128, d),
        dw_pg.reshape(N_DEV, 128, d),
        dw_pl.reshape(PLE_DIM, N_DEV, 128).transpose(1, 0, 2).reshape(N_DEV, 32, d)], axis=1)
    dproj, dgates, dbias, dg_head, (r_rows,) = _mlstm_bwd(
        dproj, proj, gates, gate_bias, g_head, h_raw, c_states, aux, dyb_pre, _DirectComm([s_rows], "exchange"))
    dgates_b = dgates.astype(BF)
    dw_main = _mm(hnt, dproj, form="nn", out_dtype=BF, name="dw_main")
    dw_gate = _mm(hnt, dgates_b, form="nn", out_dtype=BF, name="dw_gate")
    tail_g = jnp.roll(jnp.concatenate([dw_main[:, 12288:], dw_gate], axis=1), 8, axis=1)
    dw_glob = jnp.concatenate([dw_main[:, 8192:12288], dw_main[:, 0:8192], tail_g], axis=1)
    s_win = jnp.stack([dw_glob[:, WIN_STRIDE * j:WIN_STRIDE * j + WIN_W] for j in range(N_DEV)])
    dhn, (r_win,) = _mm(dproj, w_main, form="nt", out_dtype=F32, name="dhn",
                        comm=_DirectComm([s_win], "exchange"))
    dhn = _mm(dgates_b, w_gate, form="nt", out_dtype=F32, name="dhn_gates", add=dhn)
    grad_x, _, dg_mix = _rms_bwd(x2d, g_mix, dhn, dx1, name="rms_mix_bwd")

    vec = jnp.concatenate([dg_mix[0], dconv[3], dg_head[0], dg_ple[0], small_fin[1],
                           dbias[0, IG_LANE:], jnp.zeros((7 * d - 6152,), F32)]).reshape(7, d)
    conv_part = jnp.pad(dconv[:3].reshape(3, N_DEV, 128).transpose(1, 0, 2).reshape(N_DEV, 1, 384),
                        ((0, 0), (0, 0), (0, d - 384)))
    s_f32 = jnp.concatenate([jnp.broadcast_to(vec[None], (N_DEV, 7, d)), conv_part], axis=1)
    (r_f32,) = _comm_call(_DirectComm([s_f32], "exchange"), name="small_grads_exchange")
    sum_win = _sum_slots(r_win, name="sum_win", tr=128)
    sum_rows = _sum_slots(r_rows, name="sum_rows", tr=96)
    sum_f32 = _sum_slots(r_f32, name="sum_f32", tr=8)

    g_w_in = lax.dynamic_slice(sum_win, (0, me), (d, SHARD_W))
    g_w_a = sum_rows[0:128]
    g_w_b = sum_rows[128:384]
    g_w_o = sum_rows[384:512]
    g_w_pg = sum_rows[512:640]
    g_w_pl = sum_rows[640:672].reshape(PLE_DIM, 128)
    vsum = sum_f32[:7].reshape(7 * d)
    g_g_mix = vsum[0:1024].reshape(1, d)
    g_conv_b = vsum[1024:2048].reshape(1, d)
    g_g_head = vsum[2048:4096].reshape(1, V_DIM)
    g_g_ple = vsum[4096:5120].reshape(1, d)
    g_g_final = vsum[5120:6144].reshape(1, d)
    g_b_gates = vsum[6144:6152].reshape(1, 8)
    g_conv_w = sum_f32[7, :384].reshape(3, 128)

    loss = lax.psum(jnp.sum(small_fin[0]), ("x", "y", "c"))

    names = ["g_mix", "w_in", "conv_w", "conv_b", "w_a_out", "b_gates", "g_head", "w_b_out", "w_o", "g_ple",
             "w_ple_gate", "w_ple", "g_final"]
    weights = [g_mix, w_in, conv_w, conv_b, w_a_out, b_gates, g_head, w_b_out, w_o, g_ple, w_ple_gate, w_ple,
               g_final]
    moms = [m_g_mix, m_w_in, m_conv_w, m_conv_b, m_w_a_out, m_b_gates, m_g_head, m_w_b_out, m_w_o, m_g_ple,
            m_w_ple_gate, m_w_ple, m_g_final]
    vels = [v_g_mix, v_w_in, v_conv_w, v_conv_b, v_w_a_out, v_b_gates, v_g_head, v_w_b_out, v_w_o, v_g_ple,
            v_w_ple_gate, v_w_ple, v_g_final]
    grads2d = [g_g_mix, g_w_in, g_conv_w, g_conv_b, g_w_a, g_b_gates, g_g_head, g_w_b, g_w_o, g_g_ple, g_w_pg,
               g_w_pl, g_g_final]
    grads, deltas, new_m, new_v = [], [], [], []
    for nm, w, m_, v_, g2 in zip(names, weights, moms, vels, grads2d):
        shp = w.shape
        w2 = w.reshape(g2.shape)
        dl, mn, vn = _adamw(w2, g2, m_.reshape(g2.shape), v_.reshape(g2.shape), name="adamw_" + nm)
        grads.append(g2.reshape(shp))
        deltas.append(dl.reshape(shp))
        new_m.append(mn.reshape(shp))
        new_v.append(vn.reshape(shp))
    return (loss, grad_x.reshape(x.shape), *grads, *deltas, *new_m, *new_v)
```

```python
import functools

import jax
import jax.numpy as jnp
from jax import lax
from jax.experimental import pallas as pl
from jax.experimental.pallas import tpu as pltpu

F32 = jnp.float32
BF = jnp.bfloat16

D_MODEL = 1024
N_HEADS = 4
DK = 256
DV = 512
V_DIM = 2048
PLE_DIM = 256
N_IN = 14344
N_MAIN = 14336
EPS = 1e-6
QK_SCALE = DK ** -0.5
NEG = -1e30
N_DEV = 8
SHARD_W = 1793
WIN_STRIDE = 1792
WIN_W = 1920
ROWS_PACK = 672
_CHUNK = 256
IG_LANE = 120
FG_LANE = 124

ADAM_LR = 0.001
ADAM_B1 = 0.9
ADAM_B2 = 0.999
ADAM_EPS = 1e-08
ADAM_WD = 0.01
ADAM_STEP = 10

VMEM_LIMIT = 56 * 1024 * 1024
MESH = pl.DeviceIdType.MESH


def _cparams(sem):
    return pltpu.CompilerParams(dimension_semantics=sem, vmem_limit_bytes=VMEM_LIMIT)


def _sigmoid(x):
    return 1.0 / (1.0 + jnp.exp(-x))


def _log_sigmoid(x):
    return jnp.minimum(x, 0.0) - jnp.log(1.0 + jnp.exp(-jnp.abs(x)))


def _dot(a, b):
    return jnp.dot(a, b, preferred_element_type=F32)


def _dot_nt(a, b):
    return lax.dot_general(a, b, (((1,), (1,)), ((), ())), preferred_element_type=F32)


def _dot_tn(a, b):
    return lax.dot_general(a, b, (((0,), (0,)), ((), ())), preferred_element_type=F32)


class _DirectComm:
    def __init__(self, srcs, kind):
        self.srcs = list(srcs)
        self.kind = kind
        self.n = len(self.srcs)
        if kind == "exchange":
            self.out_shapes = [jax.ShapeDtypeStruct(s.shape, s.dtype) for s in self.srcs]
        else:
            self.out_shapes = [jax.ShapeDtypeStruct((N_DEV,) + s.shape, s.dtype) for s in self.srcs]
        self.scratch = [pltpu.SemaphoreType.DMA((self.n, 7)), pltpu.SemaphoreType.DMA((self.n, 7)),
                        pltpu.SemaphoreType.DMA((self.n,))]

    def ops(self, src, dst, send_sems, recv_sems, local_sems):
        exchange = self.kind == "exchange"

        def descriptors():
            x, y, c = _position()
            me_lin = 4 * x + 2 * y + c
            local = [pltpu.make_async_copy(src[b].at[me_lin] if exchange else src[b], dst[b].at[me_lin],
                                           local_sems.at[b]) for b in range(self.n)]
            sends, recvs = [], []
            for f in range(1, N_DEV):
                px = (1 - x) if (f >> 2) & 1 else x
                py = (1 - y) if (f >> 1) & 1 else y
                pc = (1 - c) if f & 1 else c
                peer_lin = 4 * px + 2 * py + pc
                for b in range(self.n):
                    out = src[b].at[peer_lin] if exchange else src[b]
                    common = dict(send_sem=send_sems.at[b, f - 1], recv_sem=recv_sems.at[b, f - 1],
                                  device_id=(px, py, pc), device_id_type=MESH)
                    sends.append(pltpu.make_async_remote_copy(src_ref=out, dst_ref=dst[b].at[me_lin], **common))
                    recvs.append(pltpu.make_async_remote_copy(src_ref=out, dst_ref=dst[b].at[peer_lin], **common))
            return local, sends, recvs

        def start():
            local, sends, _ = descriptors()
            for cp in local + sends:
                cp.start()

        def wait():
            local, sends, recvs = descriptors()
            for cp in recvs:
                cp.wait_recv()
            for cp in sends:
                cp.wait_send()
            for cp in local:
                cp.wait()

        return start, wait


def _mm(a, b, *, form, out_dtype, name, tm=1024, tn=1024, tk=1024, add=None, comm=None):
    if form == "nn":
        m, kc = a.shape
        n = b.shape[1]
    elif form == "nt":
        m, kc = a.shape
        n = b.shape[0]
    else:
        kc, m = a.shape
        n = b.shape[1]
    tm, tn, tk = min(tm, m), min(tn, n), min(tk, kc)
    assert m % tm == 0 and n % tn == 0 and kc % tk == 0, (name, a.shape, b.shape)
    nk = kc // tk
    if form == "tn":
        a_spec = pl.BlockSpec((tk, tm), lambda i, j, k: (k, i))
    else:
        a_spec = pl.BlockSpec((tm, tk), lambda i, j, k: (i, k))
    if form == "nt":
        b_spec = pl.BlockSpec((tn, tk), lambda i, j, k: (j, k))
    else:
        b_spec = pl.BlockSpec((tk, tn), lambda i, j, k: (k, j))
    o_spec = pl.BlockSpec((tm, tn), lambda i, j, k: (i, j))
    dot = {"nn": _dot, "nt": _dot_nt, "tn": _dot_tn}[form]
    has_add = add is not None

    use_acc = nk > 1 and (has_add or out_dtype != F32)
    n_in = 2 + int(has_add) + (comm.n if comm else 0)
    grid = (m // tm, n // tn, nk)

    def body(*refs):
        a_ref, b_ref = refs[0], refs[1]
        add_ref = refs[2] if has_add else None
        o_ref = refs[n_in]
        acc = refs[-1] if use_acc else o_ref
        i, j, k = pl.program_id(0), pl.program_id(1), pl.program_id(2)
        if comm:
            start, wait = comm.ops(refs[n_in - comm.n:n_in], refs[n_in + 1:n_in + 1 + comm.n],
                                   *refs[n_in + 1 + comm.n:n_in + 4 + comm.n])
            pl.when((i == 0) & (j == 0) & (k == 0))(start)

        def part():
            return dot(a_ref[...].astype(BF), b_ref[...].astype(BF))

        def finish(total):
            if has_add:
                total = total + add_ref[...].astype(F32)
            o_ref[...] = total.astype(out_dtype)

        if nk == 1:
            finish(part())
        else:
            @pl.when(k == 0)
            def _():
                acc[...] = part()

            @pl.when(k > 0)
            def _():
                acc[...] += part()

            if use_acc:
                @pl.when(k == nk - 1)
                def _():
                    finish(acc[...])
        if comm:
            pl.when((i == grid[0] - 1) & (j == grid[1] - 1) & (k == nk - 1))(wait)

    any_spec = pl.BlockSpec(memory_space=pl.ANY)
    in_specs = [a_spec, b_spec] + ([o_spec] if has_add else []) + ([any_spec] * comm.n if comm else [])
    args = (a, b) + ((add,) if has_add else ()) + (tuple(comm.srcs) if comm else ())
    out_specs = [o_spec] + ([any_spec] * comm.n if comm else [])
    out_shape = [jax.ShapeDtypeStruct((m, n), out_dtype)] + (comm.out_shapes if comm else [])
    scratch = (comm.scratch if comm else []) + ([pltpu.VMEM((tm, tn), F32)] if use_acc else [])
    sem = ("arbitrary",) * 3 if comm else ("parallel", "parallel", "arbitrary")
    res = pl.pallas_call(
        body, name=name, grid=grid, in_specs=in_specs, out_specs=out_specs, out_shape=out_shape,
        scratch_shapes=scratch, compiler_params=_cparams(sem),
    )(*args)
    return (res[0], list(res[1:])) if comm else res[0]


def _rms_fwd(x, g, *, name, tm=512):
    t, d = x.shape
    tm = min(tm, t)

    def body(x_ref, g_ref, hn_ref, hnt_ref):
        xv = x_ref[...]
        r = lax.rsqrt(jnp.mean(xv * xv, axis=1, keepdims=True) + EPS)
        hn = xv * r * g_ref[...]
        hn_ref[...] = hn.astype(BF)
        hnt_ref[...] = hn.T.astype(BF)

    return pl.pallas_call(
        body, name=name, grid=(t // tm,),
        in_specs=[pl.BlockSpec((tm, d), lambda i: (i, 0)), pl.BlockSpec((1, d), lambda i: (0, 0))],
        out_specs=[pl.BlockSpec((tm, d), lambda i: (i, 0)), pl.BlockSpec((d, tm), lambda i: (0, i))],
        out_shape=[jax.ShapeDtypeStruct((t, d), BF), jax.ShapeDtypeStruct((d, t), BF)],
        compiler_params=_cparams(("parallel",)),
    )(x, g)


HALO = 16
XA_BLK, BA_BLK, CA_BLK, ZA_BLK = 8, 9, 10, 11


def _shift_down(u, prev, n):
    tm = u.shape[0]
    rolled = pltpu.roll(u, n, 0)
    row = lax.broadcasted_iota(jnp.int32, u.shape, 0)
    out = rolled
    for j in range(n):
        out = jnp.where(row == j, prev[HALO - n + j:HALO - n + j + 1, :], out)
    return out


def _shift_up(u, nxt, n):
    tm = u.shape[0]
    rolled = pltpu.roll(u, tm - n, 0)
    row = lax.broadcasted_iota(jnp.int32, u.shape, 0)
    out = rolled
    for j in range(n):
        out = jnp.where(row == tm - n + j, nxt[j:j + 1, :], out)
    return out


def _branch_a_fwd(proj, conv_w8, conv_b, *, tm=512):
    t = proj.shape[0]
    d = D_MODEL
    tm = min(tm, t)
    hb = tm // HALO

    def body(xa_ref, ba_ref, ca_ref, za_ref, xap_ref, cap_ref, w_ref, b_ref, o_ref):
        i = pl.program_id(0)
        u = ca_ref[...].astype(F32) * xa_ref[...].astype(F32)
        up = cap_ref[...].astype(F32) * xap_ref[...].astype(F32)
        up = jnp.where(i == 0, 0.0, up)
        u1 = _shift_down(u, up, 1)
        u2 = _shift_down(u, up, 2)
        cv = w_ref[0:1, :] * u2 + w_ref[1:2, :] * u1 + w_ref[2:3, :] * u + b_ref[...]
        za = za_ref[...].astype(F32)
        o_ref[...] = (ba_ref[...].astype(F32) * cv * (za * _sigmoid(za))).astype(BF)

    def col(blk):
        return pl.BlockSpec((tm, d), lambda i: (i, blk))

    def prev(blk):
        return pl.BlockSpec((HALO, d), lambda i: (jnp.maximum(i * hb - 1, 0), blk))

    return pl.pallas_call(
        body, name="branch_a_fwd", grid=(t // tm,),
        in_specs=[col(XA_BLK), col(BA_BLK), col(CA_BLK), col(ZA_BLK), prev(XA_BLK), prev(CA_BLK),
                  pl.BlockSpec((8, d), lambda i: (0, 0)), pl.BlockSpec((1, d), lambda i: (0, 0))],
        out_specs=pl.BlockSpec((tm, d), lambda i: (i, 0)),
        out_shape=jax.ShapeDtypeStruct((t, d), BF),
        compiler_params=_cparams(("parallel",)),
    )(proj, proj, proj, proj, proj, proj, conv_w8, conv_b)


def _branch_a_bwd(dproj, proj, dya_pre, conv_w8, conv_b, *, tm=512):
    t = proj.shape[0]
    d = D_MODEL
    tm = min(tm, t)
    hb = tm // HALO
    nt = t // tm

    def body(dp_ref, xa_ref, ba_ref, ca_ref, za_ref, xap_ref, cap_ref, dy_ref, ban_ref, zan_ref, dyn_ref,
             w_ref, b_ref, o_ref, dc_ref):
        del dp_ref
        i = pl.program_id(0)
        xa = xa_ref[...].astype(F32)
        ca = ca_ref[...].astype(F32)
        ba = ba_ref[...].astype(F32)
        za = za_ref[...].astype(F32)
        u = ca * xa
        up = cap_ref[...].astype(F32) * xap_ref[...].astype(F32)
        up = jnp.where(i == 0, 0.0, up)
        u1 = _shift_down(u, up, 1)
        u2 = _shift_down(u, up, 2)
        w0, w1, w2 = w_ref[0:1, :], w_ref[1:2, :], w_ref[2:3, :]
        cv = w0 * u2 + w1 * u1 + w2 * u + b_ref[...]
        sg = _sigmoid(za)
        sz = za * sg
        dy = dy_ref[...].astype(F32)
        dcv = dy * ba * sz
        zan = zan_ref[...].astype(F32)
        dcvn = dyn_ref[...].astype(F32) * ban_ref[...].astype(F32) * (zan * _sigmoid(zan))
        dcvn = jnp.where(i == nt - 1, 0.0, dcvn)
        du = w2 * dcv + w1 * _shift_up(dcv, dcvn, 1) + w0 * _shift_up(dcv, dcvn, 2)
        o_ref[:, 0:d] = (du * ca).astype(BF)
        o_ref[:, d:2 * d] = (dy * cv * sz).astype(BF)
        o_ref[:, 2 * d:3 * d] = (du * xa).astype(BF)
        o_ref[:, 3 * d:4 * d] = (dy * ba * cv * sg * (1.0 + za * (1.0 - sg))).astype(BF)

        @pl.when(i == 0)
        def _():
            dc_ref[...] = jnp.zeros_like(dc_ref)

        dc_ref[0:1, :] += jnp.sum(dcv * u2, axis=0, keepdims=True)
        dc_ref[1:2, :] += jnp.sum(dcv * u1, axis=0, keepdims=True)
        dc_ref[2:3, :] += jnp.sum(dcv * u, axis=0, keepdims=True)
        dc_ref[3:4, :] += jnp.sum(dcv, axis=0, keepdims=True)

    def col(blk):
        return pl.BlockSpec((tm, d), lambda i: (i, blk))

    def prev(blk):
        return pl.BlockSpec((HALO, d), lambda i: (jnp.maximum(i * hb - 1, 0), blk))

    def nxt(blk):
        return pl.BlockSpec((HALO, d), lambda i: (jnp.minimum((i + 1) * hb, t // HALO - 1), blk))

    return pl.pallas_call(
        body, name="branch_a_bwd", grid=(nt,),
        in_specs=[pl.BlockSpec(memory_space=pl.ANY),
                  col(XA_BLK), col(BA_BLK), col(CA_BLK), col(ZA_BLK), prev(XA_BLK), prev(CA_BLK),
                  pl.BlockSpec((tm, d), lambda i: (i, 0)), nxt(BA_BLK), nxt(ZA_BLK),
                  pl.BlockSpec((HALO, d), lambda i: (jnp.minimum((i + 1) * hb, t // HALO - 1), 0)),
                  pl.BlockSpec((8, d), lambda i: (0, 0)), pl.BlockSpec((1, d), lambda i: (0, 0))],
        out_specs=[pl.BlockSpec((tm, 4 * d), lambda i: (i, 2)), pl.BlockSpec((8, d), lambda i: (0, 0))],
        out_shape=[jax.ShapeDtypeStruct(dproj.shape, BF), jax.ShapeDtypeStruct((8, d), F32)],
        input_output_aliases={0: 0},
        compiler_params=_cparams(("arbitrary",)),
    )(dproj, proj, proj, proj, proj, proj, proj, dya_pre, proj, proj, dya_pre, conv_w8, conv_b)


def _gate_vectors(gc, gt, h, lane_i, sub_i):
    ig_c = jnp.sum(jnp.where(lane_i == IG_LANE + h, gc, 0.0), axis=1, keepdims=True)
    fg_c = jnp.sum(jnp.where(lane_i == FG_LANE + h, gc, 0.0), axis=1, keepdims=True)
    ig_r = jnp.sum(jnp.where(sub_i == IG_LANE + h, gt, 0.0), axis=0, keepdims=True)
    fg_r = jnp.sum(jnp.where(sub_i == FG_LANE + h, gt, 0.0), axis=0, keepdims=True)
    return ig_c, fg_c, ig_r, fg_r


def _chunk_common(q, k, ig_c, fg_c, ig_r, fg_r, m_prev, n_prev, row, col):
    lf_c = _log_sigmoid(fg_c)
    lf_r = _log_sigmoid(fg_r)
    causal = col <= row
    b_c = jnp.sum(jnp.where(causal, lf_r, 0.0), axis=1, keepdims=True)
    b_r = jnp.sum(jnp.where(row <= col, lf_c, 0.0), axis=0, keepdims=True)
    dmat = jnp.where(causal, b_c - b_r + ig_r, NEG)
    a = b_c + m_prev
    m_row = jnp.maximum(a, jnp.max(dmat, axis=1, keepdims=True))
    est = jnp.exp(dmat - m_row)
    s = _dot_nt(q, k) * QK_SCALE * est
    inter = jnp.exp(a - m_row)
    den = jnp.sum(s, axis=1, keepdims=True) + inter * QK_SCALE * jnp.sum(
        q.astype(F32) * n_prev, axis=1, keepdims=True)
    expm = jnp.exp(-m_row)
    mx = jnp.maximum(jnp.abs(den), expm)
    b_last = jnp.sum(lf_r, axis=1, keepdims=True)
    g_r = b_last - b_r + ig_r
    g_c = b_last - b_c + ig_c
    m_new = jnp.maximum(b_last + m_prev, jnp.max(g_r, axis=1, keepdims=True))
    w_c = jnp.exp(g_c - m_new)
    decay = jnp.exp(b_last + m_prev - m_new)
    return est, s, inter, den, expm, mx, m_new, w_c, decay


def _mlstm_fwd(proj, gates, gate_bias, g_head, w_b):
    t = proj.shape[0]
    lc = min(_CHUNK, t)
    nc = t // lc

    def body(q_ref, k_ref, v_ref, o_ref, z_ref, g_ref, gb_ref, gh_ref, wb_ref,
             yb_ref, ybo_ref, hr_ref, cs_ref, aux_ref, c_scr, nm_scr):
        c = pl.program_id(0)

        @pl.when(c == 0)
        def _():
            c_scr[...] = jnp.zeros_like(c_scr)
            nm_scr[...] = jnp.zeros_like(nm_scr)
            nm_scr[:, 1:2, :] = jnp.full((N_HEADS, 1, DK), NEG, F32)

        row = lax.broadcasted_iota(jnp.int32, (lc, lc), 0)
        col = lax.broadcasted_iota(jnp.int32, (lc, lc), 1)
        gc = g_ref[...] + gb_ref[...]
        gt = gc.T
        lane_i = lax.broadcasted_iota(jnp.int32, gc.shape, 1)
        sub_i = lax.broadcasted_iota(jnp.int32, gt.shape, 0)
        for h in range(N_HEADS):
            ks = slice(h * DK, (h + 1) * DK)
            vs = slice(h * DV, (h + 1) * DV)
            q = q_ref[:, ks]
            k = k_ref[:, ks]
            v = v_ref[:, vs]
            ig_c, fg_c, ig_r, fg_r = _gate_vectors(gc, gt, h, lane_i, sub_i)
            n_prev = nm_scr[h, 0:1, :]
            m_prev = nm_scr[h, 1:2, 0:1]
            est, s, inter, den, expm, mx, m_new, w_c, decay = _chunk_common(
                q, k, ig_c, fg_c, ig_r, fg_r, m_prev, n_prev, row, col)
            c_prev = c_scr[h]
            c_prev_b = c_prev.astype(BF)
            num = _dot(s.astype(BF), v) + (inter * QK_SCALE) * _dot(q, c_prev_b)
            hh = num / mx
            r = lax.rsqrt(jnp.mean(hh * hh, axis=1, keepdims=True) + EPS)
            hbn = hh * r * gh_ref[:, vs]
            o = o_ref[:, vs].astype(F32)
            z = z_ref[:, vs].astype(F32)
            yb_ref[:, vs] = (_sigmoid(o) * hbn * (z * _sigmoid(z))).astype(BF)
            hr_ref[:, vs] = hh.astype(BF)
            cs_ref[0, h] = c_prev_b
            aux_ref[0, h] = nm_scr[h]
            kw = k.astype(F32) * w_c
            c_scr[h] = decay * c_prev + _dot_tn(kw.astype(BF), v)
            nm_scr[h, 0:1, :] = decay * n_prev + jnp.sum(kw, axis=0, keepdims=True)
            nm_scr[h, 1:2, :] = jnp.broadcast_to(m_new, (1, DK))
        ybo_ref[...] = _dot(yb_ref[...], wb_ref[...]).astype(BF)

    return pl.pallas_call(
        body, name="mlstm_fwd", grid=(nc,),
        in_specs=[pl.BlockSpec((lc, 1024), lambda c: (c, 0)),
                  pl.BlockSpec((lc, 1024), lambda c: (c, 1)),
                  pl.BlockSpec((lc, 2048), lambda c: (c, 1)),
                  pl.BlockSpec((lc, 2048), lambda c: (c, 2)),
                  pl.BlockSpec((lc, 2048), lambda c: (c, 3)),
                  pl.BlockSpec((lc, 128), lambda c: (c, 0)),
                  pl.BlockSpec((1, 128), lambda c: (0, 0)),
                  pl.BlockSpec((1, V_DIM), lambda c: (0, 0)),
                  pl.BlockSpec((V_DIM, D_MODEL), lambda c: (0, 0))],
        out_specs=[pl.BlockSpec((lc, V_DIM), lambda c: (c, 0)),
                   pl.BlockSpec((lc, D_MODEL), lambda c: (c, 0)),
                   pl.BlockSpec((lc, V_DIM), lambda c: (c, 0)),
                   pl.BlockSpec((1, N_HEADS, DK, DV), lambda c: (c, 0, 0, 0)),
                   pl.BlockSpec((1, N_HEADS, 8, DK), lambda c: (c, 0, 0, 0))],
        out_shape=[jax.ShapeDtypeStruct((t, V_DIM), BF), jax.ShapeDtypeStruct((t, D_MODEL), BF),
                   jax.ShapeDtypeStruct((t, V_DIM), BF),
                   jax.ShapeDtypeStruct((nc, N_HEADS, DK, DV), BF),
                   jax.ShapeDtypeStruct((nc, N_HEADS, 8, DK), F32)],
        scratch_shapes=[pltpu.VMEM((N_HEADS, DK, DV), F32), pltpu.VMEM((N_HEADS, 8, DK), F32)],
        compiler_params=_cparams(("arbitrary",)),
    )(proj, proj, proj, proj, proj, gates, gate_bias, g_head, w_b)


def _mlstm_bwd(dproj, proj, gates, gate_bias, g_head, h_raw, c_states, aux, yb_pre, dyb, w_b):
    t = proj.shape[0]
    lc = min(_CHUNK, t)
    nc = t // lc

    def body(dpin_ref, q_ref, k_ref, v_ref, o_ref, z_ref, g_ref, gb_ref, gh_ref, hr_ref, cs_ref, aux_ref,
             ybp_ref, dyb_ref, wb_ref, dp_ref, dg_ref, dbias_ref, dgh_ref, dwb_ref, dc_scr, dn_scr, dy_ref,
             dwb_scr):
        del dpin_ref
        step = pl.program_id(0)

        @pl.when(step == 0)
        def _():
            dc_scr[...] = jnp.zeros_like(dc_scr)
            dn_scr[...] = jnp.zeros_like(dn_scr)
            dbias_ref[...] = jnp.zeros_like(dbias_ref)
            dgh_ref[...] = jnp.zeros_like(dgh_ref)
            dwb_scr[...] = jnp.zeros_like(dwb_scr)

        dyb = dyb_ref[...]
        dy_ref[...] = _dot_nt(dyb, wb_ref[...])
        dwb_scr[...] += _dot_tn(ybp_ref[...], dyb)

        row = lax.broadcasted_iota(jnp.int32, (lc, lc), 0)
        col = lax.broadcasted_iota(jnp.int32, (lc, lc), 1)
        eye = row == col
        gc = g_ref[...] + gb_ref[...]
        gt = gc.T
        lane_i = lax.broadcasted_iota(jnp.int32, gc.shape, 1)
        sub_i = lax.broadcasted_iota(jnp.int32, gt.shape, 0)
        dgates = jnp.zeros(gc.shape, F32)
        for h in range(N_HEADS):
            ks = slice(h * DK, (h + 1) * DK)
            vs = slice(h * DV, (h + 1) * DV)
            q = q_ref[:, ks]
            k = k_ref[:, ks]
            v = v_ref[:, vs]
            ig_c, fg_c, ig_r, fg_r = _gate_vectors(gc, gt, h, lane_i, sub_i)
            n_prev = aux_ref[0, h, 0:1, :]
            m_prev = aux_ref[0, h, 1:2, 0:1]
            c_prev_b = cs_ref[0, h]
            est, s, inter, den, expm, mx, m_new, w_c, decay = _chunk_common(
                q, k, ig_c, fg_c, ig_r, fg_r, m_prev, n_prev, row, col)
            hb = hr_ref[:, vs].astype(F32)
            dyp = dy_ref[:, vs]
            o = o_ref[:, vs].astype(F32)
            z = z_ref[:, vs].astype(F32)
            so = _sigmoid(o)
            sgz = _sigmoid(z)
            sz = z * sgz
            r = lax.rsqrt(jnp.mean(hb * hb, axis=1, keepdims=True) + EPS)
            xh = hb * r
            gh = gh_ref[:, vs]
            hbn = xh * gh
            d_o = dyp * hbn * sz * so * (1.0 - so)
            d_z = dyp * so * hbn * sgz * (1.0 + z * (1.0 - sgz))
            dhbn = dyp * so * sz
            dgh_ref[0:1, vs] += jnp.sum(dhbn * xh, axis=0, keepdims=True)
            dxh = dhbn * gh
            dh = r * (dxh - xh * jnp.mean(dxh * xh, axis=1, keepdims=True))
            dnm = dh / mx
            hd = jnp.sum(dh * hb, axis=1, keepdims=True)
            cond = jnp.abs(den) > expm
            dden = jnp.where(cond, -hd / mx * jnp.sign(den), 0.0)
            dnm_b = dnm.astype(BF)
            p = _dot_nt(dnm_b, v) + dden
            dqk = (p * est * QK_SCALE).astype(BF)
            dq_inter = (inter * QK_SCALE) * (_dot_nt(dnm_b, c_prev_b) + dden * n_prev)
            dq = _dot(dqk, k) + dq_inter
            dc_new = dc_scr[h]
            dc_new_b = dc_new.astype(BF)
            dn_new = dn_scr[h, 0:1, :]
            kf = k.astype(F32)
            dk_state = w_c * (_dot_nt(v, dc_new_b) + dn_new)
            dk = _dot_tn(dqk, q) + dk_state
            dv = _dot_tn(s.astype(BF), dnm_b) + w_c * _dot(k, dc_new_b)
            dv1_r = jnp.sum(s * dden, axis=0, keepdims=True)
            dv1_c = (jnp.sum(jnp.where(eye, dv1_r, 0.0), axis=1, keepdims=True)
                     + w_c * jnp.sum(kf * dn_new, axis=1, keepdims=True))
            dli_c = jnp.sum(v.astype(F32) * dv, axis=1, keepdims=True) + dv1_c
            hmat = _dot((p * s).astype(BF), (row < col).astype(BF))
            from_prev_c = jnp.sum(q.astype(F32) * dq_inter, axis=1, keepdims=True)
            to_next_c = jnp.sum(kf * dk_state, axis=1, keepdims=True)
            through = decay * (
                jnp.sum(jnp.sum(dc_new * c_prev_b.astype(F32), axis=1, keepdims=True), axis=0, keepdims=True)
                + jnp.sum(dn_new * n_prev, axis=1, keepdims=True))
            dlf_r = through + jnp.sum(jnp.where(row >= col, hmat + from_prev_c, to_next_c), axis=0, keepdims=True)
            dlf_c = jnp.sum(jnp.where(eye, dlf_r, 0.0), axis=1, keepdims=True)
            dfg_c = dlf_c * _sigmoid(-fg_c)
            dgates = dgates + jnp.where(lane_i == IG_LANE + h, dli_c, 0.0) + jnp.where(
                lane_i == FG_LANE + h, dfg_c, 0.0)
            qi = q.astype(F32) * (inter * QK_SCALE)
            dc_scr[h] = decay * dc_new + _dot_tn(qi.astype(BF), dnm_b)
            dn_scr[h, 0:1, :] = decay * dn_new + jnp.sum(qi * dden, axis=0, keepdims=True)
            dp_ref[:, h * DK:(h + 1) * DK] = dq.astype(BF)
            dp_ref[:, 1024 + h * DK:1024 + (h + 1) * DK] = dk.astype(BF)
            dp_ref[:, 2048 + h * DV:2048 + (h + 1) * DV] = dv.astype(BF)
            dp_ref[:, 4096 + h * DV:4096 + (h + 1) * DV] = d_o.astype(BF)
            dp_ref[:, 6144 + h * DV:6144 + (h + 1) * DV] = d_z.astype(BF)
        dg_ref[...] = dgates
        dbias_ref[0:1, :] += jnp.sum(dgates, axis=0, keepdims=True)

        @pl.when(step == nc - 1)
        def _():
            dwb_ref[...] = dwb_scr[...].astype(BF)

    def rev(c):
        return nc - 1 - c

    return pl.pallas_call(
        body, name="mlstm_bwd", grid=(nc,),
        input_output_aliases={0: 0},
        in_specs=[pl.BlockSpec(memory_space=pl.ANY),
                  pl.BlockSpec((lc, 1024), lambda c: (rev(c), 0)),
                  pl.BlockSpec((lc, 1024), lambda c: (rev(c), 1)),
                  pl.BlockSpec((lc, 2048), lambda c: (rev(c), 1)),
                  pl.BlockSpec((lc, 2048), lambda c: (rev(c), 2)),
                  pl.BlockSpec((lc, 2048), lambda c: (rev(c), 3)),
                  pl.BlockSpec((lc, 128), lambda c: (rev(c), 0)),
                  pl.BlockSpec((1, 128), lambda c: (0, 0)),
                  pl.BlockSpec((1, V_DIM), lambda c: (0, 0)),
                  pl.BlockSpec((lc, V_DIM), lambda c: (rev(c), 0)),
                  pl.BlockSpec((1, N_HEADS, DK, DV), lambda c: (rev(c), 0, 0, 0)),
                  pl.BlockSpec((1, N_HEADS, 8, DK), lambda c: (rev(c), 0, 0, 0)),
                  pl.BlockSpec((lc, V_DIM), lambda c: (rev(c), 0)),
                  pl.BlockSpec((lc, D_MODEL), lambda c: (rev(c), 0)),
                  pl.BlockSpec((V_DIM, D_MODEL), lambda c: (0, 0))],
        out_specs=[pl.BlockSpec((lc, 8192), lambda c: (rev(c), 0)),
                   pl.BlockSpec((lc, 128), lambda c: (rev(c), 0)),
                   pl.BlockSpec((8, 128), lambda c: (0, 0)),
                   pl.BlockSpec((8, V_DIM), lambda c: (0, 0)),
                   pl.BlockSpec((V_DIM, D_MODEL), lambda c: (0, 0))],
        out_shape=[jax.ShapeDtypeStruct((t, N_MAIN), BF), jax.ShapeDtypeStruct((t, 128), F32),
                   jax.ShapeDtypeStruct((8, 128), F32), jax.ShapeDtypeStruct((8, V_DIM), F32),
                   jax.ShapeDtypeStruct((V_DIM, D_MODEL), BF)],
        scratch_shapes=[pltpu.VMEM((N_HEADS, DK, DV), F32), pltpu.VMEM((N_HEADS, 8, DK), F32),
                        pltpu.VMEM((lc, V_DIM), F32), pltpu.VMEM((V_DIM, D_MODEL), F32)],
        compiler_params=_cparams(("arbitrary",)),
    )(dproj, proj, proj, proj, proj, proj, gates, gate_bias, g_head, h_raw, c_states, aux, yb_pre, dyb, w_b)


GA_BLK, GB_BLK = 12, 13


def _full(shape):
    return pl.BlockSpec(shape, lambda i: (0,) * len(shape))


def _rms_backward(xv, g, dhn, dres):
    r = lax.rsqrt(jnp.mean(xv * xv, axis=1, keepdims=True) + EPS)
    xh = xv * r
    dxh = dhn * g
    dx = dres + r * (dxh - xh * jnp.mean(dxh * xh, axis=1, keepdims=True))
    return dx, jnp.sum(dhn * xh, axis=0, keepdims=True)


def _mid_fwd(ya_pre, w_a, yb, proj, w_o, x, g_ple, *, tm=512):
    t, d = x.shape
    tm = min(tm, t)

    def body(yap_ref, wa_ref, yb_ref, ga_ref, gb_ref, wo_ref, x_ref, g_ref,
             ya_ref, mg_ref, x1_ref, hn_ref, hnt_ref):
        ya = _dot(yap_ref[...], wa_ref[...]).astype(BF)
        ya_ref[...] = ya
        merged = (_sigmoid(ga_ref[...].astype(F32)) * ya.astype(F32)
                  + _sigmoid(gb_ref[...].astype(F32)) * yb_ref[...].astype(F32)).astype(BF)
        mg_ref[...] = merged
        x1 = _dot(merged, wo_ref[...]) + x_ref[...]
        x1_ref[...] = x1
        r = lax.rsqrt(jnp.mean(x1 * x1, axis=1, keepdims=True) + EPS)
        hn = x1 * r * g_ref[...]
        hn_ref[...] = hn.astype(BF)
        hnt_ref[...] = hn.T.astype(BF)

    row = pl.BlockSpec((tm, d), lambda i: (i, 0))
    return pl.pallas_call(
        body, name="mid_fwd", grid=(t // tm,),
        in_specs=[row, _full((d, d)), row, pl.BlockSpec((tm, d), lambda i: (i, GA_BLK)),
                  pl.BlockSpec((tm, d), lambda i: (i, GB_BLK)), _full((d, d)), row, _full((1, d))],
        out_specs=[row, row, row, row, pl.BlockSpec((d, tm), lambda i: (0, i))],
        out_shape=[jax.ShapeDtypeStruct((t, d), BF), jax.ShapeDtypeStruct((t, d), BF),
                   jax.ShapeDtypeStruct((t, d), F32), jax.ShapeDtypeStruct((t, d), BF),
                   jax.ShapeDtypeStruct((d, t), BF)],
        compiler_params=_cparams(("parallel",)),
    )(ya_pre, w_a, yb, proj, proj, w_o, x, g_ple)


def _final_fwd_bwd(hn2, w_pg, p, w_pl, x1, target, g_final, *, tm=512):
    t, d = x1.shape
    tm = min(tm, t)

    def body(hn_ref, wpg_ref, p_ref, wpl_ref, x_ref, tg_ref, g_ref, dx_ref, dgp_ref, dpe_ref, sm_ref):
        i = pl.program_id(0)
        gate = _sigmoid(_dot(hn_ref[...], wpg_ref[...]))
        pe_v = _dot(p_ref[...].astype(BF), wpl_ref[...])
        x2 = x_ref[...] + gate * pe_v
        r = lax.rsqrt(jnp.mean(x2 * x2, axis=1, keepdims=True) + EPS)
        xh = x2 * r
        g = g_ref[...]
        err = xh * g - tg_ref[...]
        dy = err * (1.0 / d)
        dxh = dy * g
        dx2 = r * (dxh - xh * jnp.mean(dxh * xh, axis=1, keepdims=True))
        dx_ref[...] = dx2
        dgp_ref[...] = (dx2 * pe_v * gate * (1.0 - gate)).astype(BF)
        dpe_ref[...] = (dx2 * gate).astype(BF)

        @pl.when(i == 0)
        def _():
            sm_ref[...] = jnp.zeros_like(sm_ref)

        sm_ref[0:1, :] += (0.5 / d) * jnp.sum(err * err, axis=0, keepdims=True)
        sm_ref[1:2, :] += jnp.sum(dy * xh, axis=0, keepdims=True)

    row = pl.BlockSpec((tm, d), lambda i: (i, 0))
    return pl.pallas_call(
        body, name="final_fwd_bwd", grid=(t // tm,),
        in_specs=[row, _full((d, d)), pl.BlockSpec((tm, PLE_DIM), lambda i: (i, 0)), _full((PLE_DIM, d)),
                  row, row, _full((1, d))],
        out_specs=[row, row, row, _full((8, d))],
        out_shape=[jax.ShapeDtypeStruct((t, d), F32), jax.ShapeDtypeStruct((t, d), BF),
                   jax.ShapeDtypeStruct((t, d), BF), jax.ShapeDtypeStruct((8, d), F32)],
        compiler_params=_cparams(("arbitrary",)),
    )(hn2, w_pg, p, w_pl, x1, target, g_final)


def _ple_bwd(dgpre, w_pg, x1, g_ple, dx2, *, tm=512):
    t, d = x1.shape
    tm = min(tm, t)

    def body(dgp_ref, wpg_ref, x_ref, g_ref, dres_ref, dx_ref, dxb_ref, dg_ref):
        i = pl.program_id(0)
        dhn = _dot_nt(dgp_ref[...], wpg_ref[...])
        dx, dg = _rms_backward(x_ref[...], g_ref[...], dhn, dres_ref[...])
        dx_ref[...] = dx
        dxb_ref[...] = dx.astype(BF)

        @pl.when(i == 0)
        def _():
            dg_ref[...] = jnp.zeros_like(dg_ref)

        dg_ref[0:1, :] += dg

    row = pl.BlockSpec((tm, d), lambda i: (i, 0))
    return pl.pallas_call(
        body, name="ple_bwd", grid=(t // tm,),
        in_specs=[row, _full((d, d)), row, _full((1, d)), row],
        out_specs=[row, row, _full((8, d))],
        out_shape=[jax.ShapeDtypeStruct((t, d), F32), jax.ShapeDtypeStruct((t, d), BF),
                   jax.ShapeDtypeStruct((8, d), F32)],
        compiler_params=_cparams(("arbitrary",)),
    )(dgpre, w_pg, x1, g_ple, dx2)


def _mid_bwd(dx1b, w_o, proj, ya, yb, w_a, *, tm=512):
    t, d = dx1b.shape
    tm = min(tm, t)

    def body(dx_ref, wo_ref, ga_ref, gb_ref, ya_ref, yb_ref, wa_ref, o_ref, dya_ref, dyb_ref, dyap_ref):
        dm = _dot_nt(dx_ref[...], wo_ref[...])
        sa = _sigmoid(ga_ref[...].astype(F32))
        sb = _sigmoid(gb_ref[...].astype(F32))
        dya = (dm * sa).astype(BF)
        dya_ref[...] = dya
        dyb_ref[...] = (dm * sb).astype(BF)
        o_ref[:, 0:d] = (dm * ya_ref[...].astype(F32) * sa * (1.0 - sa)).astype(BF)
        o_ref[:, d:2 * d] = (dm * yb_ref[...].astype(F32) * sb * (1.0 - sb)).astype(BF)
        dyap_ref[...] = _dot_nt(dya, wa_ref[...]).astype(BF)

    row = pl.BlockSpec((tm, d), lambda i: (i, 0))
    return pl.pallas_call(
        body, name="mid_bwd", grid=(t // tm,),
        in_specs=[row, _full((d, d)), pl.BlockSpec((tm, d), lambda i: (i, GA_BLK)),
                  pl.BlockSpec((tm, d), lambda i: (i, GB_BLK)), row, row, _full((d, d))],
        out_specs=[pl.BlockSpec((tm, 2 * d), lambda i: (i, 6)), row, row, row],
        out_shape=[jax.ShapeDtypeStruct((t, N_MAIN), BF), jax.ShapeDtypeStruct((t, d), BF),
                   jax.ShapeDtypeStruct((t, d), BF), jax.ShapeDtypeStruct((t, d), BF)],
        compiler_params=_cparams(("parallel",)),
    )(dx1b, w_o, proj, proj, ya, yb, w_a)


def _dhn_mix_bwd(dproj, w_main, dgates, w_gate, x, g_mix, dx1, comm, *, tm=1024, tk=1024):
    t, d = x.shape
    tm = min(tm, t)
    nk = N_MAIN // tk
    ni = t // tm
    n_in = 7 + comm.n

    def body(*refs):
        dp_ref, w_ref, dgt_ref, wg_ref, x_ref, g_ref, dres_ref = refs[:7]
        dx_ref, dg_ref = refs[n_in], refs[n_in + 1]
        acc = refs[-1]
        start, wait = comm.ops(refs[7:n_in], refs[n_in + 2:n_in + 2 + comm.n], *refs[n_in + 2 + comm.n:-1])
        i, k = pl.program_id(0), pl.program_id(1)
        pl.when((i == 0) & (k == 0))(start)

        @pl.when(k == 0)
        def _():
            acc[...] = _dot_nt(dp_ref[...], w_ref[...]) + _dot_nt(dgt_ref[...], wg_ref[...])

        @pl.when(k > 0)
        def _():
            acc[...] += _dot_nt(dp_ref[...], w_ref[...])

        @pl.when((i == 0) & (k == 0))
        def _():
            dg_ref[...] = jnp.zeros_like(dg_ref)

        @pl.when(k == nk - 1)
        def _():
            dx, dg = _rms_backward(x_ref[...], g_ref[...], acc[...], dres_ref[...])
            dx_ref[...] = dx
            dg_ref[0:1, :] += dg

        pl.when((i == ni - 1) & (k == nk - 1))(wait)

    any_spec = pl.BlockSpec(memory_space=pl.ANY)
    row = pl.BlockSpec((tm, d), lambda i, k: (i, 0))
    res = pl.pallas_call(
        body, name="dhn_mix_bwd", grid=(ni, nk),
        in_specs=[pl.BlockSpec((tm, tk), lambda i, k: (i, k)), pl.BlockSpec((d, tk), lambda i, k: (0, k)),
                  pl.BlockSpec((tm, 128), lambda i, k: (i, 0)), pl.BlockSpec((d, 128), lambda i, k: (0, 0)),
                  row, pl.BlockSpec((1, d), lambda i, k: (0, 0)), row] + [any_spec] * comm.n,
        out_specs=[row, pl.BlockSpec((8, d), lambda i, k: (0, 0))] + [any_spec] * comm.n,
        out_shape=[jax.ShapeDtypeStruct((t, d), F32), jax.ShapeDtypeStruct((8, d), F32)] + comm.out_shapes,
        scratch_shapes=comm.scratch + [pltpu.VMEM((tm, d), F32)],
        compiler_params=_cparams(("arbitrary", "arbitrary")),
    )(dproj, w_main, dgates, w_gate, x, g_mix, dx1, *comm.srcs)
    return res[0], res[1], list(res[2:])


def _position():
    x, y, c = lax.axis_index("x"), lax.axis_index("y"), lax.axis_index("c")
    return x, y, c


def _all_gather(srcs):
    nb = len(srcs)
    any_spec = pl.BlockSpec(memory_space=pl.ANY)

    def body(*refs):
        src = refs[:nb]
        dst = refs[nb:2 * nb]
        send_sems, recv_sems, local_sems = refs[2 * nb:]
        x, y, c = _position()
        me, sibling = (x, y, c), (x, y, 1 - c)
        chips = [(1 - x, y), (x, 1 - y), (1 - x, 1 - y)]

        def slot(b, px, py, pc):
            return dst[b].at[4 * px + 2 * py + pc]

        def copy(k, b, block, to, from_src=False):
            return pltpu.make_async_remote_copy(
                src_ref=src[b] if from_src else slot(b, *block), dst_ref=slot(b, *block),
                send_sem=send_sems.at[b, k], recv_sem=recv_sems.at[b, k],
                device_id=to, device_id_type=MESH)

        mine = [pltpu.make_async_copy(src[b], slot(b, *me), local_sems.at[b]) for b in range(nb)]
        for cp in mine:
            cp.start()
        first = [copy(0, b, me, sibling, True) for b in range(nb)]
        first += [copy(1 + j, b, me, (*chip, c), True) for j, chip in enumerate(chips) for b in range(nb)]
        for cp in first:
            cp.start()
        passed = []
        for j, chip in enumerate(chips):
            for b in range(nb):
                copy(1 + j, b, (*chip, c), me).wait_recv()
                fwd = copy(4 + j, b, (*chip, c), sibling)
                fwd.start()
                passed.append(fwd)
        for b in range(nb):
            copy(0, b, sibling, me).wait_recv()
        for j, chip in enumerate(chips):
            for b in range(nb):
                copy(4 + j, b, (*chip, 1 - c), me).wait_recv()
        for cp in first + passed:
            cp.wait_send()
        for cp in mine:
            cp.wait()

    return pl.pallas_call(
        body, name="weights_all_gather",
        in_specs=[any_spec] * nb, out_specs=[any_spec] * nb,
        out_shape=[jax.ShapeDtypeStruct((N_DEV,) + s.shape, s.dtype) for s in srcs],
        scratch_shapes=[pltpu.SemaphoreType.DMA((nb, 7)), pltpu.SemaphoreType.DMA((nb, 7)),
                        pltpu.SemaphoreType.DMA((nb,))],
    )(*srcs)


def _comm_call(comm, *, name):
    any_spec = pl.BlockSpec(memory_space=pl.ANY)

    def body(*refs):
        start, wait = comm.ops(refs[:comm.n], refs[comm.n:2 * comm.n], *refs[2 * comm.n:])
        start()
        wait()

    return pl.pallas_call(
        body, name=name, in_specs=[any_spec] * comm.n, out_specs=[any_spec] * comm.n,
        out_shape=comm.out_shapes, scratch_shapes=comm.scratch,
    )(*comm.srcs)


def _sum_slots(recv, *, name, tr):
    _, r, cdim = recv.shape
    tr = min(tr, r)

    def body(r_ref, o_ref):
        total = r_ref[0].astype(F32)
        for s in range(1, N_DEV):
            total = total + r_ref[s].astype(F32)
        o_ref[...] = total

    return pl.pallas_call(
        body, name=name, grid=(r // tr,),
        in_specs=[pl.BlockSpec((N_DEV, tr, cdim), lambda i: (0, i, 0))],
        out_specs=pl.BlockSpec((tr, cdim), lambda i: (i, 0)),
        out_shape=jax.ShapeDtypeStruct((r, cdim), F32),
        compiler_params=_cparams(("parallel",)),
    )(recv)


def _adamw(w, g, m, v, *, name):
    r, cdim = w.shape
    tr = 128 if r % 128 == 0 else r
    c1 = 1.0 - ADAM_B1 ** ADAM_STEP
    c2 = 1.0 - ADAM_B2 ** ADAM_STEP

    def body(w_ref, g_ref, m_ref, v_ref, d_ref, mo_ref, vo_ref):
        gv = g_ref[...]
        mn = ADAM_B1 * m_ref[...] + (1.0 - ADAM_B1) * gv
        vn = ADAM_B2 * v_ref[...] + (1.0 - ADAM_B2) * (gv * gv)
        d_ref[...] = -ADAM_LR * ((mn / c1) / (jnp.sqrt(vn / c2) + ADAM_EPS) + ADAM_WD * w_ref[...])
        mo_ref[...] = mn
        vo_ref[...] = vn

    blk = pl.BlockSpec((tr, cdim), lambda i: (i, 0))
    shp = jax.ShapeDtypeStruct((r, cdim), F32)
    return pl.pallas_call(
        body, name=name, grid=(r // tr,),
        in_specs=[blk] * 4, out_specs=[blk] * 3, out_shape=[shp] * 3,
        compiler_params=_cparams(("parallel",)),
    )(w, g, m, v)


def kernel(x, p, g_mix, w_in, conv_w, conv_b, w_a_out, b_gates, g_head, w_b_out, w_o, g_ple, w_ple_gate, w_ple, g_final, loss_target, m_g_mix, m_w_in, m_conv_w, m_conv_b, m_w_a_out, m_b_gates, m_g_head, m_w_b_out, m_w_o, m_g_ple, m_w_ple_gate, m_w_ple, m_g_final, v_g_mix, v_w_in, v_conv_w, v_conv_b, v_w_a_out, v_b_gates, v_g_head, v_w_b_out, v_w_o, v_g_ple, v_w_ple_gate, v_w_ple, v_g_final):
    d = D_MODEL
    xi, yi, ci = _position()
    me = 4 * xi + 2 * yi + ci
    t = x.shape[1]
    x2d = x.reshape(t, d)
    p2d = p.reshape(t, PLE_DIM)
    tgt = loss_target.reshape(t, d)

    win = lax.dynamic_update_slice(jnp.zeros((d, WIN_W), BF), w_in[0].astype(BF), (0, me))
    rows = jnp.concatenate([w_a_out[0].astype(BF), w_b_out[0].astype(BF), w_o[0].astype(BF),
                            w_ple_gate[0].astype(BF), w_ple[0].astype(BF).reshape(32, d)], axis=0)
    cfl = jnp.pad(conv_w[0], ((0, 5), (0, 0)))
    g_win, g_cf = _all_gather([win, cfl])

    parts = []
    for k in range(N_DEV):
        main = g_win[k, :, :WIN_STRIDE]
        if k > 0:
            main = jnp.concatenate([main[:, :128] + g_win[k - 1, :, WIN_STRIDE:], main[:, 128:]], axis=1)
        parts.append(main)
    parts.append(g_win[N_DEV - 1, :, WIN_STRIDE:])
    w_glob = jnp.concatenate(parts, axis=1)
    tail = jnp.roll(w_glob[:, 12288:], -8, axis=1)
    w_main = jnp.concatenate([w_glob[:, 4096:12288], w_glob[:, 0:4096], tail[:, :2048]], axis=1)
    w_gate = tail[:, 2048:]
    conv_w8 = jnp.pad(g_cf[:, :3, :].transpose(1, 0, 2).reshape(3, d), ((0, 5), (0, 0)))
    gate_bias = jnp.pad(b_gates, ((0, 0), (IG_LANE, 0)))

    hn, hnt = _rms_fwd(x2d, g_mix, name="rms_mix")
    proj, (g_rows,) = _mm(hn, w_main, form="nn", out_dtype=BF, name="proj", tm=2048,
                          comm=_DirectComm([rows], "gather"))
    w_a = g_rows[:, 0:128].reshape(d, d)
    w_b = g_rows[:, 128:384].reshape(V_DIM, d)
    w_of = g_rows[:, 384:512].reshape(d, d)
    w_pg = g_rows[:, 512:640].reshape(d, d)
    w_pl = g_rows[:, 640:672].reshape(N_DEV, PLE_DIM, 128).transpose(1, 0, 2).reshape(PLE_DIM, d)
    gates = _mm(hn, w_gate, form="nn", out_dtype=F32, name="proj_gates", tm=2048)
    ya_pre = _branch_a_fwd(proj, conv_w8, conv_b)
    yb_pre, yb, h_raw, c_states, aux = _mlstm_fwd(proj, gates, gate_bias, g_head, w_b)
    ya, merged, x1, hn2, hn2t = _mid_fwd(ya_pre, w_a, yb, proj, w_of, x2d, g_ple)
    dx2, dgpre, dpe, small_fin = _final_fwd_bwd(hn2, w_pg, p2d, w_pl, x1, tgt, g_final.reshape(1, d))

    dw_pg = _mm(hn2t, dgpre, form="nn", out_dtype=BF, name="dw_pg")
    dw_pl = _mm(p2d, dpe, form="tn", out_dtype=BF, name="dw_ple")
    dx1, dx1b, dg_ple = _ple_bwd(dgpre, w_pg, x1, g_ple, dx2)
    dproj, dya, dyb, dya_pre = _mid_bwd(dx1b, w_of, proj, ya, yb, w_a)
    dw_o = _mm(merged, dx1b, form="tn", out_dtype=BF, name="dw_o")
    dw_a = _mm(ya_pre, dya, form="tn", out_dtype=BF, name="dw_a")
    dproj, dconv = _branch_a_bwd(dproj, proj, dya_pre, conv_w8, conv_b)
    dproj, dgates, dbias, dg_head, dw_b = _mlstm_bwd(dproj, proj, gates, gate_bias, g_head, h_raw, c_states, aux,
                                                     yb_pre, dyb, w_b)
    s_rows = jnp.concatenate([
        dw_a.reshape(N_DEV, 128, d), dw_b.reshape(N_DEV, 256, d), dw_o.reshape(N_DEV, 128, d),
        dw_pg.reshape(N_DEV, 128, d),
        dw_pl.reshape(PLE_DIM, N_DEV, 128).transpose(1, 0, 2).reshape(N_DEV, 32, d)], axis=1)
    dgates_b = dgates.astype(BF)
    dw_main, (r_rows,) = _mm(hnt, dproj, form="nn", out_dtype=BF, name="dw_main",
                             comm=_DirectComm([s_rows], "exchange"))
    dw_gate = _mm(hnt, dgates_b, form="nn", out_dtype=BF, name="dw_gate")
    tail_g = jnp.roll(jnp.concatenate([dw_main[:, 12288:], dw_gate], axis=1), 8, axis=1)
    dw_glob = jnp.concatenate([dw_main[:, 8192:12288], dw_main[:, 0:8192], tail_g], axis=1)
    s_win = jnp.stack([dw_glob[:, WIN_STRIDE * j:WIN_STRIDE * j + WIN_W] for j in range(N_DEV)])
    grad_x, dg_mix, (r_win,) = _dhn_mix_bwd(dproj, w_main, dgates_b, w_gate, x2d, g_mix, dx1,
                                            _DirectComm([s_win], "exchange"))

    vec = jnp.concatenate([dg_mix[0], dconv[3], dg_head[0], dg_ple[0], small_fin[1],
                           dbias[0, IG_LANE:], jnp.zeros((7 * d - 6152,), F32)]).reshape(7, d)
    conv_part = jnp.pad(dconv[:3].reshape(3, N_DEV, 128).transpose(1, 0, 2).reshape(N_DEV, 1, 384),
                        ((0, 0), (0, 0), (0, d - 384)))
    s_f32 = jnp.concatenate([jnp.broadcast_to(vec[None], (N_DEV, 7, d)), conv_part], axis=1)
    (r_f32,) = _comm_call(_DirectComm([s_f32], "exchange"), name="small_grads_exchange")
    sum_win = _sum_slots(r_win, name="sum_win", tr=128)
    sum_rows = _sum_slots(r_rows, name="sum_rows", tr=96)
    sum_f32 = _sum_slots(r_f32, name="sum_f32", tr=8)

    g_w_in = lax.dynamic_slice(sum_win, (0, me), (d, SHARD_W))
    g_w_a = sum_rows[0:128]
    g_w_b = sum_rows[128:384]
    g_w_o = sum_rows[384:512]
    g_w_pg = sum_rows[512:640]
    g_w_pl = sum_rows[640:672].reshape(PLE_DIM, 128)
    vsum = sum_f32[:7].reshape(7 * d)
    g_g_mix = vsum[0:1024].reshape(1, d)
    g_conv_b = vsum[1024:2048].reshape(1, d)
    g_g_head = vsum[2048:4096].reshape(1, V_DIM)
    g_g_ple = vsum[4096:5120].reshape(1, d)
    g_g_final = vsum[5120:6144].reshape(1, d)
    g_b_gates = vsum[6144:6152].reshape(1, 8)
    g_conv_w = sum_f32[7, :384].reshape(3, 128)

    loss = lax.psum(jnp.sum(small_fin[0]), ("x", "y", "c"))

    names = ["g_mix", "w_in", "conv_w", "conv_b", "w_a_out", "b_gates", "g_head", "w_b_out", "w_o", "g_ple",
             "w_ple_gate", "w_ple", "g_final"]
    weights = [g_mix, w_in, conv_w, conv_b, w_a_out, b_gates, g_head, w_b_out, w_o, g_ple, w_ple_gate, w_ple,
               g_final]
    moms = [m_g_mix, m_w_in, m_conv_w, m_conv_b, m_w_a_out, m_b_gates, m_g_head, m_w_b_out, m_w_o, m_g_ple,
            m_w_ple_gate, m_w_ple, m_g_final]
    vels = [v_g_mix, v_w_in, v_conv_w, v_conv_b, v_w_a_out, v_b_gates, v_g_head, v_w_b_out, v_w_o, v_g_ple,
            v_w_ple_gate, v_w_ple, v_g_final]
    grads2d = [g_g_mix, g_w_in, g_conv_w, g_conv_b, g_w_a, g_b_gates, g_g_head, g_w_b, g_w_o, g_g_ple, g_w_pg,
               g_w_pl, g_g_final]
    grads, deltas, new_m, new_v = [], [], [], []
    for nm, w, m_, v_, g2 in zip(names, weights, moms, vels, grads2d):
        shp = w.shape
        w2 = w.reshape(g2.shape)
        dl, mn, vn = _adamw(w2, g2, m_.reshape(g2.shape), v_.reshape(g2.shape), name="adamw_" + nm)
        grads.append(g2.reshape(shp))
        deltas.append(dl.reshape(shp))
        new_m.append(mn.reshape(shp))
        new_v.append(vn.reshape(shp))
    return (loss, grad_x.reshape(x.shape), *grads, *deltas, *new_m, *new_v)
```

```python
import functools

import jax
import jax.numpy as jnp
from jax import lax
from jax.experimental import pallas as pl
from jax.experimental.pallas import tpu as pltpu

F32 = jnp.float32
BF = jnp.bfloat16

D_MODEL = 1024
N_HEADS = 4
DK = 256
DV = 512
V_DIM = 2048
PLE_DIM = 256
N_IN = 14344
N_MAIN = 14336
EPS = 1e-6
QK_SCALE = DK ** -0.5
NEG = -1e30
N_DEV = 8
SHARD_W = 1793
WIN_STRIDE = 1792
WIN_W = 1920
ROWS_PACK = 672
_CHUNK = 256
IG_LANE = 120
FG_LANE = 124

ADAM_LR = 0.001
ADAM_B1 = 0.9
ADAM_B2 = 0.999
ADAM_EPS = 1e-08
ADAM_WD = 0.01
ADAM_STEP = 10

VMEM_LIMIT = 56 * 1024 * 1024
MESH = pl.DeviceIdType.MESH


def _cparams(sem):
    return pltpu.CompilerParams(dimension_semantics=sem, vmem_limit_bytes=VMEM_LIMIT)


def _sigmoid(x):
    return 1.0 / (1.0 + jnp.exp(-x))


def _log_sigmoid(x):
    return jnp.minimum(x, 0.0) - jnp.log(1.0 + jnp.exp(-jnp.abs(x)))


def _dot(a, b):
    return jnp.dot(a, b, preferred_element_type=F32)


def _dot_nt(a, b):
    return lax.dot_general(a, b, (((1,), (1,)), ((), ())), preferred_element_type=F32)


def _dot_tn(a, b):
    return lax.dot_general(a, b, (((0,), (0,)), ((), ())), preferred_element_type=F32)


class _DirectComm:
    def __init__(self, srcs, kind):
        self.srcs = list(srcs)
        self.kind = kind
        self.n = len(self.srcs)
        if kind == "exchange":
            self.out_shapes = [jax.ShapeDtypeStruct(s.shape, s.dtype) for s in self.srcs]
        else:
            self.out_shapes = [jax.ShapeDtypeStruct((N_DEV,) + s.shape, s.dtype) for s in self.srcs]
        self.scratch = [pltpu.SemaphoreType.DMA((self.n, 7)), pltpu.SemaphoreType.DMA((self.n, 7)),
                        pltpu.SemaphoreType.DMA((self.n,))]

    def ops(self, src, dst, send_sems, recv_sems, local_sems):
        exchange = self.kind == "exchange"

        def descriptors():
            x, y, c = _position()
            me_lin = 4 * x + 2 * y + c
            local = [pltpu.make_async_copy(src[b].at[me_lin] if exchange else src[b], dst[b].at[me_lin],
                                           local_sems.at[b]) for b in range(self.n)]
            sends, recvs = [], []
            for f in range(1, N_DEV):
                px = (1 - x) if (f >> 2) & 1 else x
                py = (1 - y) if (f >> 1) & 1 else y
                pc = (1 - c) if f & 1 else c
                peer_lin = 4 * px + 2 * py + pc
                for b in range(self.n):
                    out = src[b].at[peer_lin] if exchange else src[b]
                    common = dict(send_sem=send_sems.at[b, f - 1], recv_sem=recv_sems.at[b, f - 1],
                                  device_id=(px, py, pc), device_id_type=MESH)
                    sends.append(pltpu.make_async_remote_copy(src_ref=out, dst_ref=dst[b].at[me_lin], **common))
                    recvs.append(pltpu.make_async_remote_copy(src_ref=out, dst_ref=dst[b].at[peer_lin], **common))
            return local, sends, recvs

        def start():
            local, sends, _ = descriptors()
            for cp in local + sends:
                cp.start()

        def wait():
            local, sends, recvs = descriptors()
            for cp in recvs:
                cp.wait_recv()
            for cp in sends:
                cp.wait_send()
            for cp in local:
                cp.wait()

        return start, wait


def _mm(a, b, *, form, out_dtype, name, tm=1024, tn=1024, tk=1024, add=None, extra=None, comm=None):
    assert extra is None or form == "nn"
    if form == "nn":
        m, kc = a.shape
        n = b.shape[1]
    elif form == "nt":
        m, kc = a.shape
        n = b.shape[0]
    else:
        kc, m = a.shape
        n = b.shape[1]
    tm, tn, tk = min(tm, m), min(tn, n), min(tk, kc)
    assert m % tm == 0 and n % tn == 0 and kc % tk == 0, (name, a.shape, b.shape)
    nk = kc // tk
    if form == "tn":
        a_spec = pl.BlockSpec((tk, tm), lambda i, j, k: (k, i))
    else:
        a_spec = pl.BlockSpec((tm, tk), lambda i, j, k: (i, k))
    if form == "nt":
        b_spec = pl.BlockSpec((tn, tk), lambda i, j, k: (j, k))
    else:
        b_spec = pl.BlockSpec((tk, tn), lambda i, j, k: (k, j))
    o_spec = pl.BlockSpec((tm, tn), lambda i, j, k: (i, j))
    dot = {"nn": _dot, "nt": _dot_nt, "tn": _dot_tn}[form]
    has_add = add is not None

    use_acc = nk > 1 and (has_add or out_dtype != F32)
    has_x = extra is not None
    n2 = extra[0].shape[1] if has_x else 0
    use_acc2 = has_x and nk > 1 and extra[1] != F32
    n_comm = comm.n if comm else 0
    n_in = 2 + int(has_add) + int(has_x) + n_comm
    n_out = 1 + int(has_x) + n_comm
    grid = (m // tm, n // tn, nk)

    def body(*refs):
        a_ref, b_ref = refs[0], refs[1]
        add_ref = refs[2] if has_add else None
        b2_ref = refs[2 + int(has_add)] if has_x else None
        o_ref = refs[n_in]
        o2_ref = refs[n_in + 1] if has_x else None
        scr = list(refs[n_in + n_out + (3 if comm else 0):])
        acc = scr.pop(0) if use_acc else o_ref
        acc2 = scr.pop(0) if use_acc2 else o2_ref
        i, j, k = pl.program_id(0), pl.program_id(1), pl.program_id(2)
        if comm:
            start, wait = comm.ops(refs[n_in - n_comm:n_in], refs[n_in + n_out - n_comm:n_in + n_out],
                                   *refs[n_in + n_out:n_in + n_out + 3])
            pl.when((i == 0) & (j == 0) & (k == 0))(start)

        def part():
            return dot(a_ref[...].astype(BF), b_ref[...].astype(BF))

        def part2():
            return dot(a_ref[...].astype(BF), b2_ref[...].astype(BF))

        def finish(total):
            if has_add:
                total = total + add_ref[...].astype(F32)
            o_ref[...] = total.astype(out_dtype)

        if nk == 1:
            finish(part())
            if has_x:
                @pl.when(j == 0)
                def _():
                    o2_ref[...] = part2().astype(extra[1])
        else:
            @pl.when(k == 0)
            def _():
                acc[...] = part()

            @pl.when(k > 0)
            def _():
                acc[...] += part()

            if use_acc:
                @pl.when(k == nk - 1)
                def _():
                    finish(acc[...])
            if has_x:
                @pl.when((j == 0) & (k == 0))
                def _():
                    acc2[...] = part2()

                @pl.when((j == 0) & (k > 0))
                def _():
                    acc2[...] += part2()

                if use_acc2:
                    @pl.when((j == 0) & (k == nk - 1))
                    def _():
                        o2_ref[...] = acc2[...].astype(extra[1])
        if comm:
            pl.when((i == grid[0] - 1) & (j == grid[1] - 1) & (k == nk - 1))(wait)

    any_spec = pl.BlockSpec(memory_space=pl.ANY)
    o2_spec = pl.BlockSpec((tm, n2), lambda i, j, k: (i, 0))
    in_specs = ([a_spec, b_spec] + ([o_spec] if has_add else [])
                + ([pl.BlockSpec((tk, n2), lambda i, j, k: (k, 0))] if has_x else []) + [any_spec] * n_comm)
    args = (a, b) + ((add,) if has_add else ()) + ((extra[0],) if has_x else ()) + (tuple(comm.srcs) if comm else ())
    out_specs = [o_spec] + ([o2_spec] if has_x else []) + [any_spec] * n_comm
    out_shape = ([jax.ShapeDtypeStruct((m, n), out_dtype)]
                 + ([jax.ShapeDtypeStruct((m, n2), extra[1])] if has_x else []) + (comm.out_shapes if comm else []))
    scratch = ((comm.scratch if comm else []) + ([pltpu.VMEM((tm, tn), F32)] if use_acc else [])
               + ([pltpu.VMEM((tm, n2), F32)] if use_acc2 else []))
    if comm:
        sem = ("arbitrary",) * 3
    else:
        sem = ("parallel", "arbitrary" if has_x else "parallel", "arbitrary")
    res = pl.pallas_call(
        body, name=name, grid=grid, in_specs=in_specs, out_specs=out_specs, out_shape=out_shape,
        scratch_shapes=scratch, compiler_params=_cparams(sem),
    )(*args)
    if not (comm or has_x):
        return res[0]
    return tuple(res[:1 + int(has_x)]) + ((list(res[1 + int(has_x):]),) if comm else ())


def _rms_fwd(x, g, *, name, tm=512):
    t, d = x.shape
    tm = min(tm, t)

    def body(x_ref, g_ref, hn_ref, hnt_ref):
        xv = x_ref[...]
        r = lax.rsqrt(jnp.mean(xv * xv, axis=1, keepdims=True) + EPS)
        hn = xv * r * g_ref[...]
        hn_ref[...] = hn.astype(BF)
        hnt_ref[...] = hn.T.astype(BF)

    return pl.pallas_call(
        body, name=name, grid=(t // tm,),
        in_specs=[pl.BlockSpec((tm, d), lambda i: (i, 0)), pl.BlockSpec((1, d), lambda i: (0, 0))],
        out_specs=[pl.BlockSpec((tm, d), lambda i: (i, 0)), pl.BlockSpec((d, tm), lambda i: (0, i))],
        out_shape=[jax.ShapeDtypeStruct((t, d), BF), jax.ShapeDtypeStruct((d, t), BF)],
        compiler_params=_cparams(("parallel",)),
    )(x, g)


HALO = 16
XA_BLK, BA_BLK, CA_BLK, ZA_BLK = 8, 9, 10, 11


def _shift_down(u, prev, n):
    tm = u.shape[0]
    rolled = pltpu.roll(u, n, 0)
    row = lax.broadcasted_iota(jnp.int32, u.shape, 0)
    out = rolled
    for j in range(n):
        out = jnp.where(row == j, prev[HALO - n + j:HALO - n + j + 1, :], out)
    return out


def _shift_up(u, nxt, n):
    tm = u.shape[0]
    rolled = pltpu.roll(u, tm - n, 0)
    row = lax.broadcasted_iota(jnp.int32, u.shape, 0)
    out = rolled
    for j in range(n):
        out = jnp.where(row == tm - n + j, nxt[j:j + 1, :], out)
    return out


def _branch_a_fwd(proj, conv_w8, conv_b, *, tm=512):
    t = proj.shape[0]
    d = D_MODEL
    tm = min(tm, t)
    hb = tm // HALO

    def body(xa_ref, ba_ref, ca_ref, za_ref, xap_ref, cap_ref, w_ref, b_ref, o_ref):
        i = pl.program_id(0)
        u = ca_ref[...].astype(F32) * xa_ref[...].astype(F32)
        up = cap_ref[...].astype(F32) * xap_ref[...].astype(F32)
        up = jnp.where(i == 0, 0.0, up)
        u1 = _shift_down(u, up, 1)
        u2 = _shift_down(u, up, 2)
        cv = w_ref[0:1, :] * u2 + w_ref[1:2, :] * u1 + w_ref[2:3, :] * u + b_ref[...]
        za = za_ref[...].astype(F32)
        o_ref[...] = (ba_ref[...].astype(F32) * cv * (za * _sigmoid(za))).astype(BF)

    def col(blk):
        return pl.BlockSpec((tm, d), lambda i: (i, blk))

    def prev(blk):
        return pl.BlockSpec((HALO, d), lambda i: (jnp.maximum(i * hb - 1, 0), blk))

    return pl.pallas_call(
        body, name="branch_a_fwd", grid=(t // tm,),
        in_specs=[col(XA_BLK), col(BA_BLK), col(CA_BLK), col(ZA_BLK), prev(XA_BLK), prev(CA_BLK),
                  pl.BlockSpec((8, d), lambda i: (0, 0)), pl.BlockSpec((1, d), lambda i: (0, 0))],
        out_specs=pl.BlockSpec((tm, d), lambda i: (i, 0)),
        out_shape=jax.ShapeDtypeStruct((t, d), BF),
        compiler_params=_cparams(("parallel",)),
    )(proj, proj, proj, proj, proj, proj, conv_w8, conv_b)


def _branch_a_bwd(dproj, proj, dya_pre, conv_w8, conv_b, *, tm=512):
    t = proj.shape[0]
    d = D_MODEL
    tm = min(tm, t)
    hb = tm // HALO
    nt = t // tm

    def body(dp_ref, xa_ref, ba_ref, ca_ref, za_ref, xap_ref, cap_ref, dy_ref, ban_ref, zan_ref, dyn_ref,
             w_ref, b_ref, o_ref, dc_ref):
        del dp_ref
        i = pl.program_id(0)
        xa = xa_ref[...].astype(F32)
        ca = ca_ref[...].astype(F32)
        ba = ba_ref[...].astype(F32)
        za = za_ref[...].astype(F32)
        u = ca * xa
        up = cap_ref[...].astype(F32) * xap_ref[...].astype(F32)
        up = jnp.where(i == 0, 0.0, up)
        u1 = _shift_down(u, up, 1)
        u2 = _shift_down(u, up, 2)
        w0, w1, w2 = w_ref[0:1, :], w_ref[1:2, :], w_ref[2:3, :]
        cv = w0 * u2 + w1 * u1 + w2 * u + b_ref[...]
        sg = _sigmoid(za)
        sz = za * sg
        dy = dy_ref[...].astype(F32)
        dcv = dy * ba * sz
        zan = zan_ref[...].astype(F32)
        dcvn = dyn_ref[...].astype(F32) * ban_ref[...].astype(F32) * (zan * _sigmoid(zan))
        dcvn = jnp.where(i == nt - 1, 0.0, dcvn)
        du = w2 * dcv + w1 * _shift_up(dcv, dcvn, 1) + w0 * _shift_up(dcv, dcvn, 2)
        o_ref[:, 0:d] = (du * ca).astype(BF)
        o_ref[:, d:2 * d] = (dy * cv * sz).astype(BF)
        o_ref[:, 2 * d:3 * d] = (du * xa).astype(BF)
        o_ref[:, 3 * d:4 * d] = (dy * ba * cv * sg * (1.0 + za * (1.0 - sg))).astype(BF)

        @pl.when(i == 0)
        def _():
            dc_ref[...] = jnp.zeros_like(dc_ref)

        dc_ref[0:1, :] += jnp.sum(dcv * u2, axis=0, keepdims=True)
        dc_ref[1:2, :] += jnp.sum(dcv * u1, axis=0, keepdims=True)
        dc_ref[2:3, :] += jnp.sum(dcv * u, axis=0, keepdims=True)
        dc_ref[3:4, :] += jnp.sum(dcv, axis=0, keepdims=True)

    def col(blk):
        return pl.BlockSpec((tm, d), lambda i: (i, blk))

    def prev(blk):
        return pl.BlockSpec((HALO, d), lambda i: (jnp.maximum(i * hb - 1, 0), blk))

    def nxt(blk):
        return pl.BlockSpec((HALO, d), lambda i: (jnp.minimum((i + 1) * hb, t // HALO - 1), blk))

    return pl.pallas_call(
        body, name="branch_a_bwd", grid=(nt,),
        in_specs=[pl.BlockSpec(memory_space=pl.ANY),
                  col(XA_BLK), col(BA_BLK), col(CA_BLK), col(ZA_BLK), prev(XA_BLK), prev(CA_BLK),
                  pl.BlockSpec((tm, d), lambda i: (i, 0)), nxt(BA_BLK), nxt(ZA_BLK),
                  pl.BlockSpec((HALO, d), lambda i: (jnp.minimum((i + 1) * hb, t // HALO - 1), 0)),
                  pl.BlockSpec((8, d), lambda i: (0, 0)), pl.BlockSpec((1, d), lambda i: (0, 0))],
        out_specs=[pl.BlockSpec((tm, 4 * d), lambda i: (i, 2)), pl.BlockSpec((8, d), lambda i: (0, 0))],
        out_shape=[jax.ShapeDtypeStruct(dproj.shape, BF), jax.ShapeDtypeStruct((8, d), F32)],
        input_output_aliases={0: 0},
        compiler_params=_cparams(("arbitrary",)),
    )(dproj, proj, proj, proj, proj, proj, proj, dya_pre, proj, proj, dya_pre, conv_w8, conv_b)


def _gate_vectors(gc, gt, h, lane_i, sub_i):
    ig_c = jnp.sum(jnp.where(lane_i == IG_LANE + h, gc, 0.0), axis=1, keepdims=True)
    fg_c = jnp.sum(jnp.where(lane_i == FG_LANE + h, gc, 0.0), axis=1, keepdims=True)
    ig_r = jnp.sum(jnp.where(sub_i == IG_LANE + h, gt, 0.0), axis=0, keepdims=True)
    fg_r = jnp.sum(jnp.where(sub_i == FG_LANE + h, gt, 0.0), axis=0, keepdims=True)
    return ig_c, fg_c, ig_r, fg_r


def _chunk_common(q, k, ig_c, fg_c, ig_r, fg_r, m_prev, n_prev, row, col):
    lf_c = _log_sigmoid(fg_c)
    lf_r = _log_sigmoid(fg_r)
    causal = col <= row
    b_c = jnp.sum(jnp.where(causal, lf_r, 0.0), axis=1, keepdims=True)
    b_r = jnp.sum(jnp.where(row <= col, lf_c, 0.0), axis=0, keepdims=True)
    dmat = jnp.where(causal, b_c - b_r + ig_r, NEG)
    a = b_c + m_prev
    m_row = jnp.maximum(a, jnp.max(dmat, axis=1, keepdims=True))
    est = jnp.exp(dmat - m_row)
    s = _dot_nt(q, k) * QK_SCALE * est
    inter = jnp.exp(a - m_row)
    den = jnp.sum(s, axis=1, keepdims=True) + inter * QK_SCALE * jnp.sum(
        q.astype(F32) * n_prev, axis=1, keepdims=True)
    expm = jnp.exp(-m_row)
    mx = jnp.maximum(jnp.abs(den), expm)
    b_last = jnp.sum(lf_r, axis=1, keepdims=True)
    g_r = b_last - b_r + ig_r
    g_c = b_last - b_c + ig_c
    m_new = jnp.maximum(b_last + m_prev, jnp.max(g_r, axis=1, keepdims=True))
    w_c = jnp.exp(g_c - m_new)
    decay = jnp.exp(b_last + m_prev - m_new)
    return est, s, inter, den, expm, mx, m_new, w_c, decay


def _mlstm_fwd(proj, gates, gate_bias, g_head, w_b):
    t = proj.shape[0]
    lc = min(_CHUNK, t)
    nc = t // lc

    def body(q_ref, k_ref, v_ref, o_ref, z_ref, g_ref, gb_ref, gh_ref, wb_ref,
             yb_ref, ybo_ref, hr_ref, cs_ref, aux_ref, c_scr, nm_scr):
        c = pl.program_id(0)

        @pl.when(c == 0)
        def _():
            c_scr[...] = jnp.zeros_like(c_scr)
            nm_scr[...] = jnp.zeros_like(nm_scr)
            nm_scr[:, 1:2, :] = jnp.full((N_HEADS, 1, DK), NEG, F32)

        row = lax.broadcasted_iota(jnp.int32, (lc, lc), 0)
        col = lax.broadcasted_iota(jnp.int32, (lc, lc), 1)
        gc = g_ref[...] + gb_ref[...]
        gt = gc.T
        lane_i = lax.broadcasted_iota(jnp.int32, gc.shape, 1)
        sub_i = lax.broadcasted_iota(jnp.int32, gt.shape, 0)
        for h in range(N_HEADS):
            ks = slice(h * DK, (h + 1) * DK)
            vs = slice(h * DV, (h + 1) * DV)
            q = q_ref[:, ks]
            k = k_ref[:, ks]
            v = v_ref[:, vs]
            ig_c, fg_c, ig_r, fg_r = _gate_vectors(gc, gt, h, lane_i, sub_i)
            n_prev = nm_scr[h, 0:1, :]
            m_prev = nm_scr[h, 1:2, 0:1]
            est, s, inter, den, expm, mx, m_new, w_c, decay = _chunk_common(
                q, k, ig_c, fg_c, ig_r, fg_r, m_prev, n_prev, row, col)
            c_prev = c_scr[h]
            c_prev_b = c_prev.astype(BF)
            num = _dot(s.astype(BF), v) + (inter * QK_SCALE) * _dot(q, c_prev_b)
            hh = num / mx
            r = lax.rsqrt(jnp.mean(hh * hh, axis=1, keepdims=True) + EPS)
            hbn = hh * r * gh_ref[:, vs]
            o = o_ref[:, vs].astype(F32)
            z = z_ref[:, vs].astype(F32)
            yb_ref[:, vs] = (_sigmoid(o) * hbn * (z * _sigmoid(z))).astype(BF)
            hr_ref[:, vs] = hh.astype(BF)
            cs_ref[0, h] = c_prev_b
            aux_ref[0, h] = nm_scr[h]
            kw = k.astype(F32) * w_c
            c_scr[h] = decay * c_prev + _dot_tn(kw.astype(BF), v)
            nm_scr[h, 0:1, :] = decay * n_prev + jnp.sum(kw, axis=0, keepdims=True)
            nm_scr[h, 1:2, :] = jnp.broadcast_to(m_new, (1, DK))
        ybo_ref[...] = _dot(yb_ref[...], wb_ref[...]).astype(BF)

    return pl.pallas_call(
        body, name="mlstm_fwd", grid=(nc,),
        in_specs=[pl.BlockSpec((lc, 1024), lambda c: (c, 0)),
                  pl.BlockSpec((lc, 1024), lambda c: (c, 1)),
                  pl.BlockSpec((lc, 2048), lambda c: (c, 1)),
                  pl.BlockSpec((lc, 2048), lambda c: (c, 2)),
                  pl.BlockSpec((lc, 2048), lambda c: (c, 3)),
                  pl.BlockSpec((lc, 128), lambda c: (c, 0)),
                  pl.BlockSpec((1, 128), lambda c: (0, 0)),
                  pl.BlockSpec((1, V_DIM), lambda c: (0, 0)),
                  pl.BlockSpec((V_DIM, D_MODEL), lambda c: (0, 0))],
        out_specs=[pl.BlockSpec((lc, V_DIM), lambda c: (c, 0)),
                   pl.BlockSpec((lc, D_MODEL), lambda c: (c, 0)),
                   pl.BlockSpec((lc, V_DIM), lambda c: (c, 0)),
                   pl.BlockSpec((1, N_HEADS, DK, DV), lambda c: (c, 0, 0, 0)),
                   pl.BlockSpec((1, N_HEADS, 8, DK), lambda c: (c, 0, 0, 0))],
        out_shape=[jax.ShapeDtypeStruct((t, V_DIM), BF), jax.ShapeDtypeStruct((t, D_MODEL), BF),
                   jax.ShapeDtypeStruct((t, V_DIM), BF),
                   jax.ShapeDtypeStruct((nc, N_HEADS, DK, DV), BF),
                   jax.ShapeDtypeStruct((nc, N_HEADS, 8, DK), F32)],
        scratch_shapes=[pltpu.VMEM((N_HEADS, DK, DV), F32), pltpu.VMEM((N_HEADS, 8, DK), F32)],
        compiler_params=_cparams(("arbitrary",)),
    )(proj, proj, proj, proj, proj, gates, gate_bias, g_head, w_b)


def _mlstm_bwd(dproj, proj, gates, gate_bias, g_head, h_raw, c_states, aux, yb_pre, dyb, w_b):
    t = proj.shape[0]
    lc = min(_CHUNK, t)
    nc = t // lc

    def body(dpin_ref, q_ref, k_ref, v_ref, o_ref, z_ref, g_ref, gb_ref, gh_ref, hr_ref, cs_ref, aux_ref,
             ybp_ref, dyb_ref, wb_ref, dp_ref, dg_ref, dbias_ref, dgh_ref, dwb_ref, dc_scr, dn_scr, dy_ref,
             dwb_scr):
        del dpin_ref
        step = pl.program_id(0)

        @pl.when(step == 0)
        def _():
            dc_scr[...] = jnp.zeros_like(dc_scr)
            dn_scr[...] = jnp.zeros_like(dn_scr)
            dbias_ref[...] = jnp.zeros_like(dbias_ref)
            dgh_ref[...] = jnp.zeros_like(dgh_ref)
            dwb_scr[...] = jnp.zeros_like(dwb_scr)

        dyb = dyb_ref[...]
        dy_ref[...] = _dot_nt(dyb, wb_ref[...])
        dwb_scr[...] += _dot_tn(ybp_ref[...], dyb)

        row = lax.broadcasted_iota(jnp.int32, (lc, lc), 0)
        col = lax.broadcasted_iota(jnp.int32, (lc, lc), 1)
        eye = row == col
        gc = g_ref[...] + gb_ref[...]
        gt = gc.T
        lane_i = lax.broadcasted_iota(jnp.int32, gc.shape, 1)
        sub_i = lax.broadcasted_iota(jnp.int32, gt.shape, 0)
        dgates = jnp.zeros(gc.shape, F32)
        for h in range(N_HEADS):
            ks = slice(h * DK, (h + 1) * DK)
            vs = slice(h * DV, (h + 1) * DV)
            q = q_ref[:, ks]
            k = k_ref[:, ks]
            v = v_ref[:, vs]
            ig_c, fg_c, ig_r, fg_r = _gate_vectors(gc, gt, h, lane_i, sub_i)
            n_prev = aux_ref[0, h, 0:1, :]
            m_prev = aux_ref[0, h, 1:2, 0:1]
            c_prev_b = cs_ref[0, h]
            est, s, inter, den, expm, mx, m_new, w_c, decay = _chunk_common(
                q, k, ig_c, fg_c, ig_r, fg_r, m_prev, n_prev, row, col)
            hb = hr_ref[:, vs].astype(F32)
            dyp = dy_ref[:, vs]
            o = o_ref[:, vs].astype(F32)
            z = z_ref[:, vs].astype(F32)
            so = _sigmoid(o)
            sgz = _sigmoid(z)
            sz = z * sgz
            r = lax.rsqrt(jnp.mean(hb * hb, axis=1, keepdims=True) + EPS)
            xh = hb * r
            gh = gh_ref[:, vs]
            hbn = xh * gh
            dyso = dyp * so
            dyso_h = dyso * hbn
            d_o = dyso_h * sz * (1.0 - so)
            d_z = dyso_h * sgz * (1.0 + z * (1.0 - sgz))
            dhbn = dyso * sz
            dgh_ref[0:1, vs] += jnp.sum(dhbn * xh, axis=0, keepdims=True)
            dxh = dhbn * gh
            dh = r * (dxh - xh * jnp.mean(dxh * xh, axis=1, keepdims=True))
            dnm = dh / mx
            hd = jnp.sum(dh * hb, axis=1, keepdims=True)
            cond = jnp.abs(den) > expm
            dden = jnp.where(cond, -hd / mx * jnp.sign(den), 0.0)
            dnm_b = dnm.astype(BF)
            p = _dot_nt(dnm_b, v) + dden
            dqk = (p * est * QK_SCALE).astype(BF)
            dq_inter = (inter * QK_SCALE) * (_dot_nt(dnm_b, c_prev_b) + dden * n_prev)
            dq = _dot(dqk, k) + dq_inter
            dc_new = dc_scr[h]
            dc_new_b = dc_new.astype(BF)
            dn_new = dn_scr[h, 0:1, :]
            kf = k.astype(F32)
            dk_state = w_c * (_dot_nt(v, dc_new_b) + dn_new)
            dk = _dot_tn(dqk, q) + dk_state
            dv = _dot_tn(s.astype(BF), dnm_b) + w_c * _dot(k, dc_new_b)
            dv1_r = jnp.sum(s * dden, axis=0, keepdims=True)
            dv1_c = (jnp.sum(jnp.where(eye, dv1_r, 0.0), axis=1, keepdims=True)
                     + w_c * jnp.sum(kf * dn_new, axis=1, keepdims=True))
            dli_c = jnp.sum(v.astype(F32) * dv, axis=1, keepdims=True) + dv1_c
            hmat = _dot((p * s).astype(BF), (row < col).astype(BF))
            from_prev_c = jnp.sum(q.astype(F32) * dq_inter, axis=1, keepdims=True)
            to_next_c = jnp.sum(kf * dk_state, axis=1, keepdims=True)
            through = decay * (
                jnp.sum(jnp.sum(dc_new * c_prev_b.astype(F32), axis=1, keepdims=True), axis=0, keepdims=True)
                + jnp.sum(dn_new * n_prev, axis=1, keepdims=True))
            dlf_r = through + jnp.sum(jnp.where(row >= col, hmat + from_prev_c, to_next_c), axis=0, keepdims=True)
            dlf_c = jnp.sum(jnp.where(eye, dlf_r, 0.0), axis=1, keepdims=True)
            dfg_c = dlf_c * _sigmoid(-fg_c)
            dgates = dgates + jnp.where(lane_i == IG_LANE + h, dli_c, 0.0) + jnp.where(
                lane_i == FG_LANE + h, dfg_c, 0.0)
            qi = q.astype(F32) * (inter * QK_SCALE)
            dc_scr[h] = decay * dc_new + _dot_tn(qi.astype(BF), dnm_b)
            dn_scr[h, 0:1, :] = decay * dn_new + jnp.sum(qi * dden, axis=0, keepdims=True)
            dp_ref[:, h * DK:(h + 1) * DK] = dq.astype(BF)
            dp_ref[:, 1024 + h * DK:1024 + (h + 1) * DK] = dk.astype(BF)
            dp_ref[:, 2048 + h * DV:2048 + (h + 1) * DV] = dv.astype(BF)
            dp_ref[:, 4096 + h * DV:4096 + (h + 1) * DV] = d_o.astype(BF)
            dp_ref[:, 6144 + h * DV:6144 + (h + 1) * DV] = d_z.astype(BF)
        dg_ref[...] = dgates
        dbias_ref[0:1, :] += jnp.sum(dgates, axis=0, keepdims=True)

        @pl.when(step == nc - 1)
        def _():
            dwb_ref[...] = dwb_scr[...].astype(BF)

    def rev(c):
        return nc - 1 - c

    return pl.pallas_call(
        body, name="mlstm_bwd", grid=(nc,),
        input_output_aliases={0: 0},
        in_specs=[pl.BlockSpec(memory_space=pl.ANY),
                  pl.BlockSpec((lc, 1024), lambda c: (rev(c), 0)),
                  pl.BlockSpec((lc, 1024), lambda c: (rev(c), 1)),
                  pl.BlockSpec((lc, 2048), lambda c: (rev(c), 1)),
                  pl.BlockSpec((lc, 2048), lambda c: (rev(c), 2)),
                  pl.BlockSpec((lc, 2048), lambda c: (rev(c), 3)),
                  pl.BlockSpec((lc, 128), lambda c: (rev(c), 0)),
                  pl.BlockSpec((1, 128), lambda c: (0, 0)),
                  pl.BlockSpec((1, V_DIM), lambda c: (0, 0)),
                  pl.BlockSpec((lc, V_DIM), lambda c: (rev(c), 0)),
                  pl.BlockSpec((1, N_HEADS, DK, DV), lambda c: (rev(c), 0, 0, 0)),
                  pl.BlockSpec((1, N_HEADS, 8, DK), lambda c: (rev(c), 0, 0, 0)),
                  pl.BlockSpec((lc, V_DIM), lambda c: (rev(c), 0)),
                  pl.BlockSpec((lc, D_MODEL), lambda c: (rev(c), 0)),
                  pl.BlockSpec((V_DIM, D_MODEL), lambda c: (0, 0))],
        out_specs=[pl.BlockSpec((lc, 8192), lambda c: (rev(c), 0)),
                   pl.BlockSpec((lc, 128), lambda c: (rev(c), 0)),
                   pl.BlockSpec((8, 128), lambda c: (0, 0)),
                   pl.BlockSpec((8, V_DIM), lambda c: (0, 0)),
                   pl.BlockSpec((V_DIM, D_MODEL), lambda c: (0, 0))],
        out_shape=[jax.ShapeDtypeStruct((t, N_MAIN), BF), jax.ShapeDtypeStruct((t, 128), F32),
                   jax.ShapeDtypeStruct((8, 128), F32), jax.ShapeDtypeStruct((8, V_DIM), F32),
                   jax.ShapeDtypeStruct((V_DIM, D_MODEL), BF)],
        scratch_shapes=[pltpu.VMEM((N_HEADS, DK, DV), F32), pltpu.VMEM((N_HEADS, 8, DK), F32),
                        pltpu.VMEM((lc, V_DIM), F32), pltpu.VMEM((V_DIM, D_MODEL), F32)],
        compiler_params=_cparams(("arbitrary",)),
    )(dproj, proj, proj, proj, proj, proj, gates, gate_bias, g_head, h_raw, c_states, aux, yb_pre, dyb, w_b)


GA_BLK, GB_BLK = 12, 13


def _full(shape):
    return pl.BlockSpec(shape, lambda i: (0,) * len(shape))


def _rms_backward(xv, g, dhn, dres):
    r = lax.rsqrt(jnp.mean(xv * xv, axis=1, keepdims=True) + EPS)
    xh = xv * r
    dxh = dhn * g
    dx = dres + r * (dxh - xh * jnp.mean(dxh * xh, axis=1, keepdims=True))
    return dx, jnp.sum(dhn * xh, axis=0, keepdims=True)


def _mid_fwd(ya_pre, w_a, yb, proj, w_o, x, g_ple, *, tm=512):
    t, d = x.shape
    tm = min(tm, t)

    def body(yap_ref, wa_ref, yb_ref, ga_ref, gb_ref, wo_ref, x_ref, g_ref,
             ya_ref, mg_ref, x1_ref, hn_ref, hnt_ref):
        ya = _dot(yap_ref[...], wa_ref[...]).astype(BF)
        ya_ref[...] = ya
        merged = (_sigmoid(ga_ref[...].astype(F32)) * ya.astype(F32)
                  + _sigmoid(gb_ref[...].astype(F32)) * yb_ref[...].astype(F32)).astype(BF)
        mg_ref[...] = merged
        x1 = _dot(merged, wo_ref[...]) + x_ref[...]
        x1_ref[...] = x1
        r = lax.rsqrt(jnp.mean(x1 * x1, axis=1, keepdims=True) + EPS)
        hn = x1 * r * g_ref[...]
        hn_ref[...] = hn.astype(BF)
        hnt_ref[...] = hn.T.astype(BF)

    row = pl.BlockSpec((tm, d), lambda i: (i, 0))
    return pl.pallas_call(
        body, name="mid_fwd", grid=(t // tm,),
        in_specs=[row, _full((d, d)), row, pl.BlockSpec((tm, d), lambda i: (i, GA_BLK)),
                  pl.BlockSpec((tm, d), lambda i: (i, GB_BLK)), _full((d, d)), row, _full((1, d))],
        out_specs=[row, row, row, row, pl.BlockSpec((d, tm), lambda i: (0, i))],
        out_shape=[jax.ShapeDtypeStruct((t, d), BF), jax.ShapeDtypeStruct((t, d), BF),
                   jax.ShapeDtypeStruct((t, d), F32), jax.ShapeDtypeStruct((t, d), BF),
                   jax.ShapeDtypeStruct((d, t), BF)],
        compiler_params=_cparams(("parallel",)),
    )(ya_pre, w_a, yb, proj, proj, w_o, x, g_ple)


def _final_fwd_bwd(hn2, w_pg, p, w_pl, x1, target, g_final, *, tm=512):
    t, d = x1.shape
    tm = min(tm, t)

    def body(hn_ref, wpg_ref, p_ref, wpl_ref, x_ref, tg_ref, g_ref, dx_ref, dgp_ref, dpe_ref, sm_ref):
        i = pl.program_id(0)
        gate = _sigmoid(_dot(hn_ref[...], wpg_ref[...]))
        pe_v = _dot(p_ref[...].astype(BF), wpl_ref[...])
        x2 = x_ref[...] + gate * pe_v
        r = lax.rsqrt(jnp.mean(x2 * x2, axis=1, keepdims=True) + EPS)
        xh = x2 * r
        g = g_ref[...]
        err = xh * g - tg_ref[...]
        dy = err * (1.0 / d)
        dxh = dy * g
        dx2 = r * (dxh - xh * jnp.mean(dxh * xh, axis=1, keepdims=True))
        dx_ref[...] = dx2
        dgp_ref[...] = (dx2 * pe_v * gate * (1.0 - gate)).astype(BF)
        dpe_ref[...] = (dx2 * gate).astype(BF)

        @pl.when(i == 0)
        def _():
            sm_ref[...] = jnp.zeros_like(sm_ref)

        sm_ref[0:1, :] += (0.5 / d) * jnp.sum(err * err, axis=0, keepdims=True)
        sm_ref[1:2, :] += jnp.sum(dy * xh, axis=0, keepdims=True)

    row = pl.BlockSpec((tm, d), lambda i: (i, 0))
    return pl.pallas_call(
        body, name="final_fwd_bwd", grid=(t // tm,),
        in_specs=[row, _full((d, d)), pl.BlockSpec((tm, PLE_DIM), lambda i: (i, 0)), _full((PLE_DIM, d)),
                  row, row, _full((1, d))],
        out_specs=[row, row, row, _full((8, d))],
        out_shape=[jax.ShapeDtypeStruct((t, d), F32), jax.ShapeDtypeStruct((t, d), BF),
                   jax.ShapeDtypeStruct((t, d), BF), jax.ShapeDtypeStruct((8, d), F32)],
        compiler_params=_cparams(("arbitrary",)),
    )(hn2, w_pg, p, w_pl, x1, target, g_final)


def _ple_bwd(dgpre, w_pg, x1, g_ple, dx2, *, tm=512):
    t, d = x1.shape
    tm = min(tm, t)

    def body(dgp_ref, wpg_ref, x_ref, g_ref, dres_ref, dx_ref, dxb_ref, dg_ref):
        i = pl.program_id(0)
        dhn = _dot_nt(dgp_ref[...], wpg_ref[...])
        dx, dg = _rms_backward(x_ref[...], g_ref[...], dhn, dres_ref[...])
        dx_ref[...] = dx
        dxb_ref[...] = dx.astype(BF)

        @pl.when(i == 0)
        def _():
            dg_ref[...] = jnp.zeros_like(dg_ref)

        dg_ref[0:1, :] += dg

    row = pl.BlockSpec((tm, d), lambda i: (i, 0))
    return pl.pallas_call(
        body, name="ple_bwd", grid=(t // tm,),
        in_specs=[row, _full((d, d)), row, _full((1, d)), row],
        out_specs=[row, row, _full((8, d))],
        out_shape=[jax.ShapeDtypeStruct((t, d), F32), jax.ShapeDtypeStruct((t, d), BF),
                   jax.ShapeDtypeStruct((8, d), F32)],
        compiler_params=_cparams(("arbitrary",)),
    )(dgpre, w_pg, x1, g_ple, dx2)


def _mid_bwd(dx1b, w_o, proj, ya, yb, w_a, *, tm=512):
    t, d = dx1b.shape
    tm = min(tm, t)

    def body(dx_ref, wo_ref, ga_ref, gb_ref, ya_ref, yb_ref, wa_ref, o_ref, dya_ref, dyb_ref, dyap_ref):
        dm = _dot_nt(dx_ref[...], wo_ref[...])
        sa = _sigmoid(ga_ref[...].astype(F32))
        sb = _sigmoid(gb_ref[...].astype(F32))
        dya = (dm * sa).astype(BF)
        dya_ref[...] = dya
        dyb_ref[...] = (dm * sb).astype(BF)
        o_ref[:, 0:d] = (dm * ya_ref[...].astype(F32) * sa * (1.0 - sa)).astype(BF)
        o_ref[:, d:2 * d] = (dm * yb_ref[...].astype(F32) * sb * (1.0 - sb)).astype(BF)
        dyap_ref[...] = _dot_nt(dya, wa_ref[...]).astype(BF)

    row = pl.BlockSpec((tm, d), lambda i: (i, 0))
    return pl.pallas_call(
        body, name="mid_bwd", grid=(t // tm,),
        in_specs=[row, _full((d, d)), pl.BlockSpec((tm, d), lambda i: (i, GA_BLK)),
                  pl.BlockSpec((tm, d), lambda i: (i, GB_BLK)), row, row, _full((d, d))],
        out_specs=[pl.BlockSpec((tm, 2 * d), lambda i: (i, 6)), row, row, row],
        out_shape=[jax.ShapeDtypeStruct((t, N_MAIN), BF), jax.ShapeDtypeStruct((t, d), BF),
                   jax.ShapeDtypeStruct((t, d), BF), jax.ShapeDtypeStruct((t, d), BF)],
        compiler_params=_cparams(("parallel",)),
    )(dx1b, w_o, proj, proj, ya, yb, w_a)


def _dhn_mix_bwd(dproj, w_main, dgates, w_gate, x, g_mix, dx1, comm, *, tm=1024, tk=2048):
    t, d = x.shape
    tm = min(tm, t)
    nk = N_MAIN // tk
    ni = t // tm
    n_in = 7 + comm.n

    def body(*refs):
        dp_ref, w_ref, dgt_ref, wg_ref, x_ref, g_ref, dres_ref = refs[:7]
        dx_ref, dg_ref = refs[n_in], refs[n_in + 1]
        acc = refs[-1]
        start, wait = comm.ops(refs[7:n_in], refs[n_in + 2:n_in + 2 + comm.n], *refs[n_in + 2 + comm.n:-1])
        i, k = pl.program_id(0), pl.program_id(1)
        pl.when((i == 0) & (k == 0))(start)

        @pl.when(k == 0)
        def _():
            acc[...] = _dot_nt(dp_ref[...], w_ref[...]) + _dot_nt(dgt_ref[...], wg_ref[...])

        @pl.when(k > 0)
        def _():
            acc[...] += _dot_nt(dp_ref[...], w_ref[...])

        @pl.when((i == 0) & (k == 0))
        def _():
            dg_ref[...] = jnp.zeros_like(dg_ref)

        @pl.when(k == nk - 1)
        def _():
            dx, dg = _rms_backward(x_ref[...], g_ref[...], acc[...], dres_ref[...])
            dx_ref[...] = dx
            dg_ref[0:1, :] += dg

        pl.when((i == ni - 1) & (k == nk - 1))(wait)

    any_spec = pl.BlockSpec(memory_space=pl.ANY)
    row = pl.BlockSpec((tm, d), lambda i, k: (i, 0))
    res = pl.pallas_call(
        body, name="dhn_mix_bwd", grid=(ni, nk),
        in_specs=[pl.BlockSpec((tm, tk), lambda i, k: (i, k)), pl.BlockSpec((d, tk), lambda i, k: (0, k)),
                  pl.BlockSpec((tm, 128), lambda i, k: (i, 0)), pl.BlockSpec((d, 128), lambda i, k: (0, 0)),
                  row, pl.BlockSpec((1, d), lambda i, k: (0, 0)), row] + [any_spec] * comm.n,
        out_specs=[row, pl.BlockSpec((8, d), lambda i, k: (0, 0))] + [any_spec] * comm.n,
        out_shape=[jax.ShapeDtypeStruct((t, d), F32), jax.ShapeDtypeStruct((8, d), F32)] + comm.out_shapes,
        scratch_shapes=comm.scratch + [pltpu.VMEM((tm, d), F32)],
        compiler_params=_cparams(("arbitrary", "arbitrary")),
    )(dproj, w_main, dgates, w_gate, x, g_mix, dx1, *comm.srcs)
    return res[0], res[1], list(res[2:])


def _position():
    x, y, c = lax.axis_index("x"), lax.axis_index("y"), lax.axis_index("c")
    return x, y, c


def _all_gather(srcs):
    nb = len(srcs)
    any_spec = pl.BlockSpec(memory_space=pl.ANY)

    def body(*refs):
        src = refs[:nb]
        dst = refs[nb:2 * nb]
        send_sems, recv_sems, local_sems = refs[2 * nb:]
        x, y, c = _position()
        me, sibling = (x, y, c), (x, y, 1 - c)
        chips = [(1 - x, y), (x, 1 - y), (1 - x, 1 - y)]

        def slot(b, px, py, pc):
            return dst[b].at[4 * px + 2 * py + pc]

        def copy(k, b, block, to, from_src=False):
            return pltpu.make_async_remote_copy(
                src_ref=src[b] if from_src else slot(b, *block), dst_ref=slot(b, *block),
                send_sem=send_sems.at[b, k], recv_sem=recv_sems.at[b, k],
                device_id=to, device_id_type=MESH)

        mine = [pltpu.make_async_copy(src[b], slot(b, *me), local_sems.at[b]) for b in range(nb)]
        for cp in mine:
            cp.start()
        first = [copy(0, b, me, sibling, True) for b in range(nb)]
        first += [copy(1 + j, b, me, (*chip, c), True) for j, chip in enumerate(chips) for b in range(nb)]
        for cp in first:
            cp.start()
        passed = []
        for j, chip in enumerate(chips):
            for b in range(nb):
                copy(1 + j, b, (*chip, c), me).wait_recv()
                fwd = copy(4 + j, b, (*chip, c), sibling)
                fwd.start()
                passed.append(fwd)
        for b in range(nb):
            copy(0, b, sibling, me).wait_recv()
        for j, chip in enumerate(chips):
            for b in range(nb):
                copy(4 + j, b, (*chip, 1 - c), me).wait_recv()
        for cp in first + passed:
            cp.wait_send()
        for cp in mine:
            cp.wait()

    return pl.pallas_call(
        body, name="weights_all_gather",
        in_specs=[any_spec] * nb, out_specs=[any_spec] * nb,
        out_shape=[jax.ShapeDtypeStruct((N_DEV,) + s.shape, s.dtype) for s in srcs],
        scratch_shapes=[pltpu.SemaphoreType.DMA((nb, 7)), pltpu.SemaphoreType.DMA((nb, 7)),
                        pltpu.SemaphoreType.DMA((nb,))],
    )(*srcs)


def _comm_call(comm, *, name):
    any_spec = pl.BlockSpec(memory_space=pl.ANY)

    def body(*refs):
        start, wait = comm.ops(refs[:comm.n], refs[comm.n:2 * comm.n], *refs[2 * comm.n:])
        start()
        wait()

    return pl.pallas_call(
        body, name=name, in_specs=[any_spec] * comm.n, out_specs=[any_spec] * comm.n,
        out_shape=comm.out_shapes, scratch_shapes=comm.scratch,
    )(*comm.srcs)


def _sum_slots(recv, *, name, tr):
    _, r, cdim = recv.shape
    tr = min(tr, r)

    def body(r_ref, o_ref):
        total = r_ref[0].astype(F32)
        for s in range(1, N_DEV):
            total = total + r_ref[s].astype(F32)
        o_ref[...] = total

    return pl.pallas_call(
        body, name=name, grid=(r // tr,),
        in_specs=[pl.BlockSpec((N_DEV, tr, cdim), lambda i: (0, i, 0))],
        out_specs=pl.BlockSpec((tr, cdim), lambda i: (i, 0)),
        out_shape=jax.ShapeDtypeStruct((r, cdim), F32),
        compiler_params=_cparams(("parallel",)),
    )(recv)


def _adamw(w, g, m, v, *, name):
    lead = w.ndim - 2
    r, cdim = w.shape[-2:]
    tr = 128 if r % 128 == 0 else r
    c1 = 1.0 - ADAM_B1 ** ADAM_STEP
    c2 = 1.0 - ADAM_B2 ** ADAM_STEP

    def body(w_ref, g_ref, m_ref, v_ref, d_ref, mo_ref, vo_ref):
        gv = g_ref[...]
        mn = ADAM_B1 * m_ref[...] + (1.0 - ADAM_B1) * gv
        vn = ADAM_B2 * v_ref[...] + (1.0 - ADAM_B2) * (gv * gv)
        d_ref[...] = -ADAM_LR * ((mn / c1) / (jnp.sqrt(vn / c2) + ADAM_EPS) + ADAM_WD * w_ref[...])
        mo_ref[...] = mn
        vo_ref[...] = vn

    blk = pl.BlockSpec((1,) * lead + (tr, cdim), lambda i: (0,) * lead + (i, 0))
    shp = jax.ShapeDtypeStruct(w.shape, F32)
    return pl.pallas_call(
        body, name=name, grid=(r // tr,),
        in_specs=[blk] * 4, out_specs=[blk] * 3, out_shape=[shp] * 3,
        compiler_params=_cparams(("parallel",)),
    )(w, g, m, v)


def kernel(x, p, g_mix, w_in, conv_w, conv_b, w_a_out, b_gates, g_head, w_b_out, w_o, g_ple, w_ple_gate, w_ple, g_final, loss_target, m_g_mix, m_w_in, m_conv_w, m_conv_b, m_w_a_out, m_b_gates, m_g_head, m_w_b_out, m_w_o, m_g_ple, m_w_ple_gate, m_w_ple, m_g_final, v_g_mix, v_w_in, v_conv_w, v_conv_b, v_w_a_out, v_b_gates, v_g_head, v_w_b_out, v_w_o, v_g_ple, v_w_ple_gate, v_w_ple, v_g_final):
    d = D_MODEL
    t = x.shape[1]
    x2d = x.reshape(t, d)
    p2d = p.reshape(t, PLE_DIM)
    tgt = loss_target.reshape(t, d)

    win = jnp.pad(w_in[0].astype(BF), ((0, 0), (0, WIN_W - SHARD_W)))
    rows = jnp.concatenate([w_a_out[0].astype(BF), w_b_out[0].astype(BF), w_o[0].astype(BF),
                            w_ple_gate[0].astype(BF), w_ple[0].astype(BF).reshape(32, d)], axis=0)
    cfl = jnp.pad(conv_w[0], ((0, 5), (0, 0)))
    g_win, g_cf = _all_gather([win, cfl])

    w_glob = jnp.concatenate([g_win[k, :, :SHARD_W] for k in range(N_DEV)]
                             + [jnp.zeros((d, 14464 - N_IN), BF)], axis=1)
    tail = jnp.roll(w_glob[:, 12288:], -8, axis=1)
    w_main = jnp.concatenate([w_glob[:, 4096:12288], w_glob[:, 0:4096], tail[:, :2048]], axis=1)
    w_gate = tail[:, 2048:]
    conv_w8 = jnp.pad(g_cf[:, :3, :].transpose(1, 0, 2).reshape(3, d), ((0, 5), (0, 0)))
    gate_bias = jnp.pad(b_gates, ((0, 0), (IG_LANE, 0)))

    hn, hnt = _rms_fwd(x2d, g_mix, name="rms_mix")
    proj, gates, (g_rows,) = _mm(hn, w_main, form="nn", out_dtype=BF, name="proj", tm=2048,
                                 extra=(w_gate, F32), comm=_DirectComm([rows], "gather"))
    w_a = g_rows[:, 0:128].reshape(d, d)
    w_b = g_rows[:, 128:384].reshape(V_DIM, d)
    w_of = g_rows[:, 384:512].reshape(d, d)
    w_pg = g_rows[:, 512:640].reshape(d, d)
    w_pl = g_rows[:, 640:672].reshape(N_DEV, PLE_DIM, 128).transpose(1, 0, 2).reshape(PLE_DIM, d)
    ya_pre = _branch_a_fwd(proj, conv_w8, conv_b)
    yb_pre, yb, h_raw, c_states, aux = _mlstm_fwd(proj, gates, gate_bias, g_head, w_b)
    ya, merged, x1, hn2, hn2t = _mid_fwd(ya_pre, w_a, yb, proj, w_of, x2d, g_ple)
    dx2, dgpre, dpe, small_fin = _final_fwd_bwd(hn2, w_pg, p2d, w_pl, x1, tgt, g_final.reshape(1, d))

    dw_pg = _mm(hn2t, dgpre, form="nn", out_dtype=BF, name="dw_pg")
    dw_pl = _mm(p2d, dpe, form="tn", out_dtype=BF, name="dw_ple")
    dx1, dx1b, dg_ple = _ple_bwd(dgpre, w_pg, x1, g_ple, dx2)
    dproj, dya, dyb, dya_pre = _mid_bwd(dx1b, w_of, proj, ya, yb, w_a)
    dw_o = _mm(merged, dx1b, form="tn", out_dtype=BF, name="dw_o")
    dw_a = _mm(ya_pre, dya, form="tn", out_dtype=BF, name="dw_a")
    dproj, dconv = _branch_a_bwd(dproj, proj, dya_pre, conv_w8, conv_b)
    dproj, dgates, dbias, dg_head, dw_b = _mlstm_bwd(dproj, proj, gates, gate_bias, g_head, h_raw, c_states, aux,
                                                     yb_pre, dyb, w_b)
    s_rows = jnp.concatenate([
        dw_a.reshape(N_DEV, 128, d), dw_b.reshape(N_DEV, 256, d), dw_o.reshape(N_DEV, 128, d),
        dw_pg.reshape(N_DEV, 128, d),
        dw_pl.reshape(PLE_DIM, N_DEV, 128).transpose(1, 0, 2).reshape(N_DEV, 32, d)], axis=1)
    dgates_b = dgates.astype(BF)
    dw_main, dw_gate, (r_rows,) = _mm(hnt, dproj, form="nn", out_dtype=BF, name="dw_main", tk=2048,
                                      extra=(dgates_b, BF), comm=_DirectComm([s_rows], "exchange"))
    tail_g = jnp.roll(jnp.concatenate([dw_main[:, 12288:], dw_gate], axis=1), 8, axis=1)
    dw_glob = jnp.concatenate([dw_main[:, 8192:12288], dw_main[:, 0:8192], tail_g], axis=1)
    s_win = jnp.stack([jnp.pad(dw_glob[:, SHARD_W * j:SHARD_W * (j + 1)], ((0, 0), (0, WIN_W - SHARD_W)))
                       for j in range(N_DEV)])
    grad_x, dg_mix, (r_win,) = _dhn_mix_bwd(dproj, w_main, dgates_b, w_gate, x2d, g_mix, dx1,
                                            _DirectComm([s_win], "exchange"))

    vec = jnp.concatenate([dg_mix[0], dconv[3], dg_head[0], dg_ple[0], small_fin[1],
                           dbias[0, IG_LANE:], jnp.zeros((7 * d - 6152,), F32)]).reshape(7, d)
    conv_part = jnp.pad(dconv[:3].reshape(3, N_DEV, 128).transpose(1, 0, 2).reshape(N_DEV, 1, 384),
                        ((0, 0), (0, 0), (0, d - 384)))
    s_f32 = jnp.concatenate([jnp.broadcast_to(vec[None], (N_DEV, 7, d)), conv_part], axis=1)
    (r_f32,) = _comm_call(_DirectComm([s_f32], "exchange"), name="small_grads_exchange")
    sum_win = _sum_slots(r_win, name="sum_win", tr=128)
    sum_rows = _sum_slots(r_rows, name="sum_rows", tr=96)
    sum_f32 = _sum_slots(r_f32, name="sum_f32", tr=8)

    g_w_in = sum_win[:, :SHARD_W]
    g_w_a = sum_rows[0:128]
    g_w_b = sum_rows[128:384]
    g_w_o = sum_rows[384:512]
    g_w_pg = sum_rows[512:640]
    g_w_pl = sum_rows[640:672].reshape(PLE_DIM, 128)
    vsum = sum_f32[:7].reshape(7 * d)
    g_g_mix = vsum[0:1024].reshape(1, d)
    g_conv_b = vsum[1024:2048].reshape(1, d)
    g_g_head = vsum[2048:4096].reshape(1, V_DIM)
    g_g_ple = vsum[4096:5120].reshape(1, d)
    g_g_final = vsum[5120:6144].reshape(1, d)
    g_b_gates = vsum[6144:6152].reshape(1, 8)
    g_conv_w = sum_f32[7, :384].reshape(3, 128)

    loss = lax.psum(jnp.sum(small_fin[0]), ("x", "y", "c"))

    names = ["g_mix", "w_in", "conv_w", "conv_b", "w_a_out", "b_gates", "g_head", "w_b_out", "w_o", "g_ple",
             "w_ple_gate", "w_ple", "g_final"]
    weights = [g_mix, w_in, conv_w, conv_b, w_a_out, b_gates, g_head, w_b_out, w_o, g_ple, w_ple_gate, w_ple,
               g_final]
    moms = [m_g_mix, m_w_in, m_conv_w, m_conv_b, m_w_a_out, m_b_gates, m_g_head, m_w_b_out, m_w_o, m_g_ple,
            m_w_ple_gate, m_w_ple, m_g_final]
    vels = [v_g_mix, v_w_in, v_conv_w, v_conv_b, v_w_a_out, v_b_gates, v_g_head, v_w_b_out, v_w_o, v_g_ple,
            v_w_ple_gate, v_w_ple, v_g_final]
    grads2d = [g_g_mix, g_w_in, g_conv_w, g_conv_b, g_w_a, g_b_gates, g_g_head, g_w_b, g_w_o, g_g_ple, g_w_pg,
               g_w_pl, g_g_final]
    grads, deltas, new_m, new_v = [], [], [], []
    for nm, w, m_, v_, g2 in zip(names, weights, moms, vels, grads2d):
        shp = w.shape
        kshp = shp if w.ndim >= 2 else (1,) + shp
        gk = g2.reshape(kshp)
        dl, mn, vn = _adamw(w.reshape(kshp), gk, m_.reshape(kshp), v_.reshape(kshp), name="adamw_" + nm)
        grads.append(gk.reshape(shp))
        deltas.append(dl.reshape(shp))
        new_m.append(mn.reshape(shp))
        new_v.append(vn.reshape(shp))
    return (loss, grad_x.reshape(x.shape), *grads, *deltas, *new_m, *new_v)
```

```python
import functools

import jax
import jax.numpy as jnp
from jax import lax
from jax.experimental import pallas as pl
from jax.experimental.pallas import tpu as pltpu

F32 = jnp.float32
BF = jnp.bfloat16

D_MODEL = 1024
N_HEADS = 4
DK = 256
DV = 512
V_DIM = 2048
PLE_DIM = 256
N_IN = 14344
N_MAIN = 14336
EPS = 1e-6
QK_SCALE = DK ** -0.5
NEG = -1e30
N_DEV = 8
SHARD_W = 1793
WIN_STRIDE = 1792
WIN_W = 1920
ROWS_PACK = 672
_CHUNK = 256
IG_LANE = 120
FG_LANE = 124

ADAM_LR = 0.001
ADAM_B1 = 0.9
ADAM_B2 = 0.999
ADAM_EPS = 1e-08
ADAM_WD = 0.01
ADAM_STEP = 10

VMEM_LIMIT = 56 * 1024 * 1024
MESH = pl.DeviceIdType.MESH


def _cparams(sem):
    return pltpu.CompilerParams(dimension_semantics=sem, vmem_limit_bytes=VMEM_LIMIT)


def _sigmoid(x):
    return 1.0 / (1.0 + jnp.exp(-x))


def _log_sigmoid(x):
    return jnp.minimum(x, 0.0) - jnp.log(1.0 + jnp.exp(-jnp.abs(x)))


def _dot(a, b):
    return jnp.dot(a, b, preferred_element_type=F32)


def _dot_nt(a, b):
    return lax.dot_general(a, b, (((1,), (1,)), ((), ())), preferred_element_type=F32)


def _dot_tn(a, b):
    return lax.dot_general(a, b, (((0,), (0,)), ((), ())), preferred_element_type=F32)


class _DirectComm:
    def __init__(self, srcs, kind):
        self.srcs = list(srcs)
        self.kind = kind
        self.n = len(self.srcs)
        if kind == "exchange":
            self.out_shapes = [jax.ShapeDtypeStruct(s.shape, s.dtype) for s in self.srcs]
        else:
            self.out_shapes = [jax.ShapeDtypeStruct((N_DEV,) + s.shape, s.dtype) for s in self.srcs]
        self.scratch = [pltpu.SemaphoreType.DMA((self.n, 7)), pltpu.SemaphoreType.DMA((self.n, 7)),
                        pltpu.SemaphoreType.DMA((self.n,))]

    def ops(self, src, dst, send_sems, recv_sems, local_sems):
        exchange = self.kind == "exchange"

        def descriptors():
            x, y, c = _position()
            me_lin = 4 * x + 2 * y + c
            local = [pltpu.make_async_copy(src[b].at[me_lin] if exchange else src[b], dst[b].at[me_lin],
                                           local_sems.at[b]) for b in range(self.n)]
            sends, recvs = [], []
            for f in range(1, N_DEV):
                px = (1 - x) if (f >> 2) & 1 else x
                py = (1 - y) if (f >> 1) & 1 else y
                pc = (1 - c) if f & 1 else c
                peer_lin = 4 * px + 2 * py + pc
                for b in range(self.n):
                    out = src[b].at[peer_lin] if exchange else src[b]
                    common = dict(send_sem=send_sems.at[b, f - 1], recv_sem=recv_sems.at[b, f - 1],
                                  device_id=(px, py, pc), device_id_type=MESH)
                    sends.append(pltpu.make_async_remote_copy(src_ref=out, dst_ref=dst[b].at[me_lin], **common))
                    recvs.append(pltpu.make_async_remote_copy(src_ref=out, dst_ref=dst[b].at[peer_lin], **common))
            return local, sends, recvs

        def start():
            local, sends, _ = descriptors()
            for cp in local + sends:
                cp.start()

        def wait():
            local, sends, recvs = descriptors()
            for cp in recvs:
                cp.wait_recv()
            for cp in sends:
                cp.wait_send()
            for cp in local:
                cp.wait()

        return start, wait


def _mm(a, b, *, form, out_dtype, name, tm=1024, tn=1024, tk=1024, add=None, extra=None, comm=None):
    assert extra is None or form == "nn"
    if form == "nn":
        m, kc = a.shape
        n = b.shape[1]
    elif form == "nt":
        m, kc = a.shape
        n = b.shape[0]
    else:
        kc, m = a.shape
        n = b.shape[1]
    tm, tn, tk = min(tm, m), min(tn, n), min(tk, kc)
    assert m % tm == 0 and n % tn == 0 and kc % tk == 0, (name, a.shape, b.shape)
    nk = kc // tk
    if form == "tn":
        a_spec = pl.BlockSpec((tk, tm), lambda i, j, k: (k, i))
    else:
        a_spec = pl.BlockSpec((tm, tk), lambda i, j, k: (i, k))
    if form == "nt":
        b_spec = pl.BlockSpec((tn, tk), lambda i, j, k: (j, k))
    else:
        b_spec = pl.BlockSpec((tk, tn), lambda i, j, k: (k, j))
    o_spec = pl.BlockSpec((tm, tn), lambda i, j, k: (i, j))
    dot = {"nn": _dot, "nt": _dot_nt, "tn": _dot_tn}[form]
    has_add = add is not None

    use_acc = nk > 1 and (has_add or out_dtype != F32)
    has_x = extra is not None
    n2 = extra[0].shape[1] if has_x else 0
    use_acc2 = has_x and nk > 1 and extra[1] != F32
    n_comm = comm.n if comm else 0
    n_in = 2 + int(has_add) + int(has_x) + n_comm
    n_out = 1 + int(has_x) + n_comm
    grid = (m // tm, n // tn, nk)

    def body(*refs):
        a_ref, b_ref = refs[0], refs[1]
        add_ref = refs[2] if has_add else None
        b2_ref = refs[2 + int(has_add)] if has_x else None
        o_ref = refs[n_in]
        o2_ref = refs[n_in + 1] if has_x else None
        scr = list(refs[n_in + n_out + (3 if comm else 0):])
        acc = scr.pop(0) if use_acc else o_ref
        acc2 = scr.pop(0) if use_acc2 else o2_ref
        i, j, k = pl.program_id(0), pl.program_id(1), pl.program_id(2)
        if comm:
            start, wait = comm.ops(refs[n_in - n_comm:n_in], refs[n_in + n_out - n_comm:n_in + n_out],
                                   *refs[n_in + n_out:n_in + n_out + 3])
            pl.when((i == 0) & (j == 0) & (k == 0))(start)

        def part():
            return dot(a_ref[...].astype(BF), b_ref[...].astype(BF))

        def part2():
            return dot(a_ref[...].astype(BF), b2_ref[...].astype(BF))

        def finish(total):
            if has_add:
                total = total + add_ref[...].astype(F32)
            o_ref[...] = total.astype(out_dtype)

        if nk == 1:
            finish(part())
            if has_x:
                @pl.when(j == 0)
                def _():
                    o2_ref[...] = part2().astype(extra[1])
        else:
            @pl.when(k == 0)
            def _():
                acc[...] = part()

            @pl.when(k > 0)
            def _():
                acc[...] += part()

            if use_acc:
                @pl.when(k == nk - 1)
                def _():
                    finish(acc[...])
            if has_x:
                @pl.when((j == 0) & (k == 0))
                def _():
                    acc2[...] = part2()

                @pl.when((j == 0) & (k > 0))
                def _():
                    acc2[...] += part2()

                if use_acc2:
                    @pl.when((j == 0) & (k == nk - 1))
                    def _():
                        o2_ref[...] = acc2[...].astype(extra[1])
        if comm:
            pl.when((i == grid[0] - 1) & (j == grid[1] - 1) & (k == nk - 1))(wait)

    any_spec = pl.BlockSpec(memory_space=pl.ANY)
    o2_spec = pl.BlockSpec((tm, n2), lambda i, j, k: (i, 0))
    in_specs = ([a_spec, b_spec] + ([o_spec] if has_add else [])
                + ([pl.BlockSpec((tk, n2), lambda i, j, k: (k, 0))] if has_x else []) + [any_spec] * n_comm)
    args = (a, b) + ((add,) if has_add else ()) + ((extra[0],) if has_x else ()) + (tuple(comm.srcs) if comm else ())
    out_specs = [o_spec] + ([o2_spec] if has_x else []) + [any_spec] * n_comm
    out_shape = ([jax.ShapeDtypeStruct((m, n), out_dtype)]
                 + ([jax.ShapeDtypeStruct((m, n2), extra[1])] if has_x else []) + (comm.out_shapes if comm else []))
    scratch = ((comm.scratch if comm else []) + ([pltpu.VMEM((tm, tn), F32)] if use_acc else [])
               + ([pltpu.VMEM((tm, n2), F32)] if use_acc2 else []))
    if comm:
        sem = ("arbitrary",) * 3
    else:
        sem = ("parallel", "arbitrary" if has_x else "parallel", "arbitrary")
    res = pl.pallas_call(
        body, name=name, grid=grid, in_specs=in_specs, out_specs=out_specs, out_shape=out_shape,
        scratch_shapes=scratch, compiler_params=_cparams(sem),
    )(*args)
    if not (comm or has_x):
        return res[0]
    return tuple(res[:1 + int(has_x)]) + ((list(res[1 + int(has_x):]),) if comm else ())


def _rms_fwd(x, g, *, name, tm=512):
    t, d = x.shape
    tm = min(tm, t)

    def body(x_ref, g_ref, hn_ref, hnt_ref):
        xv = x_ref[...]
        r = lax.rsqrt(jnp.mean(xv * xv, axis=1, keepdims=True) + EPS)
        hn = xv * r * g_ref[...]
        hn_ref[...] = hn.astype(BF)
        hnt_ref[...] = hn.T.astype(BF)

    return pl.pallas_call(
        body, name=name, grid=(t // tm,),
        in_specs=[pl.BlockSpec((tm, d), lambda i: (i, 0)), pl.BlockSpec((1, d), lambda i: (0, 0))],
        out_specs=[pl.BlockSpec((tm, d), lambda i: (i, 0)), pl.BlockSpec((d, tm), lambda i: (0, i))],
        out_shape=[jax.ShapeDtypeStruct((t, d), BF), jax.ShapeDtypeStruct((d, t), BF)],
        compiler_params=_cparams(("parallel",)),
    )(x, g)


HALO = 16
XA_BLK, BA_BLK, CA_BLK, ZA_BLK = 8, 9, 10, 11


def _shift_down(u, prev, n):
    tm = u.shape[0]
    rolled = pltpu.roll(u, n, 0)
    row = lax.broadcasted_iota(jnp.int32, u.shape, 0)
    out = rolled
    for j in range(n):
        out = jnp.where(row == j, prev[HALO - n + j:HALO - n + j + 1, :], out)
    return out


def _shift_up(u, nxt, n):
    tm = u.shape[0]
    rolled = pltpu.roll(u, tm - n, 0)
    row = lax.broadcasted_iota(jnp.int32, u.shape, 0)
    out = rolled
    for j in range(n):
        out = jnp.where(row == tm - n + j, nxt[j:j + 1, :], out)
    return out


def _branch_a_fwd(proj, conv_w8, conv_b, *, tm=512):
    t = proj.shape[0]
    d = D_MODEL
    tm = min(tm, t)
    hb = tm // HALO

    def body(xa_ref, ba_ref, ca_ref, za_ref, xap_ref, cap_ref, w_ref, b_ref, o_ref):
        i = pl.program_id(0)
        u = ca_ref[...].astype(F32) * xa_ref[...].astype(F32)
        up = cap_ref[...].astype(F32) * xap_ref[...].astype(F32)
        up = jnp.where(i == 0, 0.0, up)
        u1 = _shift_down(u, up, 1)
        u2 = _shift_down(u, up, 2)
        cv = w_ref[0:1, :] * u2 + w_ref[1:2, :] * u1 + w_ref[2:3, :] * u + b_ref[...]
        za = za_ref[...].astype(F32)
        o_ref[...] = (ba_ref[...].astype(F32) * cv * (za * _sigmoid(za))).astype(BF)

    def col(blk):
        return pl.BlockSpec((tm, d), lambda i: (i, blk))

    def prev(blk):
        return pl.BlockSpec((HALO, d), lambda i: (jnp.maximum(i * hb - 1, 0), blk))

    return pl.pallas_call(
        body, name="branch_a_fwd", grid=(t // tm,),
        in_specs=[col(XA_BLK), col(BA_BLK), col(CA_BLK), col(ZA_BLK), prev(XA_BLK), prev(CA_BLK),
                  pl.BlockSpec((8, d), lambda i: (0, 0)), pl.BlockSpec((1, d), lambda i: (0, 0))],
        out_specs=pl.BlockSpec((tm, d), lambda i: (i, 0)),
        out_shape=jax.ShapeDtypeStruct((t, d), BF),
        compiler_params=_cparams(("parallel",)),
    )(proj, proj, proj, proj, proj, proj, conv_w8, conv_b)


def _branch_a_bwd(dproj, proj, dya_pre, conv_w8, conv_b, *, tm=512):
    t = proj.shape[0]
    d = D_MODEL
    tm = min(tm, t)
    hb = tm // HALO
    nt = t // tm

    def body(dp_ref, xa_ref, ba_ref, ca_ref, za_ref, xap_ref, cap_ref, dy_ref, ban_ref, zan_ref, dyn_ref,
             w_ref, b_ref, o_ref, dc_ref):
        del dp_ref
        i = pl.program_id(0)
        xa = xa_ref[...].astype(F32)
        ca = ca_ref[...].astype(F32)
        ba = ba_ref[...].astype(F32)
        za = za_ref[...].astype(F32)
        u = ca * xa
        up = cap_ref[...].astype(F32) * xap_ref[...].astype(F32)
        up = jnp.where(i == 0, 0.0, up)
        u1 = _shift_down(u, up, 1)
        u2 = _shift_down(u, up, 2)
        w0, w1, w2 = w_ref[0:1, :], w_ref[1:2, :], w_ref[2:3, :]
        cv = w0 * u2 + w1 * u1 + w2 * u + b_ref[...]
        sg = _sigmoid(za)
        sz = za * sg
        dy = dy_ref[...].astype(F32)
        dcv = dy * ba * sz
        zan = zan_ref[...].astype(F32)
        dcvn = dyn_ref[...].astype(F32) * ban_ref[...].astype(F32) * (zan * _sigmoid(zan))
        dcvn = jnp.where(i == nt - 1, 0.0, dcvn)
        du = w2 * dcv + w1 * _shift_up(dcv, dcvn, 1) + w0 * _shift_up(dcv, dcvn, 2)
        o_ref[:, 0:d] = (du * ca).astype(BF)
        o_ref[:, d:2 * d] = (dy * cv * sz).astype(BF)
        o_ref[:, 2 * d:3 * d] = (du * xa).astype(BF)
        o_ref[:, 3 * d:4 * d] = (dy * ba * cv * sg * (1.0 + za * (1.0 - sg))).astype(BF)

        @pl.when(i == 0)
        def _():
            dc_ref[...] = jnp.zeros_like(dc_ref)

        dc_ref[0:1, :] += jnp.sum(dcv * u2, axis=0, keepdims=True)
        dc_ref[1:2, :] += jnp.sum(dcv * u1, axis=0, keepdims=True)
        dc_ref[2:3, :] += jnp.sum(dcv * u, axis=0, keepdims=True)
        dc_ref[3:4, :] += jnp.sum(dcv, axis=0, keepdims=True)

    def col(blk):
        return pl.BlockSpec((tm, d), lambda i: (i, blk))

    def prev(blk):
        return pl.BlockSpec((HALO, d), lambda i: (jnp.maximum(i * hb - 1, 0), blk))

    def nxt(blk):
        return pl.BlockSpec((HALO, d), lambda i: (jnp.minimum((i + 1) * hb, t // HALO - 1), blk))

    return pl.pallas_call(
        body, name="branch_a_bwd", grid=(nt,),
        in_specs=[pl.BlockSpec(memory_space=pl.ANY),
                  col(XA_BLK), col(BA_BLK), col(CA_BLK), col(ZA_BLK), prev(XA_BLK), prev(CA_BLK),
                  pl.BlockSpec((tm, d), lambda i: (i, 0)), nxt(BA_BLK), nxt(ZA_BLK),
                  pl.BlockSpec((HALO, d), lambda i: (jnp.minimum((i + 1) * hb, t // HALO - 1), 0)),
                  pl.BlockSpec((8, d), lambda i: (0, 0)), pl.BlockSpec((1, d), lambda i: (0, 0))],
        out_specs=[pl.BlockSpec((tm, 4 * d), lambda i: (i, 2)), pl.BlockSpec((8, d), lambda i: (0, 0))],
        out_shape=[jax.ShapeDtypeStruct(dproj.shape, BF), jax.ShapeDtypeStruct((8, d), F32)],
        input_output_aliases={0: 0},
        compiler_params=_cparams(("arbitrary",)),
    )(dproj, proj, proj, proj, proj, proj, proj, dya_pre, proj, proj, dya_pre, conv_w8, conv_b)


def _gate_vectors(gc, gt, h, lane_i, sub_i):
    ig_c = jnp.sum(jnp.where(lane_i == IG_LANE + h, gc, 0.0), axis=1, keepdims=True)
    fg_c = jnp.sum(jnp.where(lane_i == FG_LANE + h, gc, 0.0), axis=1, keepdims=True)
    ig_r = jnp.sum(jnp.where(sub_i == IG_LANE + h, gt, 0.0), axis=0, keepdims=True)
    fg_r = jnp.sum(jnp.where(sub_i == FG_LANE + h, gt, 0.0), axis=0, keepdims=True)
    return ig_c, fg_c, ig_r, fg_r


def _chunk_common(q, k, ig_c, fg_c, ig_r, fg_r, m_prev, n_prev, row, col):
    lf_c = _log_sigmoid(fg_c)
    lf_r = _log_sigmoid(fg_r)
    causal = col <= row
    b_c = jnp.sum(jnp.where(causal, lf_r, 0.0), axis=1, keepdims=True)
    b_r = jnp.sum(jnp.where(row <= col, lf_c, 0.0), axis=0, keepdims=True)
    dmat = jnp.where(causal, b_c - b_r + ig_r, NEG)
    a = b_c + m_prev
    m_row = jnp.maximum(a, jnp.max(dmat, axis=1, keepdims=True))
    est = jnp.exp(dmat - m_row)
    s = _dot_nt(q, k) * QK_SCALE * est
    inter = jnp.exp(a - m_row)
    den = jnp.sum(s, axis=1, keepdims=True) + inter * QK_SCALE * jnp.sum(
        q.astype(F32) * n_prev, axis=1, keepdims=True)
    expm = jnp.exp(-m_row)
    mx = jnp.maximum(jnp.abs(den), expm)
    b_last = jnp.sum(lf_r, axis=1, keepdims=True)
    g_r = b_last - b_r + ig_r
    g_c = b_last - b_c + ig_c
    m_new = jnp.maximum(b_last + m_prev, jnp.max(g_r, axis=1, keepdims=True))
    w_c = jnp.exp(g_c - m_new)
    decay = jnp.exp(b_last + m_prev - m_new)
    return est, s, inter, den, expm, mx, m_new, w_c, decay


def _mlstm_fwd(proj, gates, gate_bias, g_head, w_b):
    t = proj.shape[0]
    lc = min(_CHUNK, t)
    nc = t // lc

    def body(q_ref, k_ref, v_ref, o_ref, z_ref, g_ref, gb_ref, gh_ref, wb_ref,
             yb_ref, ybo_ref, hr_ref, cs_ref, aux_ref, c_scr, nm_scr):
        c = pl.program_id(0)

        @pl.when(c == 0)
        def _():
            c_scr[...] = jnp.zeros_like(c_scr)
            nm_scr[...] = jnp.zeros_like(nm_scr)
            nm_scr[:, 1:2, :] = jnp.full((N_HEADS, 1, DK), NEG, F32)

        row = lax.broadcasted_iota(jnp.int32, (lc, lc), 0)
        col = lax.broadcasted_iota(jnp.int32, (lc, lc), 1)
        gc = g_ref[...] + gb_ref[...]
        gt = gc.T
        lane_i = lax.broadcasted_iota(jnp.int32, gc.shape, 1)
        sub_i = lax.broadcasted_iota(jnp.int32, gt.shape, 0)
        for h in range(N_HEADS):
            ks = slice(h * DK, (h + 1) * DK)
            vs = slice(h * DV, (h + 1) * DV)
            q = q_ref[:, ks]
            k = k_ref[:, ks]
            v = v_ref[:, vs]
            ig_c, fg_c, ig_r, fg_r = _gate_vectors(gc, gt, h, lane_i, sub_i)
            n_prev = nm_scr[h, 0:1, :]
            m_prev = nm_scr[h, 1:2, 0:1]
            est, s, inter, den, expm, mx, m_new, w_c, decay = _chunk_common(
                q, k, ig_c, fg_c, ig_r, fg_r, m_prev, n_prev, row, col)
            c_prev = c_scr[h]
            c_prev_b = c_prev.astype(BF)
            num = _dot(s.astype(BF), v) + (inter * QK_SCALE) * _dot(q, c_prev_b)
            hh = num / mx
            r = lax.rsqrt(jnp.mean(hh * hh, axis=1, keepdims=True) + EPS)
            hbn = hh * r * gh_ref[:, vs]
            o = o_ref[:, vs].astype(F32)
            z = z_ref[:, vs].astype(F32)
            yb_ref[:, vs] = (_sigmoid(o) * hbn * (z * _sigmoid(z))).astype(BF)
            hr_ref[:, vs] = hh.astype(BF)
            cs_ref[0, h] = c_prev_b
            aux_ref[0, h] = nm_scr[h]
            kw = k.astype(F32) * w_c
            c_scr[h] = decay * c_prev + _dot_tn(kw.astype(BF), v)
            nm_scr[h, 0:1, :] = decay * n_prev + jnp.sum(kw, axis=0, keepdims=True)
            nm_scr[h, 1:2, :] = jnp.broadcast_to(m_new, (1, DK))
        ybo_ref[...] = _dot(yb_ref[...], wb_ref[...]).astype(BF)

    return pl.pallas_call(
        body, name="mlstm_fwd", grid=(nc,),
        in_specs=[pl.BlockSpec((lc, 1024), lambda c: (c, 0)),
                  pl.BlockSpec((lc, 1024), lambda c: (c, 1)),
                  pl.BlockSpec((lc, 2048), lambda c: (c, 1)),
                  pl.BlockSpec((lc, 2048), lambda c: (c, 2)),
                  pl.BlockSpec((lc, 2048), lambda c: (c, 3)),
                  pl.BlockSpec((lc, 128), lambda c: (c, 0)),
                  pl.BlockSpec((1, 128), lambda c: (0, 0)),
                  pl.BlockSpec((1, V_DIM), lambda c: (0, 0)),
                  pl.BlockSpec((V_DIM, D_MODEL), lambda c: (0, 0))],
        out_specs=[pl.BlockSpec((lc, V_DIM), lambda c: (c, 0)),
                   pl.BlockSpec((lc, D_MODEL), lambda c: (c, 0)),
                   pl.BlockSpec((lc, V_DIM), lambda c: (c, 0)),
                   pl.BlockSpec((1, N_HEADS, DK, DV), lambda c: (c, 0, 0, 0)),
                   pl.BlockSpec((1, N_HEADS, 8, DK), lambda c: (c, 0, 0, 0))],
        out_shape=[jax.ShapeDtypeStruct((t, V_DIM), BF), jax.ShapeDtypeStruct((t, D_MODEL), BF),
                   jax.ShapeDtypeStruct((t, V_DIM), BF),
                   jax.ShapeDtypeStruct((nc, N_HEADS, DK, DV), BF),
                   jax.ShapeDtypeStruct((nc, N_HEADS, 8, DK), F32)],
        scratch_shapes=[pltpu.VMEM((N_HEADS, DK, DV), F32), pltpu.VMEM((N_HEADS, 8, DK), F32)],
        compiler_params=_cparams(("arbitrary",)),
    )(proj, proj, proj, proj, proj, gates, gate_bias, g_head, w_b)


def _mlstm_bwd(dproj, proj, gates, gate_bias, g_head, h_raw, c_states, aux, yb_pre, dyb, w_b):
    t = proj.shape[0]
    lc = min(_CHUNK, t)
    nc = t // lc

    def body(dpin_ref, q_ref, k_ref, v_ref, o_ref, z_ref, g_ref, gb_ref, gh_ref, hr_ref, cs_ref, aux_ref,
             ybp_ref, dyb_ref, wb_ref, dp_ref, dg_ref, dbias_ref, dgh_ref, dwb_ref, dc_scr, dn_scr, dy_ref,
             dwb_scr):
        del dpin_ref
        step = pl.program_id(0)

        @pl.when(step == 0)
        def _():
            dc_scr[...] = jnp.zeros_like(dc_scr)
            dn_scr[...] = jnp.zeros_like(dn_scr)
            dbias_ref[...] = jnp.zeros_like(dbias_ref)
            dgh_ref[...] = jnp.zeros_like(dgh_ref)
            dwb_scr[...] = jnp.zeros_like(dwb_scr)

        dyb = dyb_ref[...]
        dy_ref[...] = _dot_nt(dyb, wb_ref[...])
        dwb_scr[...] += _dot_tn(ybp_ref[...], dyb)

        row = lax.broadcasted_iota(jnp.int32, (lc, lc), 0)
        col = lax.broadcasted_iota(jnp.int32, (lc, lc), 1)
        eye = row == col
        gc = g_ref[...] + gb_ref[...]
        gt = gc.T
        lane_i = lax.broadcasted_iota(jnp.int32, gc.shape, 1)
        sub_i = lax.broadcasted_iota(jnp.int32, gt.shape, 0)
        dgates = jnp.zeros(gc.shape, F32)
        for h in range(N_HEADS):
            ks = slice(h * DK, (h + 1) * DK)
            vs = slice(h * DV, (h + 1) * DV)
            q = q_ref[:, ks]
            k = k_ref[:, ks]
            v = v_ref[:, vs]
            ig_c, fg_c, ig_r, fg_r = _gate_vectors(gc, gt, h, lane_i, sub_i)
            n_prev = aux_ref[0, h, 0:1, :]
            m_prev = aux_ref[0, h, 1:2, 0:1]
            c_prev_b = cs_ref[0, h]
            est, s, inter, den, expm, mx, m_new, w_c, decay = _chunk_common(
                q, k, ig_c, fg_c, ig_r, fg_r, m_prev, n_prev, row, col)
            hb = hr_ref[:, vs].astype(F32)
            dyp = dy_ref[:, vs]
            o = o_ref[:, vs].astype(F32)
            z = z_ref[:, vs].astype(F32)
            so = _sigmoid(o)
            sgz = _sigmoid(z)
            sz = z * sgz
            r = lax.rsqrt(jnp.mean(hb * hb, axis=1, keepdims=True) + EPS)
            xh = hb * r
            gh = gh_ref[:, vs]
            hbn = xh * gh
            dyso = dyp * so
            dyso_h = dyso * hbn
            d_o = dyso_h * sz * (1.0 - so)
            d_z = dyso_h * sgz * (1.0 + z * (1.0 - sgz))
            dhbn = dyso * sz
            dgh_ref[0:1, vs] += jnp.sum(dhbn * xh, axis=0, keepdims=True)
            dxh = dhbn * gh
            dh = r * (dxh - xh * jnp.mean(dxh * xh, axis=1, keepdims=True))
            dnm = dh / mx
            hd = jnp.sum(dh * hb, axis=1, keepdims=True)
            cond = jnp.abs(den) > expm
            dden = jnp.where(cond, -hd / mx * jnp.sign(den), 0.0)
            dnm_b = dnm.astype(BF)
            p = _dot_nt(dnm_b, v) + dden
            dqk = (p * est * QK_SCALE).astype(BF)
            dq_inter = (inter * QK_SCALE) * (_dot_nt(dnm_b, c_prev_b) + dden * n_prev)
            dq = _dot(dqk, k) + dq_inter
            dc_new = dc_scr[h]
            dc_new_b = dc_new.astype(BF)
            dn_new = dn_scr[h, 0:1, :]
            kf = k.astype(F32)
            dk_state = w_c * (_dot_nt(v, dc_new_b) + dn_new)
            dk = _dot_tn(dqk, q) + dk_state
            dv = _dot_tn(s.astype(BF), dnm_b) + w_c * _dot(k, dc_new_b)
            dv1_r = jnp.sum(s * dden, axis=0, keepdims=True)
            dv1_c = (jnp.sum(jnp.where(eye, dv1_r, 0.0), axis=1, keepdims=True)
                     + w_c * jnp.sum(kf * dn_new, axis=1, keepdims=True))
            dli_c = jnp.sum(v.astype(F32) * dv, axis=1, keepdims=True) + dv1_c
            hmat = _dot((p * s).astype(BF), (row < col).astype(BF))
            from_prev_c = jnp.sum(q.astype(F32) * dq_inter, axis=1, keepdims=True)
            to_next_c = jnp.sum(kf * dk_state, axis=1, keepdims=True)
            through = decay * (
                jnp.sum(jnp.sum(dc_new * c_prev_b.astype(F32), axis=1, keepdims=True), axis=0, keepdims=True)
                + jnp.sum(dn_new * n_prev, axis=1, keepdims=True))
            dlf_r = through + jnp.sum(jnp.where(row >= col, hmat + from_prev_c, to_next_c), axis=0, keepdims=True)
            dlf_c = jnp.sum(jnp.where(eye, dlf_r, 0.0), axis=1, keepdims=True)
            dfg_c = dlf_c * _sigmoid(-fg_c)
            dgates = dgates + jnp.where(lane_i == IG_LANE + h, dli_c, 0.0) + jnp.where(
                lane_i == FG_LANE + h, dfg_c, 0.0)
            qi = q.astype(F32) * (inter * QK_SCALE)
            dc_scr[h] = decay * dc_new + _dot_tn(qi.astype(BF), dnm_b)
            dn_scr[h, 0:1, :] = decay * dn_new + jnp.sum(qi * dden, axis=0, keepdims=True)
            dp_ref[:, h * DK:(h + 1) * DK] = dq.astype(BF)
            dp_ref[:, 1024 + h * DK:1024 + (h + 1) * DK] = dk.astype(BF)
            dp_ref[:, 2048 + h * DV:2048 + (h + 1) * DV] = dv.astype(BF)
            dp_ref[:, 4096 + h * DV:4096 + (h + 1) * DV] = d_o.astype(BF)
            dp_ref[:, 6144 + h * DV:6144 + (h + 1) * DV] = d_z.astype(BF)
        dg_ref[...] = dgates
        dbias_ref[0:1, :] += jnp.sum(dgates, axis=0, keepdims=True)

        @pl.when(step == nc - 1)
        def _():
            dwb_ref[...] = dwb_scr[...].astype(BF)

    def rev(c):
        return nc - 1 - c

    return pl.pallas_call(
        body, name="mlstm_bwd", grid=(nc,),
        input_output_aliases={0: 0},
        in_specs=[pl.BlockSpec(memory_space=pl.ANY),
                  pl.BlockSpec((lc, 1024), lambda c: (rev(c), 0)),
                  pl.BlockSpec((lc, 1024), lambda c: (rev(c), 1)),
                  pl.BlockSpec((lc, 2048), lambda c: (rev(c), 1)),
                  pl.BlockSpec((lc, 2048), lambda c: (rev(c), 2)),
                  pl.BlockSpec((lc, 2048), lambda c: (rev(c), 3)),
                  pl.BlockSpec((lc, 128), lambda c: (rev(c), 0)),
                  pl.BlockSpec((1, 128), lambda c: (0, 0)),
                  pl.BlockSpec((1, V_DIM), lambda c: (0, 0)),
                  pl.BlockSpec((lc, V_DIM), lambda c: (rev(c), 0)),
                  pl.BlockSpec((1, N_HEADS, DK, DV), lambda c: (rev(c), 0, 0, 0)),
                  pl.BlockSpec((1, N_HEADS, 8, DK), lambda c: (rev(c), 0, 0, 0)),
                  pl.BlockSpec((lc, V_DIM), lambda c: (rev(c), 0)),
                  pl.BlockSpec((lc, D_MODEL), lambda c: (rev(c), 0)),
                  pl.BlockSpec((V_DIM, D_MODEL), lambda c: (0, 0))],
        out_specs=[pl.BlockSpec((lc, 8192), lambda c: (rev(c), 0)),
                   pl.BlockSpec((lc, 128), lambda c: (rev(c), 0)),
                   pl.BlockSpec((8, 128), lambda c: (0, 0)),
                   pl.BlockSpec((8, V_DIM), lambda c: (0, 0)),
                   pl.BlockSpec((V_DIM, D_MODEL), lambda c: (0, 0))],
        out_shape=[jax.ShapeDtypeStruct((t, N_MAIN), BF), jax.ShapeDtypeStruct((t, 128), F32),
                   jax.ShapeDtypeStruct((8, 128), F32), jax.ShapeDtypeStruct((8, V_DIM), F32),
                   jax.ShapeDtypeStruct((V_DIM, D_MODEL), BF)],
        scratch_shapes=[pltpu.VMEM((N_HEADS, DK, DV), F32), pltpu.VMEM((N_HEADS, 8, DK), F32),
                        pltpu.VMEM((lc, V_DIM), F32), pltpu.VMEM((V_DIM, D_MODEL), F32)],
        compiler_params=_cparams(("arbitrary",)),
    )(dproj, proj, proj, proj, proj, proj, gates, gate_bias, g_head, h_raw, c_states, aux, yb_pre, dyb, w_b)


GA_BLK, GB_BLK = 12, 13


def _full(shape):
    return pl.BlockSpec(shape, lambda i: (0,) * len(shape))


def _rms_backward(xv, g, dhn, dres):
    r = lax.rsqrt(jnp.mean(xv * xv, axis=1, keepdims=True) + EPS)
    xh = xv * r
    dxh = dhn * g
    dx = dres + r * (dxh - xh * jnp.mean(dxh * xh, axis=1, keepdims=True))
    return dx, jnp.sum(dhn * xh, axis=0, keepdims=True)


def _token_chain(ya_pre, w_a, yb, proj, w_o, x, g_ple, w_pg, p, w_pl, target, g_final, *, tm=256):
    t, d = x.shape
    tm = min(tm, t)

    def body(yap_ref, wa_ref, yb_ref, ga_ref, gb_ref, wo_ref, x_ref, gp_ref, wpg_ref, p_ref, wpl_ref, tg_ref,
             gf_ref, mg_ref, hnt_ref, dgp_ref, dpe_ref, dx_ref, dxb_ref, dya_ref, dyb_ref, dyap_ref, o_ref,
             sm_ref):
        i = pl.program_id(0)
        ya = _dot(yap_ref[...], wa_ref[...]).astype(BF).astype(F32)
        yb_v = yb_ref[...].astype(F32)
        sa = _sigmoid(ga_ref[...].astype(F32))
        sb = _sigmoid(gb_ref[...].astype(F32))
        merged = (sa * ya + sb * yb_v).astype(BF)
        mg_ref[...] = merged
        x1 = _dot(merged, wo_ref[...]) + x_ref[...]
        r1 = lax.rsqrt(jnp.mean(x1 * x1, axis=1, keepdims=True) + EPS)
        xh1 = x1 * r1
        gp = gp_ref[...]
        hn2 = xh1 * gp
        hn2_b = hn2.astype(BF)
        hnt_ref[...] = hn2.T.astype(BF)
        gate = _sigmoid(_dot(hn2_b, wpg_ref[...]))
        pe_v = _dot(p_ref[...].astype(BF), wpl_ref[...])
        x2 = x1 + gate * pe_v
        r2 = lax.rsqrt(jnp.mean(x2 * x2, axis=1, keepdims=True) + EPS)
        xh2 = x2 * r2
        gf = gf_ref[...]
        err = xh2 * gf - tg_ref[...]
        dy = err * (1.0 / d)
        dxh2 = dy * gf
        dx2 = r2 * (dxh2 - xh2 * jnp.mean(dxh2 * xh2, axis=1, keepdims=True))
        dgpre = (dx2 * pe_v * gate * (1.0 - gate)).astype(BF)
        dgp_ref[...] = dgpre
        dpe_ref[...] = (dx2 * gate).astype(BF)
        dhn2 = _dot_nt(dgpre, wpg_ref[...])
        dxh1 = dhn2 * gp
        dx1 = dx2 + r1 * (dxh1 - xh1 * jnp.mean(dxh1 * xh1, axis=1, keepdims=True))
        dx_ref[...] = dx1
        dx1_b = dx1.astype(BF)
        dxb_ref[...] = dx1_b
        dm = _dot_nt(dx1_b, wo_ref[...])
        dya = (dm * sa).astype(BF)
        dya_ref[...] = dya
        dyb_ref[...] = (dm * sb).astype(BF)
        o_ref[:, 0:d] = (dm * ya * sa * (1.0 - sa)).astype(BF)
        o_ref[:, d:2 * d] = (dm * yb_v * sb * (1.0 - sb)).astype(BF)
        dyap_ref[...] = _dot_nt(dya, wa_ref[...]).astype(BF)

        @pl.when(i == 0)
        def _():
            sm_ref[...] = jnp.zeros_like(sm_ref)

        sm_ref[0:1, :] += (0.5 / d) * jnp.sum(err * err, axis=0, keepdims=True)
        sm_ref[1:2, :] += jnp.sum(dy * xh2, axis=0, keepdims=True)
        sm_ref[2:3, :] += jnp.sum(dhn2 * xh1, axis=0, keepdims=True)

    row = pl.BlockSpec((tm, d), lambda i: (i, 0))
    bf = jax.ShapeDtypeStruct((t, d), BF)
    return pl.pallas_call(
        body, name="token_chain", grid=(t // tm,),
        in_specs=[row, _full((d, d)), row, pl.BlockSpec((tm, d), lambda i: (i, GA_BLK)),
                  pl.BlockSpec((tm, d), lambda i: (i, GB_BLK)), _full((d, d)), row, _full((1, d)), _full((d, d)),
                  pl.BlockSpec((tm, PLE_DIM), lambda i: (i, 0)), _full((PLE_DIM, d)), row, _full((1, d))],
        out_specs=[row, pl.BlockSpec((d, tm), lambda i: (0, i)), row, row, row, row, row, row, row,
                   pl.BlockSpec((tm, 2 * d), lambda i: (i, 6)), _full((8, d))],
        out_shape=[bf, jax.ShapeDtypeStruct((d, t), BF), bf, bf, jax.ShapeDtypeStruct((t, d), F32), bf, bf, bf, bf,
                   jax.ShapeDtypeStruct((t, N_MAIN), BF), jax.ShapeDtypeStruct((8, d), F32)],
        compiler_params=_cparams(("arbitrary",)),
    )(ya_pre, w_a, yb, proj, proj, w_o, x, g_ple, w_pg, p, w_pl, target, g_final)


def _dhn_mix_bwd(dproj, w_main, dgates, w_gate, x, g_mix, dx1, comm, *, tm=1024, tk=2048):
    t, d = x.shape
    tm = min(tm, t)
    nk = N_MAIN // tk
    ni = t // tm
    n_in = 7 + comm.n

    def body(*refs):
        dp_ref, w_ref, dgt_ref, wg_ref, x_ref, g_ref, dres_ref = refs[:7]
        dx_ref, dg_ref = refs[n_in], refs[n_in + 1]
        acc = refs[-1]
        start, wait = comm.ops(refs[7:n_in], refs[n_in + 2:n_in + 2 + comm.n], *refs[n_in + 2 + comm.n:-1])
        i, k = pl.program_id(0), pl.program_id(1)
        pl.when((i == 0) & (k == 0))(start)

        @pl.when(k == 0)
        def _():
            acc[...] = _dot_nt(dp_ref[...], w_ref[...]) + _dot_nt(dgt_ref[...], wg_ref[...])

        @pl.when(k > 0)
        def _():
            acc[...] += _dot_nt(dp_ref[...], w_ref[...])

        @pl.when((i == 0) & (k == 0))
        def _():
            dg_ref[...] = jnp.zeros_like(dg_ref)

        @pl.when(k == nk - 1)
        def _():
            dx, dg = _rms_backward(x_ref[...], g_ref[...], acc[...], dres_ref[...])
            dx_ref[...] = dx
            dg_ref[0:1, :] += dg

        pl.when((i == ni - 1) & (k == nk - 1))(wait)

    any_spec = pl.BlockSpec(memory_space=pl.ANY)
    row = pl.BlockSpec((tm, d), lambda i, k: (i, 0))
    res = pl.pallas_call(
        body, name="dhn_mix_bwd", grid=(ni, nk),
        in_specs=[pl.BlockSpec((tm, tk), lambda i, k: (i, k)), pl.BlockSpec((d, tk), lambda i, k: (0, k)),
                  pl.BlockSpec((tm, 128), lambda i, k: (i, 0)), pl.BlockSpec((d, 128), lambda i, k: (0, 0)),
                  row, pl.BlockSpec((1, d), lambda i, k: (0, 0)), row] + [any_spec] * comm.n,
        out_specs=[row, pl.BlockSpec((8, d), lambda i, k: (0, 0))] + [any_spec] * comm.n,
        out_shape=[jax.ShapeDtypeStruct((t, d), F32), jax.ShapeDtypeStruct((8, d), F32)] + comm.out_shapes,
        scratch_shapes=comm.scratch + [pltpu.VMEM((tm, d), F32)],
        compiler_params=_cparams(("arbitrary", "arbitrary")),
    )(dproj, w_main, dgates, w_gate, x, g_mix, dx1, *comm.srcs)
    return res[0], res[1], list(res[2:])


def _position():
    x, y, c = lax.axis_index("x"), lax.axis_index("y"), lax.axis_index("c")
    return x, y, c


def _all_gather(srcs):
    nb = len(srcs)
    any_spec = pl.BlockSpec(memory_space=pl.ANY)

    def body(*refs):
        src = refs[:nb]
        dst = refs[nb:2 * nb]
        send_sems, recv_sems, local_sems = refs[2 * nb:]
        x, y, c = _position()
        me, sibling = (x, y, c), (x, y, 1 - c)
        chips = [(1 - x, y), (x, 1 - y), (1 - x, 1 - y)]

        def slot(b, px, py, pc):
            return dst[b].at[4 * px + 2 * py + pc]

        def copy(k, b, block, to, from_src=False):
            return pltpu.make_async_remote_copy(
                src_ref=src[b] if from_src else slot(b, *block), dst_ref=slot(b, *block),
                send_sem=send_sems.at[b, k], recv_sem=recv_sems.at[b, k],
                device_id=to, device_id_type=MESH)

        mine = [pltpu.make_async_copy(src[b], slot(b, *me), local_sems.at[b]) for b in range(nb)]
        for cp in mine:
            cp.start()
        first = [copy(0, b, me, sibling, True) for b in range(nb)]
        first += [copy(1 + j, b, me, (*chip, c), True) for j, chip in enumerate(chips) for b in range(nb)]
        for cp in first:
            cp.start()
        passed = []
        for j, chip in enumerate(chips):
            for b in range(nb):
                copy(1 + j, b, (*chip, c), me).wait_recv()
                fwd = copy(4 + j, b, (*chip, c), sibling)
                fwd.start()
                passed.append(fwd)
        for b in range(nb):
            copy(0, b, sibling, me).wait_recv()
        for j, chip in enumerate(chips):
            for b in range(nb):
                copy(4 + j, b, (*chip, 1 - c), me).wait_recv()
        for cp in first + passed:
            cp.wait_send()
        for cp in mine:
            cp.wait()

    return pl.pallas_call(
        body, name="weights_all_gather",
        in_specs=[any_spec] * nb, out_specs=[any_spec] * nb,
        out_shape=[jax.ShapeDtypeStruct((N_DEV,) + s.shape, s.dtype) for s in srcs],
        scratch_shapes=[pltpu.SemaphoreType.DMA((nb, 7)), pltpu.SemaphoreType.DMA((nb, 7)),
                        pltpu.SemaphoreType.DMA((nb,))],
    )(*srcs)


def _comm_call(comm, *, name):
    any_spec = pl.BlockSpec(memory_space=pl.ANY)

    def body(*refs):
        start, wait = comm.ops(refs[:comm.n], refs[comm.n:2 * comm.n], *refs[2 * comm.n:])
        start()
        wait()

    return pl.pallas_call(
        body, name=name, in_specs=[any_spec] * comm.n, out_specs=[any_spec] * comm.n,
        out_shape=comm.out_shapes, scratch_shapes=comm.scratch,
    )(*comm.srcs)


def _sum_slots(recv, *, name, tr):
    _, r, cdim = recv.shape
    tr = min(tr, r)

    def body(r_ref, o_ref):
        total = r_ref[0].astype(F32)
        for s in range(1, N_DEV):
            total = total + r_ref[s].astype(F32)
        o_ref[...] = total

    return pl.pallas_call(
        body, name=name, grid=(r // tr,),
        in_specs=[pl.BlockSpec((N_DEV, tr, cdim), lambda i: (0, i, 0))],
        out_specs=pl.BlockSpec((tr, cdim), lambda i: (i, 0)),
        out_shape=jax.ShapeDtypeStruct((r, cdim), F32),
        compiler_params=_cparams(("parallel",)),
    )(recv)


def _adamw(w, g, m, v, *, name):
    lead = w.ndim - 2
    r, cdim = w.shape[-2:]
    if r % 128 == 0:
        tr, tc = 128, cdim
    elif cdim % 128 == 0:
        tr, tc = r, 128
    else:
        tr, tc = r, cdim
    c1 = 1.0 - ADAM_B1 ** ADAM_STEP
    c2 = 1.0 - ADAM_B2 ** ADAM_STEP

    def body(w_ref, g_ref, m_ref, v_ref, d_ref, mo_ref, vo_ref):
        gv = g_ref[...]
        mn = ADAM_B1 * m_ref[...] + (1.0 - ADAM_B1) * gv
        vn = ADAM_B2 * v_ref[...] + (1.0 - ADAM_B2) * (gv * gv)
        d_ref[...] = -ADAM_LR * ((mn / c1) / (jnp.sqrt(vn / c2) + ADAM_EPS) + ADAM_WD * w_ref[...])
        mo_ref[...] = mn
        vo_ref[...] = vn

    blk = pl.BlockSpec((1,) * lead + (tr, tc), lambda i, j: (0,) * lead + (i, j))
    shp = jax.ShapeDtypeStruct(w.shape, F32)
    return pl.pallas_call(
        body, name=name, grid=(r // tr, cdim // tc),
        in_specs=[blk] * 4, out_specs=[blk] * 3, out_shape=[shp] * 3,
        compiler_params=_cparams(("parallel", "parallel")),
    )(w, g, m, v)


def kernel(x, p, g_mix, w_in, conv_w, conv_b, w_a_out, b_gates, g_head, w_b_out, w_o, g_ple, w_ple_gate, w_ple, g_final, loss_target, m_g_mix, m_w_in, m_conv_w, m_conv_b, m_w_a_out, m_b_gates, m_g_head, m_w_b_out, m_w_o, m_g_ple, m_w_ple_gate, m_w_ple, m_g_final, v_g_mix, v_w_in, v_conv_w, v_conv_b, v_w_a_out, v_b_gates, v_g_head, v_w_b_out, v_w_o, v_g_ple, v_w_ple_gate, v_w_ple, v_g_final):
    d = D_MODEL
    t = x.shape[1]
    x2d = x.reshape(t, d)
    p2d = p.reshape(t, PLE_DIM)
    tgt = loss_target.reshape(t, d)

    win = jnp.pad(w_in[0].astype(BF), ((0, 0), (0, WIN_W - SHARD_W)))
    rows = jnp.concatenate([w_a_out[0].astype(BF), w_b_out[0].astype(BF), w_o[0].astype(BF),
                            w_ple_gate[0].astype(BF), w_ple[0].astype(BF).reshape(32, d)], axis=0)
    cfl = jnp.pad(conv_w[0], ((0, 5), (0, 0)))
    g_win, g_cf = _all_gather([win, cfl])

    w_glob = jnp.concatenate([g_win[k, :, :SHARD_W] for k in range(N_DEV)]
                             + [jnp.zeros((d, 14464 - N_IN), BF)], axis=1)
    tail = jnp.roll(w_glob[:, 12288:], -8, axis=1)
    w_main = jnp.concatenate([w_glob[:, 4096:12288], w_glob[:, 0:4096], tail[:, :2048]], axis=1)
    w_gate = tail[:, 2048:]
    conv_w8 = jnp.pad(g_cf[:, :3, :].transpose(1, 0, 2).reshape(3, d), ((0, 5), (0, 0)))
    gate_bias = jnp.pad(b_gates, ((0, 0), (IG_LANE, 0)))

    hn, hnt = _rms_fwd(x2d, g_mix, name="rms_mix")
    proj, gates, (g_rows,) = _mm(hn, w_main, form="nn", out_dtype=BF, name="proj", tm=2048,
                                 extra=(w_gate, F32), comm=_DirectComm([rows], "gather"))
    w_a = g_rows[:, 0:128].reshape(d, d)
    w_b = g_rows[:, 128:384].reshape(V_DIM, d)
    w_of = g_rows[:, 384:512].reshape(d, d)
    w_pg = g_rows[:, 512:640].reshape(d, d)
    w_pl = g_rows[:, 640:672].reshape(N_DEV, PLE_DIM, 128).transpose(1, 0, 2).reshape(PLE_DIM, d)
    ya_pre = _branch_a_fwd(proj, conv_w8, conv_b)
    yb_pre, yb, h_raw, c_states, aux = _mlstm_fwd(proj, gates, gate_bias, g_head, w_b)
    (merged, hn2t, dgpre, dpe, dx1, dx1b, dya, dyb, dya_pre, dproj, small_fin) = _token_chain(
        ya_pre, w_a, yb, proj, w_of, x2d, g_ple, w_pg, p2d, w_pl, tgt, g_final.reshape(1, d))

    dw_pg = _mm(hn2t, dgpre, form="nn", out_dtype=BF, name="dw_pg")
    dw_pl = _mm(p2d, dpe, form="tn", out_dtype=BF, name="dw_ple")
    dw_o = _mm(merged, dx1b, form="tn", out_dtype=BF, name="dw_o")
    dw_a = _mm(ya_pre, dya, form="tn", out_dtype=BF, name="dw_a")
    dproj, dconv = _branch_a_bwd(dproj, proj, dya_pre, conv_w8, conv_b)
    dproj, dgates, dbias, dg_head, dw_b = _mlstm_bwd(dproj, proj, gates, gate_bias, g_head, h_raw, c_states, aux,
                                                     yb_pre, dyb, w_b)
    s_rows = jnp.concatenate([
        dw_a.reshape(N_DEV, 128, d), dw_b.reshape(N_DEV, 256, d), dw_o.reshape(N_DEV, 128, d),
        dw_pg.reshape(N_DEV, 128, d),
        dw_pl.reshape(PLE_DIM, N_DEV, 128).transpose(1, 0, 2).reshape(N_DEV, 32, d)], axis=1)
    dgates_b = dgates.astype(BF)
    dw_main, dw_gate, (r_rows,) = _mm(hnt, dproj, form="nn", out_dtype=BF, name="dw_main", tk=2048,
                                      extra=(dgates_b, BF), comm=_DirectComm([s_rows], "exchange"))
    tail_g = jnp.roll(jnp.concatenate([dw_main[:, 12288:], dw_gate], axis=1), 8, axis=1)
    dw_glob = jnp.concatenate([dw_main[:, 8192:12288], dw_main[:, 0:8192], tail_g], axis=1)
    s_win = jnp.stack([jnp.pad(dw_glob[:, SHARD_W * j:SHARD_W * (j + 1)], ((0, 0), (0, WIN_W - SHARD_W)))
                       for j in range(N_DEV)])
    grad_x, dg_mix, (r_win,) = _dhn_mix_bwd(dproj, w_main, dgates_b, w_gate, x2d, g_mix, dx1,
                                            _DirectComm([s_win], "exchange"))

    vec = jnp.concatenate([dg_mix[0], dconv[3], dg_head[0], small_fin[2], small_fin[1],
                           dbias[0, IG_LANE:], jnp.zeros((7 * d - 6152,), F32)]).reshape(7, d)
    conv_part = jnp.pad(dconv[:3].reshape(3, N_DEV, 128).transpose(1, 0, 2).reshape(N_DEV, 1, 384),
                        ((0, 0), (0, 0), (0, d - 384)))
    s_f32 = jnp.concatenate([jnp.broadcast_to(vec[None], (N_DEV, 7, d)), conv_part], axis=1)
    (r_f32,) = _comm_call(_DirectComm([s_f32], "exchange"), name="small_grads_exchange")
    sum_win = _sum_slots(r_win, name="sum_win", tr=128)
    sum_rows = _sum_slots(r_rows, name="sum_rows", tr=96)
    sum_f32 = _sum_slots(r_f32, name="sum_f32", tr=8)

    g_w_in = sum_win[:, :SHARD_W]
    g_w_a = sum_rows[0:128]
    g_w_b = sum_rows[128:384]
    g_w_o = sum_rows[384:512]
    g_w_pg = sum_rows[512:640]
    g_w_pl = sum_rows[640:672].reshape(PLE_DIM, 128)
    vsum = sum_f32[:7].reshape(7 * d)
    g_g_mix = vsum[0:1024].reshape(1, d)
    g_conv_b = vsum[1024:2048].reshape(1, d)
    g_g_head = vsum[2048:4096].reshape(1, V_DIM)
    g_g_ple = vsum[4096:5120].reshape(1, d)
    g_g_final = vsum[5120:6144].reshape(1, d)
    g_b_gates = vsum[6144:6152].reshape(1, 8)
    g_conv_w = sum_f32[7, :384].reshape(3, 128)

    loss = lax.psum(jnp.sum(small_fin[0]), ("x", "y", "c"))

    names = ["g_mix", "w_in", "conv_w", "conv_b", "w_a_out", "b_gates", "g_head", "w_b_out", "w_o", "g_ple",
             "w_ple_gate", "w_ple", "g_final"]
    weights = [g_mix, w_in, conv_w, conv_b, w_a_out, b_gates, g_head, w_b_out, w_o, g_ple, w_ple_gate, w_ple,
               g_final]
    moms = [m_g_mix, m_w_in, m_conv_w, m_conv_b, m_w_a_out, m_b_gates, m_g_head, m_w_b_out, m_w_o, m_g_ple,
            m_w_ple_gate, m_w_ple, m_g_final]
    vels = [v_g_mix, v_w_in, v_conv_w, v_conv_b, v_w_a_out, v_b_gates, v_g_head, v_w_b_out, v_w_o, v_g_ple,
            v_w_ple_gate, v_w_ple, v_g_final]
    grads2d = [g_g_mix, g_w_in, g_conv_w, g_conv_b, g_w_a, g_b_gates, g_g_head, g_w_b, g_w_o, g_g_ple, g_w_pg,
               g_w_pl, g_g_final]
    grads, deltas, new_m, new_v = [], [], [], []
    for nm, w, m_, v_, g2 in zip(names, weights, moms, vels, grads2d):
        shp = w.shape
        if nm == "w_in":
            dl, mn, vn = _adamw(w[0].T, g2.T, m_[0].T, v_[0].T, name="adamw_" + nm)
            grads.append(g2.reshape(shp))
            deltas.append(dl.T.reshape(shp))
            new_m.append(mn.T.reshape(shp))
            new_v.append(vn.T.reshape(shp))
            continue
        kshp = shp if w.ndim >= 2 else (1,) + shp
        gk = g2.reshape(kshp)
        dl, mn, vn = _adamw(w.reshape(kshp), gk, m_.reshape(kshp), v_.reshape(kshp), name="adamw_" + nm)
        grads.append(gk.reshape(shp))
        deltas.append(dl.reshape(shp))
        new_m.append(mn.reshape(shp))
        new_v.append(vn.reshape(shp))
    return (loss, grad_x.reshape(x.shape), *grads, *deltas, *new_m, *new_v)
```

```python
import functools

import jax
import jax.numpy as jnp
from jax import lax
from jax.experimental import pallas as pl
from jax.experimental.pallas import tpu as pltpu

F32 = jnp.float32
BF = jnp.bfloat16

D_MODEL = 1024
N_HEADS = 4
DK = 256
DV = 512
V_DIM = 2048
PLE_DIM = 256
N_IN = 14344
N_MAIN = 14336
EPS = 1e-6
QK_SCALE = DK ** -0.5
NEG = -1e30
N_DEV = 8
SHARD_W = 1793
WIN_STRIDE = 1792
WIN_W = 1920
ROWS_PACK = 672
_CHUNK = 256
IG_LANE = 120
FG_LANE = 124

ADAM_LR = 0.001
ADAM_B1 = 0.9
ADAM_B2 = 0.999
ADAM_EPS = 1e-08
ADAM_WD = 0.01
ADAM_STEP = 10

VMEM_LIMIT = 56 * 1024 * 1024
MESH = pl.DeviceIdType.MESH


def _cparams(sem):
    return pltpu.CompilerParams(dimension_semantics=sem, vmem_limit_bytes=VMEM_LIMIT)


def _sigmoid(x):
    return 1.0 / (1.0 + jnp.exp(-x))


def _log_sigmoid(x):
    return jnp.minimum(x, 0.0) - jnp.log(1.0 + jnp.exp(-jnp.abs(x)))


def _dot(a, b):
    return jnp.dot(a, b, preferred_element_type=F32)


def _dot_nt(a, b):
    return lax.dot_general(a, b, (((1,), (1,)), ((), ())), preferred_element_type=F32)


def _dot_tn(a, b):
    return lax.dot_general(a, b, (((0,), (0,)), ((), ())), preferred_element_type=F32)


class _DirectComm:
    def __init__(self, srcs, kind):
        self.srcs = list(srcs)
        self.kind = kind
        self.n = len(self.srcs)
        if kind == "exchange":
            self.out_shapes = [jax.ShapeDtypeStruct(s.shape, s.dtype) for s in self.srcs]
        else:
            self.out_shapes = [jax.ShapeDtypeStruct((N_DEV,) + s.shape, s.dtype) for s in self.srcs]
        self.scratch = [pltpu.SemaphoreType.DMA((self.n, 7)), pltpu.SemaphoreType.DMA((self.n, 7)),
                        pltpu.SemaphoreType.DMA((self.n,))]

    def ops(self, src, dst, send_sems, recv_sems, local_sems):
        exchange = self.kind == "exchange"

        def descriptors():
            x, y, c = _position()
            me_lin = 4 * x + 2 * y + c
            local = [pltpu.make_async_copy(src[b].at[me_lin] if exchange else src[b], dst[b].at[me_lin],
                                           local_sems.at[b]) for b in range(self.n)]
            sends, recvs = [], []
            for f in range(1, N_DEV):
                px = (1 - x) if (f >> 2) & 1 else x
                py = (1 - y) if (f >> 1) & 1 else y
                pc = (1 - c) if f & 1 else c
                peer_lin = 4 * px + 2 * py + pc
                for b in range(self.n):
                    out = src[b].at[peer_lin] if exchange else src[b]
                    common = dict(send_sem=send_sems.at[b, f - 1], recv_sem=recv_sems.at[b, f - 1],
                                  device_id=(px, py, pc), device_id_type=MESH)
                    sends.append(pltpu.make_async_remote_copy(src_ref=out, dst_ref=dst[b].at[me_lin], **common))
                    recvs.append(pltpu.make_async_remote_copy(src_ref=out, dst_ref=dst[b].at[peer_lin], **common))
            return local, sends, recvs

        def start():
            local, sends, _ = descriptors()
            for cp in local + sends:
                cp.start()

        def wait():
            local, sends, recvs = descriptors()
            for cp in recvs:
                cp.wait_recv()
            for cp in sends:
                cp.wait_send()
            for cp in local:
                cp.wait()

        return start, wait


def _mm(a, b, *, form, out_dtype, name, tm=1024, tn=1024, tk=1024, add=None, extra=None, comm=None):
    assert extra is None or form == "nn"
    if form == "nn":
        m, kc = a.shape
        n = b.shape[1]
    elif form == "nt":
        m, kc = a.shape
        n = b.shape[0]
    else:
        kc, m = a.shape
        n = b.shape[1]
    tm, tn, tk = min(tm, m), min(tn, n), min(tk, kc)
    assert m % tm == 0 and n % tn == 0 and kc % tk == 0, (name, a.shape, b.shape)
    nk = kc // tk
    if form == "tn":
        a_spec = pl.BlockSpec((tk, tm), lambda i, j, k: (k, i))
    else:
        a_spec = pl.BlockSpec((tm, tk), lambda i, j, k: (i, k))
    if form == "nt":
        b_spec = pl.BlockSpec((tn, tk), lambda i, j, k: (j, k))
    else:
        b_spec = pl.BlockSpec((tk, tn), lambda i, j, k: (k, j))
    o_spec = pl.BlockSpec((tm, tn), lambda i, j, k: (i, j))
    dot = {"nn": _dot, "nt": _dot_nt, "tn": _dot_tn}[form]
    has_add = add is not None

    use_acc = nk > 1 and (has_add or out_dtype != F32)
    has_x = extra is not None
    n2 = extra[0].shape[1] if has_x else 0
    use_acc2 = has_x and nk > 1 and extra[1] != F32
    n_comm = comm.n if comm else 0
    n_in = 2 + int(has_add) + int(has_x) + n_comm
    n_out = 1 + int(has_x) + n_comm
    grid = (m // tm, n // tn, nk)

    def body(*refs):
        a_ref, b_ref = refs[0], refs[1]
        add_ref = refs[2] if has_add else None
        b2_ref = refs[2 + int(has_add)] if has_x else None
        o_ref = refs[n_in]
        o2_ref = refs[n_in + 1] if has_x else None
        scr = list(refs[n_in + n_out + (3 if comm else 0):])
        acc = scr.pop(0) if use_acc else o_ref
        acc2 = scr.pop(0) if use_acc2 else o2_ref
        i, j, k = pl.program_id(0), pl.program_id(1), pl.program_id(2)
        if comm:
            start, wait = comm.ops(refs[n_in - n_comm:n_in], refs[n_in + n_out - n_comm:n_in + n_out],
                                   *refs[n_in + n_out:n_in + n_out + 3])
            pl.when((i == 0) & (j == 0) & (k == 0))(start)

        def part():
            return dot(a_ref[...].astype(BF), b_ref[...].astype(BF))

        def part2():
            return dot(a_ref[...].astype(BF), b2_ref[...].astype(BF))

        def finish(total):
            if has_add:
                total = total + add_ref[...].astype(F32)
            o_ref[...] = total.astype(out_dtype)

        if nk == 1:
            finish(part())
            if has_x:
                @pl.when(j == 0)
                def _():
                    o2_ref[...] = part2().astype(extra[1])
        else:
            @pl.when(k == 0)
            def _():
                acc[...] = part()

            @pl.when(k > 0)
            def _():
                acc[...] += part()

            if use_acc:
                @pl.when(k == nk - 1)
                def _():
                    finish(acc[...])
            if has_x:
                @pl.when((j == 0) & (k == 0))
                def _():
                    acc2[...] = part2()

                @pl.when((j == 0) & (k > 0))
                def _():
                    acc2[...] += part2()

                if use_acc2:
                    @pl.when((j == 0) & (k == nk - 1))
                    def _():
                        o2_ref[...] = acc2[...].astype(extra[1])
        if comm:
            pl.when((i == grid[0] - 1) & (j == grid[1] - 1) & (k == nk - 1))(wait)

    any_spec = pl.BlockSpec(memory_space=pl.ANY)
    o2_spec = pl.BlockSpec((tm, n2), lambda i, j, k: (i, 0))
    in_specs = ([a_spec, b_spec] + ([o_spec] if has_add else [])
                + ([pl.BlockSpec((tk, n2), lambda i, j, k: (k, 0))] if has_x else []) + [any_spec] * n_comm)
    args = (a, b) + ((add,) if has_add else ()) + ((extra[0],) if has_x else ()) + (tuple(comm.srcs) if comm else ())
    out_specs = [o_spec] + ([o2_spec] if has_x else []) + [any_spec] * n_comm
    out_shape = ([jax.ShapeDtypeStruct((m, n), out_dtype)]
                 + ([jax.ShapeDtypeStruct((m, n2), extra[1])] if has_x else []) + (comm.out_shapes if comm else []))
    scratch = ((comm.scratch if comm else []) + ([pltpu.VMEM((tm, tn), F32)] if use_acc else [])
               + ([pltpu.VMEM((tm, n2), F32)] if use_acc2 else []))
    if comm:
        sem = ("arbitrary",) * 3
    else:
        sem = ("parallel", "arbitrary" if has_x else "parallel", "arbitrary")
    res = pl.pallas_call(
        body, name=name, grid=grid, in_specs=in_specs, out_specs=out_specs, out_shape=out_shape,
        scratch_shapes=scratch, compiler_params=_cparams(sem),
    )(*args)
    if not (comm or has_x):
        return res[0]
    return tuple(res[:1 + int(has_x)]) + ((list(res[1 + int(has_x):]),) if comm else ())


def _rms_fwd(x, g, *, name, tm=512):
    t, d = x.shape
    tm = min(tm, t)

    def body(x_ref, g_ref, hn_ref, hnt_ref):
        xv = x_ref[...]
        r = lax.rsqrt(jnp.mean(xv * xv, axis=1, keepdims=True) + EPS)
        hn = xv * r * g_ref[...]
        hn_ref[...] = hn.astype(BF)
        hnt_ref[...] = hn.T.astype(BF)

    return pl.pallas_call(
        body, name=name, grid=(t // tm,),
        in_specs=[pl.BlockSpec((tm, d), lambda i: (i, 0)), pl.BlockSpec((1, d), lambda i: (0, 0))],
        out_specs=[pl.BlockSpec((tm, d), lambda i: (i, 0)), pl.BlockSpec((d, tm), lambda i: (0, i))],
        out_shape=[jax.ShapeDtypeStruct((t, d), BF), jax.ShapeDtypeStruct((d, t), BF)],
        compiler_params=_cparams(("parallel",)),
    )(x, g)


HALO = 16
XA_BLK, BA_BLK, CA_BLK, ZA_BLK = 8, 9, 10, 11


def _shift_down(u, prev, n):
    tm = u.shape[0]
    rolled = pltpu.roll(u, n, 0)
    row = lax.broadcasted_iota(jnp.int32, u.shape, 0)
    out = rolled
    for j in range(n):
        out = jnp.where(row == j, prev[HALO - n + j:HALO - n + j + 1, :], out)
    return out


def _shift_up(u, nxt, n):
    tm = u.shape[0]
    rolled = pltpu.roll(u, tm - n, 0)
    row = lax.broadcasted_iota(jnp.int32, u.shape, 0)
    out = rolled
    for j in range(n):
        out = jnp.where(row == tm - n + j, nxt[j:j + 1, :], out)
    return out


def _branch_a_fwd(proj, conv_w8, conv_b, *, tm=512):
    t = proj.shape[0]
    d = D_MODEL
    tm = min(tm, t)
    hb = tm // HALO

    def body(xa_ref, ba_ref, ca_ref, za_ref, xap_ref, cap_ref, w_ref, b_ref, o_ref):
        i = pl.program_id(0)
        u = ca_ref[...].astype(F32) * xa_ref[...].astype(F32)
        up = cap_ref[...].astype(F32) * xap_ref[...].astype(F32)
        up = jnp.where(i == 0, 0.0, up)
        u1 = _shift_down(u, up, 1)
        u2 = _shift_down(u, up, 2)
        cv = w_ref[0:1, :] * u2 + w_ref[1:2, :] * u1 + w_ref[2:3, :] * u + b_ref[...]
        za = za_ref[...].astype(F32)
        o_ref[...] = (ba_ref[...].astype(F32) * cv * (za * _sigmoid(za))).astype(BF)

    def col(blk):
        return pl.BlockSpec((tm, d), lambda i: (i, blk))

    def prev(blk):
        return pl.BlockSpec((HALO, d), lambda i: (jnp.maximum(i * hb - 1, 0), blk))

    return pl.pallas_call(
        body, name="branch_a_fwd", grid=(t // tm,),
        in_specs=[col(XA_BLK), col(BA_BLK), col(CA_BLK), col(ZA_BLK), prev(XA_BLK), prev(CA_BLK),
                  pl.BlockSpec((8, d), lambda i: (0, 0)), pl.BlockSpec((1, d), lambda i: (0, 0))],
        out_specs=pl.BlockSpec((tm, d), lambda i: (i, 0)),
        out_shape=jax.ShapeDtypeStruct((t, d), BF),
        compiler_params=_cparams(("parallel",)),
    )(proj, proj, proj, proj, proj, proj, conv_w8, conv_b)


def _branch_a_bwd(dproj, proj, dya_pre, conv_w8, conv_b, *, tm=512):
    t = proj.shape[0]
    d = D_MODEL
    tm = min(tm, t)
    hb = tm // HALO
    nt = t // tm

    def body(dp_ref, xa_ref, ba_ref, ca_ref, za_ref, xap_ref, cap_ref, dy_ref, ban_ref, zan_ref, dyn_ref,
             w_ref, b_ref, o_ref, dc_ref):
        del dp_ref
        i = pl.program_id(0)
        xa = xa_ref[...].astype(F32)
        ca = ca_ref[...].astype(F32)
        ba = ba_ref[...].astype(F32)
        za = za_ref[...].astype(F32)
        u = ca * xa
        up = cap_ref[...].astype(F32) * xap_ref[...].astype(F32)
        up = jnp.where(i == 0, 0.0, up)
        u1 = _shift_down(u, up, 1)
        u2 = _shift_down(u, up, 2)
        w0, w1, w2 = w_ref[0:1, :], w_ref[1:2, :], w_ref[2:3, :]
        cv = w0 * u2 + w1 * u1 + w2 * u + b_ref[...]
        sg = _sigmoid(za)
        sz = za * sg
        dy = dy_ref[...].astype(F32)
        dcv = dy * ba * sz
        zan = zan_ref[...].astype(F32)
        dcvn = dyn_ref[...].astype(F32) * ban_ref[...].astype(F32) * (zan * _sigmoid(zan))
        dcvn = jnp.where(i == nt - 1, 0.0, dcvn)
        du = w2 * dcv + w1 * _shift_up(dcv, dcvn, 1) + w0 * _shift_up(dcv, dcvn, 2)
        o_ref[:, 0:d] = (du * ca).astype(BF)
        o_ref[:, d:2 * d] = (dy * cv * sz).astype(BF)
        o_ref[:, 2 * d:3 * d] = (du * xa).astype(BF)
        o_ref[:, 3 * d:4 * d] = (dy * ba * cv * sg * (1.0 + za * (1.0 - sg))).astype(BF)

        @pl.when(i == 0)
        def _():
            dc_ref[...] = jnp.zeros_like(dc_ref)

        dc_ref[0:1, :] += jnp.sum(dcv * u2, axis=0, keepdims=True)
        dc_ref[1:2, :] += jnp.sum(dcv * u1, axis=0, keepdims=True)
        dc_ref[2:3, :] += jnp.sum(dcv * u, axis=0, keepdims=True)
        dc_ref[3:4, :] += jnp.sum(dcv, axis=0, keepdims=True)

    def col(blk):
        return pl.BlockSpec((tm, d), lambda i: (i, blk))

    def prev(blk):
        return pl.BlockSpec((HALO, d), lambda i: (jnp.maximum(i * hb - 1, 0), blk))

    def nxt(blk):
        return pl.BlockSpec((HALO, d), lambda i: (jnp.minimum((i + 1) * hb, t // HALO - 1), blk))

    return pl.pallas_call(
        body, name="branch_a_bwd", grid=(nt,),
        in_specs=[pl.BlockSpec(memory_space=pl.ANY),
                  col(XA_BLK), col(BA_BLK), col(CA_BLK), col(ZA_BLK), prev(XA_BLK), prev(CA_BLK),
                  pl.BlockSpec((tm, d), lambda i: (i, 0)), nxt(BA_BLK), nxt(ZA_BLK),
                  pl.BlockSpec((HALO, d), lambda i: (jnp.minimum((i + 1) * hb, t // HALO - 1), 0)),
                  pl.BlockSpec((8, d), lambda i: (0, 0)), pl.BlockSpec((1, d), lambda i: (0, 0))],
        out_specs=[pl.BlockSpec((tm, 4 * d), lambda i: (i, 2)), pl.BlockSpec((8, d), lambda i: (0, 0))],
        out_shape=[jax.ShapeDtypeStruct(dproj.shape, BF), jax.ShapeDtypeStruct((8, d), F32)],
        input_output_aliases={0: 0},
        compiler_params=_cparams(("arbitrary",)),
    )(dproj, proj, proj, proj, proj, proj, proj, dya_pre, proj, proj, dya_pre, conv_w8, conv_b)


def _gate_vectors(gc, gt, h, lane_i, sub_i):
    ig_c = jnp.sum(jnp.where(lane_i == IG_LANE + h, gc, 0.0), axis=1, keepdims=True)
    fg_c = jnp.sum(jnp.where(lane_i == FG_LANE + h, gc, 0.0), axis=1, keepdims=True)
    ig_r = jnp.sum(jnp.where(sub_i == IG_LANE + h, gt, 0.0), axis=0, keepdims=True)
    fg_r = jnp.sum(jnp.where(sub_i == FG_LANE + h, gt, 0.0), axis=0, keepdims=True)
    return ig_c, fg_c, ig_r, fg_r


def _chunk_common(q, k, ig_c, fg_c, ig_r, fg_r, m_prev, n_prev, row, col):
    lf_c = _log_sigmoid(fg_c)
    lf_r = _log_sigmoid(fg_r)
    causal = col <= row
    b_c = jnp.sum(jnp.where(causal, lf_r, 0.0), axis=1, keepdims=True)
    b_r = jnp.sum(jnp.where(row <= col, lf_c, 0.0), axis=0, keepdims=True)
    dmat = jnp.where(causal, b_c - b_r + ig_r, NEG)
    a = b_c + m_prev
    m_row = jnp.maximum(a, jnp.max(dmat, axis=1, keepdims=True))
    est = jnp.exp(dmat - m_row)
    s = _dot_nt(q, k) * QK_SCALE * est
    inter = jnp.exp(a - m_row)
    den = jnp.sum(s, axis=1, keepdims=True) + inter * QK_SCALE * jnp.sum(
        q.astype(F32) * n_prev, axis=1, keepdims=True)
    expm = jnp.exp(-m_row)
    mx = jnp.maximum(jnp.abs(den), expm)
    b_last = jnp.sum(lf_r, axis=1, keepdims=True)
    g_r = b_last - b_r + ig_r
    g_c = b_last - b_c + ig_c
    m_new = jnp.maximum(b_last + m_prev, jnp.max(g_r, axis=1, keepdims=True))
    w_c = jnp.exp(g_c - m_new)
    decay = jnp.exp(b_last + m_prev - m_new)
    return est, s, inter, den, expm, mx, m_new, w_c, decay


def _mlstm_fwd(proj, gates, gate_bias, g_head, w_b):
    t = proj.shape[0]
    lc = min(_CHUNK, t)
    nc = t // lc

    def body(q_ref, k_ref, v_ref, o_ref, z_ref, g_ref, gb_ref, gh_ref, wb_ref,
             yb_ref, ybo_ref, hr_ref, cs_ref, aux_ref, c_scr, nm_scr):
        c = pl.program_id(0)

        @pl.when(c == 0)
        def _():
            c_scr[...] = jnp.zeros_like(c_scr)
            nm_scr[...] = jnp.zeros_like(nm_scr)
            nm_scr[:, 1:2, :] = jnp.full((N_HEADS, 1, DK), NEG, F32)

        row = lax.broadcasted_iota(jnp.int32, (lc, lc), 0)
        col = lax.broadcasted_iota(jnp.int32, (lc, lc), 1)
        gc = g_ref[...] + gb_ref[...]
        gt = gc.T
        lane_i = lax.broadcasted_iota(jnp.int32, gc.shape, 1)
        sub_i = lax.broadcasted_iota(jnp.int32, gt.shape, 0)
        for h in range(N_HEADS):
            ks = slice(h * DK, (h + 1) * DK)
            vs = slice(h * DV, (h + 1) * DV)
            q = q_ref[:, ks]
            k = k_ref[:, ks]
            v = v_ref[:, vs]
            ig_c, fg_c, ig_r, fg_r = _gate_vectors(gc, gt, h, lane_i, sub_i)
            n_prev = nm_scr[h, 0:1, :]
            m_prev = nm_scr[h, 1:2, 0:1]
            est, s, inter, den, expm, mx, m_new, w_c, decay = _chunk_common(
                q, k, ig_c, fg_c, ig_r, fg_r, m_prev, n_prev, row, col)
            c_prev = c_scr[h]
            c_prev_b = c_prev.astype(BF)
            num = _dot(s.astype(BF), v) + (inter * QK_SCALE) * _dot(q, c_prev_b)
            hh = num / mx
            r = lax.rsqrt(jnp.mean(hh * hh, axis=1, keepdims=True) + EPS)
            hbn = hh * r * gh_ref[:, vs]
            o = o_ref[:, vs].astype(F32)
            z = z_ref[:, vs].astype(F32)
            yb_ref[:, vs] = (_sigmoid(o) * hbn * (z * _sigmoid(z))).astype(BF)
            hr_ref[:, vs] = hh.astype(BF)
            cs_ref[0, h] = c_prev_b
            aux_ref[0, h] = nm_scr[h]
            kw = k.astype(F32) * w_c
            c_scr[h] = decay * c_prev + _dot_tn(kw.astype(BF), v)
            nm_scr[h, 0:1, :] = decay * n_prev + jnp.sum(kw, axis=0, keepdims=True)
            nm_scr[h, 1:2, :] = jnp.broadcast_to(m_new, (1, DK))
        ybo_ref[...] = _dot(yb_ref[...], wb_ref[...]).astype(BF)

    return pl.pallas_call(
        body, name="mlstm_fwd", grid=(nc,),
        in_specs=[pl.BlockSpec((lc, 1024), lambda c: (c, 0)),
                  pl.BlockSpec((lc, 1024), lambda c: (c, 1)),
                  pl.BlockSpec((lc, 2048), lambda c: (c, 1)),
                  pl.BlockSpec((lc, 2048), lambda c: (c, 2)),
                  pl.BlockSpec((lc, 2048), lambda c: (c, 3)),
                  pl.BlockSpec((lc, 128), lambda c: (c, 0)),
                  pl.BlockSpec((1, 128), lambda c: (0, 0)),
                  pl.BlockSpec((1, V_DIM), lambda c: (0, 0)),
                  pl.BlockSpec((V_DIM, D_MODEL), lambda c: (0, 0))],
        out_specs=[pl.BlockSpec((lc, V_DIM), lambda c: (c, 0)),
                   pl.BlockSpec((lc, D_MODEL), lambda c: (c, 0)),
                   pl.BlockSpec((lc, V_DIM), lambda c: (c, 0)),
                   pl.BlockSpec((1, N_HEADS, DK, DV), lambda c: (c, 0, 0, 0)),
                   pl.BlockSpec((1, N_HEADS, 8, DK), lambda c: (c, 0, 0, 0))],
        out_shape=[jax.ShapeDtypeStruct((t, V_DIM), BF), jax.ShapeDtypeStruct((t, D_MODEL), BF),
                   jax.ShapeDtypeStruct((t, V_DIM), BF),
                   jax.ShapeDtypeStruct((nc, N_HEADS, DK, DV), BF),
                   jax.ShapeDtypeStruct((nc, N_HEADS, 8, DK), F32)],
        scratch_shapes=[pltpu.VMEM((N_HEADS, DK, DV), F32), pltpu.VMEM((N_HEADS, 8, DK), F32)],
        compiler_params=_cparams(("arbitrary",)),
    )(proj, proj, proj, proj, proj, gates, gate_bias, g_head, w_b)


def _mlstm_bwd(dproj, proj, gates, gate_bias, g_head, h_raw, c_states, aux, yb_pre, dyb, w_b):
    t = proj.shape[0]
    lc = min(_CHUNK, t)
    nc = t // lc

    def body(dpin_ref, q_ref, k_ref, v_ref, o_ref, z_ref, g_ref, gb_ref, gh_ref, hr_ref, cs_ref, aux_ref,
             ybp_ref, dyb_ref, wb_ref, dp_ref, dg_ref, dbias_ref, dgh_ref, dwb_ref, dc_scr, dn_scr, dy_ref,
             dwb_scr):
        del dpin_ref
        step = pl.program_id(0)

        @pl.when(step == 0)
        def _():
            dc_scr[...] = jnp.zeros_like(dc_scr)
            dn_scr[...] = jnp.zeros_like(dn_scr)
            dbias_ref[...] = jnp.zeros_like(dbias_ref)
            dgh_ref[...] = jnp.zeros_like(dgh_ref)
            dwb_scr[...] = jnp.zeros_like(dwb_scr)

        dyb = dyb_ref[...]
        dy_ref[...] = _dot_nt(dyb, wb_ref[...])
        dwb_scr[...] += _dot_tn(ybp_ref[...], dyb)

        row = lax.broadcasted_iota(jnp.int32, (lc, lc), 0)
        col = lax.broadcasted_iota(jnp.int32, (lc, lc), 1)
        eye = row == col
        gc = g_ref[...] + gb_ref[...]
        gt = gc.T
        lane_i = lax.broadcasted_iota(jnp.int32, gc.shape, 1)
        sub_i = lax.broadcasted_iota(jnp.int32, gt.shape, 0)
        dgates = jnp.zeros(gc.shape, F32)
        for h in range(N_HEADS):
            ks = slice(h * DK, (h + 1) * DK)
            vs = slice(h * DV, (h + 1) * DV)
            q = q_ref[:, ks]
            k = k_ref[:, ks]
            v = v_ref[:, vs]
            ig_c, fg_c, ig_r, fg_r = _gate_vectors(gc, gt, h, lane_i, sub_i)
            n_prev = aux_ref[0, h, 0:1, :]
            m_prev = aux_ref[0, h, 1:2, 0:1]
            c_prev_b = cs_ref[0, h]
            est, s, inter, den, expm, mx, m_new, w_c, decay = _chunk_common(
                q, k, ig_c, fg_c, ig_r, fg_r, m_prev, n_prev, row, col)
            hb = hr_ref[:, vs].astype(F32)
            dyp = dy_ref[:, vs]
            o = o_ref[:, vs].astype(F32)
            z = z_ref[:, vs].astype(F32)
            so = _sigmoid(o)
            sgz = _sigmoid(z)
            sz = z * sgz
            r = lax.rsqrt(jnp.mean(hb * hb, axis=1, keepdims=True) + EPS)
            xh = hb * r
            gh = gh_ref[:, vs]
            hbn = xh * gh
            dyso = dyp * so
            dyso_h = dyso * hbn
            d_o = dyso_h * sz * (1.0 - so)
            d_z = dyso_h * sgz * (1.0 + z * (1.0 - sgz))
            dhbn = dyso * sz
            dgh_ref[0:1, vs] += jnp.sum(dhbn * xh, axis=0, keepdims=True)
            dxh = dhbn * gh
            dh = r * (dxh - xh * jnp.mean(dxh * xh, axis=1, keepdims=True))
            dnm = dh / mx
            hd = jnp.sum(dh * hb, axis=1, keepdims=True)
            cond = jnp.abs(den) > expm
            dden = jnp.where(cond, -hd / mx * jnp.sign(den), 0.0)
            dnm_b = dnm.astype(BF)
            p = _dot_nt(dnm_b, v) + dden
            dqk = (p * est * QK_SCALE).astype(BF)
            dq_inter = (inter * QK_SCALE) * (_dot_nt(dnm_b, c_prev_b) + dden * n_prev)
            dq = _dot(dqk, k) + dq_inter
            dc_new = dc_scr[h]
            dc_new_b = dc_new.astype(BF)
            dn_new = dn_scr[h, 0:1, :]
            kf = k.astype(F32)
            dk_state = w_c * (_dot_nt(v, dc_new_b) + dn_new)
            dk = _dot_tn(dqk, q) + dk_state
            dv = _dot_tn(s.astype(BF), dnm_b) + w_c * _dot(k, dc_new_b)
            dv1_r = jnp.sum(s * dden, axis=0, keepdims=True)
            dv1_c = (jnp.sum(jnp.where(eye, dv1_r, 0.0), axis=1, keepdims=True)
                     + w_c * jnp.sum(kf * dn_new, axis=1, keepdims=True))
            dli_c = jnp.sum(v.astype(F32) * dv, axis=1, keepdims=True) + dv1_c
            hmat = _dot((p * s).astype(BF), (row < col).astype(BF))
            from_prev_c = jnp.sum(q.astype(F32) * dq_inter, axis=1, keepdims=True)
            to_next_c = jnp.sum(kf * dk_state, axis=1, keepdims=True)
            through = decay * (
                jnp.sum(jnp.sum(dc_new * c_prev_b.astype(F32), axis=1, keepdims=True), axis=0, keepdims=True)
                + jnp.sum(dn_new * n_prev, axis=1, keepdims=True))
            dlf_r = through + jnp.sum(jnp.where(row >= col, hmat + from_prev_c, to_next_c), axis=0, keepdims=True)
            dlf_c = jnp.sum(jnp.where(eye, dlf_r, 0.0), axis=1, keepdims=True)
            dfg_c = dlf_c * _sigmoid(-fg_c)
            dgates = dgates + jnp.where(lane_i == IG_LANE + h, dli_c, 0.0) + jnp.where(
                lane_i == FG_LANE + h, dfg_c, 0.0)
            qi = q.astype(F32) * (inter * QK_SCALE)
            dc_scr[h] = decay * dc_new + _dot_tn(qi.astype(BF), dnm_b)
            dn_scr[h, 0:1, :] = decay * dn_new + jnp.sum(qi * dden, axis=0, keepdims=True)
            dp_ref[:, h * DK:(h + 1) * DK] = dq.astype(BF)
            dp_ref[:, 1024 + h * DK:1024 + (h + 1) * DK] = dk.astype(BF)
            dp_ref[:, 2048 + h * DV:2048 + (h + 1) * DV] = dv.astype(BF)
            dp_ref[:, 4096 + h * DV:4096 + (h + 1) * DV] = d_o.astype(BF)
            dp_ref[:, 6144 + h * DV:6144 + (h + 1) * DV] = d_z.astype(BF)
        dg_ref[...] = dgates
        dbias_ref[0:1, :] += jnp.sum(dgates, axis=0, keepdims=True)

        @pl.when(step == nc - 1)
        def _():
            dwb_ref[...] = dwb_scr[...].astype(BF)

    def rev(c):
        return nc - 1 - c

    return pl.pallas_call(
        body, name="mlstm_bwd", grid=(nc,),
        input_output_aliases={0: 0},
        in_specs=[pl.BlockSpec(memory_space=pl.ANY),
                  pl.BlockSpec((lc, 1024), lambda c: (rev(c), 0)),
                  pl.BlockSpec((lc, 1024), lambda c: (rev(c), 1)),
                  pl.BlockSpec((lc, 2048), lambda c: (rev(c), 1)),
                  pl.BlockSpec((lc, 2048), lambda c: (rev(c), 2)),
                  pl.BlockSpec((lc, 2048), lambda c: (rev(c), 3)),
                  pl.BlockSpec((lc, 128), lambda c: (rev(c), 0)),
                  pl.BlockSpec((1, 128), lambda c: (0, 0)),
                  pl.BlockSpec((1, V_DIM), lambda c: (0, 0)),
                  pl.BlockSpec((lc, V_DIM), lambda c: (rev(c), 0)),
                  pl.BlockSpec((1, N_HEADS, DK, DV), lambda c: (rev(c), 0, 0, 0)),
                  pl.BlockSpec((1, N_HEADS, 8, DK), lambda c: (rev(c), 0, 0, 0)),
                  pl.BlockSpec((lc, V_DIM), lambda c: (rev(c), 0)),
                  pl.BlockSpec((lc, D_MODEL), lambda c: (rev(c), 0)),
                  pl.BlockSpec((V_DIM, D_MODEL), lambda c: (0, 0))],
        out_specs=[pl.BlockSpec((lc, 8192), lambda c: (rev(c), 0)),
                   pl.BlockSpec((lc, 128), lambda c: (rev(c), 0)),
                   pl.BlockSpec((8, 128), lambda c: (0, 0)),
                   pl.BlockSpec((8, V_DIM), lambda c: (0, 0)),
                   pl.BlockSpec((V_DIM, D_MODEL), lambda c: (0, 0))],
        out_shape=[jax.ShapeDtypeStruct((t, N_MAIN), BF), jax.ShapeDtypeStruct((t, 128), F32),
                   jax.ShapeDtypeStruct((8, 128), F32), jax.ShapeDtypeStruct((8, V_DIM), F32),
                   jax.ShapeDtypeStruct((V_DIM, D_MODEL), BF)],
        scratch_shapes=[pltpu.VMEM((N_HEADS, DK, DV), F32), pltpu.VMEM((N_HEADS, 8, DK), F32),
                        pltpu.VMEM((lc, V_DIM), F32), pltpu.VMEM((V_DIM, D_MODEL), F32)],
        compiler_params=_cparams(("arbitrary",)),
    )(dproj, proj, proj, proj, proj, proj, gates, gate_bias, g_head, h_raw, c_states, aux, yb_pre, dyb, w_b)


GA_BLK, GB_BLK = 12, 13


def _full(shape):
    return pl.BlockSpec(shape, lambda i: (0,) * len(shape))


def _rms_backward(xv, g, dhn, dres):
    r = lax.rsqrt(jnp.mean(xv * xv, axis=1, keepdims=True) + EPS)
    xh = xv * r
    dxh = dhn * g
    dx = dres + r * (dxh - xh * jnp.mean(dxh * xh, axis=1, keepdims=True))
    return dx, jnp.sum(dhn * xh, axis=0, keepdims=True)


def _token_chain(ya_pre, w_a, yb, proj, w_o, x, g_ple, w_pg, p, w_pl, target, g_final, *, tm=256):
    t, d = x.shape
    tm = min(tm, t)

    def body(yap_ref, wa_ref, yb_ref, ga_ref, gb_ref, wo_ref, x_ref, gp_ref, wpg_ref, p_ref, wpl_ref, tg_ref,
             gf_ref, dx_ref, dyb_ref, dyap_ref, o_ref, sm_ref, dwa_ref, dwo_ref, dwpg_ref, dwpl_ref,
             dwa_acc, dwo_acc, dwpg_acc, dwpl_acc):
        i = pl.program_id(0)

        @pl.when(i == 0)
        def _():
            sm_ref[...] = jnp.zeros_like(sm_ref)
            dwa_acc[...] = jnp.zeros_like(dwa_acc)
            dwo_acc[...] = jnp.zeros_like(dwo_acc)
            dwpg_acc[...] = jnp.zeros_like(dwpg_acc)
            dwpl_acc[...] = jnp.zeros_like(dwpl_acc)

        yap = yap_ref[...]
        ya = _dot(yap, wa_ref[...]).astype(BF).astype(F32)
        yb_v = yb_ref[...].astype(F32)
        sa = _sigmoid(ga_ref[...].astype(F32))
        sb = _sigmoid(gb_ref[...].astype(F32))
        merged = (sa * ya + sb * yb_v).astype(BF)
        x1 = _dot(merged, wo_ref[...]) + x_ref[...]
        r1 = lax.rsqrt(jnp.mean(x1 * x1, axis=1, keepdims=True) + EPS)
        xh1 = x1 * r1
        gp = gp_ref[...]
        hn2 = xh1 * gp
        hn2_b = hn2.astype(BF)
        gate = _sigmoid(_dot(hn2_b, wpg_ref[...]))
        p_b = p_ref[...].astype(BF)
        pe_v = _dot(p_b, wpl_ref[...])
        x2 = x1 + gate * pe_v
        r2 = lax.rsqrt(jnp.mean(x2 * x2, axis=1, keepdims=True) + EPS)
        xh2 = x2 * r2
        gf = gf_ref[...]
        err = xh2 * gf - tg_ref[...]
        dy = err * (1.0 / d)
        dxh2 = dy * gf
        dx2 = r2 * (dxh2 - xh2 * jnp.mean(dxh2 * xh2, axis=1, keepdims=True))
        dgpre = (dx2 * pe_v * gate * (1.0 - gate)).astype(BF)
        dwpg_acc[...] += _dot_tn(hn2_b, dgpre)
        dwpl_acc[...] += _dot_tn(p_b, (dx2 * gate).astype(BF))
        dhn2 = _dot_nt(dgpre, wpg_ref[...])
        dxh1 = dhn2 * gp
        dx1 = dx2 + r1 * (dxh1 - xh1 * jnp.mean(dxh1 * xh1, axis=1, keepdims=True))
        dx_ref[...] = dx1
        dx1_b = dx1.astype(BF)
        dwo_acc[...] += _dot_tn(merged, dx1_b)
        dm = _dot_nt(dx1_b, wo_ref[...])
        dya = (dm * sa).astype(BF)
        dwa_acc[...] += _dot_tn(yap, dya)
        dyb_ref[...] = (dm * sb).astype(BF)
        o_ref[:, 0:d] = (dm * ya * sa * (1.0 - sa)).astype(BF)
        o_ref[:, d:2 * d] = (dm * yb_v * sb * (1.0 - sb)).astype(BF)
        dyap_ref[...] = _dot_nt(dya, wa_ref[...]).astype(BF)

        sm_ref[0:1, :] += (0.5 / d) * jnp.sum(err * err, axis=0, keepdims=True)
        sm_ref[1:2, :] += jnp.sum(dy * xh2, axis=0, keepdims=True)
        sm_ref[2:3, :] += jnp.sum(dhn2 * xh1, axis=0, keepdims=True)

        @pl.when(i == t // tm - 1)
        def _():
            dwa_ref[...] = dwa_acc[...].astype(BF)
            dwo_ref[...] = dwo_acc[...].astype(BF)
            dwpg_ref[...] = dwpg_acc[...].astype(BF)
            dwpl_ref[...] = dwpl_acc[...].astype(BF)

    row = pl.BlockSpec((tm, d), lambda i: (i, 0))
    bf = jax.ShapeDtypeStruct((t, d), BF)
    return pl.pallas_call(
        body, name="token_chain", grid=(t // tm,),
        in_specs=[row, _full((d, d)), row, pl.BlockSpec((tm, d), lambda i: (i, GA_BLK)),
                  pl.BlockSpec((tm, d), lambda i: (i, GB_BLK)), _full((d, d)), row, _full((1, d)), _full((d, d)),
                  pl.BlockSpec((tm, PLE_DIM), lambda i: (i, 0)), _full((PLE_DIM, d)), row, _full((1, d))],
        out_specs=[row, row, row, pl.BlockSpec((tm, 2 * d), lambda i: (i, 6)), _full((8, d)),
                   _full((d, d)), _full((d, d)), _full((d, d)), _full((PLE_DIM, d))],
        out_shape=[jax.ShapeDtypeStruct((t, d), F32), bf, bf,
                   jax.ShapeDtypeStruct((t, N_MAIN), BF), jax.ShapeDtypeStruct((8, d), F32),
                   jax.ShapeDtypeStruct((d, d), BF), jax.ShapeDtypeStruct((d, d), BF),
                   jax.ShapeDtypeStruct((d, d), BF), jax.ShapeDtypeStruct((PLE_DIM, d), BF)],
        scratch_shapes=[pltpu.VMEM((d, d), F32), pltpu.VMEM((d, d), F32), pltpu.VMEM((d, d), F32),
                        pltpu.VMEM((PLE_DIM, d), F32)],
        compiler_params=_cparams(("arbitrary",)),
    )(ya_pre, w_a, yb, proj, proj, w_o, x, g_ple, w_pg, p, w_pl, target, g_final)


def _dhn_mix_bwd(dproj, w_main, dgates, w_gate, x, g_mix, dx1, comm, *, tm=1024, tk=2048):
    t, d = x.shape
    tm = min(tm, t)
    nk = N_MAIN // tk
    ni = t // tm
    n_in = 7 + comm.n

    def body(*refs):
        dp_ref, w_ref, dgt_ref, wg_ref, x_ref, g_ref, dres_ref = refs[:7]
        dx_ref, dg_ref = refs[n_in], refs[n_in + 1]
        acc = refs[-1]
        start, wait = comm.ops(refs[7:n_in], refs[n_in + 2:n_in + 2 + comm.n], *refs[n_in + 2 + comm.n:-1])
        i, k = pl.program_id(0), pl.program_id(1)
        pl.when((i == 0) & (k == 0))(start)

        @pl.when(k == 0)
        def _():
            acc[...] = _dot_nt(dp_ref[...], w_ref[...]) + _dot_nt(dgt_ref[...], wg_ref[...])

        @pl.when(k > 0)
        def _():
            acc[...] += _dot_nt(dp_ref[...], w_ref[...])

        @pl.when((i == 0) & (k == 0))
        def _():
            dg_ref[...] = jnp.zeros_like(dg_ref)

        @pl.when(k == nk - 1)
        def _():
            dx, dg = _rms_backward(x_ref[...], g_ref[...], acc[...], dres_ref[...])
            dx_ref[...] = dx
            dg_ref[0:1, :] += dg

        pl.when((i == ni - 1) & (k == nk - 1))(wait)

    any_spec = pl.BlockSpec(memory_space=pl.ANY)
    row = pl.BlockSpec((tm, d), lambda i, k: (i, 0))
    res = pl.pallas_call(
        body, name="dhn_mix_bwd", grid=(ni, nk),
        in_specs=[pl.BlockSpec((tm, tk), lambda i, k: (i, k)), pl.BlockSpec((d, tk), lambda i, k: (0, k)),
                  pl.BlockSpec((tm, 128), lambda i, k: (i, 0)), pl.BlockSpec((d, 128), lambda i, k: (0, 0)),
                  row, pl.BlockSpec((1, d), lambda i, k: (0, 0)), row] + [any_spec] * comm.n,
        out_specs=[row, pl.BlockSpec((8, d), lambda i, k: (0, 0))] + [any_spec] * comm.n,
        out_shape=[jax.ShapeDtypeStruct((t, d), F32), jax.ShapeDtypeStruct((8, d), F32)] + comm.out_shapes,
        scratch_shapes=comm.scratch + [pltpu.VMEM((tm, d), F32)],
        compiler_params=_cparams(("arbitrary", "arbitrary")),
    )(dproj, w_main, dgates, w_gate, x, g_mix, dx1, *comm.srcs)
    return res[0], res[1], list(res[2:])


def _position():
    x, y, c = lax.axis_index("x"), lax.axis_index("y"), lax.axis_index("c")
    return x, y, c


def _all_gather(srcs):
    nb = len(srcs)
    any_spec = pl.BlockSpec(memory_space=pl.ANY)

    def body(*refs):
        src = refs[:nb]
        dst = refs[nb:2 * nb]
        send_sems, recv_sems, local_sems = refs[2 * nb:]
        x, y, c = _position()
        me, sibling = (x, y, c), (x, y, 1 - c)
        chips = [(1 - x, y), (x, 1 - y), (1 - x, 1 - y)]

        def slot(b, px, py, pc):
            return dst[b].at[4 * px + 2 * py + pc]

        def copy(k, b, block, to, from_src=False):
            return pltpu.make_async_remote_copy(
                src_ref=src[b] if from_src else slot(b, *block), dst_ref=slot(b, *block),
                send_sem=send_sems.at[b, k], recv_sem=recv_sems.at[b, k],
                device_id=to, device_id_type=MESH)

        mine = [pltpu.make_async_copy(src[b], slot(b, *me), local_sems.at[b]) for b in range(nb)]
        for cp in mine:
            cp.start()
        first = [copy(0, b, me, sibling, True) for b in range(nb)]
        first += [copy(1 + j, b, me, (*chip, c), True) for j, chip in enumerate(chips) for b in range(nb)]
        for cp in first:
            cp.start()
        passed = []
        for j, chip in enumerate(chips):
            for b in range(nb):
                copy(1 + j, b, (*chip, c), me).wait_recv()
                fwd = copy(4 + j, b, (*chip, c), sibling)
                fwd.start()
                passed.append(fwd)
        for b in range(nb):
            copy(0, b, sibling, me).wait_recv()
        for j, chip in enumerate(chips):
            for b in range(nb):
                copy(4 + j, b, (*chip, 1 - c), me).wait_recv()
        for cp in first + passed:
            cp.wait_send()
        for cp in mine:
            cp.wait()

    return pl.pallas_call(
        body, name="weights_all_gather",
        in_specs=[any_spec] * nb, out_specs=[any_spec] * nb,
        out_shape=[jax.ShapeDtypeStruct((N_DEV,) + s.shape, s.dtype) for s in srcs],
        scratch_shapes=[pltpu.SemaphoreType.DMA((nb, 7)), pltpu.SemaphoreType.DMA((nb, 7)),
                        pltpu.SemaphoreType.DMA((nb,))],
    )(*srcs)


def _comm_call(comm, *, name):
    any_spec = pl.BlockSpec(memory_space=pl.ANY)

    def body(*refs):
        start, wait = comm.ops(refs[:comm.n], refs[comm.n:2 * comm.n], *refs[2 * comm.n:])
        start()
        wait()

    return pl.pallas_call(
        body, name=name, in_specs=[any_spec] * comm.n, out_specs=[any_spec] * comm.n,
        out_shape=comm.out_shapes, scratch_shapes=comm.scratch,
    )(*comm.srcs)


def _sum_slots(recv, *, name, tr):
    _, r, cdim = recv.shape
    tr = min(tr, r)

    def body(r_ref, o_ref):
        total = r_ref[0].astype(F32)
        for s in range(1, N_DEV):
            total = total + r_ref[s].astype(F32)
        o_ref[...] = total

    return pl.pallas_call(
        body, name=name, grid=(r // tr,),
        in_specs=[pl.BlockSpec((N_DEV, tr, cdim), lambda i: (0, i, 0))],
        out_specs=pl.BlockSpec((tr, cdim), lambda i: (i, 0)),
        out_shape=jax.ShapeDtypeStruct((r, cdim), F32),
        compiler_params=_cparams(("parallel",)),
    )(recv)


def _adamw(w, g, m, v, *, name):
    lead = w.ndim - 2
    r, cdim = w.shape[-2:]
    if r % 128 == 0:
        tr, tc = 128, cdim
    elif r >= 128 and cdim % 128 == 0:
        tr, tc = r, 128
    else:
        tr, tc = r, cdim
    c1 = 1.0 - ADAM_B1 ** ADAM_STEP
    c2 = 1.0 - ADAM_B2 ** ADAM_STEP

    def body(w_ref, g_ref, m_ref, v_ref, d_ref, mo_ref, vo_ref):
        gv = g_ref[...]
        mn = ADAM_B1 * m_ref[...] + (1.0 - ADAM_B1) * gv
        vn = ADAM_B2 * v_ref[...] + (1.0 - ADAM_B2) * (gv * gv)
        d_ref[...] = -ADAM_LR * ((mn / c1) / (jnp.sqrt(vn / c2) + ADAM_EPS) + ADAM_WD * w_ref[...])
        mo_ref[...] = mn
        vo_ref[...] = vn

    blk = pl.BlockSpec((1,) * lead + (tr, tc), lambda i, j: (0,) * lead + (i, j))
    shp = jax.ShapeDtypeStruct(w.shape, F32)
    return pl.pallas_call(
        body, name=name, grid=(r // tr, cdim // tc),
        in_specs=[blk] * 4, out_specs=[blk] * 3, out_shape=[shp] * 3,
        compiler_params=_cparams(("parallel", "parallel")),
    )(w, g, m, v)


def kernel(x, p, g_mix, w_in, conv_w, conv_b, w_a_out, b_gates, g_head, w_b_out, w_o, g_ple, w_ple_gate, w_ple, g_final, loss_target, m_g_mix, m_w_in, m_conv_w, m_conv_b, m_w_a_out, m_b_gates, m_g_head, m_w_b_out, m_w_o, m_g_ple, m_w_ple_gate, m_w_ple, m_g_final, v_g_mix, v_w_in, v_conv_w, v_conv_b, v_w_a_out, v_b_gates, v_g_head, v_w_b_out, v_w_o, v_g_ple, v_w_ple_gate, v_w_ple, v_g_final):
    d = D_MODEL
    t = x.shape[1]
    x2d = x.reshape(t, d)
    p2d = p.reshape(t, PLE_DIM)
    tgt = loss_target.reshape(t, d)

    win = jnp.pad(w_in[0].astype(BF), ((0, 0), (0, WIN_W - SHARD_W)))
    rows = jnp.concatenate([w_a_out[0].astype(BF), w_b_out[0].astype(BF), w_o[0].astype(BF),
                            w_ple_gate[0].astype(BF), w_ple[0].astype(BF).reshape(32, d)], axis=0)
    cfl = jnp.pad(conv_w[0], ((0, 5), (0, 0)))
    g_win, g_cf = _all_gather([win, cfl])

    w_glob = jnp.concatenate([g_win[k, :, :SHARD_W] for k in range(N_DEV)]
                             + [jnp.zeros((d, 14464 - N_IN), BF)], axis=1)
    tail = jnp.roll(w_glob[:, 12288:], -8, axis=1)
    w_main = jnp.concatenate([w_glob[:, 4096:12288], w_glob[:, 0:4096], tail[:, :2048]], axis=1)
    w_gate = tail[:, 2048:]
    conv_w8 = jnp.pad(g_cf[:, :3, :].transpose(1, 0, 2).reshape(3, d), ((0, 5), (0, 0)))
    gate_bias = jnp.pad(b_gates, ((0, 0), (IG_LANE, 0)))

    hn, hnt = _rms_fwd(x2d, g_mix, name="rms_mix")
    proj, gates, (g_rows,) = _mm(hn, w_main, form="nn", out_dtype=BF, name="proj", tm=2048,
                                 extra=(w_gate, F32), comm=_DirectComm([rows], "gather"))
    w_a = g_rows[:, 0:128].reshape(d, d)
    w_b = g_rows[:, 128:384].reshape(V_DIM, d)
    w_of = g_rows[:, 384:512].reshape(d, d)
    w_pg = g_rows[:, 512:640].reshape(d, d)
    w_pl = g_rows[:, 640:672].reshape(N_DEV, PLE_DIM, 128).transpose(1, 0, 2).reshape(PLE_DIM, d)
    ya_pre = _branch_a_fwd(proj, conv_w8, conv_b)
    yb_pre, yb, h_raw, c_states, aux = _mlstm_fwd(proj, gates, gate_bias, g_head, w_b)
    dx1, dyb, dya_pre, dproj, small_fin, dw_a, dw_o, dw_pg, dw_pl = _token_chain(
        ya_pre, w_a, yb, proj, w_of, x2d, g_ple, w_pg, p2d, w_pl, tgt, g_final.reshape(1, d))

    dproj, dconv = _branch_a_bwd(dproj, proj, dya_pre, conv_w8, conv_b)
    dproj, dgates, dbias, dg_head, dw_b = _mlstm_bwd(dproj, proj, gates, gate_bias, g_head, h_raw, c_states, aux,
                                                     yb_pre, dyb, w_b)
    s_rows = jnp.concatenate([
        dw_a.reshape(N_DEV, 128, d), dw_b.reshape(N_DEV, 256, d), dw_o.reshape(N_DEV, 128, d),
        dw_pg.reshape(N_DEV, 128, d),
        dw_pl.reshape(PLE_DIM, N_DEV, 128).transpose(1, 0, 2).reshape(N_DEV, 32, d)], axis=1)
    dgates_b = dgates.astype(BF)
    dw_main, dw_gate, (r_rows,) = _mm(hnt, dproj, form="nn", out_dtype=BF, name="dw_main", tk=2048,
                                      extra=(dgates_b, BF), comm=_DirectComm([s_rows], "exchange"))
    tail_g = jnp.roll(jnp.concatenate([dw_main[:, 12288:], dw_gate], axis=1), 8, axis=1)
    dw_glob = jnp.concatenate([dw_main[:, 8192:12288], dw_main[:, 0:8192], tail_g], axis=1)
    s_win = jnp.stack([jnp.pad(dw_glob[:, SHARD_W * j:SHARD_W * (j + 1)], ((0, 0), (0, WIN_W - SHARD_W)))
                       for j in range(N_DEV)])
    grad_x, dg_mix, (r_win,) = _dhn_mix_bwd(dproj, w_main, dgates_b, w_gate, x2d, g_mix, dx1,
                                            _DirectComm([s_win], "exchange"))

    vec = jnp.concatenate([dg_mix[0], dconv[3], dg_head[0], small_fin[2], small_fin[1],
                           dbias[0, IG_LANE:], jnp.zeros((7 * d - 6152,), F32)]).reshape(7, d)
    conv_part = jnp.pad(dconv[:3].reshape(3, N_DEV, 128).transpose(1, 0, 2).reshape(N_DEV, 1, 384),
                        ((0, 0), (0, 0), (0, d - 384)))
    s_f32 = jnp.concatenate([jnp.broadcast_to(vec[None], (N_DEV, 7, d)), conv_part], axis=1)
    (r_f32,) = _comm_call(_DirectComm([s_f32], "exchange"), name="small_grads_exchange")
    sum_win = _sum_slots(r_win, name="sum_win", tr=128)
    sum_rows = _sum_slots(r_rows, name="sum_rows", tr=96)
    sum_f32 = _sum_slots(r_f32, name="sum_f32", tr=8)

    g_w_in = sum_win[:, :SHARD_W]
    g_w_a = sum_rows[0:128]
    g_w_b = sum_rows[128:384]
    g_w_o = sum_rows[384:512]
    g_w_pg = sum_rows[512:640]
    g_w_pl = sum_rows[640:672].reshape(PLE_DIM, 128)
    vsum = sum_f32[:7].reshape(7 * d)
    g_g_mix = vsum[0:1024].reshape(1, d)
    g_conv_b = vsum[1024:2048].reshape(1, d)
    g_g_head = vsum[2048:4096].reshape(1, V_DIM)
    g_g_ple = vsum[4096:5120].reshape(1, d)
    g_g_final = vsum[5120:6144].reshape(1, d)
    g_b_gates = vsum[6144:6152].reshape(1, 8)
    g_conv_w = sum_f32[7, :384].reshape(3, 128)

    loss = lax.psum(jnp.sum(small_fin[0]), ("x", "y", "c"))

    names = ["g_mix", "w_in", "conv_w", "conv_b", "w_a_out", "b_gates", "g_head", "w_b_out", "w_o", "g_ple",
             "w_ple_gate", "w_ple", "g_final"]
    weights = [g_mix, w_in, conv_w, conv_b, w_a_out, b_gates, g_head, w_b_out, w_o, g_ple, w_ple_gate, w_ple,
               g_final]
    moms = [m_g_mix, m_w_in, m_conv_w, m_conv_b, m_w_a_out, m_b_gates, m_g_head, m_w_b_out, m_w_o, m_g_ple,
            m_w_ple_gate, m_w_ple, m_g_final]
    vels = [v_g_mix, v_w_in, v_conv_w, v_conv_b, v_w_a_out, v_b_gates, v_g_head, v_w_b_out, v_w_o, v_g_ple,
            v_w_ple_gate, v_w_ple, v_g_final]
    grads2d = [g_g_mix, g_w_in, g_conv_w, g_conv_b, g_w_a, g_b_gates, g_g_head, g_w_b, g_w_o, g_g_ple, g_w_pg,
               g_w_pl, g_g_final]
    grads, deltas, new_m, new_v = [], [], [], []
    for nm, w, m_, v_, g2 in zip(names, weights, moms, vels, grads2d):
        shp = w.shape
        if nm == "w_in":
            dl, mn, vn = _adamw(w[0].T, g2.T, m_[0].T, v_[0].T, name="adamw_" + nm)
            grads.append(g2.reshape(shp))
            deltas.append(dl.T.reshape(shp))
            new_m.append(mn.T.reshape(shp))
            new_v.append(vn.T.reshape(shp))
            continue
        kshp = shp if w.ndim >= 2 else (1,) + shp
        gk = g2.reshape(kshp)
        dl, mn, vn = _adamw(w.reshape(kshp), gk, m_.reshape(kshp), v_.reshape(kshp), name="adamw_" + nm)
        grads.append(gk.reshape(shp))
        deltas.append(dl.reshape(shp))
        new_m.append(mn.reshape(shp))
        new_v.append(vn.reshape(shp))
    return (loss, grad_x.reshape(x.shape), *grads, *deltas, *new_m, *new_v)
```

```python
import functools

import jax
import jax.numpy as jnp
from jax import lax
from jax.experimental import pallas as pl
from jax.experimental.pallas import tpu as pltpu

F32 = jnp.float32
BF = jnp.bfloat16

D_MODEL = 1024
N_HEADS = 4
DK = 256
DV = 512
V_DIM = 2048
PLE_DIM = 256
N_IN = 14344
N_MAIN = 14336
EPS = 1e-6
QK_SCALE = DK ** -0.5
NEG = -1e30
N_DEV = 8
SHARD_W = 1793
WIN_STRIDE = 1792
WIN_W = 1920
ROWS_PACK = 672
_CHUNK = 256
IG_LANE = 120
FG_LANE = 124

ADAM_LR = 0.001
ADAM_B1 = 0.9
ADAM_B2 = 0.999
ADAM_EPS = 1e-08
ADAM_WD = 0.01
ADAM_STEP = 10

VMEM_LIMIT = 56 * 1024 * 1024
MESH = pl.DeviceIdType.MESH


def _cparams(sem):
    return pltpu.CompilerParams(dimension_semantics=sem, vmem_limit_bytes=VMEM_LIMIT)


def _sigmoid(x):
    return 1.0 / (1.0 + jnp.exp(-x))


def _log_sigmoid(x):
    return jnp.minimum(x, 0.0) - jnp.log(1.0 + jnp.exp(-jnp.abs(x)))


def _dot(a, b):
    return jnp.dot(a, b, preferred_element_type=F32)


def _dot_nt(a, b):
    return lax.dot_general(a, b, (((1,), (1,)), ((), ())), preferred_element_type=F32)


def _dot_tn(a, b):
    return lax.dot_general(a, b, (((0,), (0,)), ((), ())), preferred_element_type=F32)


class _DirectComm:
    def __init__(self, srcs, kind):
        self.srcs = list(srcs)
        self.kind = kind
        self.n = len(self.srcs)
        if kind == "exchange":
            self.out_shapes = [jax.ShapeDtypeStruct(s.shape, s.dtype) for s in self.srcs]
        else:
            self.out_shapes = [jax.ShapeDtypeStruct((N_DEV,) + s.shape, s.dtype) for s in self.srcs]
        self.scratch = [pltpu.SemaphoreType.DMA((self.n, 7)), pltpu.SemaphoreType.DMA((self.n, 7)),
                        pltpu.SemaphoreType.DMA((self.n,))]

    def ops(self, src, dst, send_sems, recv_sems, local_sems):
        exchange = self.kind == "exchange"

        def descriptors():
            x, y, c = _position()
            me_lin = 4 * x + 2 * y + c
            local = [pltpu.make_async_copy(src[b].at[me_lin] if exchange else src[b], dst[b].at[me_lin],
                                           local_sems.at[b]) for b in range(self.n)]
            sends, recvs = [], []
            for f in range(1, N_DEV):
                px = (1 - x) if (f >> 2) & 1 else x
                py = (1 - y) if (f >> 1) & 1 else y
                pc = (1 - c) if f & 1 else c
                peer_lin = 4 * px + 2 * py + pc
                for b in range(self.n):
                    out = src[b].at[peer_lin] if exchange else src[b]
                    common = dict(send_sem=send_sems.at[b, f - 1], recv_sem=recv_sems.at[b, f - 1],
                                  device_id=(px, py, pc), device_id_type=MESH)
                    sends.append(pltpu.make_async_remote_copy(src_ref=out, dst_ref=dst[b].at[me_lin], **common))
                    recvs.append(pltpu.make_async_remote_copy(src_ref=out, dst_ref=dst[b].at[peer_lin], **common))
            return local, sends, recvs

        def start():
            local, sends, _ = descriptors()
            for cp in local + sends:
                cp.start()

        def wait():
            local, sends, recvs = descriptors()
            for cp in recvs:
                cp.wait_recv()
            for cp in sends:
                cp.wait_send()
            for cp in local:
                cp.wait()

        return start, wait


def _mm(a, b, *, form, out_dtype, name, tm=1024, tn=1024, tk=1024, add=None, extra=None, comm=None):
    assert extra is None or form == "nn"
    if form == "nn":
        m, kc = a.shape
        n = b.shape[1]
    elif form == "nt":
        m, kc = a.shape
        n = b.shape[0]
    else:
        kc, m = a.shape
        n = b.shape[1]
    tm, tn, tk = min(tm, m), min(tn, n), min(tk, kc)
    assert m % tm == 0 and n % tn == 0 and kc % tk == 0, (name, a.shape, b.shape)
    nk = kc // tk
    if form == "tn":
        a_spec = pl.BlockSpec((tk, tm), lambda i, j, k: (k, i))
    else:
        a_spec = pl.BlockSpec((tm, tk), lambda i, j, k: (i, k))
    if form == "nt":
        b_spec = pl.BlockSpec((tn, tk), lambda i, j, k: (j, k))
    else:
        b_spec = pl.BlockSpec((tk, tn), lambda i, j, k: (k, j))
    o_spec = pl.BlockSpec((tm, tn), lambda i, j, k: (i, j))
    dot = {"nn": _dot, "nt": _dot_nt, "tn": _dot_tn}[form]
    has_add = add is not None

    use_acc = nk > 1 and (has_add or out_dtype != F32)
    has_x = extra is not None
    n2 = extra[0].shape[1] if has_x else 0
    use_acc2 = has_x and nk > 1 and extra[1] != F32
    n_comm = comm.n if comm else 0
    n_in = 2 + int(has_add) + int(has_x) + n_comm
    n_out = 1 + int(has_x) + n_comm
    grid = (m // tm, n // tn, nk)

    def body(*refs):
        a_ref, b_ref = refs[0], refs[1]
        add_ref = refs[2] if has_add else None
        b2_ref = refs[2 + int(has_add)] if has_x else None
        o_ref = refs[n_in]
        o2_ref = refs[n_in + 1] if has_x else None
        scr = list(refs[n_in + n_out + (3 if comm else 0):])
        acc = scr.pop(0) if use_acc else o_ref
        acc2 = scr.pop(0) if use_acc2 else o2_ref
        i, j, k = pl.program_id(0), pl.program_id(1), pl.program_id(2)
        if comm:
            start, wait = comm.ops(refs[n_in - n_comm:n_in], refs[n_in + n_out - n_comm:n_in + n_out],
                                   *refs[n_in + n_out:n_in + n_out + 3])
            pl.when((i == 0) & (j == 0) & (k == 0))(start)

        def part():
            return dot(a_ref[...].astype(BF), b_ref[...].astype(BF))

        def part2():
            return dot(a_ref[...].astype(BF), b2_ref[...].astype(BF))

        def finish(total):
            if has_add:
                total = total + add_ref[...].astype(F32)
            o_ref[...] = total.astype(out_dtype)

        if nk == 1:
            finish(part())
            if has_x:
                @pl.when(j == 0)
                def _():
                    o2_ref[...] = part2().astype(extra[1])
        else:
            @pl.when(k == 0)
            def _():
                acc[...] = part()

            @pl.when(k > 0)
            def _():
                acc[...] += part()

            if use_acc:
                @pl.when(k == nk - 1)
                def _():
                    finish(acc[...])
            if has_x:
                @pl.when((j == 0) & (k == 0))
                def _():
                    acc2[...] = part2()

                @pl.when((j == 0) & (k > 0))
                def _():
                    acc2[...] += part2()

                if use_acc2:
                    @pl.when((j == 0) & (k == nk - 1))
                    def _():
                        o2_ref[...] = acc2[...].astype(extra[1])
        if comm:
            pl.when((i == grid[0] - 1) & (j == grid[1] - 1) & (k == nk - 1))(wait)

    any_spec = pl.BlockSpec(memory_space=pl.ANY)
    o2_spec = pl.BlockSpec((tm, n2), lambda i, j, k: (i, 0))
    in_specs = ([a_spec, b_spec] + ([o_spec] if has_add else [])
                + ([pl.BlockSpec((tk, n2), lambda i, j, k: (k, 0))] if has_x else []) + [any_spec] * n_comm)
    args = (a, b) + ((add,) if has_add else ()) + ((extra[0],) if has_x else ()) + (tuple(comm.srcs) if comm else ())
    out_specs = [o_spec] + ([o2_spec] if has_x else []) + [any_spec] * n_comm
    out_shape = ([jax.ShapeDtypeStruct((m, n), out_dtype)]
                 + ([jax.ShapeDtypeStruct((m, n2), extra[1])] if has_x else []) + (comm.out_shapes if comm else []))
    scratch = ((comm.scratch if comm else []) + ([pltpu.VMEM((tm, tn), F32)] if use_acc else [])
               + ([pltpu.VMEM((tm, n2), F32)] if use_acc2 else []))
    if comm:
        sem = ("arbitrary",) * 3
    else:
        sem = ("parallel", "arbitrary" if has_x else "parallel", "arbitrary")
    res = pl.pallas_call(
        body, name=name, grid=grid, in_specs=in_specs, out_specs=out_specs, out_shape=out_shape,
        scratch_shapes=scratch, compiler_params=_cparams(sem),
    )(*args)
    if not (comm or has_x):
        return res[0]
    return tuple(res[:1 + int(has_x)]) + ((list(res[1 + int(has_x):]),) if comm else ())


def _rms_fwd(x, g, *, name, tm=512):
    t, d = x.shape
    tm = min(tm, t)

    def body(x_ref, g_ref, hn_ref, hnt_ref):
        xv = x_ref[...]
        r = lax.rsqrt(jnp.mean(xv * xv, axis=1, keepdims=True) + EPS)
        hn = xv * r * g_ref[...]
        hn_ref[...] = hn.astype(BF)
        hnt_ref[...] = hn.T.astype(BF)

    return pl.pallas_call(
        body, name=name, grid=(t // tm,),
        in_specs=[pl.BlockSpec((tm, d), lambda i: (i, 0)), pl.BlockSpec((1, d), lambda i: (0, 0))],
        out_specs=[pl.BlockSpec((tm, d), lambda i: (i, 0)), pl.BlockSpec((d, tm), lambda i: (0, i))],
        out_shape=[jax.ShapeDtypeStruct((t, d), BF), jax.ShapeDtypeStruct((d, t), BF)],
        compiler_params=_cparams(("parallel",)),
    )(x, g)


HALO = 16
XA_BLK, BA_BLK, CA_BLK, ZA_BLK = 8, 9, 10, 11


def _shift_down(u, prev, n):
    tm = u.shape[0]
    rolled = pltpu.roll(u, n, 0)
    row = lax.broadcasted_iota(jnp.int32, u.shape, 0)
    out = rolled
    for j in range(n):
        out = jnp.where(row == j, prev[HALO - n + j:HALO - n + j + 1, :], out)
    return out


def _shift_up(u, nxt, n):
    tm = u.shape[0]
    rolled = pltpu.roll(u, tm - n, 0)
    row = lax.broadcasted_iota(jnp.int32, u.shape, 0)
    out = rolled
    for j in range(n):
        out = jnp.where(row == tm - n + j, nxt[j:j + 1, :], out)
    return out


def _branch_a_fwd(proj, conv_w8, conv_b, *, tm=512):
    t = proj.shape[0]
    d = D_MODEL
    tm = min(tm, t)
    hb = tm // HALO

    def body(xa_ref, ba_ref, ca_ref, za_ref, xap_ref, cap_ref, w_ref, b_ref, o_ref):
        i = pl.program_id(0)
        u = ca_ref[...].astype(F32) * xa_ref[...].astype(F32)
        up = cap_ref[...].astype(F32) * xap_ref[...].astype(F32)
        up = jnp.where(i == 0, 0.0, up)
        u1 = _shift_down(u, up, 1)
        u2 = _shift_down(u, up, 2)
        cv = w_ref[0:1, :] * u2 + w_ref[1:2, :] * u1 + w_ref[2:3, :] * u + b_ref[...]
        za = za_ref[...].astype(F32)
        o_ref[...] = (ba_ref[...].astype(F32) * cv * (za * _sigmoid(za))).astype(BF)

    def col(blk):
        return pl.BlockSpec((tm, d), lambda i: (i, blk))

    def prev(blk):
        return pl.BlockSpec((HALO, d), lambda i: (jnp.maximum(i * hb - 1, 0), blk))

    return pl.pallas_call(
        body, name="branch_a_fwd", grid=(t // tm,),
        in_specs=[col(XA_BLK), col(BA_BLK), col(CA_BLK), col(ZA_BLK), prev(XA_BLK), prev(CA_BLK),
                  pl.BlockSpec((8, d), lambda i: (0, 0)), pl.BlockSpec((1, d), lambda i: (0, 0))],
        out_specs=pl.BlockSpec((tm, d), lambda i: (i, 0)),
        out_shape=jax.ShapeDtypeStruct((t, d), BF),
        compiler_params=_cparams(("parallel",)),
    )(proj, proj, proj, proj, proj, proj, conv_w8, conv_b)


def _branch_a_bwd(dproj, proj, dya_pre, conv_w8, conv_b, *, tm=512):
    t = proj.shape[0]
    d = D_MODEL
    tm = min(tm, t)
    hb = tm // HALO
    nt = t // tm

    def body(dp_ref, xa_ref, ba_ref, ca_ref, za_ref, xap_ref, cap_ref, dy_ref, ban_ref, zan_ref, dyn_ref,
             w_ref, b_ref, o_ref, dc_ref):
        del dp_ref
        i = pl.program_id(0)
        xa = xa_ref[...].astype(F32)
        ca = ca_ref[...].astype(F32)
        ba = ba_ref[...].astype(F32)
        za = za_ref[...].astype(F32)
        u = ca * xa
        up = cap_ref[...].astype(F32) * xap_ref[...].astype(F32)
        up = jnp.where(i == 0, 0.0, up)
        u1 = _shift_down(u, up, 1)
        u2 = _shift_down(u, up, 2)
        w0, w1, w2 = w_ref[0:1, :], w_ref[1:2, :], w_ref[2:3, :]
        cv = w0 * u2 + w1 * u1 + w2 * u + b_ref[...]
        sg = _sigmoid(za)
        sz = za * sg
        dy = dy_ref[...].astype(F32)
        dcv = dy * ba * sz
        zan = zan_ref[...].astype(F32)
        dcvn = dyn_ref[...].astype(F32) * ban_ref[...].astype(F32) * (zan * _sigmoid(zan))
        dcvn = jnp.where(i == nt - 1, 0.0, dcvn)
        du = w2 * dcv + w1 * _shift_up(dcv, dcvn, 1) + w0 * _shift_up(dcv, dcvn, 2)
        o_ref[:, 0:d] = (du * ca).astype(BF)
        o_ref[:, d:2 * d] = (dy * cv * sz).astype(BF)
        o_ref[:, 2 * d:3 * d] = (du * xa).astype(BF)
        o_ref[:, 3 * d:4 * d] = (dy * ba * cv * sg * (1.0 + za * (1.0 - sg))).astype(BF)

        @pl.when(i == 0)
        def _():
            dc_ref[...] = jnp.zeros_like(dc_ref)

        dc_ref[0:1, :] += jnp.sum(dcv * u2, axis=0, keepdims=True)
        dc_ref[1:2, :] += jnp.sum(dcv * u1, axis=0, keepdims=True)
        dc_ref[2:3, :] += jnp.sum(dcv * u, axis=0, keepdims=True)
        dc_ref[3:4, :] += jnp.sum(dcv, axis=0, keepdims=True)

    def col(blk):
        return pl.BlockSpec((tm, d), lambda i: (i, blk))

    def prev(blk):
        return pl.BlockSpec((HALO, d), lambda i: (jnp.maximum(i * hb - 1, 0), blk))

    def nxt(blk):
        return pl.BlockSpec((HALO, d), lambda i: (jnp.minimum((i + 1) * hb, t // HALO - 1), blk))

    return pl.pallas_call(
        body, name="branch_a_bwd", grid=(nt,),
        in_specs=[pl.BlockSpec(memory_space=pl.ANY),
                  col(XA_BLK), col(BA_BLK), col(CA_BLK), col(ZA_BLK), prev(XA_BLK), prev(CA_BLK),
                  pl.BlockSpec((tm, d), lambda i: (i, 0)), nxt(BA_BLK), nxt(ZA_BLK),
                  pl.BlockSpec((HALO, d), lambda i: (jnp.minimum((i + 1) * hb, t // HALO - 1), 0)),
                  pl.BlockSpec((8, d), lambda i: (0, 0)), pl.BlockSpec((1, d), lambda i: (0, 0))],
        out_specs=[pl.BlockSpec((tm, 4 * d), lambda i: (i, 2)), pl.BlockSpec((8, d), lambda i: (0, 0))],
        out_shape=[jax.ShapeDtypeStruct(dproj.shape, BF), jax.ShapeDtypeStruct((8, d), F32)],
        input_output_aliases={0: 0},
        compiler_params=_cparams(("arbitrary",)),
    )(dproj, proj, proj, proj, proj, proj, proj, dya_pre, proj, proj, dya_pre, conv_w8, conv_b)


def _gate_vectors(gc, gt, h, lane_i, sub_i):
    ig_c = jnp.sum(jnp.where(lane_i == IG_LANE + h, gc, 0.0), axis=1, keepdims=True)
    fg_c = jnp.sum(jnp.where(lane_i == FG_LANE + h, gc, 0.0), axis=1, keepdims=True)
    ig_r = jnp.sum(jnp.where(sub_i == IG_LANE + h, gt, 0.0), axis=0, keepdims=True)
    fg_r = jnp.sum(jnp.where(sub_i == FG_LANE + h, gt, 0.0), axis=0, keepdims=True)
    return ig_c, fg_c, ig_r, fg_r


def _chunk_common(q, k, ig_c, fg_c, ig_r, fg_r, m_prev, n_prev, row, col):
    lf_c = _log_sigmoid(fg_c)
    lf_r = _log_sigmoid(fg_r)
    causal = col <= row
    b_c = jnp.sum(jnp.where(causal, lf_r, 0.0), axis=1, keepdims=True)
    b_r = jnp.sum(jnp.where(row <= col, lf_c, 0.0), axis=0, keepdims=True)
    dmat = jnp.where(causal, b_c - b_r + ig_r, NEG)
    a = b_c + m_prev
    m_row = jnp.maximum(a, jnp.max(dmat, axis=1, keepdims=True))
    est = jnp.exp(dmat - m_row)
    s = _dot_nt(q, k) * QK_SCALE * est
    inter = jnp.exp(a - m_row)
    den = jnp.sum(s, axis=1, keepdims=True) + inter * QK_SCALE * jnp.sum(
        q.astype(F32) * n_prev, axis=1, keepdims=True)
    expm = jnp.exp(-m_row)
    mx = jnp.maximum(jnp.abs(den), expm)
    b_last = jnp.sum(lf_r, axis=1, keepdims=True)
    g_r = b_last - b_r + ig_r
    g_c = b_last - b_c + ig_c
    m_new = jnp.maximum(b_last + m_prev, jnp.max(g_r, axis=1, keepdims=True))
    w_c = jnp.exp(g_c - m_new)
    decay = jnp.exp(b_last + m_prev - m_new)
    return est, s, inter, den, expm, mx, m_new, w_c, decay


def _mlstm_fwd(proj, gates, gate_bias, g_head, w_b):
    t = proj.shape[0]
    lc = min(_CHUNK, t)
    nc = t // lc

    def body(q_ref, k_ref, v_ref, o_ref, z_ref, g_ref, gb_ref, gh_ref, wb_ref,
             yb_ref, ybo_ref, hr_ref, cs_ref, aux_ref, c_scr, nm_scr):
        c = pl.program_id(0)

        @pl.when(c == 0)
        def _():
            c_scr[...] = jnp.zeros_like(c_scr)
            nm_scr[...] = jnp.zeros_like(nm_scr)
            nm_scr[:, 1:2, :] = jnp.full((N_HEADS, 1, DK), NEG, F32)

        row = lax.broadcasted_iota(jnp.int32, (lc, lc), 0)
        col = lax.broadcasted_iota(jnp.int32, (lc, lc), 1)
        gc = g_ref[...] + gb_ref[...]
        gt = gc.T
        lane_i = lax.broadcasted_iota(jnp.int32, gc.shape, 1)
        sub_i = lax.broadcasted_iota(jnp.int32, gt.shape, 0)
        for h in range(N_HEADS):
            ks = slice(h * DK, (h + 1) * DK)
            vs = slice(h * DV, (h + 1) * DV)
            q = q_ref[:, ks]
            k = k_ref[:, ks]
            v = v_ref[:, vs]
            ig_c, fg_c, ig_r, fg_r = _gate_vectors(gc, gt, h, lane_i, sub_i)
            n_prev = nm_scr[h, 0:1, :]
            m_prev = nm_scr[h, 1:2, 0:1]
            est, s, inter, den, expm, mx, m_new, w_c, decay = _chunk_common(
                q, k, ig_c, fg_c, ig_r, fg_r, m_prev, n_prev, row, col)
            c_prev = c_scr[h]
            c_prev_b = c_prev.astype(BF)
            num = _dot(s.astype(BF), v) + (inter * QK_SCALE) * _dot(q, c_prev_b)
            hh = num / mx
            r = lax.rsqrt(jnp.mean(hh * hh, axis=1, keepdims=True) + EPS)
            hbn = hh * r * gh_ref[:, vs]
            o = o_ref[:, vs].astype(F32)
            z = z_ref[:, vs].astype(F32)
            yb_ref[:, vs] = (_sigmoid(o) * hbn * (z * _sigmoid(z))).astype(BF)
            hr_ref[:, vs] = hh.astype(BF)
            cs_ref[0, h] = c_prev_b
            aux_ref[0, h] = nm_scr[h]
            kw = k.astype(F32) * w_c
            c_scr[h] = decay * c_prev + _dot_tn(kw.astype(BF), v)
            nm_scr[h, 0:1, :] = decay * n_prev + jnp.sum(kw, axis=0, keepdims=True)
            nm_scr[h, 1:2, :] = jnp.broadcast_to(m_new, (1, DK))
        ybo_ref[...] = _dot(yb_ref[...], wb_ref[...]).astype(BF)

    return pl.pallas_call(
        body, name="mlstm_fwd", grid=(nc,),
        in_specs=[pl.BlockSpec((lc, 1024), lambda c: (c, 0)),
                  pl.BlockSpec((lc, 1024), lambda c: (c, 1)),
                  pl.BlockSpec((lc, 2048), lambda c: (c, 1)),
                  pl.BlockSpec((lc, 2048), lambda c: (c, 2)),
                  pl.BlockSpec((lc, 2048), lambda c: (c, 3)),
                  pl.BlockSpec((lc, 128), lambda c: (c, 0)),
                  pl.BlockSpec((1, 128), lambda c: (0, 0)),
                  pl.BlockSpec((1, V_DIM), lambda c: (0, 0)),
                  pl.BlockSpec((V_DIM, D_MODEL), lambda c: (0, 0))],
        out_specs=[pl.BlockSpec((lc, V_DIM), lambda c: (c, 0)),
                   pl.BlockSpec((lc, D_MODEL), lambda c: (c, 0)),
                   pl.BlockSpec((lc, V_DIM), lambda c: (c, 0)),
                   pl.BlockSpec((1, N_HEADS, DK, DV), lambda c: (c, 0, 0, 0)),
                   pl.BlockSpec((1, N_HEADS, 8, DK), lambda c: (c, 0, 0, 0))],
        out_shape=[jax.ShapeDtypeStruct((t, V_DIM), BF), jax.ShapeDtypeStruct((t, D_MODEL), BF),
                   jax.ShapeDtypeStruct((t, V_DIM), BF),
                   jax.ShapeDtypeStruct((nc, N_HEADS, DK, DV), BF),
                   jax.ShapeDtypeStruct((nc, N_HEADS, 8, DK), F32)],
        scratch_shapes=[pltpu.VMEM((N_HEADS, DK, DV), F32), pltpu.VMEM((N_HEADS, 8, DK), F32)],
        compiler_params=_cparams(("arbitrary",)),
    )(proj, proj, proj, proj, proj, gates, gate_bias, g_head, w_b)


def _mlstm_bwd(dproj, proj, gates, gate_bias, g_head, h_raw, c_states, aux, yb_pre, dyb, w_b):
    t = proj.shape[0]
    lc = min(_CHUNK, t)
    nc = t // lc

    def body(dpin_ref, q_ref, k_ref, v_ref, o_ref, z_ref, g_ref, gb_ref, gh_ref, hr_ref, cs_ref, aux_ref,
             ybp_ref, dyb_ref, wb_ref, dp_ref, dg_ref, dbias_ref, dgh_ref, dwb_ref, dc_scr, dn_scr, dy_ref,
             dwb_scr):
        del dpin_ref
        step = pl.program_id(0)

        @pl.when(step == 0)
        def _():
            dc_scr[...] = jnp.zeros_like(dc_scr)
            dn_scr[...] = jnp.zeros_like(dn_scr)
            dbias_ref[...] = jnp.zeros_like(dbias_ref)
            dgh_ref[...] = jnp.zeros_like(dgh_ref)
            dwb_scr[...] = jnp.zeros_like(dwb_scr)

        dyb = dyb_ref[...]
        dy_ref[...] = _dot_nt(dyb, wb_ref[...])
        dwb_scr[...] += _dot_tn(ybp_ref[...], dyb)

        row = lax.broadcasted_iota(jnp.int32, (lc, lc), 0)
        col = lax.broadcasted_iota(jnp.int32, (lc, lc), 1)
        eye = row == col
        gc = g_ref[...] + gb_ref[...]
        gt = gc.T
        lane_i = lax.broadcasted_iota(jnp.int32, gc.shape, 1)
        sub_i = lax.broadcasted_iota(jnp.int32, gt.shape, 0)
        dgates = jnp.zeros(gc.shape, F32)
        for h in range(N_HEADS):
            ks = slice(h * DK, (h + 1) * DK)
            vs = slice(h * DV, (h + 1) * DV)
            q = q_ref[:, ks]
            k = k_ref[:, ks]
            v = v_ref[:, vs]
            ig_c, fg_c, ig_r, fg_r = _gate_vectors(gc, gt, h, lane_i, sub_i)
            n_prev = aux_ref[0, h, 0:1, :]
            m_prev = aux_ref[0, h, 1:2, 0:1]
            c_prev_b = cs_ref[0, h]
            est, s, inter, den, expm, mx, m_new, w_c, decay = _chunk_common(
                q, k, ig_c, fg_c, ig_r, fg_r, m_prev, n_prev, row, col)
            hb = hr_ref[:, vs].astype(F32)
            dyp = dy_ref[:, vs]
            o = o_ref[:, vs].astype(F32)
            z = z_ref[:, vs].astype(F32)
            so = _sigmoid(o)
            sgz = _sigmoid(z)
            sz = z * sgz
            r = lax.rsqrt(jnp.mean(hb * hb, axis=1, keepdims=True) + EPS)
            xh = hb * r
            gh = gh_ref[:, vs]
            hbn = xh * gh
            dyso = dyp * so
            dyso_h = dyso * hbn
            d_o = dyso_h * sz * (1.0 - so)
            d_z = dyso_h * sgz * (1.0 + z * (1.0 - sgz))
            dhbn = dyso * sz
            dgh_ref[0:1, vs] += jnp.sum(dhbn * xh, axis=0, keepdims=True)
            dxh = dhbn * gh
            dh = r * (dxh - xh * jnp.mean(dxh * xh, axis=1, keepdims=True))
            dnm = dh / mx
            hd = jnp.sum(dh * hb, axis=1, keepdims=True)
            cond = jnp.abs(den) > expm
            dden = jnp.where(cond, -hd / mx * jnp.sign(den), 0.0)
            dnm_b = dnm.astype(BF)
            p = _dot_nt(dnm_b, v) + dden
            dqk = (p * est * QK_SCALE).astype(BF)
            dq_inter = (inter * QK_SCALE) * (_dot_nt(dnm_b, c_prev_b) + dden * n_prev)
            dq = _dot(dqk, k) + dq_inter
            dc_new = dc_scr[h]
            dc_new_b = dc_new.astype(BF)
            dn_new = dn_scr[h, 0:1, :]
            kf = k.astype(F32)
            dk_state = w_c * (_dot_nt(v, dc_new_b) + dn_new)
            dk = _dot_tn(dqk, q) + dk_state
            dv = _dot_tn(s.astype(BF), dnm_b) + w_c * _dot(k, dc_new_b)
            dv1_r = jnp.sum(s * dden, axis=0, keepdims=True)
            dv1_c = (jnp.sum(jnp.where(eye, dv1_r, 0.0), axis=1, keepdims=True)
                     + w_c * jnp.sum(kf * dn_new, axis=1, keepdims=True))
            dli_c = jnp.sum(v.astype(F32) * dv, axis=1, keepdims=True) + dv1_c
            hmat = _dot((p * s).astype(BF), (row < col).astype(BF))
            from_prev_c = jnp.sum(q.astype(F32) * dq_inter, axis=1, keepdims=True)
            to_next_c = jnp.sum(kf * dk_state, axis=1, keepdims=True)
            through = decay * (
                jnp.sum(jnp.sum(dc_new * c_prev_b.astype(F32), axis=1, keepdims=True), axis=0, keepdims=True)
                + jnp.sum(dn_new * n_prev, axis=1, keepdims=True))
            dlf_r = through + jnp.sum(jnp.where(row >= col, hmat + from_prev_c, to_next_c), axis=0, keepdims=True)
            dlf_c = jnp.sum(jnp.where(eye, dlf_r, 0.0), axis=1, keepdims=True)
            dfg_c = dlf_c * _sigmoid(-fg_c)
            dgates = dgates + jnp.where(lane_i == IG_LANE + h, dli_c, 0.0) + jnp.where(
                lane_i == FG_LANE + h, dfg_c, 0.0)
            qi = q.astype(F32) * (inter * QK_SCALE)
            dc_scr[h] = decay * dc_new + _dot_tn(qi.astype(BF), dnm_b)
            dn_scr[h, 0:1, :] = decay * dn_new + jnp.sum(qi * dden, axis=0, keepdims=True)
            dp_ref[:, h * DK:(h + 1) * DK] = dq.astype(BF)
            dp_ref[:, 1024 + h * DK:1024 + (h + 1) * DK] = dk.astype(BF)
            dp_ref[:, 2048 + h * DV:2048 + (h + 1) * DV] = dv.astype(BF)
            dp_ref[:, 4096 + h * DV:4096 + (h + 1) * DV] = d_o.astype(BF)
            dp_ref[:, 6144 + h * DV:6144 + (h + 1) * DV] = d_z.astype(BF)
        dg_ref[...] = dgates
        dbias_ref[0:1, :] += jnp.sum(dgates, axis=0, keepdims=True)

        @pl.when(step == nc - 1)
        def _():
            dwb_ref[...] = dwb_scr[...].astype(BF)

    def rev(c):
        return nc - 1 - c

    return pl.pallas_call(
        body, name="mlstm_bwd", grid=(nc,),
        input_output_aliases={0: 0},
        in_specs=[pl.BlockSpec(memory_space=pl.ANY),
                  pl.BlockSpec((lc, 1024), lambda c: (rev(c), 0)),
                  pl.BlockSpec((lc, 1024), lambda c: (rev(c), 1)),
                  pl.BlockSpec((lc, 2048), lambda c: (rev(c), 1)),
                  pl.BlockSpec((lc, 2048), lambda c: (rev(c), 2)),
                  pl.BlockSpec((lc, 2048), lambda c: (rev(c), 3)),
                  pl.BlockSpec((lc, 128), lambda c: (rev(c), 0)),
                  pl.BlockSpec((1, 128), lambda c: (0, 0)),
                  pl.BlockSpec((1, V_DIM), lambda c: (0, 0)),
                  pl.BlockSpec((lc, V_DIM), lambda c: (rev(c), 0)),
                  pl.BlockSpec((1, N_HEADS, DK, DV), lambda c: (rev(c), 0, 0, 0)),
                  pl.BlockSpec((1, N_HEADS, 8, DK), lambda c: (rev(c), 0, 0, 0)),
                  pl.BlockSpec((lc, V_DIM), lambda c: (rev(c), 0)),
                  pl.BlockSpec((lc, D_MODEL), lambda c: (rev(c), 0)),
                  pl.BlockSpec((V_DIM, D_MODEL), lambda c: (0, 0))],
        out_specs=[pl.BlockSpec((lc, 8192), lambda c: (rev(c), 0)),
                   pl.BlockSpec((lc, 128), lambda c: (rev(c), 0)),
                   pl.BlockSpec((8, 128), lambda c: (0, 0)),
                   pl.BlockSpec((8, V_DIM), lambda c: (0, 0)),
                   pl.BlockSpec((V_DIM, D_MODEL), lambda c: (0, 0))],
        out_shape=[jax.ShapeDtypeStruct((t, N_MAIN), BF), jax.ShapeDtypeStruct((t, 128), F32),
                   jax.ShapeDtypeStruct((8, 128), F32), jax.ShapeDtypeStruct((8, V_DIM), F32),
                   jax.ShapeDtypeStruct((V_DIM, D_MODEL), BF)],
        scratch_shapes=[pltpu.VMEM((N_HEADS, DK, DV), F32), pltpu.VMEM((N_HEADS, 8, DK), F32),
                        pltpu.VMEM((lc, V_DIM), F32), pltpu.VMEM((V_DIM, D_MODEL), F32)],
        compiler_params=_cparams(("arbitrary",)),
    )(dproj, proj, proj, proj, proj, proj, gates, gate_bias, g_head, h_raw, c_states, aux, yb_pre, dyb, w_b)


GA_BLK, GB_BLK = 12, 13


def _full(shape):
    return pl.BlockSpec(shape, lambda i: (0,) * len(shape))


def _rms_backward(xv, g, dhn, dres):
    r = lax.rsqrt(jnp.mean(xv * xv, axis=1, keepdims=True) + EPS)
    xh = xv * r
    dxh = dhn * g
    dx = dres + r * (dxh - xh * jnp.mean(dxh * xh, axis=1, keepdims=True))
    return dx, jnp.sum(dhn * xh, axis=0, keepdims=True)


def _token_chain(ya_pre, w_a, yb, proj, w_o, x, g_ple, w_pg, p, w_pl, target, g_final, *, tm=256):
    t, d = x.shape
    tm = min(tm, t)

    def body(yap_ref, wa_ref, yb_ref, ga_ref, gb_ref, wo_ref, x_ref, gp_ref, wpg_ref, p_ref, wpl_ref, tg_ref,
             gf_ref, dx_ref, dyb_ref, dyap_ref, o_ref, sm_ref, dwa_ref, dwo_ref, dwpg_ref, dwpl_ref,
             dwa_acc, dwo_acc, dwpg_acc, dwpl_acc):
        i = pl.program_id(0)

        @pl.when(i == 0)
        def _():
            sm_ref[...] = jnp.zeros_like(sm_ref)
            dwa_acc[...] = jnp.zeros_like(dwa_acc)
            dwo_acc[...] = jnp.zeros_like(dwo_acc)
            dwpg_acc[...] = jnp.zeros_like(dwpg_acc)
            dwpl_acc[...] = jnp.zeros_like(dwpl_acc)

        yap = yap_ref[...]
        ya = _dot(yap, wa_ref[...]).astype(BF).astype(F32)
        yb_v = yb_ref[...].astype(F32)
        sa = _sigmoid(ga_ref[...].astype(F32))
        sb = _sigmoid(gb_ref[...].astype(F32))
        merged = (sa * ya + sb * yb_v).astype(BF)
        x1 = _dot(merged, wo_ref[...]) + x_ref[...]
        r1 = lax.rsqrt(jnp.mean(x1 * x1, axis=1, keepdims=True) + EPS)
        xh1 = x1 * r1
        gp = gp_ref[...]
        hn2 = xh1 * gp
        hn2_b = hn2.astype(BF)
        gate = _sigmoid(_dot(hn2_b, wpg_ref[...]))
        p_b = p_ref[...].astype(BF)
        pe_v = _dot(p_b, wpl_ref[...])
        x2 = x1 + gate * pe_v
        r2 = lax.rsqrt(jnp.mean(x2 * x2, axis=1, keepdims=True) + EPS)
        xh2 = x2 * r2
        gf = gf_ref[...]
        err = xh2 * gf - tg_ref[...]
        dy = err * (1.0 / d)
        dxh2 = dy * gf
        dx2 = r2 * (dxh2 - xh2 * jnp.mean(dxh2 * xh2, axis=1, keepdims=True))
        dgpre = (dx2 * pe_v * gate * (1.0 - gate)).astype(BF)
        dwpg_acc[...] += _dot_tn(hn2_b, dgpre)
        dwpl_acc[...] += _dot_tn(p_b, (dx2 * gate).astype(BF))
        dhn2 = _dot_nt(dgpre, wpg_ref[...])
        dxh1 = dhn2 * gp
        dx1 = dx2 + r1 * (dxh1 - xh1 * jnp.mean(dxh1 * xh1, axis=1, keepdims=True))
        dx_ref[...] = dx1
        dx1_b = dx1.astype(BF)
        dwo_acc[...] += _dot_tn(merged, dx1_b)
        dm = _dot_nt(dx1_b, wo_ref[...])
        dya = (dm * sa).astype(BF)
        dwa_acc[...] += _dot_tn(yap, dya)
        dyb_ref[...] = (dm * sb).astype(BF)
        o_ref[:, 0:d] = (dm * ya * sa * (1.0 - sa)).astype(BF)
        o_ref[:, d:2 * d] = (dm * yb_v * sb * (1.0 - sb)).astype(BF)
        dyap_ref[...] = _dot_nt(dya, wa_ref[...]).astype(BF)

        sm_ref[0:1, :] += (0.5 / d) * jnp.sum(err * err, axis=0, keepdims=True)
        sm_ref[1:2, :] += jnp.sum(dy * xh2, axis=0, keepdims=True)
        sm_ref[2:3, :] += jnp.sum(dhn2 * xh1, axis=0, keepdims=True)

        @pl.when(i == t // tm - 1)
        def _():
            dwa_ref[...] = dwa_acc[...].astype(BF)
            dwo_ref[...] = dwo_acc[...].astype(BF)
            dwpg_ref[...] = dwpg_acc[...].astype(BF)
            dwpl_ref[...] = dwpl_acc[...].astype(BF)

    row = pl.BlockSpec((tm, d), lambda i: (i, 0))
    bf = jax.ShapeDtypeStruct((t, d), BF)
    return pl.pallas_call(
        body, name="token_chain", grid=(t // tm,),
        in_specs=[row, _full((d, d)), row, pl.BlockSpec((tm, d), lambda i: (i, GA_BLK)),
                  pl.BlockSpec((tm, d), lambda i: (i, GB_BLK)), _full((d, d)), row, _full((1, d)), _full((d, d)),
                  pl.BlockSpec((tm, PLE_DIM), lambda i: (i, 0)), _full((PLE_DIM, d)), row, _full((1, d))],
        out_specs=[row, row, row, pl.BlockSpec((tm, 2 * d), lambda i: (i, 6)), _full((8, d)),
                   _full((d, d)), _full((d, d)), _full((d, d)), _full((PLE_DIM, d))],
        out_shape=[jax.ShapeDtypeStruct((t, d), F32), bf, bf,
                   jax.ShapeDtypeStruct((t, N_MAIN), BF), jax.ShapeDtypeStruct((8, d), F32),
                   jax.ShapeDtypeStruct((d, d), BF), jax.ShapeDtypeStruct((d, d), BF),
                   jax.ShapeDtypeStruct((d, d), BF), jax.ShapeDtypeStruct((PLE_DIM, d), BF)],
        scratch_shapes=[pltpu.VMEM((d, d), F32), pltpu.VMEM((d, d), F32), pltpu.VMEM((d, d), F32),
                        pltpu.VMEM((PLE_DIM, d), F32)],
        compiler_params=_cparams(("arbitrary",)),
    )(ya_pre, w_a, yb, proj, proj, w_o, x, g_ple, w_pg, p, w_pl, target, g_final)


def _dhn_mix_bwd(dproj, w_main, dgates, w_gate, x, g_mix, dx1, comm, *, tm=1024, tk=2048):
    t, d = x.shape
    tm = min(tm, t)
    nk = N_MAIN // tk
    ni = t // tm
    n_in = 7 + comm.n

    def body(*refs):
        dp_ref, w_ref, dgt_ref, wg_ref, x_ref, g_ref, dres_ref = refs[:7]
        dx_ref, dg_ref = refs[n_in], refs[n_in + 1]
        acc = refs[-1]
        start, wait = comm.ops(refs[7:n_in], refs[n_in + 2:n_in + 2 + comm.n], *refs[n_in + 2 + comm.n:-1])
        i, k = pl.program_id(0), pl.program_id(1)
        pl.when((i == 0) & (k == 0))(start)

        @pl.when(k == 0)
        def _():
            acc[...] = _dot_nt(dp_ref[...], w_ref[...]) + _dot_nt(dgt_ref[...], wg_ref[...])

        @pl.when(k > 0)
        def _():
            acc[...] += _dot_nt(dp_ref[...], w_ref[...])

        @pl.when((i == 0) & (k == 0))
        def _():
            dg_ref[...] = jnp.zeros_like(dg_ref)

        @pl.when(k == nk - 1)
        def _():
            dx, dg = _rms_backward(x_ref[...], g_ref[...], acc[...], dres_ref[...])
            dx_ref[...] = dx
            dg_ref[0:1, :] += dg

        pl.when((i == ni - 1) & (k == nk - 1))(wait)

    any_spec = pl.BlockSpec(memory_space=pl.ANY)
    row = pl.BlockSpec((tm, d), lambda i, k: (i, 0))
    res = pl.pallas_call(
        body, name="dhn_mix_bwd", grid=(ni, nk),
        in_specs=[pl.BlockSpec((tm, tk), lambda i, k: (i, k)), pl.BlockSpec((d, tk), lambda i, k: (0, k)),
                  pl.BlockSpec((tm, 128), lambda i, k: (i, 0)), pl.BlockSpec((d, 128), lambda i, k: (0, 0)),
                  row, pl.BlockSpec((1, d), lambda i, k: (0, 0)), row] + [any_spec] * comm.n,
        out_specs=[row, pl.BlockSpec((8, d), lambda i, k: (0, 0))] + [any_spec] * comm.n,
        out_shape=[jax.ShapeDtypeStruct((t, d), F32), jax.ShapeDtypeStruct((8, d), F32)] + comm.out_shapes,
        scratch_shapes=comm.scratch + [pltpu.VMEM((tm, d), F32)],
        compiler_params=_cparams(("arbitrary", "arbitrary")),
    )(dproj, w_main, dgates, w_gate, x, g_mix, dx1, *comm.srcs)
    return res[0], res[1], list(res[2:])


def _position():
    x, y, c = lax.axis_index("x"), lax.axis_index("y"), lax.axis_index("c")
    return x, y, c


def _all_gather(srcs):
    nb = len(srcs)
    any_spec = pl.BlockSpec(memory_space=pl.ANY)

    def body(*refs):
        src = refs[:nb]
        dst = refs[nb:2 * nb]
        send_sems, recv_sems, local_sems = refs[2 * nb:]
        x, y, c = _position()
        me, sibling = (x, y, c), (x, y, 1 - c)
        chips = [(1 - x, y), (x, 1 - y), (1 - x, 1 - y)]

        def slot(b, px, py, pc):
            return dst[b].at[4 * px + 2 * py + pc]

        def copy(k, b, block, to, from_src=False):
            return pltpu.make_async_remote_copy(
                src_ref=src[b] if from_src else slot(b, *block), dst_ref=slot(b, *block),
                send_sem=send_sems.at[b, k], recv_sem=recv_sems.at[b, k],
                device_id=to, device_id_type=MESH)

        mine = [pltpu.make_async_copy(src[b], slot(b, *me), local_sems.at[b]) for b in range(nb)]
        for cp in mine:
            cp.start()
        first = [copy(0, b, me, sibling, True) for b in range(nb)]
        first += [copy(1 + j, b, me, (*chip, c), True) for j, chip in enumerate(chips) for b in range(nb)]
        for cp in first:
            cp.start()
        passed = []
        for j, chip in enumerate(chips):
            for b in range(nb):
                copy(1 + j, b, (*chip, c), me).wait_recv()
                fwd = copy(4 + j, b, (*chip, c), sibling)
                fwd.start()
                passed.append(fwd)
        for b in range(nb):
            copy(0, b, sibling, me).wait_recv()
        for j, chip in enumerate(chips):
            for b in range(nb):
                copy(4 + j, b, (*chip, 1 - c), me).wait_recv()
        for cp in first + passed:
            cp.wait_send()
        for cp in mine:
            cp.wait()

    return pl.pallas_call(
        body, name="weights_all_gather",
        in_specs=[any_spec] * nb, out_specs=[any_spec] * nb,
        out_shape=[jax.ShapeDtypeStruct((N_DEV,) + s.shape, s.dtype) for s in srcs],
        scratch_shapes=[pltpu.SemaphoreType.DMA((nb, 7)), pltpu.SemaphoreType.DMA((nb, 7)),
                        pltpu.SemaphoreType.DMA((nb,))],
    )(*srcs)


def _comm_call(comm, *, name):
    any_spec = pl.BlockSpec(memory_space=pl.ANY)

    def body(*refs):
        start, wait = comm.ops(refs[:comm.n], refs[comm.n:2 * comm.n], *refs[2 * comm.n:])
        start()
        wait()

    return pl.pallas_call(
        body, name=name, in_specs=[any_spec] * comm.n, out_specs=[any_spec] * comm.n,
        out_shape=comm.out_shapes, scratch_shapes=comm.scratch,
    )(*comm.srcs)


LAYOUT_ROWS = 128
N_GLOB_BLK = 113
N_TAIL_BLK = 17


def _lane(tr):
    return lax.broadcasted_iota(jnp.int32, (tr, 128), 1)


def _assemble_w(g_win):
    d = g_win.shape[1]
    tr = LAYOUT_ROWS

    def body(win_ref, wm_ref, wg_ref):
        lane = _lane(tr)

        def shifted(k, j):
            cur = win_ref[k, :, 128 * j:128 * (j + 1)].astype(F32)
            if k == 0:
                return cur
            cur = pltpu.roll(cur, k, 1)
            if j == 0:
                return jnp.where(lane < k, 0.0, cur)
            prev = win_ref[k, :, 128 * (j - 1):128 * j].astype(F32)
            return jnp.where(lane < k, pltpu.roll(prev, k, 1), cur)

        def glob(gb):
            k = min(gb // 14, N_DEV - 1)
            j = gb - 14 * k
            v = shifted(k, j)
            if j == 0 and k > 0:
                v = v + shifted(k - 1, 14)
            return v

        for c in range(64):
            wm_ref[:, 128 * c:128 * (c + 1)] = glob(32 + c).astype(BF)
        for c in range(32):
            wm_ref[:, 8192 + 128 * c:8192 + 128 * (c + 1)] = glob(c).astype(BF)
        tail = [glob(96 + j) for j in range(N_TAIL_BLK)]
        for j in range(N_TAIL_BLK):
            v = jnp.where(lane < 120, pltpu.roll(tail[j], 120, 1),
                          pltpu.roll(tail[(j + 1) % N_TAIL_BLK], 120, 1)).astype(BF)
            if j < 16:
                wm_ref[:, 12288 + 128 * j:12288 + 128 * (j + 1)] = v
            else:
                wg_ref[...] = v

    return pl.pallas_call(
        body, name="assemble_w", grid=(d // tr,),
        in_specs=[pl.BlockSpec((N_DEV, tr, WIN_W), lambda i: (0, i, 0))],
        out_specs=[pl.BlockSpec((tr, N_MAIN), lambda i: (i, 0)), pl.BlockSpec((tr, 128), lambda i: (i, 0))],
        out_shape=[jax.ShapeDtypeStruct((d, N_MAIN), BF), jax.ShapeDtypeStruct((d, 128), BF)],
        compiler_params=_cparams(("parallel",)),
    )(g_win)


def _pack_windows(dw_main, dw_gate):
    d = dw_main.shape[0]
    tr = LAYOUT_ROWS

    def body(dm_ref, dg_ref, o_ref):
        lane = _lane(tr)

        def main_tail(j):
            return (dm_ref[:, 12288 + 128 * j:12288 + 128 * (j + 1)] if j < 16 else dg_ref[...]).astype(F32)

        def glob(gb):
            if gb >= N_GLOB_BLK:
                return jnp.zeros((tr, 128), F32)
            if gb < 32:
                return dm_ref[:, 8192 + 128 * gb:8192 + 128 * (gb + 1)].astype(F32)
            if gb < 96:
                return dm_ref[:, 128 * (gb - 32):128 * (gb - 31)].astype(F32)
            j = gb - 96
            return jnp.where(lane < 8, pltpu.roll(main_tail((j - 1) % N_TAIL_BLK), 8, 1),
                             pltpu.roll(main_tail(j), 8, 1))

        for j in range(N_DEV):
            cur = glob(14 * j)
            for m in range(WIN_W // 128):
                nxt = glob(14 * j + m + 1)
                if j == 0:
                    v = cur
                else:
                    v = jnp.where(lane < 128 - j, pltpu.roll(cur, 128 - j, 1), pltpu.roll(nxt, 128 - j, 1))
                o_ref[j, :, 128 * m:128 * (m + 1)] = v.astype(BF)
                cur = nxt

    return pl.pallas_call(
        body, name="pack_windows", grid=(d // tr,),
        in_specs=[pl.BlockSpec((tr, N_MAIN), lambda i: (i, 0)), pl.BlockSpec((tr, 128), lambda i: (i, 0))],
        out_specs=pl.BlockSpec((N_DEV, tr, WIN_W), lambda i: (0, i, 0)),
        out_shape=jax.ShapeDtypeStruct((N_DEV, d, WIN_W), BF),
        compiler_params=_cparams(("parallel",)),
    )(dw_main, dw_gate)


def _sum_slots(recv, *, name, tr):
    _, r, cdim = recv.shape
    tr = min(tr, r)

    def body(r_ref, o_ref):
        total = r_ref[0].astype(F32)
        for s in range(1, N_DEV):
            total = total + r_ref[s].astype(F32)
        o_ref[...] = total

    return pl.pallas_call(
        body, name=name, grid=(r // tr,),
        in_specs=[pl.BlockSpec((N_DEV, tr, cdim), lambda i: (0, i, 0))],
        out_specs=pl.BlockSpec((tr, cdim), lambda i: (i, 0)),
        out_shape=jax.ShapeDtypeStruct((r, cdim), F32),
        compiler_params=_cparams(("parallel",)),
    )(recv)


def _adamw(w, g, m, v, *, name):
    lead = w.ndim - 2
    r, cdim = w.shape[-2:]
    if r % 128 == 0:
        tr, tc = 128, cdim
    elif r >= 128 and cdim % 128 == 0:
        tr, tc = r, 128
    else:
        tr, tc = r, cdim
    c1 = 1.0 - ADAM_B1 ** ADAM_STEP
    c2 = 1.0 - ADAM_B2 ** ADAM_STEP

    def body(w_ref, g_ref, m_ref, v_ref, d_ref, mo_ref, vo_ref):
        gv = g_ref[...]
        mn = ADAM_B1 * m_ref[...] + (1.0 - ADAM_B1) * gv
        vn = ADAM_B2 * v_ref[...] + (1.0 - ADAM_B2) * (gv * gv)
        d_ref[...] = -ADAM_LR * ((mn / c1) / (jnp.sqrt(vn / c2) + ADAM_EPS) + ADAM_WD * w_ref[...])
        mo_ref[...] = mn
        vo_ref[...] = vn

    blk = pl.BlockSpec((1,) * lead + (tr, tc), lambda i, j: (0,) * lead + (i, j))
    shp = jax.ShapeDtypeStruct(w.shape, F32)
    return pl.pallas_call(
        body, name=name, grid=(r // tr, cdim // tc),
        in_specs=[blk] * 4, out_specs=[blk] * 3, out_shape=[shp] * 3,
        compiler_params=_cparams(("parallel", "parallel")),
    )(w, g, m, v)


def kernel(x, p, g_mix, w_in, conv_w, conv_b, w_a_out, b_gates, g_head, w_b_out, w_o, g_ple, w_ple_gate, w_ple, g_final, loss_target, m_g_mix, m_w_in, m_conv_w, m_conv_b, m_w_a_out, m_b_gates, m_g_head, m_w_b_out, m_w_o, m_g_ple, m_w_ple_gate, m_w_ple, m_g_final, v_g_mix, v_w_in, v_conv_w, v_conv_b, v_w_a_out, v_b_gates, v_g_head, v_w_b_out, v_w_o, v_g_ple, v_w_ple_gate, v_w_ple, v_g_final):
    d = D_MODEL
    t = x.shape[1]
    x2d = x.reshape(t, d)
    p2d = p.reshape(t, PLE_DIM)
    tgt = loss_target.reshape(t, d)

    win = jnp.pad(w_in[0].astype(BF), ((0, 0), (0, WIN_W - SHARD_W)))
    rows = jnp.concatenate([w_a_out[0].astype(BF), w_b_out[0].astype(BF), w_o[0].astype(BF),
                            w_ple_gate[0].astype(BF), w_ple[0].astype(BF).reshape(32, d)], axis=0)
    cfl = jnp.pad(conv_w[0], ((0, 5), (0, 0)))
    g_win, g_cf = _all_gather([win, cfl])

    w_main, w_gate = _assemble_w(g_win)
    conv_w8 = jnp.pad(g_cf[:, :3, :].transpose(1, 0, 2).reshape(3, d), ((0, 5), (0, 0)))
    gate_bias = jnp.pad(b_gates, ((0, 0), (IG_LANE, 0)))

    hn, hnt = _rms_fwd(x2d, g_mix, name="rms_mix")
    proj, gates, (g_rows,) = _mm(hn, w_main, form="nn", out_dtype=BF, name="proj", tm=2048,
                                 extra=(w_gate, F32), comm=_DirectComm([rows], "gather"))
    w_a = g_rows[:, 0:128].reshape(d, d)
    w_b = g_rows[:, 128:384].reshape(V_DIM, d)
    w_of = g_rows[:, 384:512].reshape(d, d)
    w_pg = g_rows[:, 512:640].reshape(d, d)
    w_pl = g_rows[:, 640:672].reshape(N_DEV, PLE_DIM, 128).transpose(1, 0, 2).reshape(PLE_DIM, d)
    ya_pre = _branch_a_fwd(proj, conv_w8, conv_b)
    yb_pre, yb, h_raw, c_states, aux = _mlstm_fwd(proj, gates, gate_bias, g_head, w_b)
    dx1, dyb, dya_pre, dproj, small_fin, dw_a, dw_o, dw_pg, dw_pl = _token_chain(
        ya_pre, w_a, yb, proj, w_of, x2d, g_ple, w_pg, p2d, w_pl, tgt, g_final.reshape(1, d))

    dproj, dconv = _branch_a_bwd(dproj, proj, dya_pre, conv_w8, conv_b)
    dproj, dgates, dbias, dg_head, dw_b = _mlstm_bwd(dproj, proj, gates, gate_bias, g_head, h_raw, c_states, aux,
                                                     yb_pre, dyb, w_b)
    s_rows = jnp.concatenate([
        dw_a.reshape(N_DEV, 128, d), dw_b.reshape(N_DEV, 256, d), dw_o.reshape(N_DEV, 128, d),
        dw_pg.reshape(N_DEV, 128, d),
        dw_pl.reshape(PLE_DIM, N_DEV, 128).transpose(1, 0, 2).reshape(N_DEV, 32, d)], axis=1)
    dgates_b = dgates.astype(BF)
    dw_main, dw_gate, (r_rows,) = _mm(hnt, dproj, form="nn", out_dtype=BF, name="dw_main", tk=2048,
                                      extra=(dgates_b, BF), comm=_DirectComm([s_rows], "exchange"))
    s_win = _pack_windows(dw_main, dw_gate)
    grad_x, dg_mix, (r_win,) = _dhn_mix_bwd(dproj, w_main, dgates_b, w_gate, x2d, g_mix, dx1,
                                            _DirectComm([s_win], "exchange"))

    vec = jnp.concatenate([dg_mix[0], dconv[3], dg_head[0], small_fin[2], small_fin[1],
                           dbias[0, IG_LANE:], jnp.zeros((7 * d - 6152,), F32)]).reshape(7, d)
    conv_part = jnp.pad(dconv[:3].reshape(3, N_DEV, 128).transpose(1, 0, 2).reshape(N_DEV, 1, 384),
                        ((0, 0), (0, 0), (0, d - 384)))
    s_f32 = jnp.concatenate([jnp.broadcast_to(vec[None], (N_DEV, 7, d)), conv_part], axis=1)
    (r_f32,) = _comm_call(_DirectComm([s_f32], "exchange"), name="small_grads_exchange")
    sum_win = _sum_slots(r_win, name="sum_win", tr=128)
    sum_rows = _sum_slots(r_rows, name="sum_rows", tr=96)
    sum_f32 = _sum_slots(r_f32, name="sum_f32", tr=8)

    g_w_in = sum_win[:, :SHARD_W]
    g_w_a = sum_rows[0:128]
    g_w_b = sum_rows[128:384]
    g_w_o = sum_rows[384:512]
    g_w_pg = sum_rows[512:640]
    g_w_pl = sum_rows[640:672].reshape(PLE_DIM, 128)
    vsum = sum_f32[:7].reshape(7 * d)
    g_g_mix = vsum[0:1024].reshape(1, d)
    g_conv_b = vsum[1024:2048].reshape(1, d)
    g_g_head = vsum[2048:4096].reshape(1, V_DIM)
    g_g_ple = vsum[4096:5120].reshape(1, d)
    g_g_final = vsum[5120:6144].reshape(1, d)
    g_b_gates = vsum[6144:6152].reshape(1, 8)
    g_conv_w = sum_f32[7, :384].reshape(3, 128)

    loss = lax.psum(jnp.sum(small_fin[0]), ("x", "y", "c"))

    names = ["g_mix", "w_in", "conv_w", "conv_b", "w_a_out", "b_gates", "g_head", "w_b_out", "w_o", "g_ple",
             "w_ple_gate", "w_ple", "g_final"]
    weights = [g_mix, w_in, conv_w, conv_b, w_a_out, b_gates, g_head, w_b_out, w_o, g_ple, w_ple_gate, w_ple,
               g_final]
    moms = [m_g_mix, m_w_in, m_conv_w, m_conv_b, m_w_a_out, m_b_gates, m_g_head, m_w_b_out, m_w_o, m_g_ple,
            m_w_ple_gate, m_w_ple, m_g_final]
    vels = [v_g_mix, v_w_in, v_conv_w, v_conv_b, v_w_a_out, v_b_gates, v_g_head, v_w_b_out, v_w_o, v_g_ple,
            v_w_ple_gate, v_w_ple, v_g_final]
    grads2d = [g_g_mix, g_w_in, g_conv_w, g_conv_b, g_w_a, g_b_gates, g_g_head, g_w_b, g_w_o, g_g_ple, g_w_pg,
               g_w_pl, g_g_final]
    grads, deltas, new_m, new_v = [], [], [], []
    for nm, w, m_, v_, g2 in zip(names, weights, moms, vels, grads2d):
        shp = w.shape
        if nm == "w_in":
            dl, mn, vn = _adamw(w[0].T, g2.T, m_[0].T, v_[0].T, name="adamw_" + nm)
            grads.append(g2.reshape(shp))
            deltas.append(dl.T.reshape(shp))
            new_m.append(mn.T.reshape(shp))
            new_v.append(vn.T.reshape(shp))
            continue
        kshp = shp if w.ndim >= 2 else (1,) + shp
        gk = g2.reshape(kshp)
        dl, mn, vn = _adamw(w.reshape(kshp), gk, m_.reshape(kshp), v_.reshape(kshp), name="adamw_" + nm)
        grads.append(gk.reshape(shp))
        deltas.append(dl.reshape(shp))
        new_m.append(mn.reshape(shp))
        new_v.append(vn.reshape(shp))
    return (loss, grad_x.reshape(x.shape), *grads, *deltas, *new_m, *new_v)
```

```python
import functools

import jax
import jax.numpy as jnp
from jax import lax
from jax.experimental import pallas as pl
from jax.experimental.pallas import tpu as pltpu

F32 = jnp.float32
BF = jnp.bfloat16

D_MODEL = 1024
N_HEADS = 4
DK = 256
DV = 512
V_DIM = 2048
PLE_DIM = 256
N_IN = 14344
N_MAIN = 14336
EPS = 1e-6
QK_SCALE = DK ** -0.5
NEG = -1e30
N_DEV = 8
SHARD_W = 1793
WIN_STRIDE = 1792
WIN_W = 1920
ROWS_PACK = 672
_CHUNK = 256
IG_LANE = 120
FG_LANE = 124

ADAM_LR = 0.001
ADAM_B1 = 0.9
ADAM_B2 = 0.999
ADAM_EPS = 1e-08
ADAM_WD = 0.01
ADAM_STEP = 10

VMEM_LIMIT = 56 * 1024 * 1024
MESH = pl.DeviceIdType.MESH


def _cparams(sem):
    return pltpu.CompilerParams(dimension_semantics=sem, vmem_limit_bytes=VMEM_LIMIT)


def _sigmoid(x):
    return 1.0 / (1.0 + jnp.exp(-x))


def _log_sigmoid(x):
    return jnp.minimum(x, 0.0) - jnp.log(1.0 + jnp.exp(-jnp.abs(x)))


def _dot(a, b):
    return jnp.dot(a, b, preferred_element_type=F32)


def _dot_nt(a, b):
    return lax.dot_general(a, b, (((1,), (1,)), ((), ())), preferred_element_type=F32)


def _dot_tn(a, b):
    return lax.dot_general(a, b, (((0,), (0,)), ((), ())), preferred_element_type=F32)


class _DirectComm:
    def __init__(self, srcs, kind):
        self.srcs = list(srcs)
        self.kind = kind
        self.n = len(self.srcs)
        if kind == "exchange":
            self.out_shapes = [jax.ShapeDtypeStruct(s.shape, s.dtype) for s in self.srcs]
        else:
            self.out_shapes = [jax.ShapeDtypeStruct((N_DEV,) + s.shape, s.dtype) for s in self.srcs]
        self.scratch = [pltpu.SemaphoreType.DMA((self.n, 7)), pltpu.SemaphoreType.DMA((self.n, 7)),
                        pltpu.SemaphoreType.DMA((self.n,))]

    def ops(self, src, dst, send_sems, recv_sems, local_sems):
        exchange = self.kind == "exchange"

        def descriptors():
            x, y, c = _position()
            me_lin = 4 * x + 2 * y + c
            local = [pltpu.make_async_copy(src[b].at[me_lin] if exchange else src[b], dst[b].at[me_lin],
                                           local_sems.at[b]) for b in range(self.n)]
            sends, recvs = [], []
            for f in range(1, N_DEV):
                px = (1 - x) if (f >> 2) & 1 else x
                py = (1 - y) if (f >> 1) & 1 else y
                pc = (1 - c) if f & 1 else c
                peer_lin = 4 * px + 2 * py + pc
                for b in range(self.n):
                    out = src[b].at[peer_lin] if exchange else src[b]
                    common = dict(send_sem=send_sems.at[b, f - 1], recv_sem=recv_sems.at[b, f - 1],
                                  device_id=(px, py, pc), device_id_type=MESH)
                    sends.append(pltpu.make_async_remote_copy(src_ref=out, dst_ref=dst[b].at[me_lin], **common))
                    recvs.append(pltpu.make_async_remote_copy(src_ref=out, dst_ref=dst[b].at[peer_lin], **common))
            return local, sends, recvs

        def start():
            local, sends, _ = descriptors()
            for cp in local + sends:
                cp.start()

        def wait():
            local, sends, recvs = descriptors()
            for cp in recvs:
                cp.wait_recv()
            for cp in sends:
                cp.wait_send()
            for cp in local:
                cp.wait()

        return start, wait


def _mm(a, b, *, form, out_dtype, name, tm=1024, tn=1024, tk=1024, add=None, extra=None, comm=None):
    assert extra is None or form == "nn"
    if form == "nn":
        m, kc = a.shape
        n = b.shape[1]
    elif form == "nt":
        m, kc = a.shape
        n = b.shape[0]
    else:
        kc, m = a.shape
        n = b.shape[1]
    tm, tn, tk = min(tm, m), min(tn, n), min(tk, kc)
    assert m % tm == 0 and n % tn == 0 and kc % tk == 0, (name, a.shape, b.shape)
    nk = kc // tk
    if form == "tn":
        a_spec = pl.BlockSpec((tk, tm), lambda i, j, k: (k, i))
    else:
        a_spec = pl.BlockSpec((tm, tk), lambda i, j, k: (i, k))
    if form == "nt":
        b_spec = pl.BlockSpec((tn, tk), lambda i, j, k: (j, k))
    else:
        b_spec = pl.BlockSpec((tk, tn), lambda i, j, k: (k, j))
    o_spec = pl.BlockSpec((tm, tn), lambda i, j, k: (i, j))
    dot = {"nn": _dot, "nt": _dot_nt, "tn": _dot_tn}[form]
    has_add = add is not None

    use_acc = nk > 1 and (has_add or out_dtype != F32)
    has_x = extra is not None
    n2 = extra[0].shape[1] if has_x else 0
    use_acc2 = has_x and nk > 1 and extra[1] != F32
    n_comm = comm.n if comm else 0
    n_in = 2 + int(has_add) + int(has_x) + n_comm
    n_out = 1 + int(has_x) + n_comm
    grid = (m // tm, n // tn, nk)

    def body(*refs):
        a_ref, b_ref = refs[0], refs[1]
        add_ref = refs[2] if has_add else None
        b2_ref = refs[2 + int(has_add)] if has_x else None
        o_ref = refs[n_in]
        o2_ref = refs[n_in + 1] if has_x else None
        scr = list(refs[n_in + n_out + (3 if comm else 0):])
        acc = scr.pop(0) if use_acc else o_ref
        acc2 = scr.pop(0) if use_acc2 else o2_ref
        i, j, k = pl.program_id(0), pl.program_id(1), pl.program_id(2)
        if comm:
            start, wait = comm.ops(refs[n_in - n_comm:n_in], refs[n_in + n_out - n_comm:n_in + n_out],
                                   *refs[n_in + n_out:n_in + n_out + 3])
            pl.when((i == 0) & (j == 0) & (k == 0))(start)

        def part():
            return dot(a_ref[...].astype(BF), b_ref[...].astype(BF))

        def part2():
            return dot(a_ref[...].astype(BF), b2_ref[...].astype(BF))

        def finish(total):
            if has_add:
                total = total + add_ref[...].astype(F32)
            o_ref[...] = total.astype(out_dtype)

        if nk == 1:
            finish(part())
            if has_x:
                @pl.when(j == 0)
                def _():
                    o2_ref[...] = part2().astype(extra[1])
        else:
            @pl.when(k == 0)
            def _():
                acc[...] = part()

            @pl.when(k > 0)
            def _():
                acc[...] += part()

            if use_acc:
                @pl.when(k == nk - 1)
                def _():
                    finish(acc[...])
            if has_x:
                @pl.when((j == 0) & (k == 0))
                def _():
                    acc2[...] = part2()

                @pl.when((j == 0) & (k > 0))
                def _():
                    acc2[...] += part2()

                if use_acc2:
                    @pl.when((j == 0) & (k == nk - 1))
                    def _():
                        o2_ref[...] = acc2[...].astype(extra[1])
        if comm:
            pl.when((i == grid[0] - 1) & (j == grid[1] - 1) & (k == nk - 1))(wait)

    any_spec = pl.BlockSpec(memory_space=pl.ANY)
    o2_spec = pl.BlockSpec((tm, n2), lambda i, j, k: (i, 0))
    in_specs = ([a_spec, b_spec] + ([o_spec] if has_add else [])
                + ([pl.BlockSpec((tk, n2), lambda i, j, k: (k, 0))] if has_x else []) + [any_spec] * n_comm)
    args = (a, b) + ((add,) if has_add else ()) + ((extra[0],) if has_x else ()) + (tuple(comm.srcs) if comm else ())
    out_specs = [o_spec] + ([o2_spec] if has_x else []) + [any_spec] * n_comm
    out_shape = ([jax.ShapeDtypeStruct((m, n), out_dtype)]
                 + ([jax.ShapeDtypeStruct((m, n2), extra[1])] if has_x else []) + (comm.out_shapes if comm else []))
    scratch = ((comm.scratch if comm else []) + ([pltpu.VMEM((tm, tn), F32)] if use_acc else [])
               + ([pltpu.VMEM((tm, n2), F32)] if use_acc2 else []))
    if comm:
        sem = ("arbitrary",) * 3
    else:
        sem = ("parallel", "arbitrary" if has_x else "parallel", "arbitrary")
    res = pl.pallas_call(
        body, name=name, grid=grid, in_specs=in_specs, out_specs=out_specs, out_shape=out_shape,
        scratch_shapes=scratch, compiler_params=_cparams(sem),
    )(*args)
    if not (comm or has_x):
        return res[0]
    return tuple(res[:1 + int(has_x)]) + ((list(res[1 + int(has_x):]),) if comm else ())


HALO = 16
XA_BLK, BA_BLK, CA_BLK, ZA_BLK = 8, 9, 10, 11


def _shift_down(u, prev, n):
    tm = u.shape[0]
    rolled = pltpu.roll(u, n, 0)
    row = lax.broadcasted_iota(jnp.int32, u.shape, 0)
    out = rolled
    for j in range(n):
        out = jnp.where(row == j, prev[HALO - n + j:HALO - n + j + 1, :], out)
    return out


def _shift_up(u, nxt, n):
    tm = u.shape[0]
    rolled = pltpu.roll(u, tm - n, 0)
    row = lax.broadcasted_iota(jnp.int32, u.shape, 0)
    out = rolled
    for j in range(n):
        out = jnp.where(row == tm - n + j, nxt[j:j + 1, :], out)
    return out


def _branch_a_fwd(proj, conv_w8, conv_b, *, tm=512):
    t = proj.shape[0]
    d = D_MODEL
    tm = min(tm, t)
    hb = tm // HALO

    def body(xa_ref, ba_ref, ca_ref, za_ref, xap_ref, cap_ref, w_ref, b_ref, o_ref):
        i = pl.program_id(0)
        u = ca_ref[...].astype(F32) * xa_ref[...].astype(F32)
        up = cap_ref[...].astype(F32) * xap_ref[...].astype(F32)
        up = jnp.where(i == 0, 0.0, up)
        u1 = _shift_down(u, up, 1)
        u2 = _shift_down(u, up, 2)
        cv = w_ref[0:1, :] * u2 + w_ref[1:2, :] * u1 + w_ref[2:3, :] * u + b_ref[...]
        za = za_ref[...].astype(F32)
        o_ref[...] = (ba_ref[...].astype(F32) * cv * (za * _sigmoid(za))).astype(BF)

    def col(blk):
        return pl.BlockSpec((tm, d), lambda i: (i, blk))

    def prev(blk):
        return pl.BlockSpec((HALO, d), lambda i: (jnp.maximum(i * hb - 1, 0), blk))

    return pl.pallas_call(
        body, name="branch_a_fwd", grid=(t // tm,),
        in_specs=[col(XA_BLK), col(BA_BLK), col(CA_BLK), col(ZA_BLK), prev(XA_BLK), prev(CA_BLK),
                  pl.BlockSpec((8, d), lambda i: (0, 0)), pl.BlockSpec((1, d), lambda i: (0, 0))],
        out_specs=pl.BlockSpec((tm, d), lambda i: (i, 0)),
        out_shape=jax.ShapeDtypeStruct((t, d), BF),
        compiler_params=_cparams(("parallel",)),
    )(proj, proj, proj, proj, proj, proj, conv_w8, conv_b)


def _branch_a_bwd(dproj, proj, dya_pre, conv_w8, conv_b, *, tm=512):
    t = proj.shape[0]
    d = D_MODEL
    tm = min(tm, t)
    hb = tm // HALO
    nt = t // tm

    def body(dp_ref, xa_ref, ba_ref, ca_ref, za_ref, xap_ref, cap_ref, dy_ref, ban_ref, zan_ref, dyn_ref,
             w_ref, b_ref, o_ref, dc_ref):
        del dp_ref
        i = pl.program_id(0)
        xa = xa_ref[...].astype(F32)
        ca = ca_ref[...].astype(F32)
        ba = ba_ref[...].astype(F32)
        za = za_ref[...].astype(F32)
        u = ca * xa
        up = cap_ref[...].astype(F32) * xap_ref[...].astype(F32)
        up = jnp.where(i == 0, 0.0, up)
        u1 = _shift_down(u, up, 1)
        u2 = _shift_down(u, up, 2)
        w0, w1, w2 = w_ref[0:1, :], w_ref[1:2, :], w_ref[2:3, :]
        cv = w0 * u2 + w1 * u1 + w2 * u + b_ref[...]
        sg = _sigmoid(za)
        sz = za * sg
        dy = dy_ref[...].astype(F32)
        dcv = dy * ba * sz
        zan = zan_ref[...].astype(F32)
        dcvn = dyn_ref[...].astype(F32) * ban_ref[...].astype(F32) * (zan * _sigmoid(zan))
        dcvn = jnp.where(i == nt - 1, 0.0, dcvn)
        du = w2 * dcv + w1 * _shift_up(dcv, dcvn, 1) + w0 * _shift_up(dcv, dcvn, 2)
        o_ref[:, 0:d] = (du * ca).astype(BF)
        o_ref[:, d:2 * d] = (dy * cv * sz).astype(BF)
        o_ref[:, 2 * d:3 * d] = (du * xa).astype(BF)
        o_ref[:, 3 * d:4 * d] = (dy * ba * cv * sg * (1.0 + za * (1.0 - sg))).astype(BF)

        @pl.when(i == 0)
        def _():
            dc_ref[...] = jnp.zeros_like(dc_ref)

        dc_ref[0:1, :] += jnp.sum(dcv * u2, axis=0, keepdims=True)
        dc_ref[1:2, :] += jnp.sum(dcv * u1, axis=0, keepdims=True)
        dc_ref[2:3, :] += jnp.sum(dcv * u, axis=0, keepdims=True)
        dc_ref[3:4, :] += jnp.sum(dcv, axis=0, keepdims=True)

    def col(blk):
        return pl.BlockSpec((tm, d), lambda i: (i, blk))

    def prev(blk):
        return pl.BlockSpec((HALO, d), lambda i: (jnp.maximum(i * hb - 1, 0), blk))

    def nxt(blk):
        return pl.BlockSpec((HALO, d), lambda i: (jnp.minimum((i + 1) * hb, t // HALO - 1), blk))

    return pl.pallas_call(
        body, name="branch_a_bwd", grid=(nt,),
        in_specs=[pl.BlockSpec(memory_space=pl.ANY),
                  col(XA_BLK), col(BA_BLK), col(CA_BLK), col(ZA_BLK), prev(XA_BLK), prev(CA_BLK),
                  pl.BlockSpec((tm, d), lambda i: (i, 0)), nxt(BA_BLK), nxt(ZA_BLK),
                  pl.BlockSpec((HALO, d), lambda i: (jnp.minimum((i + 1) * hb, t // HALO - 1), 0)),
                  pl.BlockSpec((8, d), lambda i: (0, 0)), pl.BlockSpec((1, d), lambda i: (0, 0))],
        out_specs=[pl.BlockSpec((tm, 4 * d), lambda i: (i, 2)), pl.BlockSpec((8, d), lambda i: (0, 0))],
        out_shape=[jax.ShapeDtypeStruct(dproj.shape, BF), jax.ShapeDtypeStruct((8, d), F32)],
        input_output_aliases={0: 0},
        compiler_params=_cparams(("arbitrary",)),
    )(dproj, proj, proj, proj, proj, proj, proj, dya_pre, proj, proj, dya_pre, conv_w8, conv_b)


def _gate_vectors(gc, gt, h, lane_i, sub_i):
    ig_c = jnp.sum(jnp.where(lane_i == IG_LANE + h, gc, 0.0), axis=1, keepdims=True)
    fg_c = jnp.sum(jnp.where(lane_i == FG_LANE + h, gc, 0.0), axis=1, keepdims=True)
    ig_r = jnp.sum(jnp.where(sub_i == IG_LANE + h, gt, 0.0), axis=0, keepdims=True)
    fg_r = jnp.sum(jnp.where(sub_i == FG_LANE + h, gt, 0.0), axis=0, keepdims=True)
    return ig_c, fg_c, ig_r, fg_r


def _chunk_common(q, k, ig_c, fg_c, ig_r, fg_r, m_prev, n_prev, row, col):
    lf_c = _log_sigmoid(fg_c)
    lf_r = _log_sigmoid(fg_r)
    causal = col <= row
    b_c = jnp.sum(jnp.where(causal, lf_r, 0.0), axis=1, keepdims=True)
    b_r = jnp.sum(jnp.where(row <= col, lf_c, 0.0), axis=0, keepdims=True)
    dmat = jnp.where(causal, b_c - b_r + ig_r, NEG)
    a = b_c + m_prev
    m_row = jnp.maximum(a, jnp.max(dmat, axis=1, keepdims=True))
    est = jnp.exp(dmat - m_row)
    s = _dot_nt(q, k) * QK_SCALE * est
    inter = jnp.exp(a - m_row)
    den = jnp.sum(s, axis=1, keepdims=True) + inter * QK_SCALE * jnp.sum(
        q.astype(F32) * n_prev, axis=1, keepdims=True)
    expm = jnp.exp(-m_row)
    mx = jnp.maximum(jnp.abs(den), expm)
    b_last = jnp.sum(lf_r, axis=1, keepdims=True)
    g_r = b_last - b_r + ig_r
    g_c = b_last - b_c + ig_c
    m_new = jnp.maximum(b_last + m_prev, jnp.max(g_r, axis=1, keepdims=True))
    w_c = jnp.exp(g_c - m_new)
    decay = jnp.exp(b_last + m_prev - m_new)
    return est, s, inter, den, expm, mx, m_new, w_c, decay


def _mlstm_fwd(proj, gates, gate_bias, g_head, w_b):
    t = proj.shape[0]
    lc = min(_CHUNK, t)
    nc = t // lc

    def body(q_ref, k_ref, v_ref, o_ref, z_ref, g_ref, gb_ref, gh_ref, wb_ref,
             yb_ref, ybo_ref, hr_ref, cs_ref, aux_ref, c_scr, nm_scr):
        c = pl.program_id(0)

        @pl.when(c == 0)
        def _():
            c_scr[...] = jnp.zeros_like(c_scr)
            nm_scr[...] = jnp.zeros_like(nm_scr)
            nm_scr[:, 1:2, :] = jnp.full((N_HEADS, 1, DK), NEG, F32)

        row = lax.broadcasted_iota(jnp.int32, (lc, lc), 0)
        col = lax.broadcasted_iota(jnp.int32, (lc, lc), 1)
        gc = g_ref[...] + gb_ref[...]
        gt = gc.T
        lane_i = lax.broadcasted_iota(jnp.int32, gc.shape, 1)
        sub_i = lax.broadcasted_iota(jnp.int32, gt.shape, 0)
        for h in range(N_HEADS):
            ks = slice(h * DK, (h + 1) * DK)
            vs = slice(h * DV, (h + 1) * DV)
            q = q_ref[:, ks]
            k = k_ref[:, ks]
            v = v_ref[:, vs]
            ig_c, fg_c, ig_r, fg_r = _gate_vectors(gc, gt, h, lane_i, sub_i)
            n_prev = nm_scr[h, 0:1, :]
            m_prev = nm_scr[h, 1:2, 0:1]
            est, s, inter, den, expm, mx, m_new, w_c, decay = _chunk_common(
                q, k, ig_c, fg_c, ig_r, fg_r, m_prev, n_prev, row, col)
            c_prev = c_scr[h]
            c_prev_b = c_prev.astype(BF)
            num = _dot(s.astype(BF), v) + (inter * QK_SCALE) * _dot(q, c_prev_b)
            hh = num / mx
            r = lax.rsqrt(jnp.mean(hh * hh, axis=1, keepdims=True) + EPS)
            hbn = hh * r * gh_ref[:, vs]
            o = o_ref[:, vs].astype(F32)
            z = z_ref[:, vs].astype(F32)
            yb_ref[:, vs] = (_sigmoid(o) * hbn * (z * _sigmoid(z))).astype(BF)
            hr_ref[:, vs] = hh.astype(BF)
            cs_ref[0, h] = c_prev_b
            aux_ref[0, h] = nm_scr[h]
            kw = k.astype(F32) * w_c
            c_scr[h] = decay * c_prev + _dot_tn(kw.astype(BF), v)
            nm_scr[h, 0:1, :] = decay * n_prev + jnp.sum(kw, axis=0, keepdims=True)
            nm_scr[h, 1:2, :] = jnp.broadcast_to(m_new, (1, DK))
        ybo_ref[...] = _dot(yb_ref[...], wb_ref[...]).astype(BF)

    return pl.pallas_call(
        body, name="mlstm_fwd", grid=(nc,),
        in_specs=[pl.BlockSpec((lc, 1024), lambda c: (c, 0)),
                  pl.BlockSpec((lc, 1024), lambda c: (c, 1)),
                  pl.BlockSpec((lc, 2048), lambda c: (c, 1)),
                  pl.BlockSpec((lc, 2048), lambda c: (c, 2)),
                  pl.BlockSpec((lc, 2048), lambda c: (c, 3)),
                  pl.BlockSpec((lc, 128), lambda c: (c, 0)),
                  pl.BlockSpec((1, 128), lambda c: (0, 0)),
                  pl.BlockSpec((1, V_DIM), lambda c: (0, 0)),
                  pl.BlockSpec((V_DIM, D_MODEL), lambda c: (0, 0))],
        out_specs=[pl.BlockSpec((lc, V_DIM), lambda c: (c, 0)),
                   pl.BlockSpec((lc, D_MODEL), lambda c: (c, 0)),
                   pl.BlockSpec((lc, V_DIM), lambda c: (c, 0)),
                   pl.BlockSpec((1, N_HEADS, DK, DV), lambda c: (c, 0, 0, 0)),
                   pl.BlockSpec((1, N_HEADS, 8, DK), lambda c: (c, 0, 0, 0))],
        out_shape=[jax.ShapeDtypeStruct((t, V_DIM), BF), jax.ShapeDtypeStruct((t, D_MODEL), BF),
                   jax.ShapeDtypeStruct((t, V_DIM), BF),
                   jax.ShapeDtypeStruct((nc, N_HEADS, DK, DV), BF),
                   jax.ShapeDtypeStruct((nc, N_HEADS, 8, DK), F32)],
        scratch_shapes=[pltpu.VMEM((N_HEADS, DK, DV), F32), pltpu.VMEM((N_HEADS, 8, DK), F32)],
        compiler_params=_cparams(("arbitrary",)),
    )(proj, proj, proj, proj, proj, gates, gate_bias, g_head, w_b)


def _mlstm_bwd(dproj, proj, gates, gate_bias, g_head, h_raw, c_states, aux, yb_pre, dyb, w_b):
    t = proj.shape[0]
    lc = min(_CHUNK, t)
    nc = t // lc

    def body(dpin_ref, q_ref, k_ref, v_ref, o_ref, z_ref, g_ref, gb_ref, gh_ref, hr_ref, cs_ref, aux_ref,
             ybp_ref, dyb_ref, wb_ref, dp_ref, dg_ref, dbias_ref, dgh_ref, dwb_ref, dc_scr, dn_scr, dy_ref,
             dwb_scr):
        del dpin_ref
        step = pl.program_id(0)

        @pl.when(step == 0)
        def _():
            dc_scr[...] = jnp.zeros_like(dc_scr)
            dn_scr[...] = jnp.zeros_like(dn_scr)
            dbias_ref[...] = jnp.zeros_like(dbias_ref)
            dgh_ref[...] = jnp.zeros_like(dgh_ref)
            dwb_scr[...] = jnp.zeros_like(dwb_scr)

        dyb = dyb_ref[...]
        dy_ref[...] = _dot_nt(dyb, wb_ref[...])
        dwb_scr[...] += _dot_tn(ybp_ref[...], dyb)

        row = lax.broadcasted_iota(jnp.int32, (lc, lc), 0)
        col = lax.broadcasted_iota(jnp.int32, (lc, lc), 1)
        eye = row == col
        gc = g_ref[...] + gb_ref[...]
        gt = gc.T
        lane_i = lax.broadcasted_iota(jnp.int32, gc.shape, 1)
        sub_i = lax.broadcasted_iota(jnp.int32, gt.shape, 0)
        dgates = jnp.zeros(gc.shape, F32)
        for h in range(N_HEADS):
            ks = slice(h * DK, (h + 1) * DK)
            vs = slice(h * DV, (h + 1) * DV)
            q = q_ref[:, ks]
            k = k_ref[:, ks]
            v = v_ref[:, vs]
            ig_c, fg_c, ig_r, fg_r = _gate_vectors(gc, gt, h, lane_i, sub_i)
            n_prev = aux_ref[0, h, 0:1, :]
            m_prev = aux_ref[0, h, 1:2, 0:1]
            c_prev_b = cs_ref[0, h]
            est, s, inter, den, expm, mx, m_new, w_c, decay = _chunk_common(
                q, k, ig_c, fg_c, ig_r, fg_r, m_prev, n_prev, row, col)
            hb = hr_ref[:, vs].astype(F32)
            dyp = dy_ref[:, vs]
            o = o_ref[:, vs].astype(F32)
            z = z_ref[:, vs].astype(F32)
            so = _sigmoid(o)
            sgz = _sigmoid(z)
            sz = z * sgz
            r = lax.rsqrt(jnp.mean(hb * hb, axis=1, keepdims=True) + EPS)
            xh = hb * r
            gh = gh_ref[:, vs]
            hbn = xh * gh
            dyso = dyp * so
            dyso_h = dyso * hbn
            d_o = dyso_h * sz * (1.0 - so)
            d_z = dyso_h * sgz * (1.0 + z * (1.0 - sgz))
            dhbn = dyso * sz
            dgh_ref[0:1, vs] += jnp.sum(dhbn * xh, axis=0, keepdims=True)
            dxh = dhbn * gh
            dh = r * (dxh - xh * jnp.mean(dxh * xh, axis=1, keepdims=True))
            dnm = dh / mx
            hd = jnp.sum(dh * hb, axis=1, keepdims=True)
            cond = jnp.abs(den) > expm
            dden = jnp.where(cond, -hd / mx * jnp.sign(den), 0.0)
            dnm_b = dnm.astype(BF)
            p = _dot_nt(dnm_b, v) + dden
            dqk = (p * est * QK_SCALE).astype(BF)
            dq_inter = (inter * QK_SCALE) * (_dot_nt(dnm_b, c_prev_b) + dden * n_prev)
            dq = _dot(dqk, k) + dq_inter
            dc_new = dc_scr[h]
            dc_new_b = dc_new.astype(BF)
            dn_new = dn_scr[h, 0:1, :]
            kf = k.astype(F32)
            dk_state = w_c * (_dot_nt(v, dc_new_b) + dn_new)
            dk = _dot_tn(dqk, q) + dk_state
            dv = _dot_tn(s.astype(BF), dnm_b) + w_c * _dot(k, dc_new_b)
            dv1_r = jnp.sum(s * dden, axis=0, keepdims=True)
            dv1_c = (jnp.sum(jnp.where(eye, dv1_r, 0.0), axis=1, keepdims=True)
                     + w_c * jnp.sum(kf * dn_new, axis=1, keepdims=True))
            dli_c = jnp.sum(v.astype(F32) * dv, axis=1, keepdims=True) + dv1_c
            hmat = _dot((p * s).astype(BF), (row < col).astype(BF))
            from_prev_c = jnp.sum(q.astype(F32) * dq_inter, axis=1, keepdims=True)
            to_next_c = jnp.sum(kf * dk_state, axis=1, keepdims=True)
            through = decay * (
                jnp.sum(jnp.sum(dc_new * c_prev_b.astype(F32), axis=1, keepdims=True), axis=0, keepdims=True)
                + jnp.sum(dn_new * n_prev, axis=1, keepdims=True))
            dlf_r = through + jnp.sum(jnp.where(row >= col, hmat + from_prev_c, to_next_c), axis=0, keepdims=True)
            dlf_c = jnp.sum(jnp.where(eye, dlf_r, 0.0), axis=1, keepdims=True)
            dfg_c = dlf_c * _sigmoid(-fg_c)
            dgates = dgates + jnp.where(lane_i == IG_LANE + h, dli_c, 0.0) + jnp.where(
                lane_i == FG_LANE + h, dfg_c, 0.0)
            qi = q.astype(F32) * (inter * QK_SCALE)
            dc_scr[h] = decay * dc_new + _dot_tn(qi.astype(BF), dnm_b)
            dn_scr[h, 0:1, :] = decay * dn_new + jnp.sum(qi * dden, axis=0, keepdims=True)
            dp_ref[:, h * DK:(h + 1) * DK] = dq.astype(BF)
            dp_ref[:, 1024 + h * DK:1024 + (h + 1) * DK] = dk.astype(BF)
            dp_ref[:, 2048 + h * DV:2048 + (h + 1) * DV] = dv.astype(BF)
            dp_ref[:, 4096 + h * DV:4096 + (h + 1) * DV] = d_o.astype(BF)
            dp_ref[:, 6144 + h * DV:6144 + (h + 1) * DV] = d_z.astype(BF)
        dg_ref[...] = dgates
        dbias_ref[0:1, :] += jnp.sum(dgates, axis=0, keepdims=True)

        @pl.when(step == nc - 1)
        def _():
            dwb_ref[...] = dwb_scr[...].astype(BF)

    def rev(c):
        return nc - 1 - c

    return pl.pallas_call(
        body, name="mlstm_bwd", grid=(nc,),
        input_output_aliases={0: 0},
        in_specs=[pl.BlockSpec(memory_space=pl.ANY),
                  pl.BlockSpec((lc, 1024), lambda c: (rev(c), 0)),
                  pl.BlockSpec((lc, 1024), lambda c: (rev(c), 1)),
                  pl.BlockSpec((lc, 2048), lambda c: (rev(c), 1)),
                  pl.BlockSpec((lc, 2048), lambda c: (rev(c), 2)),
                  pl.BlockSpec((lc, 2048), lambda c: (rev(c), 3)),
                  pl.BlockSpec((lc, 128), lambda c: (rev(c), 0)),
                  pl.BlockSpec((1, 128), lambda c: (0, 0)),
                  pl.BlockSpec((1, V_DIM), lambda c: (0, 0)),
                  pl.BlockSpec((lc, V_DIM), lambda c: (rev(c), 0)),
                  pl.BlockSpec((1, N_HEADS, DK, DV), lambda c: (rev(c), 0, 0, 0)),
                  pl.BlockSpec((1, N_HEADS, 8, DK), lambda c: (rev(c), 0, 0, 0)),
                  pl.BlockSpec((lc, V_DIM), lambda c: (rev(c), 0)),
                  pl.BlockSpec((lc, D_MODEL), lambda c: (rev(c), 0)),
                  pl.BlockSpec((V_DIM, D_MODEL), lambda c: (0, 0))],
        out_specs=[pl.BlockSpec((lc, 8192), lambda c: (rev(c), 0)),
                   pl.BlockSpec((lc, 128), lambda c: (rev(c), 0)),
                   pl.BlockSpec((8, 128), lambda c: (0, 0)),
                   pl.BlockSpec((8, V_DIM), lambda c: (0, 0)),
                   pl.BlockSpec((V_DIM, D_MODEL), lambda c: (0, 0))],
        out_shape=[jax.ShapeDtypeStruct((t, N_MAIN), BF), jax.ShapeDtypeStruct((t, 128), F32),
                   jax.ShapeDtypeStruct((8, 128), F32), jax.ShapeDtypeStruct((8, V_DIM), F32),
                   jax.ShapeDtypeStruct((V_DIM, D_MODEL), BF)],
        scratch_shapes=[pltpu.VMEM((N_HEADS, DK, DV), F32), pltpu.VMEM((N_HEADS, 8, DK), F32),
                        pltpu.VMEM((lc, V_DIM), F32), pltpu.VMEM((V_DIM, D_MODEL), F32)],
        compiler_params=_cparams(("arbitrary",)),
    )(dproj, proj, proj, proj, proj, proj, gates, gate_bias, g_head, h_raw, c_states, aux, yb_pre, dyb, w_b)


GA_BLK, GB_BLK = 12, 13


def _full(shape):
    return pl.BlockSpec(shape, lambda i: (0,) * len(shape))


def _rms_backward(xv, g, dhn, dres):
    r = lax.rsqrt(jnp.mean(xv * xv, axis=1, keepdims=True) + EPS)
    xh = xv * r
    dxh = dhn * g
    dx = dres + r * (dxh - xh * jnp.mean(dxh * xh, axis=1, keepdims=True))
    return dx, jnp.sum(dhn * xh, axis=0, keepdims=True)


def _token_chain(ya_pre, w_a, yb, proj, w_o, x, g_ple, w_pg, p, w_pl, target, g_final, *, tm=256):
    t, d = x.shape
    tm = min(tm, t)

    def body(yap_ref, wa_ref, yb_ref, ga_ref, gb_ref, wo_ref, x_ref, gp_ref, wpg_ref, p_ref, wpl_ref, tg_ref,
             gf_ref, dx_ref, dyb_ref, dyap_ref, o_ref, sm_ref, dwa_ref, dwo_ref, dwpg_ref, dwpl_ref,
             dwa_acc, dwo_acc, dwpg_acc, dwpl_acc):
        i = pl.program_id(0)

        @pl.when(i == 0)
        def _():
            sm_ref[...] = jnp.zeros_like(sm_ref)
            dwa_acc[...] = jnp.zeros_like(dwa_acc)
            dwo_acc[...] = jnp.zeros_like(dwo_acc)
            dwpg_acc[...] = jnp.zeros_like(dwpg_acc)
            dwpl_acc[...] = jnp.zeros_like(dwpl_acc)

        yap = yap_ref[...]
        ya = _dot(yap, wa_ref[...]).astype(BF).astype(F32)
        yb_v = yb_ref[...].astype(F32)
        sa = _sigmoid(ga_ref[...].astype(F32))
        sb = _sigmoid(gb_ref[...].astype(F32))
        merged = (sa * ya + sb * yb_v).astype(BF)
        x1 = _dot(merged, wo_ref[...]) + x_ref[...]
        r1 = lax.rsqrt(jnp.mean(x1 * x1, axis=1, keepdims=True) + EPS)
        xh1 = x1 * r1
        gp = gp_ref[...]
        hn2 = xh1 * gp
        hn2_b = hn2.astype(BF)
        gate = _sigmoid(_dot(hn2_b, wpg_ref[...]))
        p_b = p_ref[...].astype(BF)
        pe_v = _dot(p_b, wpl_ref[...])
        x2 = x1 + gate * pe_v
        r2 = lax.rsqrt(jnp.mean(x2 * x2, axis=1, keepdims=True) + EPS)
        xh2 = x2 * r2
        gf = gf_ref[...]
        err = xh2 * gf - tg_ref[...]
        dy = err * (1.0 / d)
        dxh2 = dy * gf
        dx2 = r2 * (dxh2 - xh2 * jnp.mean(dxh2 * xh2, axis=1, keepdims=True))
        dgpre = (dx2 * pe_v * gate * (1.0 - gate)).astype(BF)
        dwpg_acc[...] += _dot_tn(hn2_b, dgpre)
        dwpl_acc[...] += _dot_tn(p_b, (dx2 * gate).astype(BF))
        dhn2 = _dot_nt(dgpre, wpg_ref[...])
        dxh1 = dhn2 * gp
        dx1 = dx2 + r1 * (dxh1 - xh1 * jnp.mean(dxh1 * xh1, axis=1, keepdims=True))
        dx_ref[...] = dx1
        dx1_b = dx1.astype(BF)
        dwo_acc[...] += _dot_tn(merged, dx1_b)
        dm = _dot_nt(dx1_b, wo_ref[...])
        dya = (dm * sa).astype(BF)
        dwa_acc[...] += _dot_tn(yap, dya)
        dyb_ref[...] = (dm * sb).astype(BF)
        o_ref[:, 0:d] = (dm * ya * sa * (1.0 - sa)).astype(BF)
        o_ref[:, d:2 * d] = (dm * yb_v * sb * (1.0 - sb)).astype(BF)
        dyap_ref[...] = _dot_nt(dya, wa_ref[...]).astype(BF)

        sm_ref[0:1, :] += (0.5 / d) * jnp.sum(err * err, axis=0, keepdims=True)
        sm_ref[1:2, :] += jnp.sum(dy * xh2, axis=0, keepdims=True)
        sm_ref[2:3, :] += jnp.sum(dhn2 * xh1, axis=0, keepdims=True)

        @pl.when(i == t // tm - 1)
        def _():
            dwa_ref[...] = dwa_acc[...].astype(BF)
            dwo_ref[...] = dwo_acc[...].astype(BF)
            dwpg_ref[...] = dwpg_acc[...].astype(BF)
            dwpl_ref[...] = dwpl_acc[...].astype(BF)

    row = pl.BlockSpec((tm, d), lambda i: (i, 0))
    bf = jax.ShapeDtypeStruct((t, d), BF)
    return pl.pallas_call(
        body, name="token_chain", grid=(t // tm,),
        in_specs=[row, _full((d, d)), row, pl.BlockSpec((tm, d), lambda i: (i, GA_BLK)),
                  pl.BlockSpec((tm, d), lambda i: (i, GB_BLK)), _full((d, d)), row, _full((1, d)), _full((d, d)),
                  pl.BlockSpec((tm, PLE_DIM), lambda i: (i, 0)), _full((PLE_DIM, d)), row, _full((1, d))],
        out_specs=[row, row, row, pl.BlockSpec((tm, 2 * d), lambda i: (i, 6)), _full((8, d)),
                   _full((d, d)), _full((d, d)), _full((d, d)), _full((PLE_DIM, d))],
        out_shape=[jax.ShapeDtypeStruct((t, d), F32), bf, bf,
                   jax.ShapeDtypeStruct((t, N_MAIN), BF), jax.ShapeDtypeStruct((8, d), F32),
                   jax.ShapeDtypeStruct((d, d), BF), jax.ShapeDtypeStruct((d, d), BF),
                   jax.ShapeDtypeStruct((d, d), BF), jax.ShapeDtypeStruct((PLE_DIM, d), BF)],
        scratch_shapes=[pltpu.VMEM((d, d), F32), pltpu.VMEM((d, d), F32), pltpu.VMEM((d, d), F32),
                        pltpu.VMEM((PLE_DIM, d), F32)],
        compiler_params=_cparams(("arbitrary",)),
    )(ya_pre, w_a, yb, proj, proj, w_o, x, g_ple, w_pg, p, w_pl, target, g_final)


def _dhn_mix_bwd(dproj, w_main, dgates, w_gate, x, g_mix, dx1, comm, *, tm=1024, tk=2048):
    t, d = x.shape
    tm = min(tm, t)
    nk = N_MAIN // tk
    ni = t // tm
    n_in = 7 + comm.n

    def body(*refs):
        dp_ref, w_ref, dgt_ref, wg_ref, x_ref, g_ref, dres_ref = refs[:7]
        dx_ref, dg_ref = refs[n_in], refs[n_in + 1]
        acc = refs[-1]
        start, wait = comm.ops(refs[7:n_in], refs[n_in + 2:n_in + 2 + comm.n], *refs[n_in + 2 + comm.n:-1])
        i, k = pl.program_id(0), pl.program_id(1)
        pl.when((i == 0) & (k == 0))(start)

        @pl.when(k == 0)
        def _():
            acc[...] = _dot_nt(dp_ref[...], w_ref[...]) + _dot_nt(dgt_ref[...], wg_ref[...])

        @pl.when(k > 0)
        def _():
            acc[...] += _dot_nt(dp_ref[...], w_ref[...])

        @pl.when((i == 0) & (k == 0))
        def _():
            dg_ref[...] = jnp.zeros_like(dg_ref)

        @pl.when(k == nk - 1)
        def _():
            dx, dg = _rms_backward(x_ref[...], g_ref[...], acc[...], dres_ref[...])
            dx_ref[...] = dx
            dg_ref[0:1, :] += dg

        pl.when((i == ni - 1) & (k == nk - 1))(wait)

    any_spec = pl.BlockSpec(memory_space=pl.ANY)
    row = pl.BlockSpec((tm, d), lambda i, k: (i, 0))
    res = pl.pallas_call(
        body, name="dhn_mix_bwd", grid=(ni, nk),
        in_specs=[pl.BlockSpec((tm, tk), lambda i, k: (i, k)), pl.BlockSpec((d, tk), lambda i, k: (0, k)),
                  pl.BlockSpec((tm, 128), lambda i, k: (i, 0)), pl.BlockSpec((d, 128), lambda i, k: (0, 0)),
                  row, pl.BlockSpec((1, d), lambda i, k: (0, 0)), row] + [any_spec] * comm.n,
        out_specs=[row, pl.BlockSpec((8, d), lambda i, k: (0, 0))] + [any_spec] * comm.n,
        out_shape=[jax.ShapeDtypeStruct((t, d), F32), jax.ShapeDtypeStruct((8, d), F32)] + comm.out_shapes,
        scratch_shapes=comm.scratch + [pltpu.VMEM((tm, d), F32)],
        compiler_params=_cparams(("arbitrary", "arbitrary")),
    )(dproj, w_main, dgates, w_gate, x, g_mix, dx1, *comm.srcs)
    return res[0], res[1], list(res[2:])


def _position():
    x, y, c = lax.axis_index("x"), lax.axis_index("y"), lax.axis_index("c")
    return x, y, c


def _rms_fwd_all_gather(x, g, srcs, *, tm=512):
    t, d = x.shape
    tm = min(tm, t)
    nt = t // tm
    nb = len(srcs)
    any_spec = pl.BlockSpec(memory_space=pl.ANY)

    def body(*refs):
        x_ref, g_ref = refs[:2]
        src = refs[2:2 + nb]
        hn_ref, hnt_ref = refs[2 + nb:4 + nb]
        dst = refs[4 + nb:4 + 2 * nb]
        send_sems, recv_sems, local_sems = refs[4 + 2 * nb:]
        i = pl.program_id(0)

        def parts():
            px, py, pc = _position()
            me, sibling = (px, py, pc), (px, py, 1 - pc)
            chips = [(1 - px, py), (px, 1 - py), (1 - px, 1 - py)]

            def slot(b, qx, qy, qc):
                return dst[b].at[4 * qx + 2 * qy + qc]

            def copy(k, b, block, to, from_src=False):
                return pltpu.make_async_remote_copy(
                    src_ref=src[b] if from_src else slot(b, *block), dst_ref=slot(b, *block),
                    send_sem=send_sems.at[b, k], recv_sem=recv_sems.at[b, k],
                    device_id=to, device_id_type=MESH)

            mine = [pltpu.make_async_copy(src[b], slot(b, *me), local_sems.at[b]) for b in range(nb)]
            first = [copy(0, b, me, sibling, True) for b in range(nb)]
            first += [copy(1 + j, b, me, (*chip, pc), True) for j, chip in enumerate(chips) for b in range(nb)]
            return me, sibling, chips, pc, copy, mine, first

        @pl.when(i == 0)
        def _():
            _, _, _, _, _, mine, first = parts()
            for cp in mine + first:
                cp.start()

        xv = x_ref[...]
        r = lax.rsqrt(jnp.mean(xv * xv, axis=1, keepdims=True) + EPS)
        hn = xv * r * g_ref[...]
        hn_ref[...] = hn.astype(BF)
        hnt_ref[...] = hn.T.astype(BF)

        @pl.when(i == nt - 1)
        def _():
            me, sibling, chips, pc, copy, mine, first = parts()
            passed = []
            for j, chip in enumerate(chips):
                for b in range(nb):
                    copy(1 + j, b, (*chip, pc), me).wait_recv()
                    fwd = copy(4 + j, b, (*chip, pc), sibling)
                    fwd.start()
                    passed.append(fwd)
            for b in range(nb):
                copy(0, b, sibling, me).wait_recv()
            for j, chip in enumerate(chips):
                for b in range(nb):
                    copy(4 + j, b, (*chip, 1 - pc), me).wait_recv()
            for cp in first + passed:
                cp.wait_send()
            for cp in mine:
                cp.wait()

    res = pl.pallas_call(
        body, name="rms_mix_all_gather", grid=(nt,),
        in_specs=[pl.BlockSpec((tm, d), lambda i: (i, 0)), pl.BlockSpec((1, d), lambda i: (0, 0))]
        + [any_spec] * nb,
        out_specs=[pl.BlockSpec((tm, d), lambda i: (i, 0)), pl.BlockSpec((d, tm), lambda i: (0, i))]
        + [any_spec] * nb,
        out_shape=[jax.ShapeDtypeStruct((t, d), BF), jax.ShapeDtypeStruct((d, t), BF)]
        + [jax.ShapeDtypeStruct((N_DEV,) + s.shape, s.dtype) for s in srcs],
        scratch_shapes=[pltpu.SemaphoreType.DMA((nb, 7)), pltpu.SemaphoreType.DMA((nb, 7)),
                        pltpu.SemaphoreType.DMA((nb,))],
        compiler_params=_cparams(("arbitrary",)),
    )(x, g, *srcs)
    return res[0], res[1], list(res[2:])


def _comm_call(comm, *, name):
    any_spec = pl.BlockSpec(memory_space=pl.ANY)

    def body(*refs):
        start, wait = comm.ops(refs[:comm.n], refs[comm.n:2 * comm.n], *refs[2 * comm.n:])
        start()
        wait()

    return pl.pallas_call(
        body, name=name, in_specs=[any_spec] * comm.n, out_specs=[any_spec] * comm.n,
        out_shape=comm.out_shapes, scratch_shapes=comm.scratch,
    )(*comm.srcs)


LAYOUT_ROWS = 128
N_GLOB_BLK = 113
N_TAIL_BLK = 17


def _lane(tr):
    return lax.broadcasted_iota(jnp.int32, (tr, 128), 1)


def _assemble_w(g_win):
    d = g_win.shape[1]
    tr = LAYOUT_ROWS

    def body(win_ref, wm_ref, wg_ref):
        lane = _lane(tr)

        def shifted(k, j):
            cur = win_ref[k, :, 128 * j:128 * (j + 1)].astype(F32)
            if k == 0:
                return cur
            cur = pltpu.roll(cur, k, 1)
            if j == 0:
                return jnp.where(lane < k, 0.0, cur)
            prev = win_ref[k, :, 128 * (j - 1):128 * j].astype(F32)
            return jnp.where(lane < k, pltpu.roll(prev, k, 1), cur)

        def glob(gb):
            k = min(gb // 14, N_DEV - 1)
            j = gb - 14 * k
            v = shifted(k, j)
            if j == 0 and k > 0:
                v = v + shifted(k - 1, 14)
            return v

        for c in range(64):
            wm_ref[:, 128 * c:128 * (c + 1)] = glob(32 + c).astype(BF)
        for c in range(32):
            wm_ref[:, 8192 + 128 * c:8192 + 128 * (c + 1)] = glob(c).astype(BF)
        tail = [glob(96 + j) for j in range(N_TAIL_BLK)]
        for j in range(N_TAIL_BLK):
            v = jnp.where(lane < 120, pltpu.roll(tail[j], 120, 1),
                          pltpu.roll(tail[(j + 1) % N_TAIL_BLK], 120, 1)).astype(BF)
            if j < 16:
                wm_ref[:, 12288 + 128 * j:12288 + 128 * (j + 1)] = v
            else:
                wg_ref[...] = v

    return pl.pallas_call(
        body, name="assemble_w", grid=(d // tr,),
        in_specs=[pl.BlockSpec((N_DEV, tr, WIN_W), lambda i: (0, i, 0))],
        out_specs=[pl.BlockSpec((tr, N_MAIN), lambda i: (i, 0)), pl.BlockSpec((tr, 128), lambda i: (i, 0))],
        out_shape=[jax.ShapeDtypeStruct((d, N_MAIN), BF), jax.ShapeDtypeStruct((d, 128), BF)],
        compiler_params=_cparams(("parallel",)),
    )(g_win)


def _pack_windows(dw_main, dw_gate):
    d = dw_main.shape[0]
    tr = LAYOUT_ROWS

    def body(dm_ref, dg_ref, o_ref):
        lane = _lane(tr)

        def main_tail(j):
            return (dm_ref[:, 12288 + 128 * j:12288 + 128 * (j + 1)] if j < 16 else dg_ref[...]).astype(F32)

        def glob(gb):
            if gb >= N_GLOB_BLK:
                return jnp.zeros((tr, 128), F32)
            if gb < 32:
                return dm_ref[:, 8192 + 128 * gb:8192 + 128 * (gb + 1)].astype(F32)
            if gb < 96:
                return dm_ref[:, 128 * (gb - 32):128 * (gb - 31)].astype(F32)
            j = gb - 96
            return jnp.where(lane < 8, pltpu.roll(main_tail((j - 1) % N_TAIL_BLK), 8, 1),
                             pltpu.roll(main_tail(j), 8, 1))

        for j in range(N_DEV):
            cur = glob(14 * j)
            for m in range(WIN_W // 128):
                nxt = glob(14 * j + m + 1)
                if j == 0:
                    v = cur
                else:
                    v = jnp.where(lane < 128 - j, pltpu.roll(cur, 128 - j, 1), pltpu.roll(nxt, 128 - j, 1))
                o_ref[j, :, 128 * m:128 * (m + 1)] = v.astype(BF)
                cur = nxt

    return pl.pallas_call(
        body, name="pack_windows", grid=(d // tr,),
        in_specs=[pl.BlockSpec((tr, N_MAIN), lambda i: (i, 0)), pl.BlockSpec((tr, 128), lambda i: (i, 0))],
        out_specs=pl.BlockSpec((N_DEV, tr, WIN_W), lambda i: (0, i, 0)),
        out_shape=jax.ShapeDtypeStruct((N_DEV, d, WIN_W), BF),
        compiler_params=_cparams(("parallel",)),
    )(dw_main, dw_gate)


def _sum_slots(recv, *, name, tr):
    _, r, cdim = recv.shape
    tr = min(tr, r)

    def body(r_ref, o_ref):
        total = r_ref[0].astype(F32)
        for s in range(1, N_DEV):
            total = total + r_ref[s].astype(F32)
        o_ref[...] = total

    return pl.pallas_call(
        body, name=name, grid=(r // tr,),
        in_specs=[pl.BlockSpec((N_DEV, tr, cdim), lambda i: (0, i, 0))],
        out_specs=pl.BlockSpec((tr, cdim), lambda i: (i, 0)),
        out_shape=jax.ShapeDtypeStruct((r, cdim), F32),
        compiler_params=_cparams(("parallel",)),
    )(recv)


def _adamw(w, g, m, v, *, name):
    lead = w.ndim - 2
    r, cdim = w.shape[-2:]
    if r % 128 == 0:
        tr, tc = 128, cdim
    elif r >= 128 and cdim % 128 == 0:
        tr, tc = r, 128
    else:
        tr, tc = r, cdim
    c1 = 1.0 - ADAM_B1 ** ADAM_STEP
    c2 = 1.0 - ADAM_B2 ** ADAM_STEP

    def body(w_ref, g_ref, m_ref, v_ref, d_ref, mo_ref, vo_ref):
        gv = g_ref[...]
        mn = ADAM_B1 * m_ref[...] + (1.0 - ADAM_B1) * gv
        vn = ADAM_B2 * v_ref[...] + (1.0 - ADAM_B2) * (gv * gv)
        d_ref[...] = -ADAM_LR * ((mn / c1) / (jnp.sqrt(vn / c2) + ADAM_EPS) + ADAM_WD * w_ref[...])
        mo_ref[...] = mn
        vo_ref[...] = vn

    blk = pl.BlockSpec((1,) * lead + (tr, tc), lambda i, j: (0,) * lead + (i, j))
    shp = jax.ShapeDtypeStruct(w.shape, F32)
    return pl.pallas_call(
        body, name=name, grid=(r // tr, cdim // tc),
        in_specs=[blk] * 4, out_specs=[blk] * 3, out_shape=[shp] * 3,
        compiler_params=_cparams(("parallel", "parallel")),
    )(w, g, m, v)


def kernel(x, p, g_mix, w_in, conv_w, conv_b, w_a_out, b_gates, g_head, w_b_out, w_o, g_ple, w_ple_gate, w_ple, g_final, loss_target, m_g_mix, m_w_in, m_conv_w, m_conv_b, m_w_a_out, m_b_gates, m_g_head, m_w_b_out, m_w_o, m_g_ple, m_w_ple_gate, m_w_ple, m_g_final, v_g_mix, v_w_in, v_conv_w, v_conv_b, v_w_a_out, v_b_gates, v_g_head, v_w_b_out, v_w_o, v_g_ple, v_w_ple_gate, v_w_ple, v_g_final):
    d = D_MODEL
    t = x.shape[1]
    x2d = x.reshape(t, d)
    p2d = p.reshape(t, PLE_DIM)
    tgt = loss_target.reshape(t, d)

    win = jnp.pad(w_in[0].astype(BF), ((0, 0), (0, WIN_W - SHARD_W)))
    rows = jnp.concatenate([w_a_out[0].astype(BF), w_b_out[0].astype(BF), w_o[0].astype(BF),
                            w_ple_gate[0].astype(BF), w_ple[0].astype(BF).reshape(32, d)], axis=0)
    cfl = jnp.pad(conv_w[0], ((0, 5), (0, 0)))
    hn, hnt, (g_win, g_cf) = _rms_fwd_all_gather(x2d, g_mix, [win, cfl])

    w_main, w_gate = _assemble_w(g_win)
    conv_w8 = jnp.pad(g_cf[:, :3, :].transpose(1, 0, 2).reshape(3, d), ((0, 5), (0, 0)))
    gate_bias = jnp.pad(b_gates, ((0, 0), (IG_LANE, 0)))

    proj, gates, (g_rows,) = _mm(hn, w_main, form="nn", out_dtype=BF, name="proj", tm=2048,
                                 extra=(w_gate, F32), comm=_DirectComm([rows], "gather"))
    w_a = g_rows[:, 0:128].reshape(d, d)
    w_b = g_rows[:, 128:384].reshape(V_DIM, d)
    w_of = g_rows[:, 384:512].reshape(d, d)
    w_pg = g_rows[:, 512:640].reshape(d, d)
    w_pl = g_rows[:, 640:672].reshape(N_DEV, PLE_DIM, 128).transpose(1, 0, 2).reshape(PLE_DIM, d)
    ya_pre = _branch_a_fwd(proj, conv_w8, conv_b)
    yb_pre, yb, h_raw, c_states, aux = _mlstm_fwd(proj, gates, gate_bias, g_head, w_b)
    dx1, dyb, dya_pre, dproj, small_fin, dw_a, dw_o, dw_pg, dw_pl = _token_chain(
        ya_pre, w_a, yb, proj, w_of, x2d, g_ple, w_pg, p2d, w_pl, tgt, g_final.reshape(1, d))

    dproj, dconv = _branch_a_bwd(dproj, proj, dya_pre, conv_w8, conv_b)
    dproj, dgates, dbias, dg_head, dw_b = _mlstm_bwd(dproj, proj, gates, gate_bias, g_head, h_raw, c_states, aux,
                                                     yb_pre, dyb, w_b)
    s_rows = jnp.concatenate([
        dw_a.reshape(N_DEV, 128, d), dw_b.reshape(N_DEV, 256, d), dw_o.reshape(N_DEV, 128, d),
        dw_pg.reshape(N_DEV, 128, d),
        dw_pl.reshape(PLE_DIM, N_DEV, 128).transpose(1, 0, 2).reshape(N_DEV, 32, d)], axis=1)
    dgates_b = dgates.astype(BF)
    dw_main, dw_gate, (r_rows,) = _mm(hnt, dproj, form="nn", out_dtype=BF, name="dw_main", tk=2048,
                                      extra=(dgates_b, BF), comm=_DirectComm([s_rows], "exchange"))
    s_win = _pack_windows(dw_main, dw_gate)
    grad_x, dg_mix, (r_win,) = _dhn_mix_bwd(dproj, w_main, dgates_b, w_gate, x2d, g_mix, dx1,
                                            _DirectComm([s_win], "exchange"))

    vec = jnp.concatenate([dg_mix[0], dconv[3], dg_head[0], small_fin[2], small_fin[1],
                           dbias[0, IG_LANE:], jnp.zeros((7 * d - 6152,), F32)]).reshape(7, d)
    conv_part = jnp.pad(dconv[:3].reshape(3, N_DEV, 128).transpose(1, 0, 2).reshape(N_DEV, 1, 384),
                        ((0, 0), (0, 0), (0, d - 384)))
    s_f32 = jnp.concatenate([jnp.broadcast_to(vec[None], (N_DEV, 7, d)), conv_part], axis=1)
    (r_f32,) = _comm_call(_DirectComm([s_f32], "exchange"), name="small_grads_exchange")
    sum_win = _sum_slots(r_win, name="sum_win", tr=128)
    sum_rows = _sum_slots(r_rows, name="sum_rows", tr=96)
    sum_f32 = _sum_slots(r_f32, name="sum_f32", tr=8)

    g_w_in = sum_win[:, :SHARD_W]
    g_w_a = sum_rows[0:128]
    g_w_b = sum_rows[128:384]
    g_w_o = sum_rows[384:512]
    g_w_pg = sum_rows[512:640]
    g_w_pl = sum_rows[640:672].reshape(PLE_DIM, 128)
    vsum = sum_f32[:7].reshape(7 * d)
    g_g_mix = vsum[0:1024].reshape(1, d)
    g_conv_b = vsum[1024:2048].reshape(1, d)
    g_g_head = vsum[2048:4096].reshape(1, V_DIM)
    g_g_ple = vsum[4096:5120].reshape(1, d)
    g_g_final = vsum[5120:6144].reshape(1, d)
    g_b_gates = vsum[6144:6152].reshape(1, 8)
    g_conv_w = sum_f32[7, :384].reshape(3, 128)

    loss = lax.psum(jnp.sum(small_fin[0]), ("x", "y", "c"))

    names = ["g_mix", "w_in", "conv_w", "conv_b", "w_a_out", "b_gates", "g_head", "w_b_out", "w_o", "g_ple",
             "w_ple_gate", "w_ple", "g_final"]
    weights = [g_mix, w_in, conv_w, conv_b, w_a_out, b_gates, g_head, w_b_out, w_o, g_ple, w_ple_gate, w_ple,
               g_final]
    moms = [m_g_mix, m_w_in, m_conv_w, m_conv_b, m_w_a_out, m_b_gates, m_g_head, m_w_b_out, m_w_o, m_g_ple,
            m_w_ple_gate, m_w_ple, m_g_final]
    vels = [v_g_mix, v_w_in, v_conv_w, v_conv_b, v_w_a_out, v_b_gates, v_g_head, v_w_b_out, v_w_o, v_g_ple,
            v_w_ple_gate, v_w_ple, v_g_final]
    grads2d = [g_g_mix, g_w_in, g_conv_w, g_conv_b, g_w_a, g_b_gates, g_g_head, g_w_b, g_w_o, g_g_ple, g_w_pg,
               g_w_pl, g_g_final]
    grads, deltas, new_m, new_v = [], [], [], []
    for nm, w, m_, v_, g2 in zip(names, weights, moms, vels, grads2d):
        shp = w.shape
        if nm == "w_in":
            dl, mn, vn = _adamw(w[0].T, g2.T, m_[0].T, v_[0].T, name="adamw_" + nm)
            grads.append(g2.reshape(shp))
            deltas.append(dl.T.reshape(shp))
            new_m.append(mn.T.reshape(shp))
            new_v.append(vn.T.reshape(shp))
            continue
        kshp = shp if w.ndim >= 2 else (1,) + shp
        gk = g2.reshape(kshp)
        dl, mn, vn = _adamw(w.reshape(kshp), gk, m_.reshape(kshp), v_.reshape(kshp), name="adamw_" + nm)
        grads.append(gk.reshape(shp))
        deltas.append(dl.reshape(shp))
        new_m.append(mn.reshape(shp))
        new_v.append(vn.reshape(shp))
    return (loss, grad_x.reshape(x.shape), *grads, *deltas, *new_m, *new_v)
```

```python
import functools

import jax
import jax.numpy as jnp
from jax import lax
from jax.experimental import pallas as pl
from jax.experimental.pallas import tpu as pltpu

F32 = jnp.float32
BF = jnp.bfloat16

D_MODEL = 1024
N_HEADS = 4
DK = 256
DV = 512
V_DIM = 2048
PLE_DIM = 256
N_IN = 14344
N_MAIN = 14336
EPS = 1e-6
QK_SCALE = DK ** -0.5
NEG = -1e30
N_DEV = 8
SHARD_W = 1793
WIN_STRIDE = 1792
WIN_W = 1920
ROWS_PACK = 672
_CHUNK = 256
IG_LANE = 120
FG_LANE = 124

ADAM_LR = 0.001
ADAM_B1 = 0.9
ADAM_B2 = 0.999
ADAM_EPS = 1e-08
ADAM_WD = 0.01
ADAM_STEP = 10

VMEM_LIMIT = 56 * 1024 * 1024
MESH = pl.DeviceIdType.MESH


def _cparams(sem):
    return pltpu.CompilerParams(dimension_semantics=sem, vmem_limit_bytes=VMEM_LIMIT)


def _sigmoid(x):
    return 1.0 / (1.0 + jnp.exp(-x))


def _log_sigmoid(x):
    return jnp.minimum(x, 0.0) - jnp.log(1.0 + jnp.exp(-jnp.abs(x)))


def _dot(a, b):
    return jnp.dot(a, b, preferred_element_type=F32)


def _dot_nt(a, b):
    return lax.dot_general(a, b, (((1,), (1,)), ((), ())), preferred_element_type=F32)


def _dot_tn(a, b):
    return lax.dot_general(a, b, (((0,), (0,)), ((), ())), preferred_element_type=F32)


class _DirectComm:
    def __init__(self, srcs, kind):
        self.srcs = list(srcs)
        self.kind = kind
        self.n = len(self.srcs)
        if kind == "exchange":
            self.out_shapes = [jax.ShapeDtypeStruct(s.shape, s.dtype) for s in self.srcs]
        else:
            self.out_shapes = [jax.ShapeDtypeStruct((N_DEV,) + s.shape, s.dtype) for s in self.srcs]
        self.scratch = [pltpu.SemaphoreType.DMA((self.n, 7)), pltpu.SemaphoreType.DMA((self.n, 7)),
                        pltpu.SemaphoreType.DMA((self.n,))]

    def ops(self, src, dst, send_sems, recv_sems, local_sems):
        exchange = self.kind == "exchange"

        def descriptors():
            x, y, c = _position()
            me_lin = 4 * x + 2 * y + c
            local = [pltpu.make_async_copy(src[b].at[me_lin] if exchange else src[b], dst[b].at[me_lin],
                                           local_sems.at[b]) for b in range(self.n)]
            sends, recvs = [], []
            for f in range(1, N_DEV):
                px = (1 - x) if (f >> 2) & 1 else x
                py = (1 - y) if (f >> 1) & 1 else y
                pc = (1 - c) if f & 1 else c
                peer_lin = 4 * px + 2 * py + pc
                for b in range(self.n):
                    out = src[b].at[peer_lin] if exchange else src[b]
                    common = dict(send_sem=send_sems.at[b, f - 1], recv_sem=recv_sems.at[b, f - 1],
                                  device_id=(px, py, pc), device_id_type=MESH)
                    sends.append(pltpu.make_async_remote_copy(src_ref=out, dst_ref=dst[b].at[me_lin], **common))
                    recvs.append(pltpu.make_async_remote_copy(src_ref=out, dst_ref=dst[b].at[peer_lin], **common))
            return local, sends, recvs

        def start():
            local, sends, _ = descriptors()
            for cp in local + sends:
                cp.start()

        def wait():
            local, sends, recvs = descriptors()
            for cp in recvs:
                cp.wait_recv()
            for cp in sends:
                cp.wait_send()
            for cp in local:
                cp.wait()

        return start, wait


def _mm(a, b, *, form, out_dtype, name, tm=1024, tn=1024, tk=1024, add=None, extra=None, comm=None):
    assert extra is None or form == "nn"
    if form == "nn":
        m, kc = a.shape
        n = b.shape[1]
    elif form == "nt":
        m, kc = a.shape
        n = b.shape[0]
    else:
        kc, m = a.shape
        n = b.shape[1]
    tm, tn, tk = min(tm, m), min(tn, n), min(tk, kc)
    assert m % tm == 0 and n % tn == 0 and kc % tk == 0, (name, a.shape, b.shape)
    nk = kc // tk
    if form == "tn":
        a_spec = pl.BlockSpec((tk, tm), lambda i, j, k: (k, i))
    else:
        a_spec = pl.BlockSpec((tm, tk), lambda i, j, k: (i, k))
    if form == "nt":
        b_spec = pl.BlockSpec((tn, tk), lambda i, j, k: (j, k))
    else:
        b_spec = pl.BlockSpec((tk, tn), lambda i, j, k: (k, j))
    o_spec = pl.BlockSpec((tm, tn), lambda i, j, k: (i, j))
    dot = {"nn": _dot, "nt": _dot_nt, "tn": _dot_tn}[form]
    has_add = add is not None

    use_acc = nk > 1 and (has_add or out_dtype != F32)
    has_x = extra is not None
    n2 = extra[0].shape[1] if has_x else 0
    use_acc2 = has_x and nk > 1 and extra[1] != F32
    n_comm = comm.n if comm else 0
    n_in = 2 + int(has_add) + int(has_x) + n_comm
    n_out = 1 + int(has_x) + n_comm
    grid = (m // tm, n // tn, nk)

    def body(*refs):
        a_ref, b_ref = refs[0], refs[1]
        add_ref = refs[2] if has_add else None
        b2_ref = refs[2 + int(has_add)] if has_x else None
        o_ref = refs[n_in]
        o2_ref = refs[n_in + 1] if has_x else None
        scr = list(refs[n_in + n_out + (3 if comm else 0):])
        acc = scr.pop(0) if use_acc else o_ref
        acc2 = scr.pop(0) if use_acc2 else o2_ref
        i, j, k = pl.program_id(0), pl.program_id(1), pl.program_id(2)
        if comm:
            start, wait = comm.ops(refs[n_in - n_comm:n_in], refs[n_in + n_out - n_comm:n_in + n_out],
                                   *refs[n_in + n_out:n_in + n_out + 3])
            pl.when((i == 0) & (j == 0) & (k == 0))(start)

        def part():
            return dot(a_ref[...].astype(BF), b_ref[...].astype(BF))

        def part2():
            return dot(a_ref[...].astype(BF), b2_ref[...].astype(BF))

        def finish(total):
            if has_add:
                total = total + add_ref[...].astype(F32)
            o_ref[...] = total.astype(out_dtype)

        if nk == 1:
            finish(part())
            if has_x:
                @pl.when(j == 0)
                def _():
                    o2_ref[...] = part2().astype(extra[1])
        else:
            @pl.when(k == 0)
            def _():
                acc[...] = part()

            @pl.when(k > 0)
            def _():
                acc[...] += part()

            if use_acc:
                @pl.when(k == nk - 1)
                def _():
                    finish(acc[...])
            if has_x:
                @pl.when((j == 0) & (k == 0))
                def _():
                    acc2[...] = part2()

                @pl.when((j == 0) & (k > 0))
                def _():
                    acc2[...] += part2()

                if use_acc2:
                    @pl.when((j == 0) & (k == nk - 1))
                    def _():
                        o2_ref[...] = acc2[...].astype(extra[1])
        if comm:
            pl.when((i == grid[0] - 1) & (j == grid[1] - 1) & (k == nk - 1))(wait)

    any_spec = pl.BlockSpec(memory_space=pl.ANY)
    o2_spec = pl.BlockSpec((tm, n2), lambda i, j, k: (i, 0))
    in_specs = ([a_spec, b_spec] + ([o_spec] if has_add else [])
                + ([pl.BlockSpec((tk, n2), lambda i, j, k: (k, 0))] if has_x else []) + [any_spec] * n_comm)
    args = (a, b) + ((add,) if has_add else ()) + ((extra[0],) if has_x else ()) + (tuple(comm.srcs) if comm else ())
    out_specs = [o_spec] + ([o2_spec] if has_x else []) + [any_spec] * n_comm
    out_shape = ([jax.ShapeDtypeStruct((m, n), out_dtype)]
                 + ([jax.ShapeDtypeStruct((m, n2), extra[1])] if has_x else []) + (comm.out_shapes if comm else []))
    scratch = ((comm.scratch if comm else []) + ([pltpu.VMEM((tm, tn), F32)] if use_acc else [])
               + ([pltpu.VMEM((tm, n2), F32)] if use_acc2 else []))
    if comm:
        sem = ("arbitrary",) * 3
    else:
        sem = ("parallel", "arbitrary" if has_x else "parallel", "arbitrary")
    res = pl.pallas_call(
        body, name=name, grid=grid, in_specs=in_specs, out_specs=out_specs, out_shape=out_shape,
        scratch_shapes=scratch, compiler_params=_cparams(sem),
    )(*args)
    if not (comm or has_x):
        return res[0]
    return tuple(res[:1 + int(has_x)]) + ((list(res[1 + int(has_x):]),) if comm else ())


HALO = 16
XA_BLK, BA_BLK, CA_BLK, ZA_BLK = 8, 9, 10, 11


def _shift_down(u, prev, n):
    return pltpu.roll(jnp.concatenate([prev, u], axis=0), n, 0)[HALO:]


def _shift_up(u, nxt, n):
    tm = u.shape[0]
    return pltpu.roll(jnp.concatenate([u, nxt], axis=0), tm + HALO - n, 0)[:tm]


def _branch_a_fwd(proj, conv_w8, conv_b, *, tm=512):
    t = proj.shape[0]
    d = D_MODEL
    tm = min(tm, t)
    hb = tm // HALO

    def body(xa_ref, ba_ref, ca_ref, za_ref, xap_ref, cap_ref, w_ref, b_ref, o_ref):
        i = pl.program_id(0)
        u = ca_ref[...].astype(F32) * xa_ref[...].astype(F32)
        up = cap_ref[...].astype(F32) * xap_ref[...].astype(F32)
        up = jnp.where(i == 0, 0.0, up)
        u1 = _shift_down(u, up, 1)
        u2 = _shift_down(u, up, 2)
        cv = w_ref[0:1, :] * u2 + w_ref[1:2, :] * u1 + w_ref[2:3, :] * u + b_ref[...]
        za = za_ref[...].astype(F32)
        o_ref[...] = (ba_ref[...].astype(F32) * cv * (za * _sigmoid(za))).astype(BF)

    def col(blk):
        return pl.BlockSpec((tm, d), lambda i: (i, blk))

    def prev(blk):
        return pl.BlockSpec((HALO, d), lambda i: (jnp.maximum(i * hb - 1, 0), blk))

    return pl.pallas_call(
        body, name="branch_a_fwd", grid=(t // tm,),
        in_specs=[col(XA_BLK), col(BA_BLK), col(CA_BLK), col(ZA_BLK), prev(XA_BLK), prev(CA_BLK),
                  pl.BlockSpec((8, d), lambda i: (0, 0)), pl.BlockSpec((1, d), lambda i: (0, 0))],
        out_specs=pl.BlockSpec((tm, d), lambda i: (i, 0)),
        out_shape=jax.ShapeDtypeStruct((t, d), BF),
        compiler_params=_cparams(("parallel",)),
    )(proj, proj, proj, proj, proj, proj, conv_w8, conv_b)


def _branch_a_bwd(dproj, proj, dya_pre, conv_w8, conv_b, *, tm=512):
    t = proj.shape[0]
    d = D_MODEL
    tm = min(tm, t)
    hb = tm // HALO
    nt = t // tm

    def body(dp_ref, xa_ref, ba_ref, ca_ref, za_ref, xap_ref, cap_ref, dy_ref, ban_ref, zan_ref, dyn_ref,
             w_ref, b_ref, o_ref, dc_ref):
        del dp_ref
        i = pl.program_id(0)
        xa = xa_ref[...].astype(F32)
        ca = ca_ref[...].astype(F32)
        ba = ba_ref[...].astype(F32)
        za = za_ref[...].astype(F32)
        u = ca * xa
        up = cap_ref[...].astype(F32) * xap_ref[...].astype(F32)
        up = jnp.where(i == 0, 0.0, up)
        u1 = _shift_down(u, up, 1)
        u2 = _shift_down(u, up, 2)
        w0, w1, w2 = w_ref[0:1, :], w_ref[1:2, :], w_ref[2:3, :]
        cv = w0 * u2 + w1 * u1 + w2 * u + b_ref[...]
        sg = _sigmoid(za)
        sz = za * sg
        dy = dy_ref[...].astype(F32)
        dcv = dy * ba * sz
        zan = zan_ref[...].astype(F32)
        dcvn = dyn_ref[...].astype(F32) * ban_ref[...].astype(F32) * (zan * _sigmoid(zan))
        dcvn = jnp.where(i == nt - 1, 0.0, dcvn)
        du = w2 * dcv + w1 * _shift_up(dcv, dcvn, 1) + w0 * _shift_up(dcv, dcvn, 2)
        o_ref[:, 0:d] = (du * ca).astype(BF)
        o_ref[:, d:2 * d] = (dy * cv * sz).astype(BF)
        o_ref[:, 2 * d:3 * d] = (du * xa).astype(BF)
        o_ref[:, 3 * d:4 * d] = (dy * ba * cv * sg * (1.0 + za * (1.0 - sg))).astype(BF)

        @pl.when(i == 0)
        def _():
            dc_ref[...] = jnp.zeros_like(dc_ref)

        dc_ref[0:1, :] += jnp.sum(dcv * u2, axis=0, keepdims=True)
        dc_ref[1:2, :] += jnp.sum(dcv * u1, axis=0, keepdims=True)
        dc_ref[2:3, :] += jnp.sum(dcv * u, axis=0, keepdims=True)
        dc_ref[3:4, :] += jnp.sum(dcv, axis=0, keepdims=True)

    def col(blk):
        return pl.BlockSpec((tm, d), lambda i: (i, blk))

    def prev(blk):
        return pl.BlockSpec((HALO, d), lambda i: (jnp.maximum(i * hb - 1, 0), blk))

    def nxt(blk):
        return pl.BlockSpec((HALO, d), lambda i: (jnp.minimum((i + 1) * hb, t // HALO - 1), blk))

    return pl.pallas_call(
        body, name="branch_a_bwd", grid=(nt,),
        in_specs=[pl.BlockSpec(memory_space=pl.ANY),
                  col(XA_BLK), col(BA_BLK), col(CA_BLK), col(ZA_BLK), prev(XA_BLK), prev(CA_BLK),
                  pl.BlockSpec((tm, d), lambda i: (i, 0)), nxt(BA_BLK), nxt(ZA_BLK),
                  pl.BlockSpec((HALO, d), lambda i: (jnp.minimum((i + 1) * hb, t // HALO - 1), 0)),
                  pl.BlockSpec((8, d), lambda i: (0, 0)), pl.BlockSpec((1, d), lambda i: (0, 0))],
        out_specs=[pl.BlockSpec((tm, 4 * d), lambda i: (i, 2)), pl.BlockSpec((8, d), lambda i: (0, 0))],
        out_shape=[jax.ShapeDtypeStruct(dproj.shape, BF), jax.ShapeDtypeStruct((8, d), F32)],
        input_output_aliases={0: 0},
        compiler_params=_cparams(("arbitrary",)),
    )(dproj, proj, proj, proj, proj, proj, proj, dya_pre, proj, proj, dya_pre, conv_w8, conv_b)


def _gate_vectors(gc, gt, h, lane_i, sub_i):
    ig_c = jnp.sum(jnp.where(lane_i == IG_LANE + h, gc, 0.0), axis=1, keepdims=True)
    fg_c = jnp.sum(jnp.where(lane_i == FG_LANE + h, gc, 0.0), axis=1, keepdims=True)
    ig_r = jnp.sum(jnp.where(sub_i == IG_LANE + h, gt, 0.0), axis=0, keepdims=True)
    fg_r = jnp.sum(jnp.where(sub_i == FG_LANE + h, gt, 0.0), axis=0, keepdims=True)
    return ig_c, fg_c, ig_r, fg_r


def _chunk_common(q, k, ig_c, fg_c, ig_r, fg_r, m_prev, n_prev, row, col):
    lf_c = _log_sigmoid(fg_c)
    lf_r = _log_sigmoid(fg_r)
    causal = col <= row
    b_c = jnp.sum(jnp.where(causal, lf_r, 0.0), axis=1, keepdims=True)
    b_r = jnp.sum(jnp.where(row <= col, lf_c, 0.0), axis=0, keepdims=True)
    dmat = jnp.where(causal, b_c - b_r + ig_r, NEG)
    a = b_c + m_prev
    m_row = jnp.maximum(a, jnp.max(dmat, axis=1, keepdims=True))
    est = jnp.exp(dmat - m_row)
    s = _dot_nt(q, k) * QK_SCALE * est
    inter = jnp.exp(a - m_row)
    den = jnp.sum(s, axis=1, keepdims=True) + inter * QK_SCALE * jnp.sum(
        q.astype(F32) * n_prev, axis=1, keepdims=True)
    expm = jnp.exp(-m_row)
    mx = jnp.maximum(jnp.abs(den), expm)
    b_last = jnp.sum(lf_r, axis=1, keepdims=True)
    g_r = b_last - b_r + ig_r
    g_c = b_last - b_c + ig_c
    m_new = jnp.maximum(b_last + m_prev, jnp.max(g_r, axis=1, keepdims=True))
    w_c = jnp.exp(g_c - m_new)
    decay = jnp.exp(b_last + m_prev - m_new)
    return est, s, inter, den, expm, mx, m_new, w_c, decay


def _mlstm_fwd(proj, gates, gate_bias, g_head, w_b):
    t = proj.shape[0]
    lc = min(_CHUNK, t)
    nc = t // lc

    def body(q_ref, k_ref, v_ref, o_ref, z_ref, g_ref, gb_ref, gh_ref, wb_ref,
             yb_ref, ybo_ref, hr_ref, cs_ref, aux_ref, c_scr, nm_scr):
        c = pl.program_id(0)

        @pl.when(c == 0)
        def _():
            c_scr[...] = jnp.zeros_like(c_scr)
            nm_scr[...] = jnp.zeros_like(nm_scr)
            nm_scr[:, 1:2, :] = jnp.full((N_HEADS, 1, DK), NEG, F32)

        row = lax.broadcasted_iota(jnp.int32, (lc, lc), 0)
        col = lax.broadcasted_iota(jnp.int32, (lc, lc), 1)
        gc = g_ref[...] + gb_ref[...]
        gt = gc.T
        lane_i = lax.broadcasted_iota(jnp.int32, gc.shape, 1)
        sub_i = lax.broadcasted_iota(jnp.int32, gt.shape, 0)
        for h in range(N_HEADS):
            ks = slice(h * DK, (h + 1) * DK)
            vs = slice(h * DV, (h + 1) * DV)
            q = q_ref[:, ks]
            k = k_ref[:, ks]
            v = v_ref[:, vs]
            ig_c, fg_c, ig_r, fg_r = _gate_vectors(gc, gt, h, lane_i, sub_i)
            n_prev = nm_scr[h, 0:1, :]
            m_prev = nm_scr[h, 1:2, 0:1]
            est, s, inter, den, expm, mx, m_new, w_c, decay = _chunk_common(
                q, k, ig_c, fg_c, ig_r, fg_r, m_prev, n_prev, row, col)
            c_prev = c_scr[h]
            c_prev_b = c_prev.astype(BF)
            num = _dot(s.astype(BF), v) + (inter * QK_SCALE) * _dot(q, c_prev_b)
            hh = num / mx
            r = lax.rsqrt(jnp.mean(hh * hh, axis=1, keepdims=True) + EPS)
            hbn = hh * r * gh_ref[:, vs]
            o = o_ref[:, vs].astype(F32)
            z = z_ref[:, vs].astype(F32)
            yb_ref[:, vs] = (_sigmoid(o) * hbn * (z * _sigmoid(z))).astype(BF)
            hr_ref[:, vs] = hh.astype(BF)
            cs_ref[0, h] = c_prev_b
            aux_ref[0, h] = nm_scr[h]
            kw = k.astype(F32) * w_c
            c_scr[h] = decay * c_prev + _dot_tn(kw.astype(BF), v)
            nm_scr[h, 0:1, :] = decay * n_prev + jnp.sum(kw, axis=0, keepdims=True)
            nm_scr[h, 1:2, :] = jnp.broadcast_to(m_new, (1, DK))
        ybo_ref[...] = _dot(yb_ref[...], wb_ref[...]).astype(BF)

    return pl.pallas_call(
        body, name="mlstm_fwd", grid=(nc,),
        in_specs=[pl.BlockSpec((lc, 1024), lambda c: (c, 0)),
                  pl.BlockSpec((lc, 1024), lambda c: (c, 1)),
                  pl.BlockSpec((lc, 2048), lambda c: (c, 1)),
                  pl.BlockSpec((lc, 2048), lambda c: (c, 2)),
                  pl.BlockSpec((lc, 2048), lambda c: (c, 3)),
                  pl.BlockSpec((lc, 128), lambda c: (c, 0)),
                  pl.BlockSpec((1, 128), lambda c: (0, 0)),
                  pl.BlockSpec((1, V_DIM), lambda c: (0, 0)),
                  pl.BlockSpec((V_DIM, D_MODEL), lambda c: (0, 0))],
        out_specs=[pl.BlockSpec((lc, V_DIM), lambda c: (c, 0)),
                   pl.BlockSpec((lc, D_MODEL), lambda c: (c, 0)),
                   pl.BlockSpec((lc, V_DIM), lambda c: (c, 0)),
                   pl.BlockSpec((1, N_HEADS, DK, DV), lambda c: (c, 0, 0, 0)),
                   pl.BlockSpec((1, N_HEADS, 8, DK), lambda c: (c, 0, 0, 0))],
        out_shape=[jax.ShapeDtypeStruct((t, V_DIM), BF), jax.ShapeDtypeStruct((t, D_MODEL), BF),
                   jax.ShapeDtypeStruct((t, V_DIM), BF),
                   jax.ShapeDtypeStruct((nc, N_HEADS, DK, DV), BF),
                   jax.ShapeDtypeStruct((nc, N_HEADS, 8, DK), F32)],
        scratch_shapes=[pltpu.VMEM((N_HEADS, DK, DV), F32), pltpu.VMEM((N_HEADS, 8, DK), F32)],
        compiler_params=_cparams(("arbitrary",)),
    )(proj, proj, proj, proj, proj, gates, gate_bias, g_head, w_b)


def _mlstm_bwd(dproj, proj, gates, gate_bias, g_head, h_raw, c_states, aux, yb_pre, dyb, w_b):
    t = proj.shape[0]
    lc = min(_CHUNK, t)
    nc = t // lc

    def body(dpin_ref, q_ref, k_ref, v_ref, o_ref, z_ref, g_ref, gb_ref, gh_ref, hr_ref, cs_ref, aux_ref,
             ybp_ref, dyb_ref, wb_ref, dp_ref, dg_ref, dbias_ref, dgh_ref, dwb_ref, dc_scr, dn_scr, dy_ref,
             dwb_scr):
        del dpin_ref
        step = pl.program_id(0)

        @pl.when(step == 0)
        def _():
            dc_scr[...] = jnp.zeros_like(dc_scr)
            dn_scr[...] = jnp.zeros_like(dn_scr)
            dbias_ref[...] = jnp.zeros_like(dbias_ref)
            dgh_ref[...] = jnp.zeros_like(dgh_ref)
            dwb_scr[...] = jnp.zeros_like(dwb_scr)

        dyb = dyb_ref[...]
        dy_ref[...] = _dot_nt(dyb, wb_ref[...])
        dwb_scr[...] += _dot_tn(ybp_ref[...], dyb)

        row = lax.broadcasted_iota(jnp.int32, (lc, lc), 0)
        col = lax.broadcasted_iota(jnp.int32, (lc, lc), 1)
        eye = row == col
        gc = g_ref[...] + gb_ref[...]
        gt = gc.T
        lane_i = lax.broadcasted_iota(jnp.int32, gc.shape, 1)
        sub_i = lax.broadcasted_iota(jnp.int32, gt.shape, 0)
        dgates = jnp.zeros(gc.shape, F32)
        for h in range(N_HEADS):
            ks = slice(h * DK, (h + 1) * DK)
            vs = slice(h * DV, (h + 1) * DV)
            q = q_ref[:, ks]
            k = k_ref[:, ks]
            v = v_ref[:, vs]
            ig_c, fg_c, ig_r, fg_r = _gate_vectors(gc, gt, h, lane_i, sub_i)
            n_prev = aux_ref[0, h, 0:1, :]
            m_prev = aux_ref[0, h, 1:2, 0:1]
            c_prev_b = cs_ref[0, h]
            est, s, inter, den, expm, mx, m_new, w_c, decay = _chunk_common(
                q, k, ig_c, fg_c, ig_r, fg_r, m_prev, n_prev, row, col)
            hb = hr_ref[:, vs].astype(F32)
            dyp = dy_ref[:, vs]
            o = o_ref[:, vs].astype(F32)
            z = z_ref[:, vs].astype(F32)
            so = _sigmoid(o)
            sgz = _sigmoid(z)
            sz = z * sgz
            r = lax.rsqrt(jnp.mean(hb * hb, axis=1, keepdims=True) + EPS)
            xh = hb * r
            gh = gh_ref[:, vs]
            hbn = xh * gh
            dyso = dyp * so
            dyso_h = dyso * hbn
            d_o = dyso_h * sz * (1.0 - so)
            d_z = dyso_h * sgz * (1.0 + z * (1.0 - sgz))
            dhbn = dyso * sz
            dgh_ref[0:1, vs] += jnp.sum(dhbn * xh, axis=0, keepdims=True)
            dxh = dhbn * gh
            dh = r * (dxh - xh * jnp.mean(dxh * xh, axis=1, keepdims=True))
            dnm = dh / mx
            hd = jnp.sum(dh * hb, axis=1, keepdims=True)
            cond = jnp.abs(den) > expm
            dden = jnp.where(cond, -hd / mx * jnp.sign(den), 0.0)
            dnm_b = dnm.astype(BF)
            p = _dot_nt(dnm_b, v) + dden
            dqk = (p * est * QK_SCALE).astype(BF)
            dq_inter = (inter * QK_SCALE) * (_dot_nt(dnm_b, c_prev_b) + dden * n_prev)
            dq = _dot(dqk, k) + dq_inter
            dc_new = dc_scr[h]
            dc_new_b = dc_new.astype(BF)
            dn_new = dn_scr[h, 0:1, :]
            kf = k.astype(F32)
            dk_state = w_c * (_dot_nt(v, dc_new_b) + dn_new)
            dk = _dot_tn(dqk, q) + dk_state
            dv = _dot_tn(s.astype(BF), dnm_b) + w_c * _dot(k, dc_new_b)
            dv1_r = jnp.sum(s * dden, axis=0, keepdims=True)
            dv1_c = (jnp.sum(jnp.where(eye, dv1_r, 0.0), axis=1, keepdims=True)
                     + w_c * jnp.sum(kf * dn_new, axis=1, keepdims=True))
            dli_c = jnp.sum(v.astype(F32) * dv, axis=1, keepdims=True) + dv1_c
            hmat = _dot((p * s).astype(BF), (row < col).astype(BF))
            from_prev_c = jnp.sum(q.astype(F32) * dq_inter, axis=1, keepdims=True)
            to_next_c = jnp.sum(kf * dk_state, axis=1, keepdims=True)
            through = decay * (
                jnp.sum(jnp.sum(dc_new * c_prev_b.astype(F32), axis=1, keepdims=True), axis=0, keepdims=True)
                + jnp.sum(dn_new * n_prev, axis=1, keepdims=True))
            dlf_r = through + jnp.sum(jnp.where(row >= col, hmat + from_prev_c, to_next_c), axis=0, keepdims=True)
            dlf_c = jnp.sum(jnp.where(eye, dlf_r, 0.0), axis=1, keepdims=True)
            dfg_c = dlf_c * _sigmoid(-fg_c)
            dgates = dgates + jnp.where(lane_i == IG_LANE + h, dli_c, 0.0) + jnp.where(
                lane_i == FG_LANE + h, dfg_c, 0.0)
            qi = q.astype(F32) * (inter * QK_SCALE)
            dc_scr[h] = decay * dc_new + _dot_tn(qi.astype(BF), dnm_b)
            dn_scr[h, 0:1, :] = decay * dn_new + jnp.sum(qi * dden, axis=0, keepdims=True)
            dp_ref[:, h * DK:(h + 1) * DK] = dq.astype(BF)
            dp_ref[:, 1024 + h * DK:1024 + (h + 1) * DK] = dk.astype(BF)
            dp_ref[:, 2048 + h * DV:2048 + (h + 1) * DV] = dv.astype(BF)
            dp_ref[:, 4096 + h * DV:4096 + (h + 1) * DV] = d_o.astype(BF)
            dp_ref[:, 6144 + h * DV:6144 + (h + 1) * DV] = d_z.astype(BF)
        dg_ref[...] = dgates.astype(BF)
        dbias_ref[0:1, :] += jnp.sum(dgates, axis=0, keepdims=True)

        @pl.when(step == nc - 1)
        def _():
            dwb_ref[...] = dwb_scr[...].astype(BF)

    def rev(c):
        return nc - 1 - c

    return pl.pallas_call(
        body, name="mlstm_bwd", grid=(nc,),
        input_output_aliases={0: 0},
        in_specs=[pl.BlockSpec(memory_space=pl.ANY),
                  pl.BlockSpec((lc, 1024), lambda c: (rev(c), 0)),
                  pl.BlockSpec((lc, 1024), lambda c: (rev(c), 1)),
                  pl.BlockSpec((lc, 2048), lambda c: (rev(c), 1)),
                  pl.BlockSpec((lc, 2048), lambda c: (rev(c), 2)),
                  pl.BlockSpec((lc, 2048), lambda c: (rev(c), 3)),
                  pl.BlockSpec((lc, 128), lambda c: (rev(c), 0)),
                  pl.BlockSpec((1, 128), lambda c: (0, 0)),
                  pl.BlockSpec((1, V_DIM), lambda c: (0, 0)),
                  pl.BlockSpec((lc, V_DIM), lambda c: (rev(c), 0)),
                  pl.BlockSpec((1, N_HEADS, DK, DV), lambda c: (rev(c), 0, 0, 0)),
                  pl.BlockSpec((1, N_HEADS, 8, DK), lambda c: (rev(c), 0, 0, 0)),
                  pl.BlockSpec((lc, V_DIM), lambda c: (rev(c), 0)),
                  pl.BlockSpec((lc, D_MODEL), lambda c: (rev(c), 0)),
                  pl.BlockSpec((V_DIM, D_MODEL), lambda c: (0, 0))],
        out_specs=[pl.BlockSpec((lc, 8192), lambda c: (rev(c), 0)),
                   pl.BlockSpec((lc, 128), lambda c: (rev(c), 0)),
                   pl.BlockSpec((8, 128), lambda c: (0, 0)),
                   pl.BlockSpec((8, V_DIM), lambda c: (0, 0)),
                   pl.BlockSpec((V_DIM, D_MODEL), lambda c: (0, 0))],
        out_shape=[jax.ShapeDtypeStruct((t, N_MAIN), BF), jax.ShapeDtypeStruct((t, 128), BF),
                   jax.ShapeDtypeStruct((8, 128), F32), jax.ShapeDtypeStruct((8, V_DIM), F32),
                   jax.ShapeDtypeStruct((V_DIM, D_MODEL), BF)],
        scratch_shapes=[pltpu.VMEM((N_HEADS, DK, DV), F32), pltpu.VMEM((N_HEADS, 8, DK), F32),
                        pltpu.VMEM((lc, V_DIM), F32), pltpu.VMEM((V_DIM, D_MODEL), F32)],
        compiler_params=_cparams(("arbitrary",)),
    )(dproj, proj, proj, proj, proj, proj, gates, gate_bias, g_head, h_raw, c_states, aux, yb_pre, dyb, w_b)


GA_BLK, GB_BLK = 12, 13


def _full(shape):
    return pl.BlockSpec(shape, lambda i: (0,) * len(shape))


def _rms_backward(xv, g, dhn, dres):
    r = lax.rsqrt(jnp.mean(xv * xv, axis=1, keepdims=True) + EPS)
    xh = xv * r
    dxh = dhn * g
    dx = dres + r * (dxh - xh * jnp.mean(dxh * xh, axis=1, keepdims=True))
    return dx, jnp.sum(dhn * xh, axis=0, keepdims=True)


def _token_chain(ya_pre, w_a, yb, proj, w_o, x, g_ple, w_pg, p, w_pl, target, g_final, *, tm=256):
    t, d = x.shape
    tm = min(tm, t)

    def body(yap_ref, wa_ref, yb_ref, ga_ref, gb_ref, wo_ref, x_ref, gp_ref, wpg_ref, p_ref, wpl_ref, tg_ref,
             gf_ref, dx_ref, dyb_ref, dyap_ref, o_ref, sm_ref, dwa_ref, dwo_ref, dwpg_ref, dwpl_ref,
             dwa_acc, dwo_acc, dwpg_acc, dwpl_acc):
        i = pl.program_id(0)

        @pl.when(i == 0)
        def _():
            sm_ref[...] = jnp.zeros_like(sm_ref)
            dwa_acc[...] = jnp.zeros_like(dwa_acc)
            dwo_acc[...] = jnp.zeros_like(dwo_acc)
            dwpg_acc[...] = jnp.zeros_like(dwpg_acc)
            dwpl_acc[...] = jnp.zeros_like(dwpl_acc)

        yap = yap_ref[...]
        ya = _dot(yap, wa_ref[...]).astype(BF).astype(F32)
        yb_v = yb_ref[...].astype(F32)
        sa = _sigmoid(ga_ref[...].astype(F32))
        sb = _sigmoid(gb_ref[...].astype(F32))
        merged = (sa * ya + sb * yb_v).astype(BF)
        x1 = _dot(merged, wo_ref[...]) + x_ref[...]
        r1 = lax.rsqrt(jnp.mean(x1 * x1, axis=1, keepdims=True) + EPS)
        xh1 = x1 * r1
        gp = gp_ref[...]
        hn2 = xh1 * gp
        hn2_b = hn2.astype(BF)
        gate = _sigmoid(_dot(hn2_b, wpg_ref[...]))
        p_b = p_ref[...].astype(BF)
        pe_v = _dot(p_b, wpl_ref[...])
        x2 = x1 + gate * pe_v
        r2 = lax.rsqrt(jnp.mean(x2 * x2, axis=1, keepdims=True) + EPS)
        xh2 = x2 * r2
        gf = gf_ref[...]
        err = xh2 * gf - tg_ref[...]
        dy = err * (1.0 / d)
        dxh2 = dy * gf
        dx2 = r2 * (dxh2 - xh2 * jnp.mean(dxh2 * xh2, axis=1, keepdims=True))
        dgpre = (dx2 * pe_v * gate * (1.0 - gate)).astype(BF)
        dwpg_acc[...] += _dot_tn(hn2_b, dgpre)
        dwpl_acc[...] += _dot_tn(p_b, (dx2 * gate).astype(BF))
        dhn2 = _dot_nt(dgpre, wpg_ref[...])
        dxh1 = dhn2 * gp
        dx1 = dx2 + r1 * (dxh1 - xh1 * jnp.mean(dxh1 * xh1, axis=1, keepdims=True))
        dx_ref[...] = dx1
        dx1_b = dx1.astype(BF)
        dwo_acc[...] += _dot_tn(merged, dx1_b)
        dm = _dot_nt(dx1_b, wo_ref[...])
        dya = (dm * sa).astype(BF)
        dwa_acc[...] += _dot_tn(yap, dya)
        dyb_ref[...] = (dm * sb).astype(BF)
        o_ref[:, 0:d] = (dm * ya * sa * (1.0 - sa)).astype(BF)
        o_ref[:, d:2 * d] = (dm * yb_v * sb * (1.0 - sb)).astype(BF)
        dyap_ref[...] = _dot_nt(dya, wa_ref[...]).astype(BF)

        sm_ref[0:1, :] += (0.5 / d) * jnp.sum(err * err, axis=0, keepdims=True)
        sm_ref[1:2, :] += jnp.sum(dy * xh2, axis=0, keepdims=True)
        sm_ref[2:3, :] += jnp.sum(dhn2 * xh1, axis=0, keepdims=True)

        @pl.when(i == t // tm - 1)
        def _():
            dwa_ref[...] = dwa_acc[...].astype(BF)
            dwo_ref[...] = dwo_acc[...].astype(BF)
            dwpg_ref[...] = dwpg_acc[...].astype(BF)
            dwpl_ref[...] = dwpl_acc[...].astype(BF)

    row = pl.BlockSpec((tm, d), lambda i: (i, 0))
    bf = jax.ShapeDtypeStruct((t, d), BF)
    return pl.pallas_call(
        body, name="token_chain", grid=(t // tm,),
        in_specs=[row, _full((d, d)), row, pl.BlockSpec((tm, d), lambda i: (i, GA_BLK)),
                  pl.BlockSpec((tm, d), lambda i: (i, GB_BLK)), _full((d, d)), row, _full((1, d)), _full((d, d)),
                  pl.BlockSpec((tm, PLE_DIM), lambda i: (i, 0)), _full((PLE_DIM, d)), row, _full((1, d))],
        out_specs=[row, row, row, pl.BlockSpec((tm, 2 * d), lambda i: (i, 6)), _full((8, d)),
                   _full((d, d)), _full((d, d)), _full((d, d)), _full((PLE_DIM, d))],
        out_shape=[jax.ShapeDtypeStruct((t, d), F32), bf, bf,
                   jax.ShapeDtypeStruct((t, N_MAIN), BF), jax.ShapeDtypeStruct((8, d), F32),
                   jax.ShapeDtypeStruct((d, d), BF), jax.ShapeDtypeStruct((d, d), BF),
                   jax.ShapeDtypeStruct((d, d), BF), jax.ShapeDtypeStruct((PLE_DIM, d), BF)],
        scratch_shapes=[pltpu.VMEM((d, d), F32), pltpu.VMEM((d, d), F32), pltpu.VMEM((d, d), F32),
                        pltpu.VMEM((PLE_DIM, d), F32)],
        compiler_params=_cparams(("arbitrary",)),
    )(ya_pre, w_a, yb, proj, proj, w_o, x, g_ple, w_pg, p, w_pl, target, g_final)


def _dhn_mix_bwd(dproj, w_main, dgates, w_gate, x, g_mix, dx1, comm, *, tm=1024, tk=2048):
    t, d = x.shape
    tm = min(tm, t)
    nk = N_MAIN // tk
    ni = t // tm
    n_in = 7 + comm.n

    def body(*refs):
        dp_ref, w_ref, dgt_ref, wg_ref, x_ref, g_ref, dres_ref = refs[:7]
        dx_ref, dg_ref = refs[n_in], refs[n_in + 1]
        acc = refs[-1]
        start, wait = comm.ops(refs[7:n_in], refs[n_in + 2:n_in + 2 + comm.n], *refs[n_in + 2 + comm.n:-1])
        i, k = pl.program_id(0), pl.program_id(1)
        pl.when((i == 0) & (k == 0))(start)

        @pl.when(k == 0)
        def _():
            acc[...] = _dot_nt(dp_ref[...], w_ref[...]) + _dot_nt(dgt_ref[...], wg_ref[...])

        @pl.when(k > 0)
        def _():
            acc[...] += _dot_nt(dp_ref[...], w_ref[...])

        @pl.when((i == 0) & (k == 0))
        def _():
            dg_ref[...] = jnp.zeros_like(dg_ref)

        @pl.when(k == nk - 1)
        def _():
            dx, dg = _rms_backward(x_ref[...], g_ref[...], acc[...], dres_ref[...])
            dx_ref[...] = dx
            dg_ref[0:1, :] += dg

        pl.when((i == ni - 1) & (k == nk - 1))(wait)

    any_spec = pl.BlockSpec(memory_space=pl.ANY)
    row = pl.BlockSpec((tm, d), lambda i, k: (i, 0))
    res = pl.pallas_call(
        body, name="dhn_mix_bwd", grid=(ni, nk),
        in_specs=[pl.BlockSpec((tm, tk), lambda i, k: (i, k)), pl.BlockSpec((d, tk), lambda i, k: (0, k)),
                  pl.BlockSpec((tm, 128), lambda i, k: (i, 0)), pl.BlockSpec((d, 128), lambda i, k: (0, 0)),
                  row, pl.BlockSpec((1, d), lambda i, k: (0, 0)), row] + [any_spec] * comm.n,
        out_specs=[row, pl.BlockSpec((8, d), lambda i, k: (0, 0))] + [any_spec] * comm.n,
        out_shape=[jax.ShapeDtypeStruct((t, d), F32), jax.ShapeDtypeStruct((8, d), F32)] + comm.out_shapes,
        scratch_shapes=comm.scratch + [pltpu.VMEM((tm, d), F32)],
        compiler_params=_cparams(("arbitrary", "arbitrary")),
    )(dproj, w_main, dgates, w_gate, x, g_mix, dx1, *comm.srcs)
    return res[0], res[1], list(res[2:])


def _position():
    x, y, c = lax.axis_index("x"), lax.axis_index("y"), lax.axis_index("c")
    return x, y, c


def _rms_fwd_all_gather(x, g, srcs, *, tm=512):
    t, d = x.shape
    tm = min(tm, t)
    nt = t // tm
    nb = len(srcs)
    any_spec = pl.BlockSpec(memory_space=pl.ANY)

    def body(*refs):
        x_ref, g_ref = refs[:2]
        src = refs[2:2 + nb]
        hn_ref, hnt_ref = refs[2 + nb:4 + nb]
        dst = refs[4 + nb:4 + 2 * nb]
        send_sems, recv_sems, local_sems = refs[4 + 2 * nb:]
        i = pl.program_id(0)

        def parts():
            px, py, pc = _position()
            me, sibling = (px, py, pc), (px, py, 1 - pc)
            chips = [(1 - px, py), (px, 1 - py), (1 - px, 1 - py)]

            def slot(b, qx, qy, qc):
                return dst[b].at[4 * qx + 2 * qy + qc]

            def copy(k, b, block, to, from_src=False):
                return pltpu.make_async_remote_copy(
                    src_ref=src[b] if from_src else slot(b, *block), dst_ref=slot(b, *block),
                    send_sem=send_sems.at[b, k], recv_sem=recv_sems.at[b, k],
                    device_id=to, device_id_type=MESH)

            mine = [pltpu.make_async_copy(src[b], slot(b, *me), local_sems.at[b]) for b in range(nb)]
            first = [copy(0, b, me, sibling, True) for b in range(nb)]
            first += [copy(1 + j, b, me, (*chip, pc), True) for j, chip in enumerate(chips) for b in range(nb)]
            return me, sibling, chips, pc, copy, mine, first

        @pl.when(i == 0)
        def _():
            _, _, _, _, _, mine, first = parts()
            for cp in mine + first:
                cp.start()

        xv = x_ref[...]
        r = lax.rsqrt(jnp.mean(xv * xv, axis=1, keepdims=True) + EPS)
        hn = xv * r * g_ref[...]
        hn_ref[...] = hn.astype(BF)
        hnt_ref[...] = hn.T.astype(BF)

        @pl.when(i == nt - 1)
        def _():
            me, sibling, chips, pc, copy, mine, first = parts()
            passed = []
            for j, chip in enumerate(chips):
                for b in range(nb):
                    copy(1 + j, b, (*chip, pc), me).wait_recv()
                    fwd = copy(4 + j, b, (*chip, pc), sibling)
                    fwd.start()
                    passed.append(fwd)
            for b in range(nb):
                copy(0, b, sibling, me).wait_recv()
            for j, chip in enumerate(chips):
                for b in range(nb):
                    copy(4 + j, b, (*chip, 1 - pc), me).wait_recv()
            for cp in first + passed:
                cp.wait_send()
            for cp in mine:
                cp.wait()

    res = pl.pallas_call(
        body, name="rms_mix_all_gather", grid=(nt,),
        in_specs=[pl.BlockSpec((tm, d), lambda i: (i, 0)), pl.BlockSpec((1, d), lambda i: (0, 0))]
        + [any_spec] * nb,
        out_specs=[pl.BlockSpec((tm, d), lambda i: (i, 0)), pl.BlockSpec((d, tm), lambda i: (0, i))]
        + [any_spec] * nb,
        out_shape=[jax.ShapeDtypeStruct((t, d), BF), jax.ShapeDtypeStruct((d, t), BF)]
        + [jax.ShapeDtypeStruct((N_DEV,) + s.shape, s.dtype) for s in srcs],
        scratch_shapes=[pltpu.SemaphoreType.DMA((nb, 7)), pltpu.SemaphoreType.DMA((nb, 7)),
                        pltpu.SemaphoreType.DMA((nb,))],
        compiler_params=_cparams(("arbitrary",)),
    )(x, g, *srcs)
    return res[0], res[1], list(res[2:])


def _comm_call(comm, *, name):
    any_spec = pl.BlockSpec(memory_space=pl.ANY)

    def body(*refs):
        start, wait = comm.ops(refs[:comm.n], refs[comm.n:2 * comm.n], *refs[2 * comm.n:])
        start()
        wait()

    return pl.pallas_call(
        body, name=name, in_specs=[any_spec] * comm.n, out_specs=[any_spec] * comm.n,
        out_shape=comm.out_shapes, scratch_shapes=comm.scratch,
    )(*comm.srcs)


LAYOUT_ROWS = 128
N_GLOB_BLK = 113
N_TAIL_BLK = 17


def _lane(tr):
    return lax.broadcasted_iota(jnp.int32, (tr, 128), 1)


def _assemble_w(g_win):
    d = g_win.shape[1]
    tr = LAYOUT_ROWS

    def body(win_ref, wm_ref, wg_ref):
        lane = _lane(tr)

        def shifted(k, j):
            cur = win_ref[k, :, 128 * j:128 * (j + 1)].astype(F32)
            if k == 0:
                return cur
            cur = pltpu.roll(cur, k, 1)
            if j == 0:
                return jnp.where(lane < k, 0.0, cur)
            prev = win_ref[k, :, 128 * (j - 1):128 * j].astype(F32)
            return jnp.where(lane < k, pltpu.roll(prev, k, 1), cur)

        def glob(gb):
            k = min(gb // 14, N_DEV - 1)
            j = gb - 14 * k
            v = shifted(k, j)
            if j == 0 and k > 0:
                v = v + shifted(k - 1, 14)
            return v

        for c in range(64):
            wm_ref[:, 128 * c:128 * (c + 1)] = glob(32 + c).astype(BF)
        for c in range(32):
            wm_ref[:, 8192 + 128 * c:8192 + 128 * (c + 1)] = glob(c).astype(BF)
        tail = [glob(96 + j) for j in range(N_TAIL_BLK)]
        for j in range(N_TAIL_BLK):
            v = jnp.where(lane < 120, pltpu.roll(tail[j], 120, 1),
                          pltpu.roll(tail[(j + 1) % N_TAIL_BLK], 120, 1)).astype(BF)
            if j < 16:
                wm_ref[:, 12288 + 128 * j:12288 + 128 * (j + 1)] = v
            else:
                wg_ref[...] = v

    return pl.pallas_call(
        body, name="assemble_w", grid=(d // tr,),
        in_specs=[pl.BlockSpec((N_DEV, tr, WIN_W), lambda i: (0, i, 0))],
        out_specs=[pl.BlockSpec((tr, N_MAIN), lambda i: (i, 0)), pl.BlockSpec((tr, 128), lambda i: (i, 0))],
        out_shape=[jax.ShapeDtypeStruct((d, N_MAIN), BF), jax.ShapeDtypeStruct((d, 128), BF)],
        compiler_params=_cparams(("parallel",)),
    )(g_win)


def _pack_windows(dw_main, dw_gate):
    d = dw_main.shape[0]
    tr = LAYOUT_ROWS

    def body(dm_ref, dg_ref, o_ref):
        lane = _lane(tr)

        def main_tail(j):
            return (dm_ref[:, 12288 + 128 * j:12288 + 128 * (j + 1)] if j < 16 else dg_ref[...]).astype(F32)

        def glob(gb):
            if gb >= N_GLOB_BLK:
                return jnp.zeros((tr, 128), F32)
            if gb < 32:
                return dm_ref[:, 8192 + 128 * gb:8192 + 128 * (gb + 1)].astype(F32)
            if gb < 96:
                return dm_ref[:, 128 * (gb - 32):128 * (gb - 31)].astype(F32)
            j = gb - 96
            return jnp.where(lane < 8, pltpu.roll(main_tail((j - 1) % N_TAIL_BLK), 8, 1),
                             pltpu.roll(main_tail(j), 8, 1))

        for j in range(N_DEV):
            cur = glob(14 * j)
            for m in range(WIN_W // 128):
                nxt = glob(14 * j + m + 1)
                if j == 0:
                    v = cur
                else:
                    v = jnp.where(lane < 128 - j, pltpu.roll(cur, 128 - j, 1), pltpu.roll(nxt, 128 - j, 1))
                o_ref[j, :, 128 * m:128 * (m + 1)] = v.astype(BF)
                cur = nxt

    return pl.pallas_call(
        body, name="pack_windows", grid=(d // tr,),
        in_specs=[pl.BlockSpec((tr, N_MAIN), lambda i: (i, 0)), pl.BlockSpec((tr, 128), lambda i: (i, 0))],
        out_specs=pl.BlockSpec((N_DEV, tr, WIN_W), lambda i: (0, i, 0)),
        out_shape=jax.ShapeDtypeStruct((N_DEV, d, WIN_W), BF),
        compiler_params=_cparams(("parallel",)),
    )(dw_main, dw_gate)


def _sum_slots(recv, *, name, tr):
    _, r, cdim = recv.shape
    tr = min(tr, r)

    def body(r_ref, o_ref):
        total = r_ref[0].astype(F32)
        for s in range(1, N_DEV):
            total = total + r_ref[s].astype(F32)
        o_ref[...] = total

    return pl.pallas_call(
        body, name=name, grid=(r // tr,),
        in_specs=[pl.BlockSpec((N_DEV, tr, cdim), lambda i: (0, i, 0))],
        out_specs=pl.BlockSpec((tr, cdim), lambda i: (i, 0)),
        out_shape=jax.ShapeDtypeStruct((r, cdim), F32),
        compiler_params=_cparams(("parallel",)),
    )(recv)


def _adamw(w, g, m, v, *, name):
    lead = w.ndim - 2
    r, cdim = w.shape[-2:]
    if r % 128 == 0:
        tr, tc = 128, cdim
    elif r >= 128 and cdim % 128 == 0:
        tr, tc = r, 128
    else:
        tr, tc = r, cdim
    c1 = 1.0 - ADAM_B1 ** ADAM_STEP
    c2 = 1.0 - ADAM_B2 ** ADAM_STEP

    def body(w_ref, g_ref, m_ref, v_ref, d_ref, mo_ref, vo_ref):
        gv = g_ref[...]
        mn = ADAM_B1 * m_ref[...] + (1.0 - ADAM_B1) * gv
        vn = ADAM_B2 * v_ref[...] + (1.0 - ADAM_B2) * (gv * gv)
        d_ref[...] = -ADAM_LR * ((mn / c1) / (jnp.sqrt(vn / c2) + ADAM_EPS) + ADAM_WD * w_ref[...])
        mo_ref[...] = mn
        vo_ref[...] = vn

    blk = pl.BlockSpec((1,) * lead + (tr, tc), lambda i, j: (0,) * lead + (i, j))
    shp = jax.ShapeDtypeStruct(w.shape, F32)
    return pl.pallas_call(
        body, name=name, grid=(r // tr, cdim // tc),
        in_specs=[blk] * 4, out_specs=[blk] * 3, out_shape=[shp] * 3,
        compiler_params=_cparams(("parallel", "parallel")),
    )(w, g, m, v)


def kernel(x, p, g_mix, w_in, conv_w, conv_b, w_a_out, b_gates, g_head, w_b_out, w_o, g_ple, w_ple_gate, w_ple, g_final, loss_target, m_g_mix, m_w_in, m_conv_w, m_conv_b, m_w_a_out, m_b_gates, m_g_head, m_w_b_out, m_w_o, m_g_ple, m_w_ple_gate, m_w_ple, m_g_final, v_g_mix, v_w_in, v_conv_w, v_conv_b, v_w_a_out, v_b_gates, v_g_head, v_w_b_out, v_w_o, v_g_ple, v_w_ple_gate, v_w_ple, v_g_final):
    d = D_MODEL
    t = x.shape[1]
    x2d = x.reshape(t, d)
    p2d = p.reshape(t, PLE_DIM)
    tgt = loss_target.reshape(t, d)

    win = jnp.pad(w_in[0].astype(BF), ((0, 0), (0, WIN_W - SHARD_W)))
    rows = jnp.concatenate([w_a_out[0].astype(BF), w_b_out[0].astype(BF), w_o[0].astype(BF),
                            w_ple_gate[0].astype(BF), w_ple[0].astype(BF).reshape(32, d)], axis=0)
    cfl = jnp.pad(conv_w[0], ((0, 5), (0, 0)))
    hn, hnt, (g_win, g_cf) = _rms_fwd_all_gather(x2d, g_mix, [win, cfl])

    w_main, w_gate = _assemble_w(g_win)
    conv_w8 = jnp.pad(g_cf[:, :3, :].transpose(1, 0, 2).reshape(3, d), ((0, 5), (0, 0)))
    gate_bias = jnp.pad(b_gates, ((0, 0), (IG_LANE, 0)))

    proj, gates, (g_rows,) = _mm(hn, w_main, form="nn", out_dtype=BF, name="proj", tm=2048,
                                 extra=(w_gate, F32), comm=_DirectComm([rows], "gather"))
    w_a = g_rows[:, 0:128].reshape(d, d)
    w_b = g_rows[:, 128:384].reshape(V_DIM, d)
    w_of = g_rows[:, 384:512].reshape(d, d)
    w_pg = g_rows[:, 512:640].reshape(d, d)
    w_pl = g_rows[:, 640:672].reshape(N_DEV, PLE_DIM, 128).transpose(1, 0, 2).reshape(PLE_DIM, d)
    ya_pre = _branch_a_fwd(proj, conv_w8, conv_b)
    yb_pre, yb, h_raw, c_states, aux = _mlstm_fwd(proj, gates, gate_bias, g_head, w_b)
    dx1, dyb, dya_pre, dproj, small_fin, dw_a, dw_o, dw_pg, dw_pl = _token_chain(
        ya_pre, w_a, yb, proj, w_of, x2d, g_ple, w_pg, p2d, w_pl, tgt, g_final.reshape(1, d))

    dproj, dconv = _branch_a_bwd(dproj, proj, dya_pre, conv_w8, conv_b)
    dproj, dgates, dbias, dg_head, dw_b = _mlstm_bwd(dproj, proj, gates, gate_bias, g_head, h_raw, c_states, aux,
                                                     yb_pre, dyb, w_b)
    s_rows = jnp.concatenate([
        dw_a.reshape(N_DEV, 128, d), dw_b.reshape(N_DEV, 256, d), dw_o.reshape(N_DEV, 128, d),
        dw_pg.reshape(N_DEV, 128, d),
        dw_pl.reshape(PLE_DIM, N_DEV, 128).transpose(1, 0, 2).reshape(N_DEV, 32, d)], axis=1)
    dw_main, dw_gate, (r_rows,) = _mm(hnt, dproj, form="nn", out_dtype=BF, name="dw_main", tk=2048,
                                      extra=(dgates, BF), comm=_DirectComm([s_rows], "exchange"))
    s_win = _pack_windows(dw_main, dw_gate)
    grad_x, dg_mix, (r_win,) = _dhn_mix_bwd(dproj, w_main, dgates, w_gate, x2d, g_mix, dx1,
                                            _DirectComm([s_win], "exchange"))

    vec = jnp.concatenate([dg_mix[0], dconv[3], dg_head[0], small_fin[2], small_fin[1],
                           dbias[0, IG_LANE:], jnp.sum(small_fin[0]).reshape(1),
                           jnp.zeros((7 * d - 6153,), F32)]).reshape(7, d)
    conv_part = jnp.pad(dconv[:3].reshape(3, N_DEV, 128).transpose(1, 0, 2).reshape(N_DEV, 1, 384),
                        ((0, 0), (0, 0), (0, d - 384)))
    s_f32 = jnp.concatenate([jnp.broadcast_to(vec[None], (N_DEV, 7, d)), conv_part], axis=1)
    (r_f32,) = _comm_call(_DirectComm([s_f32], "exchange"), name="small_grads_exchange")
    sum_win = _sum_slots(r_win, name="sum_win", tr=128)
    sum_rows = _sum_slots(r_rows, name="sum_rows", tr=96)
    sum_f32 = _sum_slots(r_f32, name="sum_f32", tr=8)

    g_w_in = sum_win[:, :SHARD_W]
    g_w_a = sum_rows[0:128]
    g_w_b = sum_rows[128:384]
    g_w_o = sum_rows[384:512]
    g_w_pg = sum_rows[512:640]
    g_w_pl = sum_rows[640:672].reshape(PLE_DIM, 128)
    vsum = sum_f32[:7].reshape(7 * d)
    g_g_mix = vsum[0:1024].reshape(1, d)
    g_conv_b = vsum[1024:2048].reshape(1, d)
    g_g_head = vsum[2048:4096].reshape(1, V_DIM)
    g_g_ple = vsum[4096:5120].reshape(1, d)
    g_g_final = vsum[5120:6144].reshape(1, d)
    g_b_gates = vsum[6144:6152].reshape(1, 8)
    g_conv_w = sum_f32[7, :384].reshape(3, 128)

    loss = vsum[6152]

    names = ["g_mix", "w_in", "conv_w", "conv_b", "w_a_out", "b_gates", "g_head", "w_b_out", "w_o", "g_ple",
             "w_ple_gate", "w_ple", "g_final"]
    weights = [g_mix, w_in, conv_w, conv_b, w_a_out, b_gates, g_head, w_b_out, w_o, g_ple, w_ple_gate, w_ple,
               g_final]
    moms = [m_g_mix, m_w_in, m_conv_w, m_conv_b, m_w_a_out, m_b_gates, m_g_head, m_w_b_out, m_w_o, m_g_ple,
            m_w_ple_gate, m_w_ple, m_g_final]
    vels = [v_g_mix, v_w_in, v_conv_w, v_conv_b, v_w_a_out, v_b_gates, v_g_head, v_w_b_out, v_w_o, v_g_ple,
            v_w_ple_gate, v_w_ple, v_g_final]
    grads2d = [g_g_mix, g_w_in, g_conv_w, g_conv_b, g_w_a, g_b_gates, g_g_head, g_w_b, g_w_o, g_g_ple, g_w_pg,
               g_w_pl, g_g_final]
    grads, deltas, new_m, new_v = [], [], [], []
    for nm, w, m_, v_, g2 in zip(names, weights, moms, vels, grads2d):
        shp = w.shape
        if nm == "w_in":
            dl, mn, vn = _adamw(w[0].T, g2.T, m_[0].T, v_[0].T, name="adamw_" + nm)
            grads.append(g2.reshape(shp))
            deltas.append(dl.T.reshape(shp))
            new_m.append(mn.T.reshape(shp))
            new_v.append(vn.T.reshape(shp))
            continue
        kshp = shp if w.ndim >= 2 else (1,) + shp
        gk = g2.reshape(kshp)
        dl, mn, vn = _adamw(w.reshape(kshp), gk, m_.reshape(kshp), v_.reshape(kshp), name="adamw_" + nm)
        grads.append(gk.reshape(shp))
        deltas.append(dl.reshape(shp))
        new_m.append(mn.reshape(shp))
        new_v.append(vn.reshape(shp))
    return (loss, grad_x.reshape(x.shape), *grads, *deltas, *new_m, *new_v)
```

```python
import functools

import jax
import jax.numpy as jnp
from jax import lax
from jax.experimental import pallas as pl
from jax.experimental.pallas import tpu as pltpu

F32 = jnp.float32
BF = jnp.bfloat16

D_MODEL = 1024
N_HEADS = 4
DK = 256
DV = 512
V_DIM = 2048
PLE_DIM = 256
N_IN = 14344
N_MAIN = 14336
EPS = 1e-6
QK_SCALE = DK ** -0.5
NEG = -1e30
N_DEV = 8
SHARD_W = 1793
WIN_STRIDE = 1792
WIN_W = 1920
ROWS_PACK = 672
_CHUNK = 256
IG_LANE = 120
FG_LANE = 124

ADAM_LR = 0.001
ADAM_B1 = 0.9
ADAM_B2 = 0.999
ADAM_EPS = 1e-08
ADAM_WD = 0.01
ADAM_STEP = 10

VMEM_LIMIT = 56 * 1024 * 1024
MESH = pl.DeviceIdType.MESH


def _cparams(sem):
    return pltpu.CompilerParams(dimension_semantics=sem, vmem_limit_bytes=VMEM_LIMIT)


def _sigmoid(x):
    return 1.0 / (1.0 + jnp.exp(-x))


def _log_sigmoid(x):
    return jnp.minimum(x, 0.0) - jnp.log(1.0 + jnp.exp(-jnp.abs(x)))


def _dot(a, b):
    return jnp.dot(a, b, preferred_element_type=F32)


def _dot_nt(a, b):
    return lax.dot_general(a, b, (((1,), (1,)), ((), ())), preferred_element_type=F32)


def _dot_tn(a, b):
    return lax.dot_general(a, b, (((0,), (0,)), ((), ())), preferred_element_type=F32)


class _DirectComm:
    def __init__(self, srcs, kind):
        self.srcs = list(srcs)
        self.kind = kind
        self.n = len(self.srcs)
        if kind == "exchange":
            self.out_shapes = [jax.ShapeDtypeStruct(s.shape, s.dtype) for s in self.srcs]
        else:
            self.out_shapes = [jax.ShapeDtypeStruct((N_DEV,) + s.shape, s.dtype) for s in self.srcs]
        self.scratch = [pltpu.SemaphoreType.DMA((self.n, 7)), pltpu.SemaphoreType.DMA((self.n, 7)),
                        pltpu.SemaphoreType.DMA((self.n,))]

    def ops(self, src, dst, send_sems, recv_sems, local_sems):
        exchange = self.kind == "exchange"

        def descriptors():
            x, y, c = _position()
            me_lin = 4 * x + 2 * y + c
            local = [pltpu.make_async_copy(src[b].at[me_lin] if exchange else src[b], dst[b].at[me_lin],
                                           local_sems.at[b]) for b in range(self.n)]
            sends, recvs = [], []
            for f in range(1, N_DEV):
                px = (1 - x) if (f >> 2) & 1 else x
                py = (1 - y) if (f >> 1) & 1 else y
                pc = (1 - c) if f & 1 else c
                peer_lin = 4 * px + 2 * py + pc
                for b in range(self.n):
                    out = src[b].at[peer_lin] if exchange else src[b]
                    common = dict(send_sem=send_sems.at[b, f - 1], recv_sem=recv_sems.at[b, f - 1],
                                  device_id=(px, py, pc), device_id_type=MESH)
                    sends.append(pltpu.make_async_remote_copy(src_ref=out, dst_ref=dst[b].at[me_lin], **common))
                    recvs.append(pltpu.make_async_remote_copy(src_ref=out, dst_ref=dst[b].at[peer_lin], **common))
            return local, sends, recvs

        def start():
            local, sends, _ = descriptors()
            for cp in local + sends:
                cp.start()

        def wait():
            local, sends, recvs = descriptors()
            for cp in recvs:
                cp.wait_recv()
            for cp in sends:
                cp.wait_send()
            for cp in local:
                cp.wait()

        return start, wait


def _mm(a, b, *, form, out_dtype, name, tm=1024, tn=1024, tk=1024, add=None, extra=None, comm=None):
    assert extra is None or form == "nn"
    if form == "nn":
        m, kc = a.shape
        n = b.shape[1]
    elif form == "nt":
        m, kc = a.shape
        n = b.shape[0]
    else:
        kc, m = a.shape
        n = b.shape[1]
    tm, tn, tk = min(tm, m), min(tn, n), min(tk, kc)
    assert m % tm == 0 and n % tn == 0 and kc % tk == 0, (name, a.shape, b.shape)
    nk = kc // tk
    if form == "tn":
        a_spec = pl.BlockSpec((tk, tm), lambda i, j, k: (k, i))
    else:
        a_spec = pl.BlockSpec((tm, tk), lambda i, j, k: (i, k))
    if form == "nt":
        b_spec = pl.BlockSpec((tn, tk), lambda i, j, k: (j, k))
    else:
        b_spec = pl.BlockSpec((tk, tn), lambda i, j, k: (k, j))
    o_spec = pl.BlockSpec((tm, tn), lambda i, j, k: (i, j))
    dot = {"nn": _dot, "nt": _dot_nt, "tn": _dot_tn}[form]
    has_add = add is not None

    use_acc = nk > 1 and (has_add or out_dtype != F32)
    has_x = extra is not None
    n2 = extra[0].shape[1] if has_x else 0
    use_acc2 = has_x and nk > 1 and extra[1] != F32
    n_comm = comm.n if comm else 0
    n_in = 2 + int(has_add) + int(has_x) + n_comm
    n_out = 1 + int(has_x) + n_comm
    grid = (m // tm, n // tn, nk)

    def body(*refs):
        a_ref, b_ref = refs[0], refs[1]
        add_ref = refs[2] if has_add else None
        b2_ref = refs[2 + int(has_add)] if has_x else None
        o_ref = refs[n_in]
        o2_ref = refs[n_in + 1] if has_x else None
        scr = list(refs[n_in + n_out + (3 if comm else 0):])
        acc = scr.pop(0) if use_acc else o_ref
        acc2 = scr.pop(0) if use_acc2 else o2_ref
        i, j, k = pl.program_id(0), pl.program_id(1), pl.program_id(2)
        if comm:
            start, wait = comm.ops(refs[n_in - n_comm:n_in], refs[n_in + n_out - n_comm:n_in + n_out],
                                   *refs[n_in + n_out:n_in + n_out + 3])
            pl.when((i == 0) & (j == 0) & (k == 0))(start)

        def part():
            return dot(a_ref[...].astype(BF), b_ref[...].astype(BF))

        def part2():
            return dot(a_ref[...].astype(BF), b2_ref[...].astype(BF))

        def finish(total):
            if has_add:
                total = total + add_ref[...].astype(F32)
            o_ref[...] = total.astype(out_dtype)

        if nk == 1:
            finish(part())
            if has_x:
                @pl.when(j == 0)
                def _():
                    o2_ref[...] = part2().astype(extra[1])
        else:
            @pl.when(k == 0)
            def _():
                acc[...] = part()

            @pl.when(k > 0)
            def _():
                acc[...] += part()

            if use_acc:
                @pl.when(k == nk - 1)
                def _():
                    finish(acc[...])
            if has_x:
                @pl.when((j == 0) & (k == 0))
                def _():
                    acc2[...] = part2()

                @pl.when((j == 0) & (k > 0))
                def _():
                    acc2[...] += part2()

                if use_acc2:
                    @pl.when((j == 0) & (k == nk - 1))
                    def _():
                        o2_ref[...] = acc2[...].astype(extra[1])
        if comm:
            pl.when((i == grid[0] - 1) & (j == grid[1] - 1) & (k == nk - 1))(wait)

    any_spec = pl.BlockSpec(memory_space=pl.ANY)
    o2_spec = pl.BlockSpec((tm, n2), lambda i, j, k: (i, 0))
    in_specs = ([a_spec, b_spec] + ([o_spec] if has_add else [])
                + ([pl.BlockSpec((tk, n2), lambda i, j, k: (k, 0))] if has_x else []) + [any_spec] * n_comm)
    args = (a, b) + ((add,) if has_add else ()) + ((extra[0],) if has_x else ()) + (tuple(comm.srcs) if comm else ())
    out_specs = [o_spec] + ([o2_spec] if has_x else []) + [any_spec] * n_comm
    out_shape = ([jax.ShapeDtypeStruct((m, n), out_dtype)]
                 + ([jax.ShapeDtypeStruct((m, n2), extra[1])] if has_x else []) + (comm.out_shapes if comm else []))
    scratch = ((comm.scratch if comm else []) + ([pltpu.VMEM((tm, tn), F32)] if use_acc else [])
               + ([pltpu.VMEM((tm, n2), F32)] if use_acc2 else []))
    if comm:
        sem = ("arbitrary",) * 3
    else:
        sem = ("parallel", "arbitrary" if has_x else "parallel", "arbitrary")
    res = pl.pallas_call(
        body, name=name, grid=grid, in_specs=in_specs, out_specs=out_specs, out_shape=out_shape,
        scratch_shapes=scratch, compiler_params=_cparams(sem),
    )(*args)
    if not (comm or has_x):
        return res[0]
    return tuple(res[:1 + int(has_x)]) + ((list(res[1 + int(has_x):]),) if comm else ())


HALO = 16
XA_BLK, BA_BLK, CA_BLK, ZA_BLK = 8, 9, 10, 11


def _shift_down(u, prev, n):
    return pltpu.roll(jnp.concatenate([prev, u], axis=0), n, 0)[HALO:]


def _shift_up(u, nxt, n):
    tm = u.shape[0]
    return pltpu.roll(jnp.concatenate([u, nxt], axis=0), tm + HALO - n, 0)[:tm]


def _branch_a_fwd(proj, conv_w8, conv_b, *, tm=512):
    t = proj.shape[0]
    d = D_MODEL
    tm = min(tm, t)
    hb = tm // HALO

    def body(xa_ref, ba_ref, ca_ref, za_ref, xap_ref, cap_ref, w_ref, b_ref, o_ref):
        i = pl.program_id(0)
        u = ca_ref[...].astype(F32) * xa_ref[...].astype(F32)
        up = cap_ref[...].astype(F32) * xap_ref[...].astype(F32)
        up = jnp.where(i == 0, 0.0, up)
        u1 = _shift_down(u, up, 1)
        u2 = _shift_down(u, up, 2)
        cv = w_ref[0:1, :] * u2 + w_ref[1:2, :] * u1 + w_ref[2:3, :] * u + b_ref[...]
        za = za_ref[...].astype(F32)
        o_ref[...] = (ba_ref[...].astype(F32) * cv * (za * _sigmoid(za))).astype(BF)

    def col(blk):
        return pl.BlockSpec((tm, d), lambda i: (i, blk))

    def prev(blk):
        return pl.BlockSpec((HALO, d), lambda i: (jnp.maximum(i * hb - 1, 0), blk))

    return pl.pallas_call(
        body, name="branch_a_fwd", grid=(t // tm,),
        in_specs=[col(XA_BLK), col(BA_BLK), col(CA_BLK), col(ZA_BLK), prev(XA_BLK), prev(CA_BLK),
                  pl.BlockSpec((8, d), lambda i: (0, 0)), pl.BlockSpec((1, d), lambda i: (0, 0))],
        out_specs=pl.BlockSpec((tm, d), lambda i: (i, 0)),
        out_shape=jax.ShapeDtypeStruct((t, d), BF),
        compiler_params=_cparams(("parallel",)),
    )(proj, proj, proj, proj, proj, proj, conv_w8, conv_b)


def _branch_a_bwd(dproj, proj, dya_pre, conv_w8, conv_b, *, tm=512):
    t = proj.shape[0]
    d = D_MODEL
    tm = min(tm, t)
    hb = tm // HALO
    nt = t // tm

    def body(dp_ref, xa_ref, ba_ref, ca_ref, za_ref, xap_ref, cap_ref, dy_ref, ban_ref, zan_ref, dyn_ref,
             w_ref, b_ref, o_ref, dc_ref):
        del dp_ref
        i = pl.program_id(0)
        xa = xa_ref[...].astype(F32)
        ca = ca_ref[...].astype(F32)
        ba = ba_ref[...].astype(F32)
        za = za_ref[...].astype(F32)
        u = ca * xa
        up = cap_ref[...].astype(F32) * xap_ref[...].astype(F32)
        up = jnp.where(i == 0, 0.0, up)
        u1 = _shift_down(u, up, 1)
        u2 = _shift_down(u, up, 2)
        w0, w1, w2 = w_ref[0:1, :], w_ref[1:2, :], w_ref[2:3, :]
        cv = w0 * u2 + w1 * u1 + w2 * u + b_ref[...]
        sg = _sigmoid(za)
        sz = za * sg
        dy = dy_ref[...].astype(F32)
        dcv = dy * ba * sz
        zan = zan_ref[...].astype(F32)
        dcvn = dyn_ref[...].astype(F32) * ban_ref[...].astype(F32) * (zan * _sigmoid(zan))
        dcvn = jnp.where(i == nt - 1, 0.0, dcvn)
        du = w2 * dcv + w1 * _shift_up(dcv, dcvn, 1) + w0 * _shift_up(dcv, dcvn, 2)
        o_ref[:, 0:d] = (du * ca).astype(BF)
        o_ref[:, d:2 * d] = (dy * cv * sz).astype(BF)
        o_ref[:, 2 * d:3 * d] = (du * xa).astype(BF)
        o_ref[:, 3 * d:4 * d] = (dy * ba * cv * sg * (1.0 + za * (1.0 - sg))).astype(BF)

        @pl.when(i == 0)
        def _():
            dc_ref[...] = jnp.zeros_like(dc_ref)

        dc_ref[0:1, :] += jnp.sum(dcv * u2, axis=0, keepdims=True)
        dc_ref[1:2, :] += jnp.sum(dcv * u1, axis=0, keepdims=True)
        dc_ref[2:3, :] += jnp.sum(dcv * u, axis=0, keepdims=True)
        dc_ref[3:4, :] += jnp.sum(dcv, axis=0, keepdims=True)

    def col(blk):
        return pl.BlockSpec((tm, d), lambda i: (i, blk))

    def prev(blk):
        return pl.BlockSpec((HALO, d), lambda i: (jnp.maximum(i * hb - 1, 0), blk))

    def nxt(blk):
        return pl.BlockSpec((HALO, d), lambda i: (jnp.minimum((i + 1) * hb, t // HALO - 1), blk))

    return pl.pallas_call(
        body, name="branch_a_bwd", grid=(nt,),
        in_specs=[pl.BlockSpec(memory_space=pl.ANY),
                  col(XA_BLK), col(BA_BLK), col(CA_BLK), col(ZA_BLK), prev(XA_BLK), prev(CA_BLK),
                  pl.BlockSpec((tm, d), lambda i: (i, 0)), nxt(BA_BLK), nxt(ZA_BLK),
                  pl.BlockSpec((HALO, d), lambda i: (jnp.minimum((i + 1) * hb, t // HALO - 1), 0)),
                  pl.BlockSpec((8, d), lambda i: (0, 0)), pl.BlockSpec((1, d), lambda i: (0, 0))],
        out_specs=[pl.BlockSpec((tm, 4 * d), lambda i: (i, 2)), pl.BlockSpec((8, d), lambda i: (0, 0))],
        out_shape=[jax.ShapeDtypeStruct(dproj.shape, BF), jax.ShapeDtypeStruct((8, d), F32)],
        input_output_aliases={0: 0},
        compiler_params=_cparams(("arbitrary",)),
    )(dproj, proj, proj, proj, proj, proj, proj, dya_pre, proj, proj, dya_pre, conv_w8, conv_b)


def _gate_vectors(gc, gt, h, lane_i, sub_i):
    ig_c = jnp.sum(jnp.where(lane_i == IG_LANE + h, gc, 0.0), axis=1, keepdims=True)
    fg_c = jnp.sum(jnp.where(lane_i == FG_LANE + h, gc, 0.0), axis=1, keepdims=True)
    ig_r = jnp.sum(jnp.where(sub_i == IG_LANE + h, gt, 0.0), axis=0, keepdims=True)
    fg_r = jnp.sum(jnp.where(sub_i == FG_LANE + h, gt, 0.0), axis=0, keepdims=True)
    return ig_c, fg_c, ig_r, fg_r


def _chunk_common(q, k, ig_c, fg_c, ig_r, fg_r, m_prev, n_prev, row, col):
    lf_c = _log_sigmoid(fg_c)
    lf_r = _log_sigmoid(fg_r)
    causal = col <= row
    b_c = jnp.sum(jnp.where(causal, lf_r, 0.0), axis=1, keepdims=True)
    b_r = jnp.sum(jnp.where(row <= col, lf_c, 0.0), axis=0, keepdims=True)
    dmat = jnp.where(causal, b_c - b_r + ig_r, NEG)
    a = b_c + m_prev
    m_row = jnp.maximum(a, jnp.max(dmat, axis=1, keepdims=True))
    est = jnp.exp(dmat - m_row)
    s = _dot_nt(q, k) * QK_SCALE * est
    inter = jnp.exp(a - m_row)
    den = jnp.sum(s, axis=1, keepdims=True) + inter * QK_SCALE * jnp.sum(
        q.astype(F32) * n_prev, axis=1, keepdims=True)
    expm = jnp.exp(-m_row)
    mx = jnp.maximum(jnp.abs(den), expm)
    b_last = jnp.sum(lf_r, axis=1, keepdims=True)
    g_r = b_last - b_r + ig_r
    g_c = b_last - b_c + ig_c
    m_new = jnp.maximum(b_last + m_prev, jnp.max(g_r, axis=1, keepdims=True))
    w_c = jnp.exp(g_c - m_new)
    decay = jnp.exp(b_last + m_prev - m_new)
    return est, s, inter, den, expm, mx, m_new, w_c, decay


def _mlstm_fwd(proj, gates, gate_bias, g_head, w_b):
    t = proj.shape[0]
    lc = min(_CHUNK, t)
    nc = t // lc

    def body(q_ref, k_ref, v_ref, o_ref, z_ref, g_ref, gb_ref, gh_ref, wb_ref,
             yb_ref, ybo_ref, hr_ref, cs_ref, aux_ref, c_scr, nm_scr):
        c = pl.program_id(0)

        @pl.when(c == 0)
        def _():
            c_scr[...] = jnp.zeros_like(c_scr)
            nm_scr[...] = jnp.zeros_like(nm_scr)
            nm_scr[:, 1:2, :] = jnp.full((N_HEADS, 1, DK), NEG, F32)

        row = lax.broadcasted_iota(jnp.int32, (lc, lc), 0)
        col = lax.broadcasted_iota(jnp.int32, (lc, lc), 1)
        gc = g_ref[...] + gb_ref[...]
        gt = gc.T
        lane_i = lax.broadcasted_iota(jnp.int32, gc.shape, 1)
        sub_i = lax.broadcasted_iota(jnp.int32, gt.shape, 0)
        for h in range(N_HEADS):
            ks = slice(h * DK, (h + 1) * DK)
            vs = slice(h * DV, (h + 1) * DV)
            q = q_ref[:, ks]
            k = k_ref[:, ks]
            v = v_ref[:, vs]
            ig_c, fg_c, ig_r, fg_r = _gate_vectors(gc, gt, h, lane_i, sub_i)
            n_prev = nm_scr[h, 0:1, :]
            m_prev = nm_scr[h, 1:2, 0:1]
            est, s, inter, den, expm, mx, m_new, w_c, decay = _chunk_common(
                q, k, ig_c, fg_c, ig_r, fg_r, m_prev, n_prev, row, col)
            c_prev = c_scr[h]
            c_prev_b = c_prev.astype(BF)
            num = _dot(s.astype(BF), v) + (inter * QK_SCALE) * _dot(q, c_prev_b)
            hh = num / mx
            r = lax.rsqrt(jnp.mean(hh * hh, axis=1, keepdims=True) + EPS)
            hbn = hh * r * gh_ref[:, vs]
            o = o_ref[:, vs].astype(F32)
            z = z_ref[:, vs].astype(F32)
            yb_ref[:, vs] = (_sigmoid(o) * hbn * (z * _sigmoid(z))).astype(BF)
            hr_ref[:, vs] = hh.astype(BF)
            cs_ref[0, h] = c_prev_b
            aux_ref[0, h] = nm_scr[h]
            kw = k.astype(F32) * w_c
            c_scr[h] = decay * c_prev + _dot_tn(kw.astype(BF), v)
            nm_scr[h, 0:1, :] = decay * n_prev + jnp.sum(kw, axis=0, keepdims=True)
            nm_scr[h, 1:2, :] = jnp.broadcast_to(m_new, (1, DK))
        ybo_ref[...] = _dot(yb_ref[...], wb_ref[...]).astype(BF)

    return pl.pallas_call(
        body, name="mlstm_fwd", grid=(nc,),
        in_specs=[pl.BlockSpec((lc, 1024), lambda c: (c, 0)),
                  pl.BlockSpec((lc, 1024), lambda c: (c, 1)),
                  pl.BlockSpec((lc, 2048), lambda c: (c, 1)),
                  pl.BlockSpec((lc, 2048), lambda c: (c, 2)),
                  pl.BlockSpec((lc, 2048), lambda c: (c, 3)),
                  pl.BlockSpec((lc, 128), lambda c: (c, 0)),
                  pl.BlockSpec((1, 128), lambda c: (0, 0)),
                  pl.BlockSpec((1, V_DIM), lambda c: (0, 0)),
                  pl.BlockSpec((V_DIM, D_MODEL), lambda c: (0, 0))],
        out_specs=[pl.BlockSpec((lc, V_DIM), lambda c: (c, 0)),
                   pl.BlockSpec((lc, D_MODEL), lambda c: (c, 0)),
                   pl.BlockSpec((lc, V_DIM), lambda c: (c, 0)),
                   pl.BlockSpec((1, N_HEADS, DK, DV), lambda c: (c, 0, 0, 0)),
                   pl.BlockSpec((1, N_HEADS, 8, DK), lambda c: (c, 0, 0, 0))],
        out_shape=[jax.ShapeDtypeStruct((t, V_DIM), BF), jax.ShapeDtypeStruct((t, D_MODEL), BF),
                   jax.ShapeDtypeStruct((t, V_DIM), BF),
                   jax.ShapeDtypeStruct((nc, N_HEADS, DK, DV), BF),
                   jax.ShapeDtypeStruct((nc, N_HEADS, 8, DK), F32)],
        scratch_shapes=[pltpu.VMEM((N_HEADS, DK, DV), F32), pltpu.VMEM((N_HEADS, 8, DK), F32)],
        compiler_params=_cparams(("arbitrary",)),
    )(proj, proj, proj, proj, proj, gates, gate_bias, g_head, w_b)


def _mlstm_bwd(dproj, proj, gates, gate_bias, g_head, h_raw, c_states, aux, yb_pre, dyb, w_b):
    t = proj.shape[0]
    lc = min(_CHUNK, t)
    nc = t // lc

    def body(dpin_ref, q_ref, k_ref, v_ref, o_ref, z_ref, g_ref, gb_ref, gh_ref, hr_ref, cs_ref, aux_ref,
             ybp_ref, dyb_ref, wb_ref, dp_ref, dg_ref, dbias_ref, dgh_ref, dwb_ref, dc_scr, dn_scr, dy_ref,
             dwb_scr):
        del dpin_ref
        step = pl.program_id(0)

        @pl.when(step == 0)
        def _():
            dc_scr[...] = jnp.zeros_like(dc_scr)
            dn_scr[...] = jnp.zeros_like(dn_scr)
            dbias_ref[...] = jnp.zeros_like(dbias_ref)
            dgh_ref[...] = jnp.zeros_like(dgh_ref)
            dwb_scr[...] = jnp.zeros_like(dwb_scr)

        dyb = dyb_ref[...]
        dy_ref[...] = _dot_nt(dyb, wb_ref[...])
        dwb_scr[...] += _dot_tn(ybp_ref[...], dyb)

        row = lax.broadcasted_iota(jnp.int32, (lc, lc), 0)
        col = lax.broadcasted_iota(jnp.int32, (lc, lc), 1)
        eye = row == col
        gc = g_ref[...] + gb_ref[...]
        gt = gc.T
        lane_i = lax.broadcasted_iota(jnp.int32, gc.shape, 1)
        sub_i = lax.broadcasted_iota(jnp.int32, gt.shape, 0)
        dgates = jnp.zeros(gc.shape, F32)
        for h in range(N_HEADS):
            ks = slice(h * DK, (h + 1) * DK)
            vs = slice(h * DV, (h + 1) * DV)
            q = q_ref[:, ks]
            k = k_ref[:, ks]
            v = v_ref[:, vs]
            ig_c, fg_c, ig_r, fg_r = _gate_vectors(gc, gt, h, lane_i, sub_i)
            n_prev = aux_ref[0, h, 0:1, :]
            m_prev = aux_ref[0, h, 1:2, 0:1]
            c_prev_b = cs_ref[0, h]
            est, s, inter, den, expm, mx, m_new, w_c, decay = _chunk_common(
                q, k, ig_c, fg_c, ig_r, fg_r, m_prev, n_prev, row, col)
            hb = hr_ref[:, vs].astype(F32)
            dyp = dy_ref[:, vs]
            o = o_ref[:, vs].astype(F32)
            z = z_ref[:, vs].astype(F32)
            so = _sigmoid(o)
            sgz = _sigmoid(z)
            sz = z * sgz
            r = lax.rsqrt(jnp.mean(hb * hb, axis=1, keepdims=True) + EPS)
            xh = hb * r
            gh = gh_ref[:, vs]
            hbn = xh * gh
            dyso = dyp * so
            dyso_h = dyso * hbn
            d_o = dyso_h * sz * (1.0 - so)
            d_z = dyso_h * sgz * (1.0 + z * (1.0 - sgz))
            dhbn = dyso * sz
            dgh_ref[0:1, vs] += jnp.sum(dhbn * xh, axis=0, keepdims=True)
            dxh = dhbn * gh
            dh = r * (dxh - xh * jnp.mean(dxh * xh, axis=1, keepdims=True))
            dnm = dh / mx
            hd = jnp.sum(dh * hb, axis=1, keepdims=True)
            cond = jnp.abs(den) > expm
            dden = jnp.where(cond, -hd / mx * jnp.sign(den), 0.0)
            dnm_b = dnm.astype(BF)
            p = _dot_nt(dnm_b, v) + dden
            dqk = (p * est * QK_SCALE).astype(BF)
            dq_inter = (inter * QK_SCALE) * (_dot_nt(dnm_b, c_prev_b) + dden * n_prev)
            dq = _dot(dqk, k) + dq_inter
            dc_new = dc_scr[h]
            dc_new_b = dc_new.astype(BF)
            dn_new = dn_scr[h, 0:1, :]
            kf = k.astype(F32)
            dk_state = w_c * (_dot_nt(v, dc_new_b) + dn_new)
            dk = _dot_tn(dqk, q) + dk_state
            dv = _dot_tn(s.astype(BF), dnm_b) + w_c * _dot(k, dc_new_b)
            pairs = p * s
            to_next_c = jnp.sum(kf * dk_state, axis=1, keepdims=True)
            dli_c = jnp.sum(jnp.where(eye, jnp.sum(pairs, axis=0, keepdims=True), 0.0), axis=1,
                            keepdims=True) + to_next_c
            hmat = _dot(pairs.astype(BF), (row < col).astype(BF))
            from_prev_c = jnp.sum(q.astype(F32) * dq_inter, axis=1, keepdims=True)
            through = decay * (
                jnp.sum(jnp.sum(dc_new * c_prev_b.astype(F32), axis=1, keepdims=True), axis=0, keepdims=True)
                + jnp.sum(dn_new * n_prev, axis=1, keepdims=True))
            dlf_r = through + jnp.sum(jnp.where(row >= col, hmat + from_prev_c, to_next_c), axis=0, keepdims=True)
            dlf_c = jnp.sum(jnp.where(eye, dlf_r, 0.0), axis=1, keepdims=True)
            dfg_c = dlf_c * _sigmoid(-fg_c)
            dgates = dgates + jnp.where(lane_i == IG_LANE + h, dli_c, 0.0) + jnp.where(
                lane_i == FG_LANE + h, dfg_c, 0.0)
            qi = q.astype(F32) * (inter * QK_SCALE)
            dc_scr[h] = decay * dc_new + _dot_tn(qi.astype(BF), dnm_b)
            dn_scr[h, 0:1, :] = decay * dn_new + jnp.sum(qi * dden, axis=0, keepdims=True)
            dp_ref[:, h * DK:(h + 1) * DK] = dq.astype(BF)
            dp_ref[:, 1024 + h * DK:1024 + (h + 1) * DK] = dk.astype(BF)
            dp_ref[:, 2048 + h * DV:2048 + (h + 1) * DV] = dv.astype(BF)
            dp_ref[:, 4096 + h * DV:4096 + (h + 1) * DV] = d_o.astype(BF)
            dp_ref[:, 6144 + h * DV:6144 + (h + 1) * DV] = d_z.astype(BF)
        dg_ref[...] = dgates.astype(BF)
        dbias_ref[0:1, :] += jnp.sum(dgates, axis=0, keepdims=True)

        @pl.when(step == nc - 1)
        def _():
            dwb_ref[...] = dwb_scr[...].astype(BF)

    def rev(c):
        return nc - 1 - c

    return pl.pallas_call(
        body, name="mlstm_bwd", grid=(nc,),
        input_output_aliases={0: 0},
        in_specs=[pl.BlockSpec(memory_space=pl.ANY),
                  pl.BlockSpec((lc, 1024), lambda c: (rev(c), 0)),
                  pl.BlockSpec((lc, 1024), lambda c: (rev(c), 1)),
                  pl.BlockSpec((lc, 2048), lambda c: (rev(c), 1)),
                  pl.BlockSpec((lc, 2048), lambda c: (rev(c), 2)),
                  pl.BlockSpec((lc, 2048), lambda c: (rev(c), 3)),
                  pl.BlockSpec((lc, 128), lambda c: (rev(c), 0)),
                  pl.BlockSpec((1, 128), lambda c: (0, 0)),
                  pl.BlockSpec((1, V_DIM), lambda c: (0, 0)),
                  pl.BlockSpec((lc, V_DIM), lambda c: (rev(c), 0)),
                  pl.BlockSpec((1, N_HEADS, DK, DV), lambda c: (rev(c), 0, 0, 0)),
                  pl.BlockSpec((1, N_HEADS, 8, DK), lambda c: (rev(c), 0, 0, 0)),
                  pl.BlockSpec((lc, V_DIM), lambda c: (rev(c), 0)),
                  pl.BlockSpec((lc, D_MODEL), lambda c: (rev(c), 0)),
                  pl.BlockSpec((V_DIM, D_MODEL), lambda c: (0, 0))],
        out_specs=[pl.BlockSpec((lc, 8192), lambda c: (rev(c), 0)),
                   pl.BlockSpec((lc, 128), lambda c: (rev(c), 0)),
                   pl.BlockSpec((8, 128), lambda c: (0, 0)),
                   pl.BlockSpec((8, V_DIM), lambda c: (0, 0)),
                   pl.BlockSpec((V_DIM, D_MODEL), lambda c: (0, 0))],
        out_shape=[jax.ShapeDtypeStruct((t, N_MAIN), BF), jax.ShapeDtypeStruct((t, 128), BF),
                   jax.ShapeDtypeStruct((8, 128), F32), jax.ShapeDtypeStruct((8, V_DIM), F32),
                   jax.ShapeDtypeStruct((V_DIM, D_MODEL), BF)],
        scratch_shapes=[pltpu.VMEM((N_HEADS, DK, DV), F32), pltpu.VMEM((N_HEADS, 8, DK), F32),
                        pltpu.VMEM((lc, V_DIM), F32), pltpu.VMEM((V_DIM, D_MODEL), F32)],
        compiler_params=_cparams(("arbitrary",)),
    )(dproj, proj, proj, proj, proj, proj, gates, gate_bias, g_head, h_raw, c_states, aux, yb_pre, dyb, w_b)


GA_BLK, GB_BLK = 12, 13


def _full(shape):
    return pl.BlockSpec(shape, lambda i: (0,) * len(shape))


def _rms_backward(xv, g, dhn, dres):
    r = lax.rsqrt(jnp.mean(xv * xv, axis=1, keepdims=True) + EPS)
    xh = xv * r
    dxh = dhn * g
    dx = dres + r * (dxh - xh * jnp.mean(dxh * xh, axis=1, keepdims=True))
    return dx, jnp.sum(dhn * xh, axis=0, keepdims=True)


def _token_chain(ya_pre, w_a, yb, proj, w_o, x, g_ple, w_pg, p, w_pl, target, g_final, *, tm=256):
    t, d = x.shape
    tm = min(tm, t)

    def body(yap_ref, wa_ref, yb_ref, ga_ref, gb_ref, wo_ref, x_ref, gp_ref, wpg_ref, p_ref, wpl_ref, tg_ref,
             gf_ref, dx_ref, dyb_ref, dyap_ref, o_ref, sm_ref, dwa_ref, dwo_ref, dwpg_ref, dwpl_ref,
             dwa_acc, dwo_acc, dwpg_acc, dwpl_acc):
        i = pl.program_id(0)

        @pl.when(i == 0)
        def _():
            sm_ref[...] = jnp.zeros_like(sm_ref)
            dwa_acc[...] = jnp.zeros_like(dwa_acc)
            dwo_acc[...] = jnp.zeros_like(dwo_acc)
            dwpg_acc[...] = jnp.zeros_like(dwpg_acc)
            dwpl_acc[...] = jnp.zeros_like(dwpl_acc)

        yap = yap_ref[...]
        ya = _dot(yap, wa_ref[...]).astype(BF).astype(F32)
        yb_v = yb_ref[...].astype(F32)
        sa = _sigmoid(ga_ref[...].astype(F32))
        sb = _sigmoid(gb_ref[...].astype(F32))
        merged = (sa * ya + sb * yb_v).astype(BF)
        x1 = _dot(merged, wo_ref[...]) + x_ref[...]
        r1 = lax.rsqrt(jnp.mean(x1 * x1, axis=1, keepdims=True) + EPS)
        xh1 = x1 * r1
        gp = gp_ref[...]
        hn2 = xh1 * gp
        hn2_b = hn2.astype(BF)
        gate = _sigmoid(_dot(hn2_b, wpg_ref[...]))
        p_b = p_ref[...].astype(BF)
        pe_v = _dot(p_b, wpl_ref[...])
        x2 = x1 + gate * pe_v
        r2 = lax.rsqrt(jnp.mean(x2 * x2, axis=1, keepdims=True) + EPS)
        xh2 = x2 * r2
        gf = gf_ref[...]
        err = xh2 * gf - tg_ref[...]
        dy = err * (1.0 / d)
        dxh2 = dy * gf
        dx2 = r2 * (dxh2 - xh2 * jnp.mean(dxh2 * xh2, axis=1, keepdims=True))
        dgpre = (dx2 * pe_v * gate * (1.0 - gate)).astype(BF)
        dwpg_acc[...] += _dot_tn(hn2_b, dgpre)
        dwpl_acc[...] += _dot_tn(p_b, (dx2 * gate).astype(BF))
        dhn2 = _dot_nt(dgpre, wpg_ref[...])
        dxh1 = dhn2 * gp
        dx1 = dx2 + r1 * (dxh1 - xh1 * jnp.mean(dxh1 * xh1, axis=1, keepdims=True))
        dx_ref[...] = dx1
        dx1_b = dx1.astype(BF)
        dwo_acc[...] += _dot_tn(merged, dx1_b)
        dm = _dot_nt(dx1_b, wo_ref[...])
        dya = (dm * sa).astype(BF)
        dwa_acc[...] += _dot_tn(yap, dya)
        dyb_ref[...] = (dm * sb).astype(BF)
        o_ref[:, 0:d] = (dm * ya * sa * (1.0 - sa)).astype(BF)
        o_ref[:, d:2 * d] = (dm * yb_v * sb * (1.0 - sb)).astype(BF)
        dyap_ref[...] = _dot_nt(dya, wa_ref[...]).astype(BF)

        sm_ref[0:1, :] += (0.5 / d) * jnp.sum(err * err, axis=0, keepdims=True)
        sm_ref[1:2, :] += jnp.sum(dy * xh2, axis=0, keepdims=True)
        sm_ref[2:3, :] += jnp.sum(dhn2 * xh1, axis=0, keepdims=True)

        @pl.when(i == t // tm - 1)
        def _():
            dwa_ref[...] = dwa_acc[...].astype(BF)
            dwo_ref[...] = dwo_acc[...].astype(BF)
            dwpg_ref[...] = dwpg_acc[...].astype(BF)
            dwpl_ref[...] = dwpl_acc[...].astype(BF)

    row = pl.BlockSpec((tm, d), lambda i: (i, 0))
    bf = jax.ShapeDtypeStruct((t, d), BF)
    return pl.pallas_call(
        body, name="token_chain", grid=(t // tm,),
        in_specs=[row, _full((d, d)), row, pl.BlockSpec((tm, d), lambda i: (i, GA_BLK)),
                  pl.BlockSpec((tm, d), lambda i: (i, GB_BLK)), _full((d, d)), row, _full((1, d)), _full((d, d)),
                  pl.BlockSpec((tm, PLE_DIM), lambda i: (i, 0)), _full((PLE_DIM, d)), row, _full((1, d))],
        out_specs=[row, row, row, pl.BlockSpec((tm, 2 * d), lambda i: (i, 6)), _full((8, d)),
                   _full((d, d)), _full((d, d)), _full((d, d)), _full((PLE_DIM, d))],
        out_shape=[jax.ShapeDtypeStruct((t, d), F32), bf, bf,
                   jax.ShapeDtypeStruct((t, N_MAIN), BF), jax.ShapeDtypeStruct((8, d), F32),
                   jax.ShapeDtypeStruct((d, d), BF), jax.ShapeDtypeStruct((d, d), BF),
                   jax.ShapeDtypeStruct((d, d), BF), jax.ShapeDtypeStruct((PLE_DIM, d), BF)],
        scratch_shapes=[pltpu.VMEM((d, d), F32), pltpu.VMEM((d, d), F32), pltpu.VMEM((d, d), F32),
                        pltpu.VMEM((PLE_DIM, d), F32)],
        compiler_params=_cparams(("arbitrary",)),
    )(ya_pre, w_a, yb, proj, proj, w_o, x, g_ple, w_pg, p, w_pl, target, g_final)


def _dhn_mix_bwd(dproj, w_main, dgates, w_gate, x, g_mix, dx1, comm, *, tm=1024, tk=2048):
    t, d = x.shape
    tm = min(tm, t)
    nk = N_MAIN // tk
    ni = t // tm
    n_in = 7 + comm.n

    def body(*refs):
        dp_ref, w_ref, dgt_ref, wg_ref, x_ref, g_ref, dres_ref = refs[:7]
        dx_ref, dg_ref = refs[n_in], refs[n_in + 1]
        acc = refs[-1]
        start, wait = comm.ops(refs[7:n_in], refs[n_in + 2:n_in + 2 + comm.n], *refs[n_in + 2 + comm.n:-1])
        i, k = pl.program_id(0), pl.program_id(1)
        pl.when((i == 0) & (k == 0))(start)

        @pl.when(k == 0)
        def _():
            acc[...] = _dot_nt(dp_ref[...], w_ref[...]) + _dot_nt(dgt_ref[...], wg_ref[...])

        @pl.when(k > 0)
        def _():
            acc[...] += _dot_nt(dp_ref[...], w_ref[...])

        @pl.when((i == 0) & (k == 0))
        def _():
            dg_ref[...] = jnp.zeros_like(dg_ref)

        @pl.when(k == nk - 1)
        def _():
            dx, dg = _rms_backward(x_ref[...], g_ref[...], acc[...], dres_ref[...])
            dx_ref[...] = dx
            dg_ref[0:1, :] += dg

        pl.when((i == ni - 1) & (k == nk - 1))(wait)

    any_spec = pl.BlockSpec(memory_space=pl.ANY)
    row = pl.BlockSpec((tm, d), lambda i, k: (i, 0))
    res = pl.pallas_call(
        body, name="dhn_mix_bwd", grid=(ni, nk),
        in_specs=[pl.BlockSpec((tm, tk), lambda i, k: (i, k)), pl.BlockSpec((d, tk), lambda i, k: (0, k)),
                  pl.BlockSpec((tm, 128), lambda i, k: (i, 0)), pl.BlockSpec((d, 128), lambda i, k: (0, 0)),
                  row, pl.BlockSpec((1, d), lambda i, k: (0, 0)), row] + [any_spec] * comm.n,
        out_specs=[row, pl.BlockSpec((8, d), lambda i, k: (0, 0))] + [any_spec] * comm.n,
        out_shape=[jax.ShapeDtypeStruct((t, d), F32), jax.ShapeDtypeStruct((8, d), F32)] + comm.out_shapes,
        scratch_shapes=comm.scratch + [pltpu.VMEM((tm, d), F32)],
        compiler_params=_cparams(("arbitrary", "arbitrary")),
    )(dproj, w_main, dgates, w_gate, x, g_mix, dx1, *comm.srcs)
    return res[0], res[1], list(res[2:])


def _position():
    x, y, c = lax.axis_index("x"), lax.axis_index("y"), lax.axis_index("c")
    return x, y, c


def _rms_fwd_all_gather(x, g, srcs, *, tm=512):
    t, d = x.shape
    tm = min(tm, t)
    nt = t // tm
    nb = len(srcs)
    any_spec = pl.BlockSpec(memory_space=pl.ANY)

    def body(*refs):
        x_ref, g_ref = refs[:2]
        src = refs[2:2 + nb]
        hn_ref, hnt_ref = refs[2 + nb:4 + nb]
        dst = refs[4 + nb:4 + 2 * nb]
        send_sems, recv_sems, local_sems = refs[4 + 2 * nb:]
        i = pl.program_id(0)

        def parts():
            px, py, pc = _position()
            me, sibling = (px, py, pc), (px, py, 1 - pc)
            chips = [(1 - px, py), (px, 1 - py), (1 - px, 1 - py)]

            def slot(b, qx, qy, qc):
                return dst[b].at[4 * qx + 2 * qy + qc]

            def copy(k, b, block, to, from_src=False):
                return pltpu.make_async_remote_copy(
                    src_ref=src[b] if from_src else slot(b, *block), dst_ref=slot(b, *block),
                    send_sem=send_sems.at[b, k], recv_sem=recv_sems.at[b, k],
                    device_id=to, device_id_type=MESH)

            mine = [pltpu.make_async_copy(src[b], slot(b, *me), local_sems.at[b]) for b in range(nb)]
            first = [copy(0, b, me, sibling, True) for b in range(nb)]
            first += [copy(1 + j, b, me, (*chip, pc), True) for j, chip in enumerate(chips) for b in range(nb)]
            return me, sibling, chips, pc, copy, mine, first

        @pl.when(i == 0)
        def _():
            _, _, _, _, _, mine, first = parts()
            for cp in mine + first:
                cp.start()

        xv = x_ref[...]
        r = lax.rsqrt(jnp.mean(xv * xv, axis=1, keepdims=True) + EPS)
        hn = xv * r * g_ref[...]
        hn_ref[...] = hn.astype(BF)
        hnt_ref[...] = hn.T.astype(BF)

        @pl.when(i == nt - 1)
        def _():
            me, sibling, chips, pc, copy, mine, first = parts()
            passed = []
            for j, chip in enumerate(chips):
                for b in range(nb):
                    copy(1 + j, b, (*chip, pc), me).wait_recv()
                    fwd = copy(4 + j, b, (*chip, pc), sibling)
                    fwd.start()
                    passed.append(fwd)
            for b in range(nb):
                copy(0, b, sibling, me).wait_recv()
            for j, chip in enumerate(chips):
                for b in range(nb):
                    copy(4 + j, b, (*chip, 1 - pc), me).wait_recv()
            for cp in first + passed:
                cp.wait_send()
            for cp in mine:
                cp.wait()

    res = pl.pallas_call(
        body, name="rms_mix_all_gather", grid=(nt,),
        in_specs=[pl.BlockSpec((tm, d), lambda i: (i, 0)), pl.BlockSpec((1, d), lambda i: (0, 0))]
        + [any_spec] * nb,
        out_specs=[pl.BlockSpec((tm, d), lambda i: (i, 0)), pl.BlockSpec((d, tm), lambda i: (0, i))]
        + [any_spec] * nb,
        out_shape=[jax.ShapeDtypeStruct((t, d), BF), jax.ShapeDtypeStruct((d, t), BF)]
        + [jax.ShapeDtypeStruct((N_DEV,) + s.shape, s.dtype) for s in srcs],
        scratch_shapes=[pltpu.SemaphoreType.DMA((nb, 7)), pltpu.SemaphoreType.DMA((nb, 7)),
                        pltpu.SemaphoreType.DMA((nb,))],
        compiler_params=_cparams(("arbitrary",)),
    )(x, g, *srcs)
    return res[0], res[1], list(res[2:])


def _comm_call(comm, *, name):
    any_spec = pl.BlockSpec(memory_space=pl.ANY)

    def body(*refs):
        start, wait = comm.ops(refs[:comm.n], refs[comm.n:2 * comm.n], *refs[2 * comm.n:])
        start()
        wait()

    return pl.pallas_call(
        body, name=name, in_specs=[any_spec] * comm.n, out_specs=[any_spec] * comm.n,
        out_shape=comm.out_shapes, scratch_shapes=comm.scratch,
    )(*comm.srcs)


LAYOUT_ROWS = 128
N_GLOB_BLK = 113
N_TAIL_BLK = 17


def _lane(tr):
    return lax.broadcasted_iota(jnp.int32, (tr, 128), 1)


def _assemble_w(g_win):
    d = g_win.shape[1]
    tr = LAYOUT_ROWS

    def body(win_ref, wm_ref, wg_ref):
        lane = _lane(tr)

        def shifted(k, j):
            cur = win_ref[k, :, 128 * j:128 * (j + 1)].astype(F32)
            if k == 0:
                return cur
            cur = pltpu.roll(cur, k, 1)
            if j == 0:
                return jnp.where(lane < k, 0.0, cur)
            prev = win_ref[k, :, 128 * (j - 1):128 * j].astype(F32)
            return jnp.where(lane < k, pltpu.roll(prev, k, 1), cur)

        def glob(gb):
            k = min(gb // 14, N_DEV - 1)
            j = gb - 14 * k
            v = shifted(k, j)
            if j == 0 and k > 0:
                v = v + shifted(k - 1, 14)
            return v

        for c in range(64):
            wm_ref[:, 128 * c:128 * (c + 1)] = glob(32 + c).astype(BF)
        for c in range(32):
            wm_ref[:, 8192 + 128 * c:8192 + 128 * (c + 1)] = glob(c).astype(BF)
        tail = [glob(96 + j) for j in range(N_TAIL_BLK)]
        for j in range(N_TAIL_BLK):
            v = jnp.where(lane < 120, pltpu.roll(tail[j], 120, 1),
                          pltpu.roll(tail[(j + 1) % N_TAIL_BLK], 120, 1)).astype(BF)
            if j < 16:
                wm_ref[:, 12288 + 128 * j:12288 + 128 * (j + 1)] = v
            else:
                wg_ref[...] = v

    return pl.pallas_call(
        body, name="assemble_w", grid=(d // tr,),
        in_specs=[pl.BlockSpec((N_DEV, tr, WIN_W), lambda i: (0, i, 0))],
        out_specs=[pl.BlockSpec((tr, N_MAIN), lambda i: (i, 0)), pl.BlockSpec((tr, 128), lambda i: (i, 0))],
        out_shape=[jax.ShapeDtypeStruct((d, N_MAIN), BF), jax.ShapeDtypeStruct((d, 128), BF)],
        compiler_params=_cparams(("parallel",)),
    )(g_win)


def _pack_windows(dw_main, dw_gate):
    d = dw_main.shape[0]
    tr = LAYOUT_ROWS

    def body(dm_ref, dg_ref, o_ref):
        lane = _lane(tr)

        def main_tail(j):
            return (dm_ref[:, 12288 + 128 * j:12288 + 128 * (j + 1)] if j < 16 else dg_ref[...]).astype(F32)

        def glob(gb):
            if gb >= N_GLOB_BLK:
                return jnp.zeros((tr, 128), F32)
            if gb < 32:
                return dm_ref[:, 8192 + 128 * gb:8192 + 128 * (gb + 1)].astype(F32)
            if gb < 96:
                return dm_ref[:, 128 * (gb - 32):128 * (gb - 31)].astype(F32)
            j = gb - 96
            return jnp.where(lane < 8, pltpu.roll(main_tail((j - 1) % N_TAIL_BLK), 8, 1),
                             pltpu.roll(main_tail(j), 8, 1))

        for j in range(N_DEV):
            cur = glob(14 * j)
            for m in range(WIN_W // 128):
                nxt = glob(14 * j + m + 1)
                if j == 0:
                    v = cur
                else:
                    v = jnp.where(lane < 128 - j, pltpu.roll(cur, 128 - j, 1), pltpu.roll(nxt, 128 - j, 1))
                o_ref[j, :, 128 * m:128 * (m + 1)] = v.astype(BF)
                cur = nxt

    return pl.pallas_call(
        body, name="pack_windows", grid=(d // tr,),
        in_specs=[pl.BlockSpec((tr, N_MAIN), lambda i: (i, 0)), pl.BlockSpec((tr, 128), lambda i: (i, 0))],
        out_specs=pl.BlockSpec((N_DEV, tr, WIN_W), lambda i: (0, i, 0)),
        out_shape=jax.ShapeDtypeStruct((N_DEV, d, WIN_W), BF),
        compiler_params=_cparams(("parallel",)),
    )(dw_main, dw_gate)


def _sum_slots(recv, *, name, tr):
    _, r, cdim = recv.shape
    tr = min(tr, r)

    def body(r_ref, o_ref):
        total = r_ref[0].astype(F32)
        for s in range(1, N_DEV):
            total = total + r_ref[s].astype(F32)
        o_ref[...] = total

    return pl.pallas_call(
        body, name=name, grid=(r // tr,),
        in_specs=[pl.BlockSpec((N_DEV, tr, cdim), lambda i: (0, i, 0))],
        out_specs=pl.BlockSpec((tr, cdim), lambda i: (i, 0)),
        out_shape=jax.ShapeDtypeStruct((r, cdim), F32),
        compiler_params=_cparams(("parallel",)),
    )(recv)


def _adamw(w, g, m, v, *, name):
    lead = w.ndim - 2
    r, cdim = w.shape[-2:]
    if r % 128 == 0:
        tr, tc = 128, cdim
    elif r >= 128 and cdim % 128 == 0:
        tr, tc = r, 128
    else:
        tr, tc = r, cdim
    c1 = 1.0 - ADAM_B1 ** ADAM_STEP
    c2 = 1.0 - ADAM_B2 ** ADAM_STEP

    def body(w_ref, g_ref, m_ref, v_ref, d_ref, mo_ref, vo_ref):
        gv = g_ref[...]
        mn = ADAM_B1 * m_ref[...] + (1.0 - ADAM_B1) * gv
        vn = ADAM_B2 * v_ref[...] + (1.0 - ADAM_B2) * (gv * gv)
        d_ref[...] = -ADAM_LR * ((mn / c1) / (jnp.sqrt(vn / c2) + ADAM_EPS) + ADAM_WD * w_ref[...])
        mo_ref[...] = mn
        vo_ref[...] = vn

    blk = pl.BlockSpec((1,) * lead + (tr, tc), lambda i, j: (0,) * lead + (i, j))
    shp = jax.ShapeDtypeStruct(w.shape, F32)
    return pl.pallas_call(
        body, name=name, grid=(r // tr, cdim // tc),
        in_specs=[blk] * 4, out_specs=[blk] * 3, out_shape=[shp] * 3,
        compiler_params=_cparams(("parallel", "parallel")),
    )(w, g, m, v)


def kernel(x, p, g_mix, w_in, conv_w, conv_b, w_a_out, b_gates, g_head, w_b_out, w_o, g_ple, w_ple_gate, w_ple, g_final, loss_target, m_g_mix, m_w_in, m_conv_w, m_conv_b, m_w_a_out, m_b_gates, m_g_head, m_w_b_out, m_w_o, m_g_ple, m_w_ple_gate, m_w_ple, m_g_final, v_g_mix, v_w_in, v_conv_w, v_conv_b, v_w_a_out, v_b_gates, v_g_head, v_w_b_out, v_w_o, v_g_ple, v_w_ple_gate, v_w_ple, v_g_final):
    d = D_MODEL
    t = x.shape[1]
    x2d = x.reshape(t, d)
    p2d = p.reshape(t, PLE_DIM)
    tgt = loss_target.reshape(t, d)

    win = jnp.pad(w_in[0].astype(BF), ((0, 0), (0, WIN_W - SHARD_W)))
    rows = jnp.concatenate([w_a_out[0].astype(BF), w_b_out[0].astype(BF), w_o[0].astype(BF),
                            w_ple_gate[0].astype(BF), w_ple[0].astype(BF).reshape(32, d)], axis=0)
    cfl = jnp.pad(conv_w[0], ((0, 5), (0, 0)))
    hn, hnt, (g_win, g_cf) = _rms_fwd_all_gather(x2d, g_mix, [win, cfl])

    w_main, w_gate = _assemble_w(g_win)
    conv_w8 = jnp.pad(g_cf[:, :3, :].transpose(1, 0, 2).reshape(3, d), ((0, 5), (0, 0)))
    gate_bias = jnp.pad(b_gates, ((0, 0), (IG_LANE, 0)))

    proj, gates, (g_rows,) = _mm(hn, w_main, form="nn", out_dtype=BF, name="proj", tm=2048,
                                 extra=(w_gate, F32), comm=_DirectComm([rows], "gather"))
    w_a = g_rows[:, 0:128].reshape(d, d)
    w_b = g_rows[:, 128:384].reshape(V_DIM, d)
    w_of = g_rows[:, 384:512].reshape(d, d)
    w_pg = g_rows[:, 512:640].reshape(d, d)
    w_pl = g_rows[:, 640:672].reshape(N_DEV, PLE_DIM, 128).transpose(1, 0, 2).reshape(PLE_DIM, d)
    ya_pre = _branch_a_fwd(proj, conv_w8, conv_b)
    yb_pre, yb, h_raw, c_states, aux = _mlstm_fwd(proj, gates, gate_bias, g_head, w_b)
    dx1, dyb, dya_pre, dproj, small_fin, dw_a, dw_o, dw_pg, dw_pl = _token_chain(
        ya_pre, w_a, yb, proj, w_of, x2d, g_ple, w_pg, p2d, w_pl, tgt, g_final.reshape(1, d))

    dproj, dconv = _branch_a_bwd(dproj, proj, dya_pre, conv_w8, conv_b)
    dproj, dgates, dbias, dg_head, dw_b = _mlstm_bwd(dproj, proj, gates, gate_bias, g_head, h_raw, c_states, aux,
                                                     yb_pre, dyb, w_b)
    s_rows = jnp.concatenate([
        dw_a.reshape(N_DEV, 128, d), dw_b.reshape(N_DEV, 256, d), dw_o.reshape(N_DEV, 128, d),
        dw_pg.reshape(N_DEV, 128, d),
        dw_pl.reshape(PLE_DIM, N_DEV, 128).transpose(1, 0, 2).reshape(N_DEV, 32, d)], axis=1)
    dw_main, dw_gate, (r_rows,) = _mm(hnt, dproj, form="nn", out_dtype=BF, name="dw_main", tk=2048,
                                      extra=(dgates, BF), comm=_DirectComm([s_rows], "exchange"))
    s_win = _pack_windows(dw_main, dw_gate)
    grad_x, dg_mix, (r_win,) = _dhn_mix_bwd(dproj, w_main, dgates, w_gate, x2d, g_mix, dx1,
                                            _DirectComm([s_win], "exchange"))

    vec = jnp.concatenate([dg_mix[0], dconv[3], dg_head[0], small_fin[2], small_fin[1],
                           dbias[0, IG_LANE:], jnp.sum(small_fin[0]).reshape(1),
                           jnp.zeros((7 * d - 6153,), F32)]).reshape(7, d)
    conv_part = jnp.pad(dconv[:3].reshape(3, N_DEV, 128).transpose(1, 0, 2).reshape(N_DEV, 1, 384),
                        ((0, 0), (0, 0), (0, d - 384)))
    s_f32 = jnp.concatenate([jnp.broadcast_to(vec[None], (N_DEV, 7, d)), conv_part], axis=1)
    (r_f32,) = _comm_call(_DirectComm([s_f32], "exchange"), name="small_grads_exchange")
    sum_win = _sum_slots(r_win, name="sum_win", tr=128)
    sum_rows = _sum_slots(r_rows, name="sum_rows", tr=96)
    sum_f32 = _sum_slots(r_f32, name="sum_f32", tr=8)

    g_w_in = sum_win[:, :SHARD_W]
    g_w_a = sum_rows[0:128]
    g_w_b = sum_rows[128:384]
    g_w_o = sum_rows[384:512]
    g_w_pg = sum_rows[512:640]
    g_w_pl = sum_rows[640:672].reshape(PLE_DIM, 128)
    vsum = sum_f32[:7].reshape(7 * d)
    g_g_mix = vsum[0:1024].reshape(1, d)
    g_conv_b = vsum[1024:2048].reshape(1, d)
    g_g_head = vsum[2048:4096].reshape(1, V_DIM)
    g_g_ple = vsum[4096:5120].reshape(1, d)
    g_g_final = vsum[5120:6144].reshape(1, d)
    g_b_gates = vsum[6144:6152].reshape(1, 8)
    g_conv_w = sum_f32[7, :384].reshape(3, 128)

    loss = vsum[6152]

    names = ["g_mix", "w_in", "conv_w", "conv_b", "w_a_out", "b_gates", "g_head", "w_b_out", "w_o", "g_ple",
             "w_ple_gate", "w_ple", "g_final"]
    weights = [g_mix, w_in, conv_w, conv_b, w_a_out, b_gates, g_head, w_b_out, w_o, g_ple, w_ple_gate, w_ple,
               g_final]
    moms = [m_g_mix, m_w_in, m_conv_w, m_conv_b, m_w_a_out, m_b_gates, m_g_head, m_w_b_out, m_w_o, m_g_ple,
            m_w_ple_gate, m_w_ple, m_g_final]
    vels = [v_g_mix, v_w_in, v_conv_w, v_conv_b, v_w_a_out, v_b_gates, v_g_head, v_w_b_out, v_w_o, v_g_ple,
            v_w_ple_gate, v_w_ple, v_g_final]
    grads2d = [g_g_mix, g_w_in, g_conv_w, g_conv_b, g_w_a, g_b_gates, g_g_head, g_w_b, g_w_o, g_g_ple, g_w_pg,
               g_w_pl, g_g_final]
    grads, deltas, new_m, new_v = [], [], [], []
    for nm, w, m_, v_, g2 in zip(names, weights, moms, vels, grads2d):
        shp = w.shape
        if nm == "w_in":
            dl, mn, vn = _adamw(w[0].T, g2.T, m_[0].T, v_[0].T, name="adamw_" + nm)
            grads.append(g2.reshape(shp))
            deltas.append(dl.T.reshape(shp))
            new_m.append(mn.T.reshape(shp))
            new_v.append(vn.T.reshape(shp))
            continue
        kshp = shp if w.ndim >= 2 else (1,) + shp
        gk = g2.reshape(kshp)
        dl, mn, vn = _adamw(w.reshape(kshp), gk, m_.reshape(kshp), v_.reshape(kshp), name="adamw_" + nm)
        grads.append(gk.reshape(shp))
        deltas.append(dl.reshape(shp))
        new_m.append(mn.reshape(shp))
        new_v.append(vn.reshape(shp))
    return (loss, grad_x.reshape(x.shape), *grads, *deltas, *new_m, *new_v)
```

```python
import functools

import jax
import jax.numpy as jnp
from jax import lax
from jax.experimental import pallas as pl
from jax.experimental.pallas import tpu as pltpu

F32 = jnp.float32
BF = jnp.bfloat16

D_MODEL = 1024
N_HEADS = 4
DK = 256
DV = 512
V_DIM = 2048
PLE_DIM = 256
N_IN = 14344
N_MAIN = 14336
EPS = 1e-6
QK_SCALE = DK ** -0.5
NEG = -1e30
N_DEV = 8
SHARD_W = 1793
WIN_STRIDE = 1792
WIN_W = 1920
ROWS_PACK = 672
_CHUNK = 256
IG_LANE = 120
FG_LANE = 124

ADAM_LR = 0.001
ADAM_B1 = 0.9
ADAM_B2 = 0.999
ADAM_EPS = 1e-08
ADAM_WD = 0.01
ADAM_STEP = 10

VMEM_LIMIT = 56 * 1024 * 1024
MESH = pl.DeviceIdType.MESH


def _cparams(sem):
    return pltpu.CompilerParams(dimension_semantics=sem, vmem_limit_bytes=VMEM_LIMIT)


def _sigmoid(x):
    return 0.5 * jnp.tanh(0.5 * x) + 0.5


def _log_sigmoid(x):
    return jnp.minimum(x, 0.0) - jnp.log(1.0 + jnp.exp(-jnp.abs(x)))


def _dot(a, b):
    return jnp.dot(a, b, preferred_element_type=F32)


def _dot_nt(a, b):
    return lax.dot_general(a, b, (((1,), (1,)), ((), ())), preferred_element_type=F32)


def _dot_tn(a, b):
    return lax.dot_general(a, b, (((0,), (0,)), ((), ())), preferred_element_type=F32)


class _DirectComm:
    def __init__(self, srcs, kind):
        self.srcs = list(srcs)
        self.kind = kind
        self.n = len(self.srcs)
        if kind == "exchange":
            self.out_shapes = [jax.ShapeDtypeStruct(s.shape, s.dtype) for s in self.srcs]
        else:
            self.out_shapes = [jax.ShapeDtypeStruct((N_DEV,) + s.shape, s.dtype) for s in self.srcs]
        self.scratch = [pltpu.SemaphoreType.DMA((self.n, 7)), pltpu.SemaphoreType.DMA((self.n, 7)),
                        pltpu.SemaphoreType.DMA((self.n,))]

    def ops(self, src, dst, send_sems, recv_sems, local_sems):
        exchange = self.kind == "exchange"

        def descriptors():
            x, y, c = _position()
            me_lin = 4 * x + 2 * y + c
            local = [pltpu.make_async_copy(src[b].at[me_lin] if exchange else src[b], dst[b].at[me_lin],
                                           local_sems.at[b]) for b in range(self.n)]
            sends, recvs = [], []
            for f in range(1, N_DEV):
                px = (1 - x) if (f >> 2) & 1 else x
                py = (1 - y) if (f >> 1) & 1 else y
                pc = (1 - c) if f & 1 else c
                peer_lin = 4 * px + 2 * py + pc
                for b in range(self.n):
                    out = src[b].at[peer_lin] if exchange else src[b]
                    common = dict(send_sem=send_sems.at[b, f - 1], recv_sem=recv_sems.at[b, f - 1],
                                  device_id=(px, py, pc), device_id_type=MESH)
                    sends.append(pltpu.make_async_remote_copy(src_ref=out, dst_ref=dst[b].at[me_lin], **common))
                    recvs.append(pltpu.make_async_remote_copy(src_ref=out, dst_ref=dst[b].at[peer_lin], **common))
            return local, sends, recvs

        def start():
            local, sends, _ = descriptors()
            for cp in local + sends:
                cp.start()

        def wait():
            local, sends, recvs = descriptors()
            for cp in recvs:
                cp.wait_recv()
            for cp in sends:
                cp.wait_send()
            for cp in local:
                cp.wait()

        return start, wait


def _mm(a, b, *, form, out_dtype, name, tm=1024, tn=1024, tk=1024, add=None, extra=None, comm=None):
    assert extra is None or form == "nn"
    if form == "nn":
        m, kc = a.shape
        n = b.shape[1]
    elif form == "nt":
        m, kc = a.shape
        n = b.shape[0]
    else:
        kc, m = a.shape
        n = b.shape[1]
    tm, tn, tk = min(tm, m), min(tn, n), min(tk, kc)
    assert m % tm == 0 and n % tn == 0 and kc % tk == 0, (name, a.shape, b.shape)
    nk = kc // tk
    if form == "tn":
        a_spec = pl.BlockSpec((tk, tm), lambda i, j, k: (k, i))
    else:
        a_spec = pl.BlockSpec((tm, tk), lambda i, j, k: (i, k))
    if form == "nt":
        b_spec = pl.BlockSpec((tn, tk), lambda i, j, k: (j, k))
    else:
        b_spec = pl.BlockSpec((tk, tn), lambda i, j, k: (k, j))
    o_spec = pl.BlockSpec((tm, tn), lambda i, j, k: (i, j))
    dot = {"nn": _dot, "nt": _dot_nt, "tn": _dot_tn}[form]
    has_add = add is not None

    use_acc = nk > 1 and (has_add or out_dtype != F32)
    has_x = extra is not None
    n2 = extra[0].shape[1] if has_x else 0
    use_acc2 = has_x and nk > 1 and extra[1] != F32
    n_comm = comm.n if comm else 0
    n_in = 2 + int(has_add) + int(has_x) + n_comm
    n_out = 1 + int(has_x) + n_comm
    grid = (m // tm, n // tn, nk)

    def body(*refs):
        a_ref, b_ref = refs[0], refs[1]
        add_ref = refs[2] if has_add else None
        b2_ref = refs[2 + int(has_add)] if has_x else None
        o_ref = refs[n_in]
        o2_ref = refs[n_in + 1] if has_x else None
        scr = list(refs[n_in + n_out + (3 if comm else 0):])
        acc = scr.pop(0) if use_acc else o_ref
        acc2 = scr.pop(0) if use_acc2 else o2_ref
        i, j, k = pl.program_id(0), pl.program_id(1), pl.program_id(2)
        if comm:
            start, wait = comm.ops(refs[n_in - n_comm:n_in], refs[n_in + n_out - n_comm:n_in + n_out],
                                   *refs[n_in + n_out:n_in + n_out + 3])
            pl.when((i == 0) & (j == 0) & (k == 0))(start)

        def part():
            return dot(a_ref[...].astype(BF), b_ref[...].astype(BF))

        def part2():
            return dot(a_ref[...].astype(BF), b2_ref[...].astype(BF))

        def finish(total):
            if has_add:
                total = total + add_ref[...].astype(F32)
            o_ref[...] = total.astype(out_dtype)

        if nk == 1:
            finish(part())
            if has_x:
                @pl.when(j == 0)
                def _():
                    o2_ref[...] = part2().astype(extra[1])
        else:
            @pl.when(k == 0)
            def _():
                acc[...] = part()

            @pl.when(k > 0)
            def _():
                acc[...] += part()

            if use_acc:
                @pl.when(k == nk - 1)
                def _():
                    finish(acc[...])
            if has_x:
                @pl.when((j == 0) & (k == 0))
                def _():
                    acc2[...] = part2()

                @pl.when((j == 0) & (k > 0))
                def _():
                    acc2[...] += part2()

                if use_acc2:
                    @pl.when((j == 0) & (k == nk - 1))
                    def _():
                        o2_ref[...] = acc2[...].astype(extra[1])
        if comm:
            pl.when((i == grid[0] - 1) & (j == grid[1] - 1) & (k == nk - 1))(wait)

    any_spec = pl.BlockSpec(memory_space=pl.ANY)
    o2_spec = pl.BlockSpec((tm, n2), lambda i, j, k: (i, 0))
    in_specs = ([a_spec, b_spec] + ([o_spec] if has_add else [])
                + ([pl.BlockSpec((tk, n2), lambda i, j, k: (k, 0))] if has_x else []) + [any_spec] * n_comm)
    args = (a, b) + ((add,) if has_add else ()) + ((extra[0],) if has_x else ()) + (tuple(comm.srcs) if comm else ())
    out_specs = [o_spec] + ([o2_spec] if has_x else []) + [any_spec] * n_comm
    out_shape = ([jax.ShapeDtypeStruct((m, n), out_dtype)]
                 + ([jax.ShapeDtypeStruct((m, n2), extra[1])] if has_x else []) + (comm.out_shapes if comm else []))
    scratch = ((comm.scratch if comm else []) + ([pltpu.VMEM((tm, tn), F32)] if use_acc else [])
               + ([pltpu.VMEM((tm, n2), F32)] if use_acc2 else []))
    if comm:
        sem = ("arbitrary",) * 3
    else:
        sem = ("parallel", "arbitrary" if has_x else "parallel", "arbitrary")
    res = pl.pallas_call(
        body, name=name, grid=grid, in_specs=in_specs, out_specs=out_specs, out_shape=out_shape,
        scratch_shapes=scratch, compiler_params=_cparams(sem),
    )(*args)
    if not (comm or has_x):
        return res[0]
    return tuple(res[:1 + int(has_x)]) + ((list(res[1 + int(has_x):]),) if comm else ())


HALO = 16
XA_BLK, BA_BLK, CA_BLK, ZA_BLK = 8, 9, 10, 11


def _shift_down(u, prev, n):
    return pltpu.roll(jnp.concatenate([prev, u], axis=0), n, 0)[HALO:]


def _shift_up(u, nxt, n):
    tm = u.shape[0]
    return pltpu.roll(jnp.concatenate([u, nxt], axis=0), tm + HALO - n, 0)[:tm]


def _branch_a_fwd(proj, conv_w8, conv_b, *, tm=512):
    t = proj.shape[0]
    d = D_MODEL
    tm = min(tm, t)
    hb = tm // HALO

    def body(xa_ref, ba_ref, ca_ref, za_ref, xap_ref, cap_ref, w_ref, b_ref, o_ref):
        i = pl.program_id(0)
        u = ca_ref[...].astype(F32) * xa_ref[...].astype(F32)
        up = cap_ref[...].astype(F32) * xap_ref[...].astype(F32)
        up = jnp.where(i == 0, 0.0, up)
        u1 = _shift_down(u, up, 1)
        u2 = _shift_down(u, up, 2)
        cv = w_ref[0:1, :] * u2 + w_ref[1:2, :] * u1 + w_ref[2:3, :] * u + b_ref[...]
        za = za_ref[...].astype(F32)
        o_ref[...] = (ba_ref[...].astype(F32) * cv * (za * _sigmoid(za))).astype(BF)

    def col(blk):
        return pl.BlockSpec((tm, d), lambda i: (i, blk))

    def prev(blk):
        return pl.BlockSpec((HALO, d), lambda i: (jnp.maximum(i * hb - 1, 0), blk))

    return pl.pallas_call(
        body, name="branch_a_fwd", grid=(t // tm,),
        in_specs=[col(XA_BLK), col(BA_BLK), col(CA_BLK), col(ZA_BLK), prev(XA_BLK), prev(CA_BLK),
                  pl.BlockSpec((8, d), lambda i: (0, 0)), pl.BlockSpec((1, d), lambda i: (0, 0))],
        out_specs=pl.BlockSpec((tm, d), lambda i: (i, 0)),
        out_shape=jax.ShapeDtypeStruct((t, d), BF),
        compiler_params=_cparams(("parallel",)),
    )(proj, proj, proj, proj, proj, proj, conv_w8, conv_b)


def _branch_a_bwd(dproj, proj, dya_pre, conv_w8, conv_b, *, tm=512):
    t = proj.shape[0]
    d = D_MODEL
    tm = min(tm, t)
    hb = tm // HALO
    nt = t // tm

    def body(dp_ref, xa_ref, ba_ref, ca_ref, za_ref, xap_ref, cap_ref, dy_ref, ban_ref, zan_ref, dyn_ref,
             w_ref, b_ref, o_ref, dc_ref):
        del dp_ref
        i = pl.program_id(0)
        xa = xa_ref[...].astype(F32)
        ca = ca_ref[...].astype(F32)
        ba = ba_ref[...].astype(F32)
        za = za_ref[...].astype(F32)
        u = ca * xa
        up = cap_ref[...].astype(F32) * xap_ref[...].astype(F32)
        up = jnp.where(i == 0, 0.0, up)
        u1 = _shift_down(u, up, 1)
        u2 = _shift_down(u, up, 2)
        w0, w1, w2 = w_ref[0:1, :], w_ref[1:2, :], w_ref[2:3, :]
        cv = w0 * u2 + w1 * u1 + w2 * u + b_ref[...]
        sg = _sigmoid(za)
        sz = za * sg
        dy = dy_ref[...].astype(F32)
        dcv = dy * ba * sz
        zan = zan_ref[...].astype(F32)
        dcvn = dyn_ref[...].astype(F32) * ban_ref[...].astype(F32) * (zan * _sigmoid(zan))
        dcvn = jnp.where(i == nt - 1, 0.0, dcvn)
        du = w2 * dcv + w1 * _shift_up(dcv, dcvn, 1) + w0 * _shift_up(dcv, dcvn, 2)
        o_ref[:, 0:d] = (du * ca).astype(BF)
        o_ref[:, d:2 * d] = (dy * cv * sz).astype(BF)
        o_ref[:, 2 * d:3 * d] = (du * xa).astype(BF)
        o_ref[:, 3 * d:4 * d] = (dy * ba * cv * sg * (1.0 + za * (1.0 - sg))).astype(BF)

        @pl.when(i == 0)
        def _():
            dc_ref[...] = jnp.zeros_like(dc_ref)

        dc_ref[0:1, :] += jnp.sum(dcv * u2, axis=0, keepdims=True)
        dc_ref[1:2, :] += jnp.sum(dcv * u1, axis=0, keepdims=True)
        dc_ref[2:3, :] += jnp.sum(dcv * u, axis=0, keepdims=True)
        dc_ref[3:4, :] += jnp.sum(dcv, axis=0, keepdims=True)

    def col(blk):
        return pl.BlockSpec((tm, d), lambda i: (i, blk))

    def prev(blk):
        return pl.BlockSpec((HALO, d), lambda i: (jnp.maximum(i * hb - 1, 0), blk))

    def nxt(blk):
        return pl.BlockSpec((HALO, d), lambda i: (jnp.minimum((i + 1) * hb, t // HALO - 1), blk))

    return pl.pallas_call(
        body, name="branch_a_bwd", grid=(nt,),
        in_specs=[pl.BlockSpec(memory_space=pl.ANY),
                  col(XA_BLK), col(BA_BLK), col(CA_BLK), col(ZA_BLK), prev(XA_BLK), prev(CA_BLK),
                  pl.BlockSpec((tm, d), lambda i: (i, 0)), nxt(BA_BLK), nxt(ZA_BLK),
                  pl.BlockSpec((HALO, d), lambda i: (jnp.minimum((i + 1) * hb, t // HALO - 1), 0)),
                  pl.BlockSpec((8, d), lambda i: (0, 0)), pl.BlockSpec((1, d), lambda i: (0, 0))],
        out_specs=[pl.BlockSpec((tm, 4 * d), lambda i: (i, 2)), pl.BlockSpec((8, d), lambda i: (0, 0))],
        out_shape=[jax.ShapeDtypeStruct(dproj.shape, BF), jax.ShapeDtypeStruct((8, d), F32)],
        input_output_aliases={0: 0},
        compiler_params=_cparams(("arbitrary",)),
    )(dproj, proj, proj, proj, proj, proj, proj, dya_pre, proj, proj, dya_pre, conv_w8, conv_b)


def _gate_vectors(gc, gt, h, lane_i, sub_i):
    ig_c = jnp.sum(jnp.where(lane_i == IG_LANE + h, gc, 0.0), axis=1, keepdims=True)
    fg_c = jnp.sum(jnp.where(lane_i == FG_LANE + h, gc, 0.0), axis=1, keepdims=True)
    ig_r = jnp.sum(jnp.where(sub_i == IG_LANE + h, gt, 0.0), axis=0, keepdims=True)
    fg_r = jnp.sum(jnp.where(sub_i == FG_LANE + h, gt, 0.0), axis=0, keepdims=True)
    return ig_c, fg_c, ig_r, fg_r


def _chunk_common(q, k, ig_c, fg_c, ig_r, fg_r, m_prev, n_prev, row, col):
    lf_c = _log_sigmoid(fg_c)
    lf_r = _log_sigmoid(fg_r)
    causal = col <= row
    b_c = jnp.sum(jnp.where(causal, lf_r, 0.0), axis=1, keepdims=True)
    b_r = jnp.sum(jnp.where(row <= col, lf_c, 0.0), axis=0, keepdims=True)
    dmat = jnp.where(causal, b_c - b_r + ig_r, NEG)
    a = b_c + m_prev
    m_row = jnp.maximum(a, jnp.max(dmat, axis=1, keepdims=True))
    est = jnp.exp(dmat - m_row)
    s = _dot_nt(q, k) * QK_SCALE * est
    inter = jnp.exp(a - m_row)
    den = jnp.sum(s, axis=1, keepdims=True) + inter * QK_SCALE * jnp.sum(
        q.astype(F32) * n_prev, axis=1, keepdims=True)
    expm = jnp.exp(-m_row)
    mx = jnp.maximum(jnp.abs(den), expm)
    b_last = jnp.sum(lf_r, axis=1, keepdims=True)
    g_r = b_last - b_r + ig_r
    g_c = b_last - b_c + ig_c
    m_new = jnp.maximum(b_last + m_prev, jnp.max(g_r, axis=1, keepdims=True))
    w_c = jnp.exp(g_c - m_new)
    decay = jnp.exp(b_last + m_prev - m_new)
    return est, s, inter, den, expm, mx, m_new, w_c, decay


def _mlstm_fwd(proj, gates, gate_bias, g_head, w_b):
    t = proj.shape[0]
    lc = min(_CHUNK, t)
    nc = t // lc

    def body(q_ref, k_ref, v_ref, o_ref, z_ref, g_ref, gb_ref, gh_ref, wb_ref,
             yb_ref, ybo_ref, hr_ref, cs_ref, aux_ref, c_scr, nm_scr):
        c = pl.program_id(0)

        @pl.when(c == 0)
        def _():
            c_scr[...] = jnp.zeros_like(c_scr)
            nm_scr[...] = jnp.zeros_like(nm_scr)
            nm_scr[:, 1:2, :] = jnp.full((N_HEADS, 1, DK), NEG, F32)

        row = lax.broadcasted_iota(jnp.int32, (lc, lc), 0)
        col = lax.broadcasted_iota(jnp.int32, (lc, lc), 1)
        gc = g_ref[...] + gb_ref[...]
        gt = gc.T
        lane_i = lax.broadcasted_iota(jnp.int32, gc.shape, 1)
        sub_i = lax.broadcasted_iota(jnp.int32, gt.shape, 0)
        for h in range(N_HEADS):
            ks = slice(h * DK, (h + 1) * DK)
            vs = slice(h * DV, (h + 1) * DV)
            q = q_ref[:, ks]
            k = k_ref[:, ks]
            v = v_ref[:, vs]
            ig_c, fg_c, ig_r, fg_r = _gate_vectors(gc, gt, h, lane_i, sub_i)
            n_prev = nm_scr[h, 0:1, :]
            m_prev = nm_scr[h, 1:2, 0:1]
            est, s, inter, den, expm, mx, m_new, w_c, decay = _chunk_common(
                q, k, ig_c, fg_c, ig_r, fg_r, m_prev, n_prev, row, col)
            c_prev = c_scr[h]
            c_prev_b = c_prev.astype(BF)
            num = _dot(s.astype(BF), v) + (inter * QK_SCALE) * _dot(q, c_prev_b)
            hh = num / mx
            r = lax.rsqrt(jnp.mean(hh * hh, axis=1, keepdims=True) + EPS)
            hbn = hh * r * gh_ref[:, vs]
            o = o_ref[:, vs].astype(F32)
            z = z_ref[:, vs].astype(F32)
            yb_ref[:, vs] = (_sigmoid(o) * hbn * (z * _sigmoid(z))).astype(BF)
            hr_ref[:, vs] = hh.astype(BF)
            cs_ref[0, h] = c_prev_b
            aux_ref[0, h] = nm_scr[h]
            kw = k.astype(F32) * w_c
            c_scr[h] = decay * c_prev + _dot_tn(kw.astype(BF), v)
            nm_scr[h, 0:1, :] = decay * n_prev + jnp.sum(kw, axis=0, keepdims=True)
            nm_scr[h, 1:2, :] = jnp.broadcast_to(m_new, (1, DK))
        ybo_ref[...] = _dot(yb_ref[...], wb_ref[...]).astype(BF)

    return pl.pallas_call(
        body, name="mlstm_fwd", grid=(nc,),
        in_specs=[pl.BlockSpec((lc, 1024), lambda c: (c, 0)),
                  pl.BlockSpec((lc, 1024), lambda c: (c, 1)),
                  pl.BlockSpec((lc, 2048), lambda c: (c, 1)),
                  pl.BlockSpec((lc, 2048), lambda c: (c, 2)),
                  pl.BlockSpec((lc, 2048), lambda c: (c, 3)),
                  pl.BlockSpec((lc, 128), lambda c: (c, 0)),
                  pl.BlockSpec((1, 128), lambda c: (0, 0)),
                  pl.BlockSpec((1, V_DIM), lambda c: (0, 0)),
                  pl.BlockSpec((V_DIM, D_MODEL), lambda c: (0, 0))],
        out_specs=[pl.BlockSpec((lc, V_DIM), lambda c: (c, 0)),
                   pl.BlockSpec((lc, D_MODEL), lambda c: (c, 0)),
                   pl.BlockSpec((lc, V_DIM), lambda c: (c, 0)),
                   pl.BlockSpec((1, N_HEADS, DK, DV), lambda c: (c, 0, 0, 0)),
                   pl.BlockSpec((1, N_HEADS, 8, DK), lambda c: (c, 0, 0, 0))],
        out_shape=[jax.ShapeDtypeStruct((t, V_DIM), BF), jax.ShapeDtypeStruct((t, D_MODEL), BF),
                   jax.ShapeDtypeStruct((t, V_DIM), BF),
                   jax.ShapeDtypeStruct((nc, N_HEADS, DK, DV), BF),
                   jax.ShapeDtypeStruct((nc, N_HEADS, 8, DK), F32)],
        scratch_shapes=[pltpu.VMEM((N_HEADS, DK, DV), F32), pltpu.VMEM((N_HEADS, 8, DK), F32)],
        compiler_params=_cparams(("arbitrary",)),
    )(proj, proj, proj, proj, proj, gates, gate_bias, g_head, w_b)


def _mlstm_bwd(dproj, proj, gates, gate_bias, g_head, h_raw, c_states, aux, yb_pre, dyb, w_b):
    t = proj.shape[0]
    lc = min(_CHUNK, t)
    nc = t // lc

    def body(dpin_ref, q_ref, k_ref, v_ref, o_ref, z_ref, g_ref, gb_ref, gh_ref, hr_ref, cs_ref, aux_ref,
             ybp_ref, dyb_ref, wb_ref, dp_ref, dg_ref, dbias_ref, dgh_ref, dwb_ref, dc_scr, dn_scr, dy_ref,
             dwb_scr):
        del dpin_ref
        step = pl.program_id(0)

        @pl.when(step == 0)
        def _():
            dc_scr[...] = jnp.zeros_like(dc_scr)
            dn_scr[...] = jnp.zeros_like(dn_scr)
            dbias_ref[...] = jnp.zeros_like(dbias_ref)
            dgh_ref[...] = jnp.zeros_like(dgh_ref)
            dwb_scr[...] = jnp.zeros_like(dwb_scr)

        dyb = dyb_ref[...]
        dy_ref[...] = _dot_nt(dyb, wb_ref[...])
        dwb_scr[...] += _dot_tn(ybp_ref[...], dyb)

        row = lax.broadcasted_iota(jnp.int32, (lc, lc), 0)
        col = lax.broadcasted_iota(jnp.int32, (lc, lc), 1)
        eye = row == col
        gc = g_ref[...] + gb_ref[...]
        gt = gc.T
        lane_i = lax.broadcasted_iota(jnp.int32, gc.shape, 1)
        sub_i = lax.broadcasted_iota(jnp.int32, gt.shape, 0)
        dgates = jnp.zeros(gc.shape, F32)
        for h in range(N_HEADS):
            ks = slice(h * DK, (h + 1) * DK)
            vs = slice(h * DV, (h + 1) * DV)
            q = q_ref[:, ks]
            k = k_ref[:, ks]
            v = v_ref[:, vs]
            ig_c, fg_c, ig_r, fg_r = _gate_vectors(gc, gt, h, lane_i, sub_i)
            n_prev = aux_ref[0, h, 0:1, :]
            m_prev = aux_ref[0, h, 1:2, 0:1]
            c_prev_b = cs_ref[0, h]
            est, s, inter, den, expm, mx, m_new, w_c, decay = _chunk_common(
                q, k, ig_c, fg_c, ig_r, fg_r, m_prev, n_prev, row, col)
            hb = hr_ref[:, vs].astype(F32)
            dyp = dy_ref[:, vs]
            o = o_ref[:, vs].astype(F32)
            z = z_ref[:, vs].astype(F32)
            so = _sigmoid(o)
            sgz = _sigmoid(z)
            sz = z * sgz
            r = lax.rsqrt(jnp.mean(hb * hb, axis=1, keepdims=True) + EPS)
            xh = hb * r
            gh = gh_ref[:, vs]
            hbn = xh * gh
            dyso = dyp * so
            dyso_h = dyso * hbn
            d_o = dyso_h * sz * (1.0 - so)
            d_z = dyso_h * sgz * (1.0 + z * (1.0 - sgz))
            dhbn = dyso * sz
            dgh_ref[0:1, vs] += jnp.sum(dhbn * xh, axis=0, keepdims=True)
            dxh = dhbn * gh
            dh = r * (dxh - xh * jnp.mean(dxh * xh, axis=1, keepdims=True))
            dnm = dh / mx
            hd = jnp.sum(dh * hb, axis=1, keepdims=True)
            cond = jnp.abs(den) > expm
            dden = jnp.where(cond, -hd / mx * jnp.sign(den), 0.0)
            dnm_b = dnm.astype(BF)
            p = _dot_nt(dnm_b, v) + dden
            dqk = (p * est * QK_SCALE).astype(BF)
            dq_inter = (inter * QK_SCALE) * (_dot_nt(dnm_b, c_prev_b) + dden * n_prev)
            dq = _dot(dqk, k) + dq_inter
            dc_new = dc_scr[h]
            dc_new_b = dc_new.astype(BF)
            dn_new = dn_scr[h, 0:1, :]
            kf = k.astype(F32)
            dk_state = w_c * (_dot_nt(v, dc_new_b) + dn_new)
            dk = _dot_tn(dqk, q) + dk_state
            dv = _dot_tn(s.astype(BF), dnm_b) + w_c * _dot(k, dc_new_b)
            pairs = p * s
            to_next_c = jnp.sum(kf * dk_state, axis=1, keepdims=True)
            dli_c = jnp.sum(jnp.where(eye, jnp.sum(pairs, axis=0, keepdims=True), 0.0), axis=1,
                            keepdims=True) + to_next_c
            hmat = _dot(pairs.astype(BF), (row < col).astype(BF))
            from_prev_c = jnp.sum(q.astype(F32) * dq_inter, axis=1, keepdims=True)
            through = decay * (
                jnp.sum(jnp.sum(dc_new * c_prev_b.astype(F32), axis=1, keepdims=True), axis=0, keepdims=True)
                + jnp.sum(dn_new * n_prev, axis=1, keepdims=True))
            dlf_r = through + jnp.sum(jnp.where(row >= col, hmat + from_prev_c, to_next_c), axis=0, keepdims=True)
            dlf_c = jnp.sum(jnp.where(eye, dlf_r, 0.0), axis=1, keepdims=True)
            dfg_c = dlf_c / (1.0 + jnp.exp(fg_c))
            dgates = dgates + jnp.where(lane_i == IG_LANE + h, dli_c, 0.0) + jnp.where(
                lane_i == FG_LANE + h, dfg_c, 0.0)
            qi = q.astype(F32) * (inter * QK_SCALE)
            dc_scr[h] = decay * dc_new + _dot_tn(qi.astype(BF), dnm_b)
            dn_scr[h, 0:1, :] = decay * dn_new + jnp.sum(qi * dden, axis=0, keepdims=True)
            dp_ref[:, h * DK:(h + 1) * DK] = dq.astype(BF)
            dp_ref[:, 1024 + h * DK:1024 + (h + 1) * DK] = dk.astype(BF)
            dp_ref[:, 2048 + h * DV:2048 + (h + 1) * DV] = dv.astype(BF)
            dp_ref[:, 4096 + h * DV:4096 + (h + 1) * DV] = d_o.astype(BF)
            dp_ref[:, 6144 + h * DV:6144 + (h + 1) * DV] = d_z.astype(BF)
        dg_ref[...] = dgates.astype(BF)
        dbias_ref[0:1, :] += jnp.sum(dgates, axis=0, keepdims=True)

        @pl.when(step == nc - 1)
        def _():
            dwb_ref[...] = dwb_scr[...].astype(BF)

    def rev(c):
        return nc - 1 - c

    return pl.pallas_call(
        body, name="mlstm_bwd", grid=(nc,),
        input_output_aliases={0: 0},
        in_specs=[pl.BlockSpec(memory_space=pl.ANY),
                  pl.BlockSpec((lc, 1024), lambda c: (rev(c), 0)),
                  pl.BlockSpec((lc, 1024), lambda c: (rev(c), 1)),
                  pl.BlockSpec((lc, 2048), lambda c: (rev(c), 1)),
                  pl.BlockSpec((lc, 2048), lambda c: (rev(c), 2)),
                  pl.BlockSpec((lc, 2048), lambda c: (rev(c), 3)),
                  pl.BlockSpec((lc, 128), lambda c: (rev(c), 0)),
                  pl.BlockSpec((1, 128), lambda c: (0, 0)),
                  pl.BlockSpec((1, V_DIM), lambda c: (0, 0)),
                  pl.BlockSpec((lc, V_DIM), lambda c: (rev(c), 0)),
                  pl.BlockSpec((1, N_HEADS, DK, DV), lambda c: (rev(c), 0, 0, 0)),
                  pl.BlockSpec((1, N_HEADS, 8, DK), lambda c: (rev(c), 0, 0, 0)),
                  pl.BlockSpec((lc, V_DIM), lambda c: (rev(c), 0)),
                  pl.BlockSpec((lc, D_MODEL), lambda c: (rev(c), 0)),
                  pl.BlockSpec((V_DIM, D_MODEL), lambda c: (0, 0))],
        out_specs=[pl.BlockSpec((lc, 8192), lambda c: (rev(c), 0)),
                   pl.BlockSpec((lc, 128), lambda c: (rev(c), 0)),
                   pl.BlockSpec((8, 128), lambda c: (0, 0)),
                   pl.BlockSpec((8, V_DIM), lambda c: (0, 0)),
                   pl.BlockSpec((V_DIM, D_MODEL), lambda c: (0, 0))],
        out_shape=[jax.ShapeDtypeStruct((t, N_MAIN), BF), jax.ShapeDtypeStruct((t, 128), BF),
                   jax.ShapeDtypeStruct((8, 128), F32), jax.ShapeDtypeStruct((8, V_DIM), F32),
                   jax.ShapeDtypeStruct((V_DIM, D_MODEL), BF)],
        scratch_shapes=[pltpu.VMEM((N_HEADS, DK, DV), F32), pltpu.VMEM((N_HEADS, 8, DK), F32),
                        pltpu.VMEM((lc, V_DIM), F32), pltpu.VMEM((V_DIM, D_MODEL), F32)],
        compiler_params=_cparams(("arbitrary",)),
    )(dproj, proj, proj, proj, proj, proj, gates, gate_bias, g_head, h_raw, c_states, aux, yb_pre, dyb, w_b)


GA_BLK, GB_BLK = 12, 13


def _full(shape):
    return pl.BlockSpec(shape, lambda i: (0,) * len(shape))


def _rms_backward(xv, g, dhn, dres):
    r = lax.rsqrt(jnp.mean(xv * xv, axis=1, keepdims=True) + EPS)
    xh = xv * r
    dxh = dhn * g
    dx = dres + r * (dxh - xh * jnp.mean(dxh * xh, axis=1, keepdims=True))
    return dx, jnp.sum(dhn * xh, axis=0, keepdims=True)


def _token_chain(ya_pre, w_a, yb, proj, w_o, x, g_ple, w_pg, p, w_pl, target, g_final, *, tm=256):
    t, d = x.shape
    tm = min(tm, t)

    def body(yap_ref, wa_ref, yb_ref, ga_ref, gb_ref, wo_ref, x_ref, gp_ref, wpg_ref, p_ref, wpl_ref, tg_ref,
             gf_ref, dx_ref, dyb_ref, dyap_ref, o_ref, sm_ref, dwa_ref, dwo_ref, dwpg_ref, dwpl_ref,
             dwa_acc, dwo_acc, dwpg_acc, dwpl_acc):
        i = pl.program_id(0)

        @pl.when(i == 0)
        def _():
            sm_ref[...] = jnp.zeros_like(sm_ref)
            dwa_acc[...] = jnp.zeros_like(dwa_acc)
            dwo_acc[...] = jnp.zeros_like(dwo_acc)
            dwpg_acc[...] = jnp.zeros_like(dwpg_acc)
            dwpl_acc[...] = jnp.zeros_like(dwpl_acc)

        yap = yap_ref[...]
        ya = _dot(yap, wa_ref[...]).astype(BF).astype(F32)
        yb_v = yb_ref[...].astype(F32)
        sa = _sigmoid(ga_ref[...].astype(F32))
        sb = _sigmoid(gb_ref[...].astype(F32))
        merged = (sa * ya + sb * yb_v).astype(BF)
        x1 = _dot(merged, wo_ref[...]) + x_ref[...]
        r1 = lax.rsqrt(jnp.mean(x1 * x1, axis=1, keepdims=True) + EPS)
        xh1 = x1 * r1
        gp = gp_ref[...]
        hn2 = xh1 * gp
        hn2_b = hn2.astype(BF)
        gate = _sigmoid(_dot(hn2_b, wpg_ref[...]))
        p_b = p_ref[...].astype(BF)
        pe_v = _dot(p_b, wpl_ref[...])
        x2 = x1 + gate * pe_v
        r2 = lax.rsqrt(jnp.mean(x2 * x2, axis=1, keepdims=True) + EPS)
        xh2 = x2 * r2
        gf = gf_ref[...]
        err = xh2 * gf - tg_ref[...]
        dy = err * (1.0 / d)
        dxh2 = dy * gf
        dx2 = r2 * (dxh2 - xh2 * jnp.mean(dxh2 * xh2, axis=1, keepdims=True))
        dgpre = (dx2 * pe_v * gate * (1.0 - gate)).astype(BF)
        dwpg_acc[...] += _dot_tn(hn2_b, dgpre)
        dwpl_acc[...] += _dot_tn(p_b, (dx2 * gate).astype(BF))
        dhn2 = _dot_nt(dgpre, wpg_ref[...])
        dxh1 = dhn2 * gp
        dx1 = dx2 + r1 * (dxh1 - xh1 * jnp.mean(dxh1 * xh1, axis=1, keepdims=True))
        dx_ref[...] = dx1
        dx1_b = dx1.astype(BF)
        dwo_acc[...] += _dot_tn(merged, dx1_b)
        dm = _dot_nt(dx1_b, wo_ref[...])
        dya = (dm * sa).astype(BF)
        dwa_acc[...] += _dot_tn(yap, dya)
        dyb_ref[...] = (dm * sb).astype(BF)
        o_ref[:, 0:d] = (dm * ya * sa * (1.0 - sa)).astype(BF)
        o_ref[:, d:2 * d] = (dm * yb_v * sb * (1.0 - sb)).astype(BF)
        dyap_ref[...] = _dot_nt(dya, wa_ref[...]).astype(BF)

        sm_ref[0:1, :] += (0.5 / d) * jnp.sum(err * err, axis=0, keepdims=True)
        sm_ref[1:2, :] += jnp.sum(dy * xh2, axis=0, keepdims=True)
        sm_ref[2:3, :] += jnp.sum(dhn2 * xh1, axis=0, keepdims=True)

        @pl.when(i == t // tm - 1)
        def _():
            dwa_ref[...] = dwa_acc[...].astype(BF)
            dwo_ref[...] = dwo_acc[...].astype(BF)
            dwpg_ref[...] = dwpg_acc[...].astype(BF)
            dwpl_ref[...] = dwpl_acc[...].astype(BF)

    row = pl.BlockSpec((tm, d), lambda i: (i, 0))
    bf = jax.ShapeDtypeStruct((t, d), BF)
    return pl.pallas_call(
        body, name="token_chain", grid=(t // tm,),
        in_specs=[row, _full((d, d)), row, pl.BlockSpec((tm, d), lambda i: (i, GA_BLK)),
                  pl.BlockSpec((tm, d), lambda i: (i, GB_BLK)), _full((d, d)), row, _full((1, d)), _full((d, d)),
                  pl.BlockSpec((tm, PLE_DIM), lambda i: (i, 0)), _full((PLE_DIM, d)), row, _full((1, d))],
        out_specs=[row, row, row, pl.BlockSpec((tm, 2 * d), lambda i: (i, 6)), _full((8, d)),
                   _full((d, d)), _full((d, d)), _full((d, d)), _full((PLE_DIM, d))],
        out_shape=[jax.ShapeDtypeStruct((t, d), F32), bf, bf,
                   jax.ShapeDtypeStruct((t, N_MAIN), BF), jax.ShapeDtypeStruct((8, d), F32),
                   jax.ShapeDtypeStruct((d, d), BF), jax.ShapeDtypeStruct((d, d), BF),
                   jax.ShapeDtypeStruct((d, d), BF), jax.ShapeDtypeStruct((PLE_DIM, d), BF)],
        scratch_shapes=[pltpu.VMEM((d, d), F32), pltpu.VMEM((d, d), F32), pltpu.VMEM((d, d), F32),
                        pltpu.VMEM((PLE_DIM, d), F32)],
        compiler_params=_cparams(("arbitrary",)),
    )(ya_pre, w_a, yb, proj, proj, w_o, x, g_ple, w_pg, p, w_pl, target, g_final)


def _dhn_mix_bwd(dproj, w_main, dgates, w_gate, x, g_mix, dx1, comm, *, tm=1024, tk=2048):
    t, d = x.shape
    tm = min(tm, t)
    nk = N_MAIN // tk
    ni = t // tm
    n_in = 7 + comm.n

    def body(*refs):
        dp_ref, w_ref, dgt_ref, wg_ref, x_ref, g_ref, dres_ref = refs[:7]
        dx_ref, dg_ref = refs[n_in], refs[n_in + 1]
        acc = refs[-1]
        start, wait = comm.ops(refs[7:n_in], refs[n_in + 2:n_in + 2 + comm.n], *refs[n_in + 2 + comm.n:-1])
        i, k = pl.program_id(0), pl.program_id(1)
        pl.when((i == 0) & (k == 0))(start)

        @pl.when(k == 0)
        def _():
            acc[...] = _dot_nt(dp_ref[...], w_ref[...]) + _dot_nt(dgt_ref[...], wg_ref[...])

        @pl.when(k > 0)
        def _():
            acc[...] += _dot_nt(dp_ref[...], w_ref[...])

        @pl.when((i == 0) & (k == 0))
        def _():
            dg_ref[...] = jnp.zeros_like(dg_ref)

        @pl.when(k == nk - 1)
        def _():
            dx, dg = _rms_backward(x_ref[...], g_ref[...], acc[...], dres_ref[...])
            dx_ref[...] = dx
            dg_ref[0:1, :] += dg

        pl.when((i == ni - 1) & (k == nk - 1))(wait)

    any_spec = pl.BlockSpec(memory_space=pl.ANY)
    row = pl.BlockSpec((tm, d), lambda i, k: (i, 0))
    res = pl.pallas_call(
        body, name="dhn_mix_bwd", grid=(ni, nk),
        in_specs=[pl.BlockSpec((tm, tk), lambda i, k: (i, k)), pl.BlockSpec((d, tk), lambda i, k: (0, k)),
                  pl.BlockSpec((tm, 128), lambda i, k: (i, 0)), pl.BlockSpec((d, 128), lambda i, k: (0, 0)),
                  row, pl.BlockSpec((1, d), lambda i, k: (0, 0)), row] + [any_spec] * comm.n,
        out_specs=[row, pl.BlockSpec((8, d), lambda i, k: (0, 0))] + [any_spec] * comm.n,
        out_shape=[jax.ShapeDtypeStruct((t, d), F32), jax.ShapeDtypeStruct((8, d), F32)] + comm.out_shapes,
        scratch_shapes=comm.scratch + [pltpu.VMEM((tm, d), F32)],
        compiler_params=_cparams(("arbitrary", "arbitrary")),
    )(dproj, w_main, dgates, w_gate, x, g_mix, dx1, *comm.srcs)
    return res[0], res[1], list(res[2:])


def _position():
    x, y, c = lax.axis_index("x"), lax.axis_index("y"), lax.axis_index("c")
    return x, y, c


def _rms_fwd_all_gather(x, g, srcs, *, tm=512):
    t, d = x.shape
    tm = min(tm, t)
    nt = t // tm
    nb = len(srcs)
    any_spec = pl.BlockSpec(memory_space=pl.ANY)

    def body(*refs):
        x_ref, g_ref = refs[:2]
        src = refs[2:2 + nb]
        hn_ref, hnt_ref = refs[2 + nb:4 + nb]
        dst = refs[4 + nb:4 + 2 * nb]
        send_sems, recv_sems, local_sems = refs[4 + 2 * nb:]
        i = pl.program_id(0)

        def parts():
            px, py, pc = _position()
            me, sibling = (px, py, pc), (px, py, 1 - pc)
            chips = [(1 - px, py), (px, 1 - py), (1 - px, 1 - py)]

            def slot(b, qx, qy, qc):
                return dst[b].at[4 * qx + 2 * qy + qc]

            def copy(k, b, block, to, from_src=False):
                return pltpu.make_async_remote_copy(
                    src_ref=src[b] if from_src else slot(b, *block), dst_ref=slot(b, *block),
                    send_sem=send_sems.at[b, k], recv_sem=recv_sems.at[b, k],
                    device_id=to, device_id_type=MESH)

            mine = [pltpu.make_async_copy(src[b], slot(b, *me), local_sems.at[b]) for b in range(nb)]
            first = [copy(0, b, me, sibling, True) for b in range(nb)]
            first += [copy(1 + j, b, me, (*chip, pc), True) for j, chip in enumerate(chips) for b in range(nb)]
            return me, sibling, chips, pc, copy, mine, first

        @pl.when(i == 0)
        def _():
            _, _, _, _, _, mine, first = parts()
            for cp in mine + first:
                cp.start()

        xv = x_ref[...]
        r = lax.rsqrt(jnp.mean(xv * xv, axis=1, keepdims=True) + EPS)
        hn = xv * r * g_ref[...]
        hn_ref[...] = hn.astype(BF)
        hnt_ref[...] = hn.T.astype(BF)

        @pl.when(i == nt - 1)
        def _():
            me, sibling, chips, pc, copy, mine, first = parts()
            passed = []
            for j, chip in enumerate(chips):
                for b in range(nb):
                    copy(1 + j, b, (*chip, pc), me).wait_recv()
                    fwd = copy(4 + j, b, (*chip, pc), sibling)
                    fwd.start()
                    passed.append(fwd)
            for b in range(nb):
                copy(0, b, sibling, me).wait_recv()
            for j, chip in enumerate(chips):
                for b in range(nb):
                    copy(4 + j, b, (*chip, 1 - pc), me).wait_recv()
            for cp in first + passed:
                cp.wait_send()
            for cp in mine:
                cp.wait()

    res = pl.pallas_call(
        body, name="rms_mix_all_gather", grid=(nt,),
        in_specs=[pl.BlockSpec((tm, d), lambda i: (i, 0)), pl.BlockSpec((1, d), lambda i: (0, 0))]
        + [any_spec] * nb,
        out_specs=[pl.BlockSpec((tm, d), lambda i: (i, 0)), pl.BlockSpec((d, tm), lambda i: (0, i))]
        + [any_spec] * nb,
        out_shape=[jax.ShapeDtypeStruct((t, d), BF), jax.ShapeDtypeStruct((d, t), BF)]
        + [jax.ShapeDtypeStruct((N_DEV,) + s.shape, s.dtype) for s in srcs],
        scratch_shapes=[pltpu.SemaphoreType.DMA((nb, 7)), pltpu.SemaphoreType.DMA((nb, 7)),
                        pltpu.SemaphoreType.DMA((nb,))],
        compiler_params=_cparams(("arbitrary",)),
    )(x, g, *srcs)
    return res[0], res[1], list(res[2:])


def _comm_call(comm, *, name):
    any_spec = pl.BlockSpec(memory_space=pl.ANY)

    def body(*refs):
        start, wait = comm.ops(refs[:comm.n], refs[comm.n:2 * comm.n], *refs[2 * comm.n:])
        start()
        wait()

    return pl.pallas_call(
        body, name=name, in_specs=[any_spec] * comm.n, out_specs=[any_spec] * comm.n,
        out_shape=comm.out_shapes, scratch_shapes=comm.scratch,
    )(*comm.srcs)


LAYOUT_ROWS = 128
N_GLOB_BLK = 113
N_TAIL_BLK = 17


def _lane(tr):
    return lax.broadcasted_iota(jnp.int32, (tr, 128), 1)


def _assemble_w(g_win):
    d = g_win.shape[1]
    tr = LAYOUT_ROWS

    def body(win_ref, wm_ref, wg_ref):
        lane = _lane(tr)

        def shifted(k, j):
            cur = win_ref[k, :, 128 * j:128 * (j + 1)].astype(F32)
            if k == 0:
                return cur
            cur = pltpu.roll(cur, k, 1)
            if j == 0:
                return jnp.where(lane < k, 0.0, cur)
            prev = win_ref[k, :, 128 * (j - 1):128 * j].astype(F32)
            return jnp.where(lane < k, pltpu.roll(prev, k, 1), cur)

        def glob(gb):
            k = min(gb // 14, N_DEV - 1)
            j = gb - 14 * k
            v = shifted(k, j)
            if j == 0 and k > 0:
                v = v + shifted(k - 1, 14)
            return v

        for c in range(64):
            wm_ref[:, 128 * c:128 * (c + 1)] = glob(32 + c).astype(BF)
        for c in range(32):
            wm_ref[:, 8192 + 128 * c:8192 + 128 * (c + 1)] = glob(c).astype(BF)
        tail = [glob(96 + j) for j in range(N_TAIL_BLK)]
        for j in range(N_TAIL_BLK):
            v = jnp.where(lane < 120, pltpu.roll(tail[j], 120, 1),
                          pltpu.roll(tail[(j + 1) % N_TAIL_BLK], 120, 1)).astype(BF)
            if j < 16:
                wm_ref[:, 12288 + 128 * j:12288 + 128 * (j + 1)] = v
            else:
                wg_ref[...] = v

    return pl.pallas_call(
        body, name="assemble_w", grid=(d // tr,),
        in_specs=[pl.BlockSpec((N_DEV, tr, WIN_W), lambda i: (0, i, 0))],
        out_specs=[pl.BlockSpec((tr, N_MAIN), lambda i: (i, 0)), pl.BlockSpec((tr, 128), lambda i: (i, 0))],
        out_shape=[jax.ShapeDtypeStruct((d, N_MAIN), BF), jax.ShapeDtypeStruct((d, 128), BF)],
        compiler_params=_cparams(("parallel",)),
    )(g_win)


def _pack_windows(dw_main, dw_gate):
    d = dw_main.shape[0]
    tr = LAYOUT_ROWS

    def body(dm_ref, dg_ref, o_ref):
        lane = _lane(tr)

        def main_tail(j):
            return (dm_ref[:, 12288 + 128 * j:12288 + 128 * (j + 1)] if j < 16 else dg_ref[...]).astype(F32)

        def glob(gb):
            if gb >= N_GLOB_BLK:
                return jnp.zeros((tr, 128), F32)
            if gb < 32:
                return dm_ref[:, 8192 + 128 * gb:8192 + 128 * (gb + 1)].astype(F32)
            if gb < 96:
                return dm_ref[:, 128 * (gb - 32):128 * (gb - 31)].astype(F32)
            j = gb - 96
            return jnp.where(lane < 8, pltpu.roll(main_tail((j - 1) % N_TAIL_BLK), 8, 1),
                             pltpu.roll(main_tail(j), 8, 1))

        for j in range(N_DEV):
            cur = glob(14 * j)
            for m in range(WIN_W // 128):
                nxt = glob(14 * j + m + 1)
                if j == 0:
                    v = cur
                else:
                    v = jnp.where(lane < 128 - j, pltpu.roll(cur, 128 - j, 1), pltpu.roll(nxt, 128 - j, 1))
                o_ref[j, :, 128 * m:128 * (m + 1)] = v.astype(BF)
                cur = nxt

    return pl.pallas_call(
        body, name="pack_windows", grid=(d // tr,),
        in_specs=[pl.BlockSpec((tr, N_MAIN), lambda i: (i, 0)), pl.BlockSpec((tr, 128), lambda i: (i, 0))],
        out_specs=pl.BlockSpec((N_DEV, tr, WIN_W), lambda i: (0, i, 0)),
        out_shape=jax.ShapeDtypeStruct((N_DEV, d, WIN_W), BF),
        compiler_params=_cparams(("parallel",)),
    )(dw_main, dw_gate)


def _sum_slots(recv, *, name, tr):
    _, r, cdim = recv.shape
    tr = min(tr, r)

    def body(r_ref, o_ref):
        total = r_ref[0].astype(F32)
        for s in range(1, N_DEV):
            total = total + r_ref[s].astype(F32)
        o_ref[...] = total

    return pl.pallas_call(
        body, name=name, grid=(r // tr,),
        in_specs=[pl.BlockSpec((N_DEV, tr, cdim), lambda i: (0, i, 0))],
        out_specs=pl.BlockSpec((tr, cdim), lambda i: (i, 0)),
        out_shape=jax.ShapeDtypeStruct((r, cdim), F32),
        compiler_params=_cparams(("parallel",)),
    )(recv)


def _adamw(w, g, m, v, *, name):
    lead = w.ndim - 2
    r, cdim = w.shape[-2:]
    if r % 128 == 0:
        tr, tc = 128, cdim
    elif r >= 128 and cdim % 128 == 0:
        tr, tc = r, 128
    else:
        tr, tc = r, cdim
    c1 = 1.0 - ADAM_B1 ** ADAM_STEP
    c2 = 1.0 - ADAM_B2 ** ADAM_STEP

    def body(w_ref, g_ref, m_ref, v_ref, d_ref, mo_ref, vo_ref):
        gv = g_ref[...]
        mn = ADAM_B1 * m_ref[...] + (1.0 - ADAM_B1) * gv
        vn = ADAM_B2 * v_ref[...] + (1.0 - ADAM_B2) * (gv * gv)
        d_ref[...] = -ADAM_LR * ((mn / c1) / (jnp.sqrt(vn / c2) + ADAM_EPS) + ADAM_WD * w_ref[...])
        mo_ref[...] = mn
        vo_ref[...] = vn

    blk = pl.BlockSpec((1,) * lead + (tr, tc), lambda i, j: (0,) * lead + (i, j))
    shp = jax.ShapeDtypeStruct(w.shape, F32)
    return pl.pallas_call(
        body, name=name, grid=(r // tr, cdim // tc),
        in_specs=[blk] * 4, out_specs=[blk] * 3, out_shape=[shp] * 3,
        compiler_params=_cparams(("parallel", "parallel")),
    )(w, g, m, v)


def kernel(x, p, g_mix, w_in, conv_w, conv_b, w_a_out, b_gates, g_head, w_b_out, w_o, g_ple, w_ple_gate, w_ple, g_final, loss_target, m_g_mix, m_w_in, m_conv_w, m_conv_b, m_w_a_out, m_b_gates, m_g_head, m_w_b_out, m_w_o, m_g_ple, m_w_ple_gate, m_w_ple, m_g_final, v_g_mix, v_w_in, v_conv_w, v_conv_b, v_w_a_out, v_b_gates, v_g_head, v_w_b_out, v_w_o, v_g_ple, v_w_ple_gate, v_w_ple, v_g_final):
    d = D_MODEL
    t = x.shape[1]
    x2d = x.reshape(t, d)
    p2d = p.reshape(t, PLE_DIM)
    tgt = loss_target.reshape(t, d)

    win = jnp.pad(w_in[0].astype(BF), ((0, 0), (0, WIN_W - SHARD_W)))
    rows = jnp.concatenate([w_a_out[0].astype(BF), w_b_out[0].astype(BF), w_o[0].astype(BF),
                            w_ple_gate[0].astype(BF), w_ple[0].astype(BF).reshape(32, d)], axis=0)
    cfl = jnp.pad(conv_w[0], ((0, 5), (0, 0)))
    hn, hnt, (g_win, g_cf) = _rms_fwd_all_gather(x2d, g_mix, [win, cfl])

    w_main, w_gate = _assemble_w(g_win)
    conv_w8 = jnp.pad(g_cf[:, :3, :].transpose(1, 0, 2).reshape(3, d), ((0, 5), (0, 0)))
    gate_bias = jnp.pad(b_gates, ((0, 0), (IG_LANE, 0)))

    proj, gates, (g_rows,) = _mm(hn, w_main, form="nn", out_dtype=BF, name="proj", tm=2048,
                                 extra=(w_gate, F32), comm=_DirectComm([rows], "gather"))
    w_a = g_rows[:, 0:128].reshape(d, d)
    w_b = g_rows[:, 128:384].reshape(V_DIM, d)
    w_of = g_rows[:, 384:512].reshape(d, d)
    w_pg = g_rows[:, 512:640].reshape(d, d)
    w_pl = g_rows[:, 640:672].reshape(N_DEV, PLE_DIM, 128).transpose(1, 0, 2).reshape(PLE_DIM, d)
    ya_pre = _branch_a_fwd(proj, conv_w8, conv_b)
    yb_pre, yb, h_raw, c_states, aux = _mlstm_fwd(proj, gates, gate_bias, g_head, w_b)
    dx1, dyb, dya_pre, dproj, small_fin, dw_a, dw_o, dw_pg, dw_pl = _token_chain(
        ya_pre, w_a, yb, proj, w_of, x2d, g_ple, w_pg, p2d, w_pl, tgt, g_final.reshape(1, d))

    dproj, dconv = _branch_a_bwd(dproj, proj, dya_pre, conv_w8, conv_b)
    dproj, dgates, dbias, dg_head, dw_b = _mlstm_bwd(dproj, proj, gates, gate_bias, g_head, h_raw, c_states, aux,
                                                     yb_pre, dyb, w_b)
    s_rows = jnp.concatenate([
        dw_a.reshape(N_DEV, 128, d), dw_b.reshape(N_DEV, 256, d), dw_o.reshape(N_DEV, 128, d),
        dw_pg.reshape(N_DEV, 128, d),
        dw_pl.reshape(PLE_DIM, N_DEV, 128).transpose(1, 0, 2).reshape(N_DEV, 32, d)], axis=1)
    dw_main, dw_gate, (r_rows,) = _mm(hnt, dproj, form="nn", out_dtype=BF, name="dw_main", tk=2048,
                                      extra=(dgates, BF), comm=_DirectComm([s_rows], "exchange"))
    s_win = _pack_windows(dw_main, dw_gate)
    grad_x, dg_mix, (r_win,) = _dhn_mix_bwd(dproj, w_main, dgates, w_gate, x2d, g_mix, dx1,
                                            _DirectComm([s_win], "exchange"))

    vec = jnp.concatenate([dg_mix[0], dconv[3], dg_head[0], small_fin[2], small_fin[1],
                           dbias[0, IG_LANE:], jnp.sum(small_fin[0]).reshape(1),
                           jnp.zeros((7 * d - 6153,), F32)]).reshape(7, d)
    conv_part = jnp.pad(dconv[:3].reshape(3, N_DEV, 128).transpose(1, 0, 2).reshape(N_DEV, 1, 384),
                        ((0, 0), (0, 0), (0, d - 384)))
    s_f32 = jnp.concatenate([jnp.broadcast_to(vec[None], (N_DEV, 7, d)), conv_part], axis=1)
    (r_f32,) = _comm_call(_DirectComm([s_f32], "exchange"), name="small_grads_exchange")
    sum_win = _sum_slots(r_win, name="sum_win", tr=128)
    sum_rows = _sum_slots(r_rows, name="sum_rows", tr=96)
    sum_f32 = _sum_slots(r_f32, name="sum_f32", tr=8)

    g_w_in = sum_win[:, :SHARD_W]
    g_w_a = sum_rows[0:128]
    g_w_b = sum_rows[128:384]
    g_w_o = sum_rows[384:512]
    g_w_pg = sum_rows[512:640]
    g_w_pl = sum_rows[640:672].reshape(PLE_DIM, 128)
    vsum = sum_f32[:7].reshape(7 * d)
    g_g_mix = vsum[0:1024].reshape(1, d)
    g_conv_b = vsum[1024:2048].reshape(1, d)
    g_g_head = vsum[2048:4096].reshape(1, V_DIM)
    g_g_ple = vsum[4096:5120].reshape(1, d)
    g_g_final = vsum[5120:6144].reshape(1, d)
    g_b_gates = vsum[6144:6152].reshape(1, 8)
    g_conv_w = sum_f32[7, :384].reshape(3, 128)

    loss = vsum[6152]

    names = ["g_mix", "w_in", "conv_w", "conv_b", "w_a_out", "b_gates", "g_head", "w_b_out", "w_o", "g_ple",
             "w_ple_gate", "w_ple", "g_final"]
    weights = [g_mix, w_in, conv_w, conv_b, w_a_out, b_gates, g_head, w_b_out, w_o, g_ple, w_ple_gate, w_ple,
               g_final]
    moms = [m_g_mix, m_w_in, m_conv_w, m_conv_b, m_w_a_out, m_b_gates, m_g_head, m_w_b_out, m_w_o, m_g_ple,
            m_w_ple_gate, m_w_ple, m_g_final]
    vels = [v_g_mix, v_w_in, v_conv_w, v_conv_b, v_w_a_out, v_b_gates, v_g_head, v_w_b_out, v_w_o, v_g_ple,
            v_w_ple_gate, v_w_ple, v_g_final]
    grads2d = [g_g_mix, g_w_in, g_conv_w, g_conv_b, g_w_a, g_b_gates, g_g_head, g_w_b, g_w_o, g_g_ple, g_w_pg,
               g_w_pl, g_g_final]
    grads, deltas, new_m, new_v = [], [], [], []
    for nm, w, m_, v_, g2 in zip(names, weights, moms, vels, grads2d):
        shp = w.shape
        if nm == "w_in":
            dl, mn, vn = _adamw(w[0].T, g2.T, m_[0].T, v_[0].T, name="adamw_" + nm)
            grads.append(g2.reshape(shp))
            deltas.append(dl.T.reshape(shp))
            new_m.append(mn.T.reshape(shp))
            new_v.append(vn.T.reshape(shp))
            continue
        kshp = shp if w.ndim >= 2 else (1,) + shp
        gk = g2.reshape(kshp)
        dl, mn, vn = _adamw(w.reshape(kshp), gk, m_.reshape(kshp), v_.reshape(kshp), name="adamw_" + nm)
        grads.append(gk.reshape(shp))
        deltas.append(dl.reshape(shp))
        new_m.append(mn.reshape(shp))
        new_v.append(vn.reshape(shp))
    return (loss, grad_x.reshape(x.shape), *grads, *deltas, *new_m, *new_v)
```
